```python
import jax, jax.numpy as jnp
from jax import lax
import numpy as np

D_MODEL = 1024
BATCH = 16
SEQ = 2048
DEPTH = 1

N_MEM = 256
EPS = 1e-6

HEAD_DIM = 64
DIL_PATTERNS = ((128, 1), (512, 4), (2048, 16))
N_GROUPS = len(DIL_PATTERNS)
ATTN_SLOTS = 4
ATTN_HEADS = N_GROUPS * ATTN_SLOTS
ATTN_WIDTH = ATTN_HEADS * HEAD_DIM
ATTN_OUT = ATTN_SLOTS * HEAD_DIM
ROT_DIM = HEAD_DIM // 4
ROPE_THETA = 500000.0

CONV_WIDTH = 768
CONV_K = 3

MEM_HEADS = 4
MEM_HEAD_DIM = 128
MEM_WIDTH = MEM_HEADS * MEM_HEAD_DIM

N_BRANCH = 3
IN_SIZES = (ATTN_WIDTH, ATTN_WIDTH, ATTN_WIDTH, CONV_WIDTH, CONV_WIDTH, CONV_WIDTH,
            MEM_WIDTH, N_BRANCH * D_MODEL)
IN_COLS = sum(IN_SIZES)
IN_SPLITS = tuple(int(c) for c in np.cumsum(IN_SIZES)[:-1])

N_EXPERT_GROUPS = 4
EXPERTS_PER_GROUP = 8
N_EXPERTS = N_EXPERT_GROUPS * EXPERTS_PER_GROUP
TOP_K = 2
EXPERT_FF = 512
MOE_BLOCK = 256

kernel_name = 'hybrid_dilated_conv_mem_hmoe_block'


def _rmsnorm(x, g):
    xf = x.astype(jnp.float32)
    y = xf * lax.rsqrt(jnp.mean(xf * xf, axis=-1, keepdims=True) + EPS)
    return (y * g.astype(jnp.float32)).astype(x.dtype)


def _rope_tables(positions):
    inv = ROPE_THETA ** (-jnp.arange(0, ROT_DIM, 2, dtype=jnp.float32) / ROT_DIM)
    ang = positions.astype(jnp.float32)[..., None] * inv
    return jnp.cos(ang)[:, :, None, :], jnp.sin(ang)[:, :, None, :]


def _partial_rope(t, cos, sin):
    tr = t[..., :ROT_DIM].astype(jnp.float32)
    t1, t2 = tr[..., :ROT_DIM // 2], tr[..., ROT_DIM // 2:]
    rot = jnp.concatenate([t1 * cos - t2 * sin, t2 * cos + t1 * sin], axis=-1)
    return jnp.concatenate([rot.astype(t.dtype), t[..., ROT_DIM:]], axis=-1)


def _banded_attention(q, k, v, half):
    b, g, L, h, dh = q.shape
    blk = half
    nb = -(-L // blk)
    lp = nb * blk
    qb = jnp.pad(q, ((0, 0), (0, 0), (0, lp - L), (0, 0), (0, 0))).reshape(b, g, nb, blk, h, dh)
    pad_k = ((0, 0), (0, 0), (blk, lp - L + blk), (0, 0), (0, 0))

    def windows(t):
        tb = jnp.pad(t, pad_k).reshape(b, g, nb + 2, blk, h, dh)
        return jnp.concatenate([tb[:, :, :-2], tb[:, :, 1:-1], tb[:, :, 2:]], axis=3)

    kw, vw = windows(k), windows(v)
    qpos = np.arange(nb)[:, None] * blk + np.arange(blk)[None, :]
    kpos = np.arange(nb)[:, None] * blk - blk + np.arange(3 * blk)[None, :]
    valid = ((np.abs(qpos[:, :, None] - kpos[:, None, :]) <= half)
             & (kpos[:, None, :] >= 0) & (kpos[:, None, :] < L))
    s = jnp.einsum('bgnqhd,bgnkhd->bgnhqk', qb, kw).astype(jnp.float32) * (dh ** -0.5)
    s = jnp.where(jnp.asarray(valid)[:, None], s, jnp.finfo(jnp.float32).min)
    m = jnp.max(s, axis=-1, keepdims=True)
    p = jnp.exp(s - m)
    den = jnp.sum(p, axis=-1)
    o = jnp.einsum('bgnhqk,bgnkhd->bgnqhd', p.astype(v.dtype), vw)
    o = o / jnp.transpose(den, (0, 1, 2, 4, 3))[..., None].astype(o.dtype)
    lse = jnp.transpose(m[..., 0] + jnp.log(den), (0, 1, 2, 4, 3))
    o = o.reshape(b, g, lp, h, dh)[:, :, :L]
    lse = lse.reshape(b, g, lp, h)[:, :, :L]
    return o, lse


def _dilated_attention(q, k, v, window, dilation):
    b, s, h, dh = q.shape
    L = s // dilation
    half = window // (2 * dilation)

    def to_sub(t):
        return jnp.transpose(t.reshape(b, L, dilation, h, dh), (0, 2, 1, 3, 4))

    o, lse = _banded_attention(to_sub(q), to_sub(k), to_sub(v), half)
    o = jnp.transpose(o, (0, 2, 1, 3, 4)).reshape(b, s, h, dh)
    lse = jnp.transpose(lse, (0, 2, 1, 3)).reshape(b, s, h)
    return o, lse


def _hierarchical_moe(h, w_rg, b_rg, w_re, b_re, w_gate, w_up, w_down):
    b, s, d = h.shape
    n = b * s
    hf = h.reshape(n, d)
    hr = hf.astype(jnp.float32)
    pg = jax.nn.softmax(hr @ w_rg.astype(jnp.float32) + b_rg.astype(jnp.float32), axis=-1)
    pg_top, g_idx = lax.top_k(pg, 1)
    el = (hr @ w_re.astype(jnp.float32) + b_re.astype(jnp.float32)).reshape(n, N_EXPERT_GROUPS, EXPERTS_PER_GROUP)
    sel = jnp.einsum('ng,nge->ne', jax.nn.one_hot(g_idx[:, 0], N_EXPERT_GROUPS, dtype=jnp.float32), el)
    pe_top, e_idx = lax.top_k(jax.nn.softmax(sel, axis=-1), TOP_K)
    weights = pg_top * pe_top / jnp.sum(pe_top, axis=-1, keepdims=True)
    ids = g_idx * EXPERTS_PER_GROUP + e_idx

    nk = n * TOP_K
    e = ids.reshape(nk).astype(jnp.int32)
    w = weights.reshape(nk)
    t = jnp.repeat(jnp.arange(n, dtype=jnp.int32), TOP_K)
    order = jnp.argsort(e, stable=True)
    e_s, t_s, w_s = e[order], t[order], w[order]
    counts = jnp.bincount(e, length=N_EXPERTS)
    padded = ((counts + MOE_BLOCK - 1) // MOE_BLOCK) * MOE_BLOCK
    start = jnp.cumsum(counts) - counts
    pend = jnp.cumsum(padded)
    pstart = pend - padded
    dest = pstart[e_s] + (jnp.arange(nk, dtype=jnp.int32) - start[e_s])
    n_rows = (-(-nk // MOE_BLOCK) + N_EXPERTS) * MOE_BLOCK
    n_blk = n_rows // MOE_BLOCK
    row_tok = jnp.zeros((n_rows,), jnp.int32).at[dest].set(t_s)
    row_w = jnp.zeros((n_rows,), jnp.float32).at[dest].set(w_s)
    blk_exp = jnp.minimum(jnp.searchsorted(pend, jnp.arange(n_blk) * MOE_BLOCK, side='right'),
                          N_EXPERTS - 1).astype(jnp.int32)

    def expert_block(args):
        tok, wt, ex = args
        xb = hf[tok]
        a = xb @ w_gate[ex]
        u = xb @ w_up[ex]
        y = (jax.nn.silu(a) * u) @ w_down[ex]
        return y * wt[:, None].astype(y.dtype)

    ys = lax.map(expert_block, (row_tok.reshape(n_blk, MOE_BLOCK),
                                row_w.reshape(n_blk, MOE_BLOCK), blk_exp))
    out = jnp.zeros((n, d), h.dtype).at[row_tok].add(ys.reshape(n_rows, d).astype(h.dtype))
    return out.reshape(b, s, d)


def setup_inputs(seed: int = 0) -> dict:
    key = jax.random.key(seed)
    ks = jax.random.split(key, 24)
    f32 = jnp.float32

    def nrm(k, shape, scale):
        return jax.random.normal(k, shape, f32) * scale

    def gain(k, dim):
        return 1.0 + 0.05 * jax.random.normal(k, (DEPTH, dim), f32)

    x = jax.random.normal(ks[0], (BATCH, SEQ, D_MODEL), f32)
    mem = jax.random.normal(ks[1], (BATCH, N_MEM, D_MODEL), f32)
    positions = (jnp.arange(SEQ, dtype=jnp.int32)[None, :]
                 + jax.random.randint(ks[2], (BATCH, 1), 0, 4096, dtype=jnp.int32))
    return {
        'x': x,
        'mem': mem,
        'positions': positions,
        'g_mix': gain(ks[3], D_MODEL),
        'g_mem': gain(ks[4], D_MODEL),
        'w_in': nrm(ks[5], (DEPTH, D_MODEL, IN_COLS), D_MODEL ** -0.5),
        'g_qn_attn': gain(ks[6], HEAD_DIM),
        'g_kn_attn': gain(ks[7], HEAD_DIM),
        'w_conv': nrm(ks[8], (DEPTH, CONV_K, CONV_WIDTH), CONV_K ** -0.5),
        'w_mem_kv': nrm(ks[9], (DEPTH, D_MODEL, 2 * MEM_WIDTH), D_MODEL ** -0.5),
        'g_qn_mem': gain(ks[10], MEM_HEAD_DIM),
        'g_kn_mem': gain(ks[11], MEM_HEAD_DIM),
        'w_proj_attn': nrm(ks[12], (DEPTH, ATTN_OUT, D_MODEL), ATTN_OUT ** -0.5),
        'w_proj_conv': nrm(ks[13], (DEPTH, CONV_WIDTH, D_MODEL), CONV_WIDTH ** -0.5),
        'w_proj_mem': nrm(ks[14], (DEPTH, MEM_WIDTH, D_MODEL), MEM_WIDTH ** -0.5),
        'w_out': nrm(ks[15], (DEPTH, D_MODEL, D_MODEL), D_MODEL ** -0.5),
        'g_ffn': gain(ks[16], D_MODEL),
        'w_router_group': nrm(ks[17], (DEPTH, D_MODEL, N_EXPERT_GROUPS), D_MODEL ** -0.5),
        'b_router_group': nrm(ks[18], (DEPTH, N_EXPERT_GROUPS), 0.01),
        'w_router_expert': nrm(ks[19], (DEPTH, D_MODEL, N_EXPERTS), D_MODEL ** -0.5),
        'b_router_expert': nrm(ks[20], (DEPTH, N_EXPERTS), 0.01),
        'w_gate': nrm(ks[21], (DEPTH, N_EXPERTS, D_MODEL, EXPERT_FF), D_MODEL ** -0.5),
        'w_up': nrm(ks[22], (DEPTH, N_EXPERTS, D_MODEL, EXPERT_FF), D_MODEL ** -0.5),
        'w_down': nrm(ks[23], (DEPTH, N_EXPERTS, EXPERT_FF, D_MODEL), EXPERT_FF ** -0.5),
    }


def reference(x, mem, positions, g_mix, g_mem, w_in, g_qn_attn, g_kn_attn, w_conv,
              w_mem_kv, g_qn_mem, g_kn_mem, w_proj_attn, w_proj_conv, w_proj_mem, w_out,
              g_ffn, w_router_group, b_router_group, w_router_expert, b_router_expert,
              w_gate, w_up, w_down):
    b, s, d = x.shape
    cos, sin = _rope_tables(positions)
    cos, sin = cos.astype(x.dtype), sin.astype(x.dtype)
    for l in range(DEPTH):
        h = _rmsnorm(x, g_mix[l])
        proj = h @ w_in[l]
        q_a, k_a, v_a, cx, cb, cc, q_m, gate_logits = jnp.split(proj, IN_SPLITS, axis=-1)

        q_a = _partial_rope(_rmsnorm(q_a.reshape(b, s, ATTN_HEADS, HEAD_DIM), g_qn_attn[l]), cos, sin)
        k_a = _partial_rope(_rmsnorm(k_a.reshape(b, s, ATTN_HEADS, HEAD_DIM), g_kn_attn[l]), cos, sin)
        v_a = v_a.reshape(b, s, ATTN_HEADS, HEAD_DIM)
        q_a = q_a.reshape(b, s, N_GROUPS, ATTN_SLOTS, HEAD_DIM)
        k_a = k_a.reshape(b, s, N_GROUPS, ATTN_SLOTS, HEAD_DIM)
        v_a = v_a.reshape(b, s, N_GROUPS, ATTN_SLOTS, HEAD_DIM)
        outs, lses = [], []
        for gi, (window, dilation) in enumerate(DIL_PATTERNS):
            o_g, lse_g = _dilated_attention(q_a[:, :, gi], k_a[:, :, gi], v_a[:, :, gi], window, dilation)
            outs.append(o_g)
            lses.append(lse_g)
        alpha = jax.nn.softmax(jnp.stack(lses, axis=2), axis=2)
        o_attn = jnp.einsum('bsgh,bsghd->bshd', alpha.astype(x.dtype),
                            jnp.stack(outs, axis=2)).reshape(b, s, ATTN_OUT)

        u = cc * cx
        up = jnp.pad(u, ((0, 0), (1, 1), (0, 0)))
        wc = w_conv[l]
        y_conv = wc[0] * up[:, :-2] + wc[1] * up[:, 1:-1] + wc[2] * up[:, 2:]
        z_conv = cb * y_conv

        mkv = _rmsnorm(mem, g_mem[l]) @ w_mem_kv[l]
        k_m, v_m = jnp.split(mkv, 2, axis=-1)
        k_m = _rmsnorm(k_m.reshape(b, N_MEM, MEM_HEADS, MEM_HEAD_DIM), g_kn_mem[l])
        v_m = v_m.reshape(b, N_MEM, MEM_HEADS, MEM_HEAD_DIM)
        q_m = _rmsnorm(q_m.reshape(b, s, MEM_HEADS, MEM_HEAD_DIM), g_qn_mem[l])
        sm = jnp.einsum('bshd,bmhd->bhsm', q_m, k_m).astype(jnp.float32) * (MEM_HEAD_DIM ** -0.5)
        pm = jax.nn.softmax(sm, axis=-1).astype(v_m.dtype)
        o_mem = jnp.einsum('bhsm,bmhd->bshd', pm, v_m).reshape(b, s, MEM_WIDTH)

        gates = jax.nn.sigmoid(gate_logits).reshape(b, s, N_BRANCH, d)
        merged = (gates[:, :, 0] * (o_attn @ w_proj_attn[l])
                  + gates[:, :, 1] * (z_conv @ w_proj_conv[l])
                  + gates[:, :, 2] * (o_mem @ w_proj_mem[l]))
        x = x + merged @ w_out[l]

        h2 = _rmsnorm(x, g_ffn[l])
        x = x + _hierarchical_moe(h2, w_router_group[l], b_router_group[l], w_router_expert[l],
                                  b_router_expert[l], w_gate[l], w_up[l], w_down[l])
    return x
```

```python
import functools

import jax
import jax.numpy as jnp
from jax import lax
from jax.experimental import pallas as pl
from jax.experimental.pallas import tpu as pltpu

F32 = jnp.float32
BF16 = jnp.bfloat16

D_MODEL = 1024
EPS = 1e-6
HEAD_DIM = 64
DILATIONS = (1, 4, 16)
BAND_HALF = 64
ATTN_SLOTS = 4
GROUP_W = ATTN_SLOTS * HEAD_DIM
QKV_W = 3 * GROUP_W
ROT_DIM = 16
ROPE_THETA = 500000.0
CONV_WIDTH = 768
MEM_HEADS = 4
MEM_HEAD_DIM = 128
MEM_WIDTH = 512
N_BRANCH = 3
N_EXPERT_GROUPS = 4
EXPERTS_PER_GROUP = 8
N_EXPERTS = 32
TOP_K = 2
EXPERT_FF = 512

LANES = 128
VMEM_LIMIT_BYTES = 56 * 1024 * 1024

ROW_TILE = 512
Q_BLOCK = 128
MOE_BLOCK = 256
NEG_BIG = -1e30


def _rms(t, gain):
    return t * lax.rsqrt(jnp.mean(t * t, axis=-1, keepdims=True) + EPS) * gain


def _inproj_kernel(x_ref, g_ref, gqm_ref, w_ref,
                   qkv0_ref, qkv1_ref, qkv2_ref, u_ref, cb_ref, qm_ref, gates_ref,
                   h_scr, hp_scr, *, tm):
    x = x_ref[0]
    h = _rms(x, g_ref[...])
    n_chunks = h.shape[1] // LANES
    for c in range(n_chunks):
        h_scr[c] = h[:, c * LANES:(c + 1) * LANES]
    hb = h.astype(BF16)

    def mm(lhs, c0, c1):
        return jnp.dot(lhs, w_ref[:, c0:c1], preferred_element_type=F32)

    qkv0_ref[0, 0] = mm(hb, 0, QKV_W).astype(BF16)
    for gi, out_ref in ((1, qkv1_ref), (2, qkv2_ref)):
        d = DILATIONS[gi]
        rows = tm // d
        for r in range(d):
            for c in range(n_chunks):
                hp_scr[r * rows:(r + 1) * rows, c * LANES:(c + 1) * LANES] = (
                    h_scr[c, pl.ds(r, rows, stride=d), :].astype(BF16))
        res = mm(hp_scr[...], gi * QKV_W, (gi + 1) * QKV_W)
        out_ref[0] = res.reshape(d, rows, QKV_W).astype(BF16)

    c = 3 * QKV_W
    cx = mm(hb, c, c + CONV_WIDTH)
    cc = mm(hb, c + 2 * CONV_WIDTH, c + 3 * CONV_WIDTH)
    u_ref[0] = (cc * cx).astype(BF16)
    cb_ref[0] = mm(hb, c + CONV_WIDTH, c + 2 * CONV_WIDTH).astype(BF16)

    c += 3 * CONV_WIDTH
    qm = mm(hb, c, c + MEM_WIDTH)
    for hh in range(MEM_HEADS):
        sl = slice(hh * MEM_HEAD_DIM, (hh + 1) * MEM_HEAD_DIM)
        qm_ref[0, :, sl] = _rms(qm[:, sl], gqm_ref[...]).astype(BF16)

    c += MEM_WIDTH
    for j in range(N_BRANCH):
        gl = mm(hb, c + j * D_MODEL, c + (j + 1) * D_MODEL)
        gates_ref[0, :, j * D_MODEL:(j + 1) * D_MODEL] = jax.nn.sigmoid(gl).astype(BF16)


def _inproj(x, g_mix, g_qn_mem, w_all):
    b, s, d = x.shape
    tm = ROW_TILE
    nt = s // tm
    outs = []
    out_specs = []
    for dil in DILATIONS:
        outs.append(jax.ShapeDtypeStruct((b, dil, s // dil, QKV_W), BF16))
        out_specs.append(pl.BlockSpec((1, dil, tm // dil, QKV_W), lambda bi, i: (bi, 0, i, 0)))
    for width in (CONV_WIDTH, CONV_WIDTH, MEM_WIDTH, N_BRANCH * D_MODEL):
        outs.append(jax.ShapeDtypeStruct((b, s, width), BF16))
        out_specs.append(pl.BlockSpec((1, tm, width), lambda bi, i: (bi, i, 0)))
    return pl.pallas_call(
        functools.partial(_inproj_kernel, tm=tm),
        grid=(b, nt),
        in_specs=[
            pl.BlockSpec((1, tm, d), lambda bi, i: (bi, i, 0)),
            pl.BlockSpec((1, d), lambda bi, i: (0, 0)),
            pl.BlockSpec((1, MEM_HEAD_DIM), lambda bi, i: (0, 0)),
            pl.BlockSpec(w_all.shape, lambda bi, i: (0, 0), pipeline_mode=pl.Buffered(1)),
        ],
        out_specs=out_specs,
        out_shape=outs,
        scratch_shapes=[pltpu.VMEM((d // LANES, tm, LANES), F32), pltpu.VMEM((tm, d), BF16)],
        compiler_params=pltpu.CompilerParams(
            dimension_semantics=("parallel", "parallel"), vmem_limit_bytes=VMEM_LIMIT_BYTES),
        name="inproj",
    )(x, g_mix, g_qn_mem, w_all)


def _memkv_kernel(mem_ref, g_ref, gk_ref, w_ref, k_ref, v_ref):
    h = _rms(mem_ref[0], g_ref[...]).astype(BF16)
    kv = jnp.dot(h, w_ref[...], preferred_element_type=F32)
    for hh in range(MEM_HEADS):
        sl = slice(hh * MEM_HEAD_DIM, (hh + 1) * MEM_HEAD_DIM)
        k_ref[0, :, sl] = _rms(kv[:, sl], gk_ref[...]).astype(BF16)
    v_ref[0] = kv[:, MEM_WIDTH:].astype(BF16)


def _memkv(mem, g_mem, g_kn_mem, w_kv):
    b, m, d = mem.shape
    return pl.pallas_call(
        _memkv_kernel,
        grid=(b,),
        in_specs=[
            pl.BlockSpec((1, m, d), lambda bi: (bi, 0, 0)),
            pl.BlockSpec((1, d), lambda bi: (0, 0)),
            pl.BlockSpec((1, MEM_HEAD_DIM), lambda bi: (0, 0)),
            pl.BlockSpec(w_kv.shape, lambda bi: (0, 0)),
        ],
        out_specs=[pl.BlockSpec((1, m, MEM_WIDTH), lambda bi: (bi, 0, 0))] * 2,
        out_shape=[jax.ShapeDtypeStruct((b, m, MEM_WIDTH), BF16)] * 2,
        compiler_params=pltpu.CompilerParams(dimension_semantics=("parallel",)),
        name="memkv",
    )(mem, g_mem, g_kn_mem, w_kv)


def _attn_kernel(q0, k0, v0, q1, k1, v1, q2, k2, v2, cos_ref, sa_ref, sb_ref, gq_ref, gk_ref,
                 out_ref, qs, ks, vs, o_scr, l_scr, *, seq):
    lane = lax.broadcasted_iota(jnp.int32, (1, LANES), 1)
    low_half = lane < HEAD_DIM
    pad = BAND_HALF
    win = Q_BLOCK + 2 * pad

    def norm_rope(t, gain, cos, sa, sb):
        ss = t * t
        s_lo = jnp.sum(ss[:, :HEAD_DIM], axis=-1, keepdims=True)
        s_hi = jnp.sum(ss[:, HEAD_DIM:], axis=-1, keepdims=True)
        ms = jnp.where(low_half, s_lo, s_hi) * (1.0 / HEAD_DIM)
        tn = t * lax.rsqrt(ms + EPS) * gain
        return (tn * cos + pltpu.roll(tn, LANES - ROT_DIM // 2, 1) * sa
                + pltpu.roll(tn, ROT_DIM // 2, 1) * sb)

    col = lax.broadcasted_iota(jnp.int32, (Q_BLOCK, win), 1)
    row = lax.broadcasted_iota(jnp.int32, (Q_BLOCK, win), 0)
    band = (col >= row) & (col <= row + 2 * pad)

    for gi, (q_ref, k_ref, v_ref) in enumerate(((q0, k0, v0), (q1, k1, v1), (q2, k2, v2))):
        d = DILATIONS[gi]
        length = seq // d
        nblk = length // Q_BLOCK
        zpad = jnp.zeros((2, pad, HEAD_DIM), BF16)
        ks[:, 0:pad, :] = zpad
        ks[:, pad + length:2 * pad + length, :] = zpad
        vs[:, 0:pad, :] = zpad
        vs[:, pad + length:2 * pad + length, :] = zpad

        def sub_body(r, carry, d=d, length=length, nblk=nblk, gi=gi,
                     q_ref=q_ref, k_ref=k_ref, v_ref=v_ref):
            rows = pl.ds(r, length, stride=d) if d > 1 else pl.ds(0, length)
            cos = cos_ref[0, rows, :]
            sa = sa_ref[0, rows, :]
            sb = sb_ref[0, rows, :]
            qn = norm_rope(q_ref[0, r].astype(F32), gq_ref[...], cos, sa, sb) * (HEAD_DIM ** -0.5)
            kn = norm_rope(k_ref[0, r].astype(F32), gk_ref[...], cos, sa, sb)
            vv = v_ref[0, r]
            for hh in range(2):
                sl = slice(hh * HEAD_DIM, (hh + 1) * HEAD_DIM)
                qs[hh, 0:length, :] = qn[:, sl].astype(BF16)
                ks[hh, pad:pad + length, :] = kn[:, sl].astype(BF16)
                vs[hh, pad:pad + length, :] = vv[:, sl]

            def blk_body(n, c2):
                base = pl.multiple_of(n * Q_BLOCK, Q_BLOCK)
                kpos = col + (base - pad)
                mask = band & (kpos >= 0) & (kpos < length)
                o_parts = []
                l_parts = []
                for hh in range(2):
                    qb = qs[hh, pl.ds(base, Q_BLOCK), :]
                    kw = ks[hh, pl.ds(base, win), :]
                    vw = vs[hh, pl.ds(base, win), :]
                    s = lax.dot_general(qb, kw, (((1,), (1,)), ((), ())),
                                        preferred_element_type=F32)
                    s = jnp.where(mask, s, NEG_BIG)
                    m = jnp.max(s, axis=-1, keepdims=True)
                    p = jnp.exp(s - m)
                    den = jnp.sum(p, axis=-1, keepdims=True)
                    o = jnp.dot(p.astype(BF16), vw, preferred_element_type=F32) / den
                    o_parts.append(o)
                    l_parts.append(jnp.broadcast_to(m + jnp.log(den), (Q_BLOCK, HEAD_DIM)))
                o128 = jnp.concatenate(o_parts, axis=1)
                l128 = jnp.concatenate(l_parts, axis=1)
                if d > 1:
                    dst = pl.ds(r + d * base, Q_BLOCK, stride=d)
                else:
                    dst = pl.ds(base, Q_BLOCK)
                o_scr[gi, dst, :] = o128
                l_scr[gi, dst, :] = l128
                return c2

            lax.fori_loop(0, nblk, blk_body, 0)
            return carry

        lax.fori_loop(0, d, sub_body, 0)

    l0 = l_scr[0]
    l1 = l_scr[1]
    l2 = l_scr[2]
    m = jnp.maximum(jnp.maximum(l0, l1), l2)
    e0 = jnp.exp(l0 - m)
    e1 = jnp.exp(l1 - m)
    e2 = jnp.exp(l2 - m)
    mix = (e0 * o_scr[0] + e1 * o_scr[1] + e2 * o_scr[2]) / (e0 + e1 + e2)
    out_ref[0] = mix.astype(BF16)


def _attention(qkv, cos_t, sa_t, sb_t, gq, gk):
    b, _, s, _ = qkv[0].shape
    in_specs = []
    args = []
    for gi, dil in enumerate(DILATIONS):
        for part in range(3):
            in_specs.append(pl.BlockSpec((1, dil, s // dil, LANES),
                                         lambda bi, hp, part=part: (bi, 0, 0, 2 * part + hp)))
            args.append(qkv[gi])
    tab_spec = pl.BlockSpec((1, s, LANES), lambda bi, hp: (bi, 0, 0))
    gain_spec = pl.BlockSpec((1, LANES), lambda bi, hp: (0, 0))
    padded = s + 2 * BAND_HALF
    return pl.pallas_call(
        functools.partial(_attn_kernel, seq=s),
        grid=(b, 2),
        in_specs=in_specs + [tab_spec] * 3 + [gain_spec] * 2,
        out_specs=pl.BlockSpec((1, s, LANES), lambda bi, hp: (bi, 0, hp)),
        out_shape=jax.ShapeDtypeStruct((b, s, GROUP_W), BF16),
        scratch_shapes=[
            pltpu.VMEM((2, s, HEAD_DIM), BF16),
            pltpu.VMEM((2, padded, HEAD_DIM), BF16),
            pltpu.VMEM((2, padded, HEAD_DIM), BF16),
            pltpu.VMEM((3, s, LANES), F32),
            pltpu.VMEM((3, s, LANES), F32),
        ],
        compiler_params=pltpu.CompilerParams(
            dimension_semantics=("parallel", "parallel"), vmem_limit_bytes=VMEM_LIMIT_BYTES),
        name="dilated_attn",
    )(*args, cos_t, sa_t, sb_t, gq, gk)


def _merge_kernel(x_ref, oa_ref, u_ref, uprev_ref, unext_ref, cb_ref, qm_ref, km_ref, vm_ref,
                  gates_ref, wconv_ref, wpa_ref, wpc_ref, wpm_ref, wout_ref, gffn_ref, wr_ref, br_ref,
                  x1_ref, logit_ref, om_scr, *, tm, nt):
    i = pl.program_id(1)
    u = u_ref[0].astype(F32)
    prev_row = uprev_ref[0].astype(F32)[15:16, :] * (i > 0).astype(F32)
    next_row = unext_ref[0].astype(F32)[0:1, :] * (i < nt - 1).astype(F32)
    row = lax.broadcasted_iota(jnp.int32, (tm, 1), 0)
    um = jnp.where(row == 0, prev_row, pltpu.roll(u, 1, 0))
    up = jnp.where(row == tm - 1, next_row, pltpu.roll(u, tm - 1, 0))
    wc = wconv_ref[...]
    y = wc[0:1, :] * um + wc[1:2, :] * u + wc[2:3, :] * up
    z = (cb_ref[0].astype(F32) * y).astype(BF16)

    for hh in range(MEM_HEADS):
        sl = slice(hh * MEM_HEAD_DIM, (hh + 1) * MEM_HEAD_DIM)
        s = lax.dot_general(qm_ref[0, :, sl], km_ref[0, :, sl], (((1,), (1,)), ((), ())),
                            preferred_element_type=F32) * (MEM_HEAD_DIM ** -0.5)
        m = jnp.max(s, axis=-1, keepdims=True)
        p = jnp.exp(s - m)
        den = jnp.sum(p, axis=-1, keepdims=True)
        o = jnp.dot(p.astype(BF16), vm_ref[0, :, sl], preferred_element_type=F32) / den
        om_scr[:, sl] = o.astype(BF16)

    pa = jnp.dot(oa_ref[0], wpa_ref[...], preferred_element_type=F32)
    pc = jnp.dot(z, wpc_ref[...], preferred_element_type=F32)
    pm = jnp.dot(om_scr[...], wpm_ref[...], preferred_element_type=F32)
    merged = (gates_ref[0, :, 0:D_MODEL].astype(F32) * pa
              + gates_ref[0, :, D_MODEL:2 * D_MODEL].astype(F32) * pc
              + gates_ref[0, :, 2 * D_MODEL:3 * D_MODEL].astype(F32) * pm)
    x1 = x_ref[0] + jnp.dot(merged.astype(BF16), wout_ref[...], preferred_element_type=F32)
    x1_ref[0] = x1

    h2 = _rms(x1, gffn_ref[...])
    logit_ref[0] = jnp.dot(h2, wr_ref[...], preferred_element_type=F32,
                           precision=lax.Precision.HIGHEST) + br_ref[...]


def _merge(x, o_attn, u, cb, qm, km, vm, gates, w_conv, wpa, wpc, wpm, wout, g_ffn, w_r, b_r):
    b, s, d = x.shape
    tm = ROW_TILE
    nt = s // tm
    halo = 16
    hb = tm // halo

    def tile(width):
        return pl.BlockSpec((1, tm, width), lambda bi, i: (bi, i, 0))

    def whole(arr):
        return pl.BlockSpec(arr.shape, lambda bi, i: (0,) * arr.ndim)

    def per_batch(arr):
        return pl.BlockSpec((1,) + arr.shape[1:], lambda bi, i: (bi, 0, 0))

    in_specs = [
        tile(d), tile(GROUP_W), tile(CONV_WIDTH),
        pl.BlockSpec((1, halo, CONV_WIDTH), lambda bi, i: (bi, jnp.maximum(i * hb - 1, 0), 0)),
        pl.BlockSpec((1, halo, CONV_WIDTH),
                     lambda bi, i: (bi, jnp.minimum((i + 1) * hb, s // halo - 1), 0)),
        tile(CONV_WIDTH), tile(MEM_WIDTH), per_batch(km), per_batch(vm), tile(N_BRANCH * d),
        whole(w_conv), whole(wpa), whole(wpc), whole(wpm), whole(wout), whole(g_ffn),
        whole(w_r), whole(b_r),
    ]
    return pl.pallas_call(
        functools.partial(_merge_kernel, tm=tm, nt=nt),
        grid=(b, nt),
        in_specs=in_specs,
        out_specs=[tile(d), tile(LANES)],
        out_shape=[jax.ShapeDtypeStruct((b, s, d), F32), jax.ShapeDtypeStruct((b, s, LANES), F32)],
        scratch_shapes=[pltpu.VMEM((tm, MEM_WIDTH), BF16)],
        compiler_params=pltpu.CompilerParams(
            dimension_semantics=("parallel", "parallel"), vmem_limit_bytes=VMEM_LIMIT_BYTES),
        name="merge",
    )(x, o_attn, u, u, u, cb, qm, km, vm, gates, w_conv, wpa, wpc, wpm, wout, g_ffn, w_r, b_r)


def _expert_kernel(blk_exp_ref, nvalid_ref, tok_ref, dst_ref, wt_ref, x1_hbm, gffn_ref,
                   wg_ref, wu_ref, wd_ref, y_hbm, xbuf, ybuf, sems):
    i = pl.program_id(0)
    nvalid = nvalid_ref[i]

    @pl.when(nvalid > 0)
    def _():
        def start_gather(j, c):
            pltpu.make_async_copy(x1_hbm.at[pl.ds(tok_ref[0, 0, j], 1), :],
                                  xbuf.at[pl.ds(j, 1), :], sems.at[0]).start()
            return c

        lax.fori_loop(0, MOE_BLOCK, start_gather, 0)
        pltpu.make_async_copy(x1_hbm.at[pl.ds(0, MOE_BLOCK), :], xbuf, sems.at[0]).wait()

        h2 = _rms(xbuf[...], gffn_ref[...]).astype(BF16)
        a = jnp.dot(h2, wg_ref[0], preferred_element_type=F32)
        up = jnp.dot(h2, wu_ref[0], preferred_element_type=F32)
        act = (a * jax.nn.sigmoid(a) * up).astype(BF16)
        y = jnp.dot(act, wd_ref[0], preferred_element_type=F32)
        ybuf[...] = y * wt_ref[0]

        def start_scatter(j, c):
            pltpu.make_async_copy(ybuf.at[pl.ds(j, 1), :],
                                  y_hbm.at[pl.ds(dst_ref[0, 0, j], 1), :], sems.at[1]).start()
            return c

        lax.fori_loop(0, nvalid, start_scatter, 0)

        def wait_scatter(j, c):
            pltpu.make_async_copy(ybuf.at[pl.ds(0, 1), :], y_hbm.at[pl.ds(0, 1), :],
                                  sems.at[1]).wait()
            return c

        lax.fori_loop(0, nvalid, wait_scatter, 0)


def _experts(x1, g_ffn, blk_exp, nvalid, row_tok, row_dst, row_w, wg, wu, wd):
    n, d = x1.shape
    n_blk = blk_exp.shape[0]
    grid_spec = pltpu.PrefetchScalarGridSpec(
        num_scalar_prefetch=2,
        grid=(n_blk,),
        in_specs=[
            pl.BlockSpec((1, 1, MOE_BLOCK), lambda i, be, nv: (i, 0, 0), memory_space=pltpu.SMEM),
            pl.BlockSpec((1, 1, MOE_BLOCK), lambda i, be, nv: (i, 0, 0), memory_space=pltpu.SMEM),
            pl.BlockSpec((1, MOE_BLOCK, 1), lambda i, be, nv: (i, 0, 0)),
            pl.BlockSpec(memory_space=pl.ANY),
            pl.BlockSpec((1, d), lambda i, be, nv: (0, 0)),
            pl.BlockSpec((1, d, EXPERT_FF), lambda i, be, nv: (be[i], 0, 0)),
            pl.BlockSpec((1, d, EXPERT_FF), lambda i, be, nv: (be[i], 0, 0)),
            pl.BlockSpec((1, EXPERT_FF, d), lambda i, be, nv: (be[i], 0, 0)),
        ],
        out_specs=pl.BlockSpec(memory_space=pl.ANY),
        scratch_shapes=[
            pltpu.VMEM((MOE_BLOCK, d), F32),
            pltpu.VMEM((MOE_BLOCK, d), F32),
            pltpu.SemaphoreType.DMA((2,)),
        ],
    )
    return pl.pallas_call(
        _expert_kernel,
        grid_spec=grid_spec,
        out_shape=jax.ShapeDtypeStruct((TOP_K * n, d), F32),
        compiler_params=pltpu.CompilerParams(
            dimension_semantics=("arbitrary",), vmem_limit_bytes=VMEM_LIMIT_BYTES),
        name="experts",
    )(blk_exp, nvalid, row_tok, row_dst, row_w, x1, g_ffn, wg, wu, wd)


def _combine_kernel(x1_ref, y_ref, o_ref):
    o_ref[...] = x1_ref[...] + y_ref[0] + y_ref[1]


def _combine(x1, y):
    n, d = x1.shape
    tm = ROW_TILE
    return pl.pallas_call(
        _combine_kernel,
        grid=(n // tm,),
        in_specs=[pl.BlockSpec((tm, d), lambda i: (i, 0)),
                  pl.BlockSpec((TOP_K, tm, d), lambda i: (0, i, 0))],
        out_specs=pl.BlockSpec((tm, d), lambda i: (i, 0)),
        out_shape=jax.ShapeDtypeStruct((n, d), F32),
        compiler_params=pltpu.CompilerParams(dimension_semantics=("parallel",)),
        name="combine",
    )(x1, y.reshape(TOP_K, n, d))


def _rope_tables(positions):
    half = ROT_DIM // 2
    inv = ROPE_THETA ** (-jnp.arange(0, ROT_DIM, 2, dtype=F32) / ROT_DIM)
    ang = positions.astype(F32)[..., None] * inv
    cos, sin = jnp.cos(ang), jnp.sin(ang)
    zeros = jnp.zeros_like(cos)
    rest = HEAD_DIM - ROT_DIM
    shape = cos.shape[:-1] + (rest,)
    cos_h = jnp.concatenate([cos, cos, jnp.ones(shape, F32)], axis=-1)
    sa_h = jnp.concatenate([-sin, zeros, jnp.zeros(shape, F32)], axis=-1)
    sb_h = jnp.concatenate([zeros, sin, jnp.zeros(shape, F32)], axis=-1)
    two = lambda t: jnp.concatenate([t, t], axis=-1)
    return two(cos_h), two(sa_h), two(sb_h)


def _route(logits, n):
    pg = jax.nn.softmax(logits[:, :N_EXPERT_GROUPS], axis=-1)
    g_idx = jnp.argmax(pg, axis=-1).astype(jnp.int32)
    pg_top = jnp.max(pg, axis=-1)
    el = logits[:, N_EXPERT_GROUPS:N_EXPERT_GROUPS + N_EXPERTS].reshape(
        n, N_EXPERT_GROUPS, EXPERTS_PER_GROUP)
    sel = jnp.take_along_axis(el, g_idx[:, None, None], axis=1)[:, 0]
    pe_top, e_idx = lax.top_k(jax.nn.softmax(sel, axis=-1), TOP_K)
    weights = pg_top[:, None] * pe_top / jnp.sum(pe_top, axis=-1, keepdims=True)
    ids = g_idx[:, None] * EXPERTS_PER_GROUP + e_idx.astype(jnp.int32)

    nk = n * TOP_K
    e = ids.reshape(nk)
    w = weights.reshape(nk)
    a = jnp.arange(nk, dtype=jnp.int32)
    _, a_s = lax.sort((e, a), num_keys=1, is_stable=True)
    counts = jnp.sum((e[:, None] == jnp.arange(N_EXPERTS, dtype=jnp.int32)[None, :]).astype(jnp.int32),
                     axis=0)
    padded = ((counts + MOE_BLOCK - 1) // MOE_BLOCK) * MOE_BLOCK
    pend = jnp.cumsum(padded)
    pstart = pend - padded
    start = jnp.cumsum(counts) - counts
    n_blk = nk // MOE_BLOCK + N_EXPERTS
    blk_start = jnp.arange(n_blk, dtype=jnp.int32) * MOE_BLOCK
    blk_exp = jnp.minimum(jnp.searchsorted(pend, blk_start, side='right'),
                          N_EXPERTS - 1).astype(jnp.int32)
    nvalid = jnp.clip(counts[blk_exp] - (blk_start - pstart[blk_exp]), 0, MOE_BLOCK)
    nvalid = jnp.where(blk_start < pend[-1], nvalid, 0).astype(jnp.int32)
    j = jnp.arange(MOE_BLOCK, dtype=jnp.int32)[None, :]
    valid = j < nvalid[:, None]
    src = jnp.clip((start[blk_exp] + blk_start - pstart[blk_exp])[:, None] + j, 0, nk - 1)
    a_p = a_s[src]
    tok = jnp.where(valid, a_p >> 1, 0).astype(jnp.int32)
    dst = jnp.where(valid, (a_p & 1) * n + (a_p >> 1), 0).astype(jnp.int32)
    w_p = jnp.where(valid, w[a_p], 0.0).astype(F32)
    return (blk_exp, nvalid, tok.reshape(n_blk, 1, MOE_BLOCK), dst.reshape(n_blk, 1, MOE_BLOCK),
            w_p.reshape(n_blk, MOE_BLOCK, 1))


def kernel(x, mem, positions, g_mix, g_mem, w_in, g_qn_attn, g_kn_attn, w_conv, w_mem_kv, g_qn_mem,
           g_kn_mem, w_proj_attn, w_proj_conv, w_proj_mem, w_out, g_ffn, w_router_group,
           b_router_group, w_router_expert, b_router_expert, w_gate, w_up, w_down):
    b, s, d = x.shape
    n = b * s
    cos_t, sa_t, sb_t = _rope_tables(positions)
    for l in range(w_in.shape[0]):
        wl = w_in[l]
        aw = N_BRANCH * GROUP_W
        cols = []
        for gi in range(len(DILATIONS)):
            for part in range(3):
                c0 = part * aw + gi * GROUP_W
                cols.append(wl[:, c0:c0 + GROUP_W])
        cols.append(wl[:, 3 * aw:])
        w_all = jnp.concatenate(cols, axis=1).astype(BF16)

        q0, q1, q2, u, cb, qm, gates = _inproj(x, g_mix[l][None], g_qn_mem[l][None], w_all)
        km, vm = _memkv(mem, g_mem[l][None], g_kn_mem[l][None], w_mem_kv[l].astype(BF16))
        gq = jnp.tile(g_qn_attn[l], 2)[None]
        gk = jnp.tile(g_kn_attn[l], 2)[None]
        o_attn = _attention((q0, q1, q2), cos_t, sa_t, sb_t, gq, gk)

        w_r = jnp.zeros((d, LANES), F32)
        w_r = w_r.at[:, :N_EXPERT_GROUPS].set(w_router_group[l])
        w_r = w_r.at[:, N_EXPERT_GROUPS:N_EXPERT_GROUPS + N_EXPERTS].set(w_router_expert[l])
        b_r = jnp.zeros((1, LANES), F32)
        b_r = b_r.at[0, :N_EXPERT_GROUPS].set(b_router_group[l])
        b_r = b_r.at[0, N_EXPERT_GROUPS:N_EXPERT_GROUPS + N_EXPERTS].set(b_router_expert[l])
        x1, logits = _merge(x, o_attn, u, cb, qm, km, vm, gates, w_conv[l],
                            w_proj_attn[l].astype(BF16), w_proj_conv[l].astype(BF16),
                            w_proj_mem[l].astype(BF16), w_out[l].astype(BF16), g_ffn[l][None],
                            w_r, b_r)

        blk_exp, nvalid, row_tok, row_dst, row_w = _route(logits.reshape(n, LANES), n)
        x1f = x1.reshape(n, d)
        y = _experts(x1f, g_ffn[l][None], blk_exp, nvalid, row_tok, row_dst, row_w,
                     w_gate[l].astype(BF16), w_up[l].astype(BF16), w_down[l].astype(BF16))
        x = _combine(x1f, y).reshape(b, s, d)
    return x
```

```python
import functools

import numpy as np
import jax
import jax.numpy as jnp
from jax import lax
from jax.experimental import pallas as pl
from jax.experimental.pallas import tpu as pltpu

F32 = jnp.float32
BF16 = jnp.bfloat16

D_MODEL = 1024
EPS = 1e-6
HEAD_DIM = 64
DILATIONS = (1, 4, 16)
BAND_HALF = 64
ATTN_SLOTS = 4
GROUP_W = ATTN_SLOTS * HEAD_DIM
QKV_W = 3 * GROUP_W
ROT_DIM = 16
ROPE_THETA = 500000.0
CONV_WIDTH = 768
MEM_HEADS = 4
MEM_HEAD_DIM = 128
MEM_WIDTH = 512
N_BRANCH = 3
N_EXPERT_GROUPS = 4
EXPERTS_PER_GROUP = 8
N_EXPERTS = 32
TOP_K = 2
EXPERT_FF = 512

LANES = 128
VMEM_LIMIT_BYTES = 56 * 1024 * 1024

ROW_TILE = 512
Q_BLOCK = 128
MOE_BLOCK = 256
ROW_CHUNKS = D_MODEL // LANES
GATHER_UNROLL = 8
NEG_BIG = -1e30


def _rms(t, gain):
    return t * lax.rsqrt(jnp.mean(t * t, axis=-1, keepdims=True) + EPS) * gain


def _inproj_kernel(x_ref, g_ref, gqm_ref, w_ref,
                   qkv0_ref, qkv1_ref, qkv2_ref, u_ref, cb_ref, qm_ref, gates_ref,
                   h_scr, hp_scr, *, tm):
    x = x_ref[0]
    h = _rms(x, g_ref[...])
    n_chunks = h.shape[1] // LANES
    for c in range(n_chunks):
        h_scr[c] = h[:, c * LANES:(c + 1) * LANES]
    hb = h.astype(BF16)

    def mm(lhs, c0, c1):
        return jnp.dot(lhs, w_ref[:, c0:c1], preferred_element_type=F32)

    qkv0_ref[0, 0] = mm(hb, 0, QKV_W).astype(BF16)
    for gi, out_ref in ((1, qkv1_ref), (2, qkv2_ref)):
        d = DILATIONS[gi]
        rows = tm // d
        for r in range(d):
            for c in range(n_chunks):
                hp_scr[r * rows:(r + 1) * rows, c * LANES:(c + 1) * LANES] = (
                    h_scr[c, pl.ds(r, rows, stride=d), :].astype(BF16))
        res = mm(hp_scr[...], gi * QKV_W, (gi + 1) * QKV_W)
        out_ref[0] = res.reshape(d, rows, QKV_W).astype(BF16)

    c = 3 * QKV_W
    cx = mm(hb, c, c + CONV_WIDTH)
    cc = mm(hb, c + 2 * CONV_WIDTH, c + 3 * CONV_WIDTH)
    u_ref[0] = (cc * cx).astype(BF16)
    cb_ref[0] = mm(hb, c + CONV_WIDTH, c + 2 * CONV_WIDTH).astype(BF16)

    c += 3 * CONV_WIDTH
    qm = mm(hb, c, c + MEM_WIDTH)
    for hh in range(MEM_HEADS):
        sl = slice(hh * MEM_HEAD_DIM, (hh + 1) * MEM_HEAD_DIM)
        qm_ref[0, :, sl] = _rms(qm[:, sl], gqm_ref[...]).astype(BF16)

    c += MEM_WIDTH
    for j in range(N_BRANCH):
        gl = mm(hb, c + j * D_MODEL, c + (j + 1) * D_MODEL)
        gates_ref[0, :, j * D_MODEL:(j + 1) * D_MODEL] = jax.nn.sigmoid(gl).astype(BF16)


def _inproj(x, g_mix, g_qn_mem, w_all):
    b, s, d = x.shape
    tm = ROW_TILE
    nt = s // tm
    outs = []
    out_specs = []
    for dil in DILATIONS:
        outs.append(jax.ShapeDtypeStruct((b, dil, s // dil, QKV_W), BF16))
        out_specs.append(pl.BlockSpec((1, dil, tm // dil, QKV_W), lambda bi, i: (bi, 0, i, 0)))
    for width in (CONV_WIDTH, CONV_WIDTH, MEM_WIDTH, N_BRANCH * D_MODEL):
        outs.append(jax.ShapeDtypeStruct((b, s, width), BF16))
        out_specs.append(pl.BlockSpec((1, tm, width), lambda bi, i: (bi, i, 0)))
    return pl.pallas_call(
        functools.partial(_inproj_kernel, tm=tm),
        grid=(b, nt),
        in_specs=[
            pl.BlockSpec((1, tm, d), lambda bi, i: (bi, i, 0)),
            pl.BlockSpec((1, d), lambda bi, i: (0, 0)),
            pl.BlockSpec((1, MEM_HEAD_DIM), lambda bi, i: (0, 0)),
            pl.BlockSpec(w_all.shape, lambda bi, i: (0, 0), pipeline_mode=pl.Buffered(1)),
        ],
        out_specs=out_specs,
        out_shape=outs,
        scratch_shapes=[pltpu.VMEM((d // LANES, tm, LANES), F32), pltpu.VMEM((tm, d), BF16)],
        compiler_params=pltpu.CompilerParams(
            dimension_semantics=("parallel", "parallel"), vmem_limit_bytes=VMEM_LIMIT_BYTES),
        name="inproj",
    )(x, g_mix, g_qn_mem, w_all)


def _memkv_kernel(mem_ref, g_ref, gk_ref, w_ref, k_ref, v_ref):
    h = _rms(mem_ref[0], g_ref[...]).astype(BF16)
    kv = jnp.dot(h, w_ref[...], preferred_element_type=F32)
    for hh in range(MEM_HEADS):
        sl = slice(hh * MEM_HEAD_DIM, (hh + 1) * MEM_HEAD_DIM)
        k_ref[0, :, sl] = _rms(kv[:, sl], gk_ref[...]).astype(BF16)
    v_ref[0] = kv[:, MEM_WIDTH:].astype(BF16)


def _memkv(mem, g_mem, g_kn_mem, w_kv):
    b, m, d = mem.shape
    return pl.pallas_call(
        _memkv_kernel,
        grid=(b,),
        in_specs=[
            pl.BlockSpec((1, m, d), lambda bi: (bi, 0, 0)),
            pl.BlockSpec((1, d), lambda bi: (0, 0)),
            pl.BlockSpec((1, MEM_HEAD_DIM), lambda bi: (0, 0)),
            pl.BlockSpec(w_kv.shape, lambda bi: (0, 0)),
        ],
        out_specs=[pl.BlockSpec((1, m, MEM_WIDTH), lambda bi: (bi, 0, 0))] * 2,
        out_shape=[jax.ShapeDtypeStruct((b, m, MEM_WIDTH), BF16)] * 2,
        compiler_params=pltpu.CompilerParams(dimension_semantics=("parallel",)),
        name="memkv",
    )(mem, g_mem, g_kn_mem, w_kv)


ATTN_WIN = Q_BLOCK + 2 * BAND_HALF
ATTN_UNROLL = 4


def _attn_consts():
    lane = np.arange(LANES)
    gsum = (lane[:, None] // HEAD_DIM == lane[None, :] // HEAD_DIM).astype(np.float32)
    half = ROT_DIM // 2
    dim = lane % HEAD_DIM
    src = np.where(dim < half, lane + half, np.where(dim < ROT_DIM, lane - half, -1))
    pswap = (lane[:, None] == src[None, :]).astype(np.float32)
    i = np.arange(Q_BLOCK)[:, None]
    c = np.arange(ATTN_WIN)[None, :]
    band = (c >= i) & (c <= i + 2 * BAND_HALF)
    first = c >= BAND_HALF
    last = c < Q_BLOCK + BAND_HALF
    variants = [band, band & first, band & last, band & first & last]
    bias = np.stack([np.where(v, 0.0, NEG_BIG) for v in variants]).astype(np.float32)
    ones = np.stack([np.broadcast_to(lane < HEAD_DIM, (ATTN_WIN, LANES)),
                     np.broadcast_to(lane >= HEAD_DIM, (ATTN_WIN, LANES))]).astype(np.float32)
    return (jnp.asarray(gsum, BF16), jnp.asarray(pswap, BF16), jnp.asarray(bias, F32),
            jnp.asarray(ones, BF16))


def _attn_kernel(q0, k0, v0, q1, k1, v1, q2, k2, v2, cos_ref, sin_ref, gq_ref, gk_ref,
                 gsum_ref, pswap_ref, bias_ref, ones_ref, out_ref, qs, ks, vs, o_scr, l_scr, *, seq):
    lane = lax.broadcasted_iota(jnp.int32, (1, LANES), 1)
    low_half = lane < HEAD_DIM
    pad = BAND_HALF
    n_blocks = seq // Q_BLOCK

    def split_dot(t, w_ref):
        hi = t.astype(BF16)
        lo = (t - hi.astype(F32)).astype(BF16)
        w = w_ref[...]
        return (jnp.dot(hi, w, preferred_element_type=F32)
                + jnp.dot(lo, w, preferred_element_type=F32))

    def norm_rope(t, gain, cos, sin):
        ms = split_dot(t * t, gsum_ref) * (1.0 / HEAD_DIM)
        tn = t * lax.rsqrt(ms + EPS) * gain
        return tn * cos + split_dot(tn, pswap_ref) * sin

    for gi, (q_ref, k_ref, v_ref) in enumerate(((q0, k0, v0), (q1, k1, v1), (q2, k2, v2))):
        d = DILATIONS[gi]
        length = seq // d
        nblk = length // Q_BLOCK
        region = length + pad

        zpad = jnp.zeros((pad, LANES), BF16)
        for r in range(d + 1):
            ks[r * region:r * region + pad, :] = zpad
            vs[0, r * region:r * region + pad, :] = zpad
            vs[1, r * region:r * region + pad, :] = zpad

        def split_index(b, nblk=nblk):
            if nblk == 1:
                return b, 0
            r = b // nblk
            return r, b - r * nblk

        def table_rows(r, n, d=d):
            if d == 1:
                return pl.ds(pl.multiple_of(n * Q_BLOCK, Q_BLOCK), Q_BLOCK)
            return pl.ds(r + d * n * Q_BLOCK, Q_BLOCK, stride=d)

        def prep(b, q_ref=q_ref, k_ref=k_ref, v_ref=v_ref, split_index=split_index,
                 table_rows=table_rows):
            r, n = split_index(b)
            src = pl.ds(pl.multiple_of(n * Q_BLOCK, Q_BLOCK), Q_BLOCK)
            rows = table_rows(r, n)
            cos = cos_ref[0, rows, :]
            sin = sin_ref[0, rows, :]
            qn = norm_rope(q_ref[0, r, src, :].astype(F32), gq_ref[...], cos, sin) * (HEAD_DIM ** -0.5)
            kn = norm_rope(k_ref[0, r, src, :].astype(F32), gk_ref[...], cos, sin)
            vv = v_ref[0, r, src, :]
            qdst = pl.ds(pl.multiple_of(b * Q_BLOCK, Q_BLOCK), Q_BLOCK)
            kdst = pl.ds(pl.multiple_of(b * Q_BLOCK + (r + 1) * pad, pad), Q_BLOCK)
            qs[0, qdst, :] = jnp.where(low_half, qn, 0.0).astype(BF16)
            qs[1, qdst, :] = jnp.where(low_half, 0.0, qn).astype(BF16)
            ks[kdst, :] = kn.astype(BF16)
            vs[0, kdst, :] = jnp.where(low_half, vv, jnp.zeros_like(vv))
            vs[1, kdst, :] = jnp.where(low_half, jnp.zeros_like(vv), vv)

        def block(b, gi=gi, nblk=nblk, split_index=split_index, table_rows=table_rows):
            r, n = split_index(b)
            if nblk == 1:
                bias = bias_ref[3]
            else:
                variant = jnp.where(n == 0, 1, 0) + jnp.where(n == nblk - 1, 2, 0)
                bias = bias_ref[variant]
            qsrc = pl.ds(pl.multiple_of(b * Q_BLOCK, Q_BLOCK), Q_BLOCK)
            ksrc = pl.ds(pl.multiple_of(b * Q_BLOCK + r * pad, pad), ATTN_WIN)
            kw = ks[ksrc, :]
            acc_o = None
            acc_d = None
            ms = []
            for hh in range(2):
                s = lax.dot_general(qs[hh, qsrc, :], kw, (((1,), (1,)), ((), ())),
                                    preferred_element_type=F32) + bias
                m = jnp.max(s, axis=-1, keepdims=True)
                p = jnp.exp(s - m).astype(BF16)
                po = jnp.dot(p, vs[hh, ksrc, :], preferred_element_type=F32)
                pd = jnp.dot(p, ones_ref[hh], preferred_element_type=F32)
                acc_o = po if acc_o is None else acc_o + po
                acc_d = pd if acc_d is None else acc_d + pd
                ms.append(m)
            m128 = jnp.where(low_half, ms[0], ms[1])
            dst = table_rows(r, n)
            o_scr[gi, dst, :] = acc_o / acc_d
            l_scr[gi, dst, :] = m128 + jnp.log(acc_d)

        def unrolled(fn):
            def body(it, carry):
                for u in range(ATTN_UNROLL):
                    fn(it * ATTN_UNROLL + u)
                return carry
            lax.fori_loop(0, n_blocks // ATTN_UNROLL, body, 0)

        unrolled(prep)
        unrolled(block)

    l0 = l_scr[0]
    l1 = l_scr[1]
    l2 = l_scr[2]
    m = jnp.maximum(jnp.maximum(l0, l1), l2)
    e0 = jnp.exp(l0 - m)
    e1 = jnp.exp(l1 - m)
    e2 = jnp.exp(l2 - m)
    mix = (e0 * o_scr[0] + e1 * o_scr[1] + e2 * o_scr[2]) / (e0 + e1 + e2)
    out_ref[0] = mix.astype(BF16)


def _attention(qkv, cos_t, sin_t, gq, gk):
    b, _, s, _ = qkv[0].shape
    in_specs = []
    args = []
    for gi, dil in enumerate(DILATIONS):
        for part in range(3):
            in_specs.append(pl.BlockSpec((1, dil, s // dil, LANES),
                                         lambda bi, hp, part=part: (bi, 0, 0, 2 * part + hp)))
            args.append(qkv[gi])
    consts = _attn_consts()
    tab_spec = pl.BlockSpec((1, s, LANES), lambda bi, hp: (bi, 0, 0))
    gain_spec = pl.BlockSpec((1, LANES), lambda bi, hp: (0, 0))
    const_specs = [pl.BlockSpec(c.shape, lambda bi, hp, nd=c.ndim: (0,) * nd) for c in consts]
    key_rows = max(d * (s // d + BAND_HALF) for d in DILATIONS) + BAND_HALF
    return pl.pallas_call(
        functools.partial(_attn_kernel, seq=s),
        grid=(b, 2),
        in_specs=in_specs + [tab_spec] * 2 + [gain_spec] * 2 + const_specs,
        out_specs=pl.BlockSpec((1, s, LANES), lambda bi, hp: (bi, 0, hp)),
        out_shape=jax.ShapeDtypeStruct((b, s, GROUP_W), BF16),
        scratch_shapes=[
            pltpu.VMEM((2, s, LANES), BF16),
            pltpu.VMEM((key_rows, LANES), BF16),
            pltpu.VMEM((2, key_rows, LANES), BF16),
            pltpu.VMEM((3, s, LANES), F32),
            pltpu.VMEM((3, s, LANES), F32),
        ],
        compiler_params=pltpu.CompilerParams(
            dimension_semantics=("parallel", "parallel"), vmem_limit_bytes=VMEM_LIMIT_BYTES),
        name="dilated_attn",
    )(*args, cos_t, sin_t, gq, gk, *consts)


def _merge_kernel(x_ref, oa_ref, u_ref, uprev_ref, unext_ref, cb_ref, qm_ref, km_ref, vm_ref,
                  gates_ref, wconv_ref, wpa_ref, wpc_ref, wpm_ref, wout_ref, gffn_ref, wr_ref, br_ref,
                  x1_ref, logit_ref, om_scr, *, tm, nt):
    i = pl.program_id(1)
    u = u_ref[0].astype(F32)
    prev_row = uprev_ref[0].astype(F32)[15:16, :] * (i > 0).astype(F32)
    next_row = unext_ref[0].astype(F32)[0:1, :] * (i < nt - 1).astype(F32)
    row = lax.broadcasted_iota(jnp.int32, (tm, 1), 0)
    um = jnp.where(row == 0, prev_row, pltpu.roll(u, 1, 0))
    up = jnp.where(row == tm - 1, next_row, pltpu.roll(u, tm - 1, 0))
    wc = wconv_ref[...]
    y = wc[0:1, :] * um + wc[1:2, :] * u + wc[2:3, :] * up
    z = (cb_ref[0].astype(F32) * y).astype(BF16)

    for hh in range(MEM_HEADS):
        sl = slice(hh * MEM_HEAD_DIM, (hh + 1) * MEM_HEAD_DIM)
        s = lax.dot_general(qm_ref[0, :, sl], km_ref[0, :, sl], (((1,), (1,)), ((), ())),
                            preferred_element_type=F32) * (MEM_HEAD_DIM ** -0.5)
        m = jnp.max(s, axis=-1, keepdims=True)
        p = jnp.exp(s - m)
        den = jnp.sum(p, axis=-1, keepdims=True)
        o = jnp.dot(p.astype(BF16), vm_ref[0, :, sl], preferred_element_type=F32) / den
        om_scr[:, sl] = o.astype(BF16)

    pa = jnp.dot(oa_ref[0], wpa_ref[...], preferred_element_type=F32)
    pc = jnp.dot(z, wpc_ref[...], preferred_element_type=F32)
    pm = jnp.dot(om_scr[...], wpm_ref[...], preferred_element_type=F32)
    merged = (gates_ref[0, :, 0:D_MODEL].astype(F32) * pa
              + gates_ref[0, :, D_MODEL:2 * D_MODEL].astype(F32) * pc
              + gates_ref[0, :, 2 * D_MODEL:3 * D_MODEL].astype(F32) * pm)
    x1 = x_ref[0] + jnp.dot(merged.astype(BF16), wout_ref[...], preferred_element_type=F32)
    for c in range(ROW_CHUNKS):
        x1_ref[pl.ds(c, tm, stride=ROW_CHUNKS), :] = x1[:, c * LANES:(c + 1) * LANES]

    h2 = _rms(x1, gffn_ref[...])
    logit_ref[0] = jnp.dot(h2, wr_ref[...], preferred_element_type=F32,
                           precision=lax.Precision.HIGHEST) + br_ref[...]


def _merge(x, o_attn, u, cb, qm, km, vm, gates, w_conv, wpa, wpc, wpm, wout, g_ffn, w_r, b_r):
    b, s, d = x.shape
    tm = ROW_TILE
    nt = s // tm
    halo = 16
    hb = tm // halo

    def tile(width):
        return pl.BlockSpec((1, tm, width), lambda bi, i: (bi, i, 0))

    def whole(arr):
        return pl.BlockSpec(arr.shape, lambda bi, i: (0,) * arr.ndim)

    def per_batch(arr):
        return pl.BlockSpec((1,) + arr.shape[1:], lambda bi, i: (bi, 0, 0))

    in_specs = [
        tile(d), tile(GROUP_W), tile(CONV_WIDTH),
        pl.BlockSpec((1, halo, CONV_WIDTH), lambda bi, i: (bi, jnp.maximum(i * hb - 1, 0), 0)),
        pl.BlockSpec((1, halo, CONV_WIDTH),
                     lambda bi, i: (bi, jnp.minimum((i + 1) * hb, s // halo - 1), 0)),
        tile(CONV_WIDTH), tile(MEM_WIDTH), per_batch(km), per_batch(vm), tile(N_BRANCH * d),
        whole(w_conv), whole(wpa), whole(wpc), whole(wpm), whole(wout), whole(g_ffn),
        whole(w_r), whole(b_r),
    ]
    return pl.pallas_call(
        functools.partial(_merge_kernel, tm=tm, nt=nt),
        grid=(b, nt),
        in_specs=in_specs,
        out_specs=[pl.BlockSpec((tm * ROW_CHUNKS, LANES), lambda bi, i: (bi * nt + i, 0)), tile(LANES)],
        out_shape=[jax.ShapeDtypeStruct((b * s * ROW_CHUNKS, LANES), F32),
                   jax.ShapeDtypeStruct((b, s, LANES), F32)],
        scratch_shapes=[pltpu.VMEM((tm, MEM_WIDTH), BF16)],
        compiler_params=pltpu.CompilerParams(
            dimension_semantics=("parallel", "parallel"), vmem_limit_bytes=VMEM_LIMIT_BYTES),
        name="merge",
    )(x, o_attn, u, u, u, cb, qm, km, vm, gates, w_conv, wpa, wpc, wpm, wout, g_ffn, w_r, b_r)


def _expert_kernel(blk_exp_ref, nvalid_ref, tok_ref, tok_next_ref, dst_ref, wt_ref, x1_hbm, gffn_ref,
                   wg_ref, wu_ref, wd_ref, y_hbm, xbuf, hbuf, ybuf, gsem, ssem, *, n_blk):
    i = pl.program_id(0)
    slot = lax.rem(i, 2)
    rows = MOE_BLOCK * ROW_CHUNKS

    def gather_copy(src_row, j, s):
        return pltpu.make_async_copy(x1_hbm.at[pl.ds(src_row, ROW_CHUNKS), :],
                                     xbuf.at[s, pl.ds(j * ROW_CHUNKS, ROW_CHUNKS), :], gsem.at[s])

    def start_gather(idx_ref, s):
        def body(it, c):
            for u in range(GATHER_UNROLL):
                j = it * GATHER_UNROLL + u
                gather_copy(pl.multiple_of(idx_ref[0, 0, j], ROW_CHUNKS), j, s).start()
            return c
        lax.fori_loop(0, MOE_BLOCK // GATHER_UNROLL, body, 0)

    def scatter_copy(j, dst_row, s):
        return pltpu.make_async_copy(ybuf.at[s, pl.ds(j * ROW_CHUNKS, ROW_CHUNKS), :],
                                     y_hbm.at[pl.ds(dst_row, ROW_CHUNKS), :], ssem.at[s])

    def wait_scatter(count, s):
        def body(j, c):
            scatter_copy(0, 0, s).wait()
            return c
        lax.fori_loop(0, count, body, 0)

    nvalid = nvalid_ref[i]

    @pl.when(jnp.logical_and(i == 0, nvalid > 0))
    def _():
        start_gather(tok_ref, 0)

    @pl.when(nvalid > 0)
    def _():
        pltpu.make_async_copy(x1_hbm.at[pl.ds(0, rows), :], xbuf.at[slot], gsem.at[slot]).wait()

    nxt = jnp.minimum(i + 1, n_blk - 1)

    @pl.when(jnp.logical_and(i + 1 < n_blk, nvalid_ref[nxt] > 0))
    def _():
        start_gather(tok_next_ref, 1 - slot)

    @pl.when(i >= 2)
    def _():
        wait_scatter(nvalid_ref[jnp.maximum(i - 2, 0)], slot)

    @pl.when(nvalid > 0)
    def _():
        ss = None
        for c in range(ROW_CHUNKS):
            xc = xbuf[slot, pl.ds(c, MOE_BLOCK, stride=ROW_CHUNKS), :]
            ss = xc * xc if ss is None else ss + xc * xc
        scale = lax.rsqrt(jnp.sum(ss, axis=-1, keepdims=True) * (1.0 / D_MODEL) + EPS)
        for c in range(ROW_CHUNKS):
            sl = slice(c * LANES, (c + 1) * LANES)
            xc = xbuf[slot, pl.ds(c, MOE_BLOCK, stride=ROW_CHUNKS), :]
            hbuf[:, sl] = (xc * scale * gffn_ref[:, sl]).astype(BF16)
        h2 = hbuf[...]
        a = jnp.dot(h2, wg_ref[0], preferred_element_type=F32)
        up = jnp.dot(h2, wu_ref[0], preferred_element_type=F32)
        act = (a * jax.nn.sigmoid(a) * up).astype(BF16)
        y = jnp.dot(act, wd_ref[0], preferred_element_type=F32) * wt_ref[0]
        for c in range(ROW_CHUNKS):
            ybuf[slot, pl.ds(c, MOE_BLOCK, stride=ROW_CHUNKS), :] = y[:, c * LANES:(c + 1) * LANES]

        def start_scatter(j, c):
            scatter_copy(j, pl.multiple_of(dst_ref[0, 0, j], ROW_CHUNKS), slot).start()
            return c
        lax.fori_loop(0, nvalid, start_scatter, 0)

    @pl.when(i == n_blk - 1)
    def _():
        wait_scatter(nvalid_ref[jnp.maximum(i - 1, 0)], 1 - slot)
        wait_scatter(nvalid, slot)


def _experts(x1r, g_ffn, blk_exp, nvalid, row_tok, row_dst, row_w, wg, wu, wd):
    rows_total, _ = x1r.shape
    d = D_MODEL
    n_blk = blk_exp.shape[0]
    idx_spec = lambda fn: pl.BlockSpec((1, 1, MOE_BLOCK), fn, memory_space=pltpu.SMEM)
    grid_spec = pltpu.PrefetchScalarGridSpec(
        num_scalar_prefetch=2,
        grid=(n_blk,),
        in_specs=[
            idx_spec(lambda i, be, nv: (i, 0, 0)),
            idx_spec(lambda i, be, nv: (jnp.minimum(i + 1, n_blk - 1), 0, 0)),
            idx_spec(lambda i, be, nv: (i, 0, 0)),
            pl.BlockSpec((1, MOE_BLOCK, 1), lambda i, be, nv: (i, 0, 0)),
            pl.BlockSpec(memory_space=pl.ANY),
            pl.BlockSpec((1, d), lambda i, be, nv: (0, 0)),
            pl.BlockSpec((1, d, EXPERT_FF), lambda i, be, nv: (be[i], 0, 0)),
            pl.BlockSpec((1, d, EXPERT_FF), lambda i, be, nv: (be[i], 0, 0)),
            pl.BlockSpec((1, EXPERT_FF, d), lambda i, be, nv: (be[i], 0, 0)),
        ],
        out_specs=pl.BlockSpec(memory_space=pl.ANY),
        scratch_shapes=[
            pltpu.VMEM((2, MOE_BLOCK * ROW_CHUNKS, LANES), F32),
            pltpu.VMEM((MOE_BLOCK, d), BF16),
            pltpu.VMEM((2, MOE_BLOCK * ROW_CHUNKS, LANES), F32),
            pltpu.SemaphoreType.DMA((2,)),
            pltpu.SemaphoreType.DMA((2,)),
        ],
    )
    return pl.pallas_call(
        functools.partial(_expert_kernel, n_blk=n_blk),
        grid_spec=grid_spec,
        out_shape=jax.ShapeDtypeStruct((TOP_K * rows_total, LANES), F32),
        compiler_params=pltpu.CompilerParams(
            dimension_semantics=("arbitrary",), vmem_limit_bytes=VMEM_LIMIT_BYTES),
        name="experts",
    )(blk_exp, nvalid, row_tok, row_tok, row_dst, row_w, x1r, g_ffn, wg, wu, wd)


def _combine_kernel(x1_ref, y0_ref, y1_ref, o_ref, *, tm):
    for c in range(ROW_CHUNKS):
        rows = pl.ds(c, tm, stride=ROW_CHUNKS)
        o_ref[:, c * LANES:(c + 1) * LANES] = x1_ref[rows, :] + y0_ref[rows, :] + y1_ref[rows, :]


def _combine(x1r, y):
    n = x1r.shape[0] // ROW_CHUNKS
    tm = ROW_TILE
    nt = n // tm
    blk = (tm * ROW_CHUNKS, LANES)
    return pl.pallas_call(
        functools.partial(_combine_kernel, tm=tm),
        grid=(nt,),
        in_specs=[pl.BlockSpec(blk, lambda i: (i, 0)),
                  pl.BlockSpec(blk, lambda i: (i, 0)),
                  pl.BlockSpec(blk, lambda i: (i + nt, 0))],
        out_specs=pl.BlockSpec((tm, D_MODEL), lambda i: (i, 0)),
        out_shape=jax.ShapeDtypeStruct((n, D_MODEL), F32),
        compiler_params=pltpu.CompilerParams(dimension_semantics=("parallel",)),
        name="combine",
    )(x1r, y, y)


def _rope_tables(positions):
    inv = ROPE_THETA ** (-jnp.arange(0, ROT_DIM, 2, dtype=F32) / ROT_DIM)
    ang = positions.astype(F32)[..., None] * inv
    cos, sin = jnp.cos(ang), jnp.sin(ang)
    rest = cos.shape[:-1] + (HEAD_DIM - ROT_DIM,)
    cos_h = jnp.concatenate([cos, cos, jnp.ones(rest, F32)], axis=-1)
    sin_h = jnp.concatenate([-sin, sin, jnp.zeros(rest, F32)], axis=-1)
    return jnp.concatenate([cos_h, cos_h], axis=-1), jnp.concatenate([sin_h, sin_h], axis=-1)


def _route(logits, n):
    pg = jax.nn.softmax(logits[:, :N_EXPERT_GROUPS], axis=-1)
    g_idx = jnp.argmax(pg, axis=-1).astype(jnp.int32)
    pg_top = jnp.max(pg, axis=-1)
    el = logits[:, N_EXPERT_GROUPS:N_EXPERT_GROUPS + N_EXPERTS].reshape(
        n, N_EXPERT_GROUPS, EXPERTS_PER_GROUP)
    sel = jnp.take_along_axis(el, g_idx[:, None, None], axis=1)[:, 0]
    pe_top, e_idx = lax.top_k(jax.nn.softmax(sel, axis=-1), TOP_K)
    weights = pg_top[:, None] * pe_top / jnp.sum(pe_top, axis=-1, keepdims=True)
    ids = g_idx[:, None] * EXPERTS_PER_GROUP + e_idx.astype(jnp.int32)

    nk = n * TOP_K
    e = ids.reshape(nk)
    w = weights.reshape(nk)
    a = jnp.arange(nk, dtype=jnp.int32)
    _, a_s = lax.sort((e, a), num_keys=1, is_stable=True)
    counts = jnp.sum((e[:, None] == jnp.arange(N_EXPERTS, dtype=jnp.int32)[None, :]).astype(jnp.int32),
                     axis=0)
    padded = ((counts + MOE_BLOCK - 1) // MOE_BLOCK) * MOE_BLOCK
    pend = jnp.cumsum(padded)
    pstart = pend - padded
    start = jnp.cumsum(counts) - counts
    n_blk = nk // MOE_BLOCK + N_EXPERTS
    blk_start = jnp.arange(n_blk, dtype=jnp.int32) * MOE_BLOCK
    blk_exp = jnp.minimum(jnp.searchsorted(pend, blk_start, side='right'),
                          N_EXPERTS - 1).astype(jnp.int32)
    nvalid = jnp.clip(counts[blk_exp] - (blk_start - pstart[blk_exp]), 0, MOE_BLOCK)
    nvalid = jnp.where(blk_start < pend[-1], nvalid, 0).astype(jnp.int32)
    j = jnp.arange(MOE_BLOCK, dtype=jnp.int32)[None, :]
    valid = j < nvalid[:, None]
    src = jnp.clip((start[blk_exp] + blk_start - pstart[blk_exp])[:, None] + j, 0, nk - 1)
    a_p = a_s[src]
    tok = (jnp.where(valid, a_p >> 1, 0) * ROW_CHUNKS).astype(jnp.int32)
    dst = (jnp.where(valid, (a_p & 1) * n + (a_p >> 1), 0) * ROW_CHUNKS).astype(jnp.int32)
    w_p = jnp.where(valid, w[a_p], 0.0).astype(F32)
    return (blk_exp, nvalid, tok.reshape(n_blk, 1, MOE_BLOCK), dst.reshape(n_blk, 1, MOE_BLOCK),
            w_p.reshape(n_blk, MOE_BLOCK, 1))


def kernel(x, mem, positions, g_mix, g_mem, w_in, g_qn_attn, g_kn_attn, w_conv, w_mem_kv, g_qn_mem,
           g_kn_mem, w_proj_attn, w_proj_conv, w_proj_mem, w_out, g_ffn, w_router_group,
           b_router_group, w_router_expert, b_router_expert, w_gate, w_up, w_down):
    b, s, d = x.shape
    n = b * s
    cos_t, sin_t = _rope_tables(positions)
    for l in range(w_in.shape[0]):
        wl = w_in[l]
        aw = N_BRANCH * GROUP_W
        cols = []
        for gi in range(len(DILATIONS)):
            for part in range(3):
                c0 = part * aw + gi * GROUP_W
                cols.append(wl[:, c0:c0 + GROUP_W])
        cols.append(wl[:, 3 * aw:])
        w_all = jnp.concatenate(cols, axis=1).astype(BF16)

        q0, q1, q2, u, cb, qm, gates = _inproj(x, g_mix[l][None], g_qn_mem[l][None], w_all)
        km, vm = _memkv(mem, g_mem[l][None], g_kn_mem[l][None], w_mem_kv[l].astype(BF16))
        gq = jnp.tile(g_qn_attn[l], 2)[None]
        gk = jnp.tile(g_kn_attn[l], 2)[None]
        o_attn = _attention((q0, q1, q2), cos_t, sin_t, gq, gk)

        w_r = jnp.zeros((d, LANES), F32)
        w_r = w_r.at[:, :N_EXPERT_GROUPS].set(w_router_group[l])
        w_r = w_r.at[:, N_EXPERT_GROUPS:N_EXPERT_GROUPS + N_EXPERTS].set(w_router_expert[l])
        b_r = jnp.zeros((1, LANES), F32)
        b_r = b_r.at[0, :N_EXPERT_GROUPS].set(b_router_group[l])
        b_r = b_r.at[0, N_EXPERT_GROUPS:N_EXPERT_GROUPS + N_EXPERTS].set(b_router_expert[l])
        x1, logits = _merge(x, o_attn, u, cb, qm, km, vm, gates, w_conv[l],
                            w_proj_attn[l].astype(BF16), w_proj_conv[l].astype(BF16),
                            w_proj_mem[l].astype(BF16), w_out[l].astype(BF16), g_ffn[l][None],
                            w_r, b_r)

        blk_exp, nvalid, row_tok, row_dst, row_w = _route(logits.reshape(n, LANES), n)
        y = _experts(x1, g_ffn[l][None], blk_exp, nvalid, row_tok, row_dst, row_w,
                     w_gate[l].astype(BF16), w_up[l].astype(BF16), w_down[l].astype(BF16))
        x = _combine(x1, y).reshape(b, s, d)
    return x
```

```python
import functools

import numpy as np
import jax
import jax.numpy as jnp
from jax import lax
from jax.experimental import pallas as pl
from jax.experimental.pallas import tpu as pltpu

F32 = jnp.float32
BF16 = jnp.bfloat16

D_MODEL = 1024
EPS = 1e-6
HEAD_DIM = 64
DILATIONS = (1, 4, 16)
BAND_HALF = 64
ATTN_SLOTS = 4
GROUP_W = ATTN_SLOTS * HEAD_DIM
QKV_W = 3 * GROUP_W
ROT_DIM = 16
ROPE_THETA = 500000.0
CONV_WIDTH = 768
MEM_HEADS = 4
MEM_HEAD_DIM = 128
MEM_WIDTH = 512
N_BRANCH = 3
N_EXPERT_GROUPS = 4
EXPERTS_PER_GROUP = 8
N_EXPERTS = 32
TOP_K = 2
EXPERT_FF = 512

LANES = 128
VMEM_LIMIT_BYTES = 56 * 1024 * 1024

ROW_TILE = 512
Q_BLOCK = 128
MOE_BLOCK = 256
ROW_CHUNKS = D_MODEL // LANES
GATHER_UNROLL = 8
NEG_BIG = -1e30


def _rms(t, gain):
    return t * lax.rsqrt(jnp.mean(t * t, axis=-1, keepdims=True) + EPS) * gain


def _inproj_kernel(x_ref, g_ref, gqm_ref, w_ref,
                   qkv0_ref, qkv1_ref, qkv2_ref, u_ref, cb_ref, qm_ref, gates_ref,
                   h_scr, hp_scr, *, tm):
    x = x_ref[0]
    h = _rms(x, g_ref[...])
    n_chunks = h.shape[1] // LANES
    for c in range(n_chunks):
        h_scr[c] = h[:, c * LANES:(c + 1) * LANES]
    hb = h.astype(BF16)

    def mm(lhs, c0, c1):
        return jnp.dot(lhs, w_ref[:, c0:c1], preferred_element_type=F32)

    qkv0_ref[0, 0] = mm(hb, 0, QKV_W).astype(BF16)
    for gi, out_ref in ((1, qkv1_ref), (2, qkv2_ref)):
        d = DILATIONS[gi]
        rows = tm // d
        for r in range(d):
            for c in range(n_chunks):
                hp_scr[r * rows:(r + 1) * rows, c * LANES:(c + 1) * LANES] = (
                    h_scr[c, pl.ds(r, rows, stride=d), :].astype(BF16))
        res = mm(hp_scr[...], gi * QKV_W, (gi + 1) * QKV_W)
        out_ref[0] = res.reshape(d, rows, QKV_W).astype(BF16)

    c = 3 * QKV_W
    cx = mm(hb, c, c + CONV_WIDTH)
    cc = mm(hb, c + 2 * CONV_WIDTH, c + 3 * CONV_WIDTH)
    u_ref[0] = (cc * cx).astype(BF16)
    cb_ref[0] = mm(hb, c + CONV_WIDTH, c + 2 * CONV_WIDTH).astype(BF16)

    c += 3 * CONV_WIDTH
    qm = mm(hb, c, c + MEM_WIDTH)
    for hh in range(MEM_HEADS):
        sl = slice(hh * MEM_HEAD_DIM, (hh + 1) * MEM_HEAD_DIM)
        qm_ref[0, :, sl] = _rms(qm[:, sl], gqm_ref[...]).astype(BF16)

    c += MEM_WIDTH
    for j in range(N_BRANCH):
        gl = mm(hb, c + j * D_MODEL, c + (j + 1) * D_MODEL)
        gates_ref[0, :, j * D_MODEL:(j + 1) * D_MODEL] = jax.nn.sigmoid(gl).astype(BF16)


def _inproj(x, g_mix, g_qn_mem, w_all):
    b, s, d = x.shape
    tm = ROW_TILE
    nt = s // tm
    outs = []
    out_specs = []
    for dil in DILATIONS:
        outs.append(jax.ShapeDtypeStruct((b, dil, s // dil, QKV_W), BF16))
        out_specs.append(pl.BlockSpec((1, dil, tm // dil, QKV_W), lambda bi, i: (bi, 0, i, 0)))
    for width in (CONV_WIDTH, CONV_WIDTH, MEM_WIDTH, N_BRANCH * D_MODEL):
        outs.append(jax.ShapeDtypeStruct((b, s, width), BF16))
        out_specs.append(pl.BlockSpec((1, tm, width), lambda bi, i: (bi, i, 0)))
    return pl.pallas_call(
        functools.partial(_inproj_kernel, tm=tm),
        grid=(b, nt),
        in_specs=[
            pl.BlockSpec((1, tm, d), lambda bi, i: (bi, i, 0)),
            pl.BlockSpec((1, d), lambda bi, i: (0, 0)),
            pl.BlockSpec((1, MEM_HEAD_DIM), lambda bi, i: (0, 0)),
            pl.BlockSpec(w_all.shape, lambda bi, i: (0, 0), pipeline_mode=pl.Buffered(1)),
        ],
        out_specs=out_specs,
        out_shape=outs,
        scratch_shapes=[pltpu.VMEM((d // LANES, tm, LANES), F32), pltpu.VMEM((tm, d), BF16)],
        compiler_params=pltpu.CompilerParams(
            dimension_semantics=("parallel", "parallel"), vmem_limit_bytes=VMEM_LIMIT_BYTES),
        name="inproj",
    )(x, g_mix, g_qn_mem, w_all)


def _memkv_kernel(mem_ref, g_ref, gk_ref, w_ref, k_ref, v_ref):
    h = _rms(mem_ref[0], g_ref[...]).astype(BF16)
    kv = jnp.dot(h, w_ref[...], preferred_element_type=F32)
    for hh in range(MEM_HEADS):
        sl = slice(hh * MEM_HEAD_DIM, (hh + 1) * MEM_HEAD_DIM)
        k_ref[0, :, sl] = _rms(kv[:, sl], gk_ref[...]).astype(BF16)
    v_ref[0] = kv[:, MEM_WIDTH:].astype(BF16)


def _memkv(mem, g_mem, g_kn_mem, w_kv):
    b, m, d = mem.shape
    return pl.pallas_call(
        _memkv_kernel,
        grid=(b,),
        in_specs=[
            pl.BlockSpec((1, m, d), lambda bi: (bi, 0, 0)),
            pl.BlockSpec((1, d), lambda bi: (0, 0)),
            pl.BlockSpec((1, MEM_HEAD_DIM), lambda bi: (0, 0)),
            pl.BlockSpec(w_kv.shape, lambda bi: (0, 0)),
        ],
        out_specs=[pl.BlockSpec((1, m, MEM_WIDTH), lambda bi: (bi, 0, 0))] * 2,
        out_shape=[jax.ShapeDtypeStruct((b, m, MEM_WIDTH), BF16)] * 2,
        compiler_params=pltpu.CompilerParams(dimension_semantics=("parallel",)),
        name="memkv",
    )(mem, g_mem, g_kn_mem, w_kv)


ATTN_WIN = Q_BLOCK + 2 * BAND_HALF
ATTN_UNROLL = 4


def _attn_consts():
    lane = np.arange(LANES)
    gsum = (lane[:, None] // HEAD_DIM == lane[None, :] // HEAD_DIM).astype(np.float32)
    half = ROT_DIM // 2
    dim = lane % HEAD_DIM
    src = np.where(dim < half, lane + half, np.where(dim < ROT_DIM, lane - half, -1))
    pswap = (lane[:, None] == src[None, :]).astype(np.float32)
    i = np.arange(Q_BLOCK)[:, None]
    c = np.arange(ATTN_WIN)[None, :]
    band = (c >= i) & (c <= i + 2 * BAND_HALF)
    first = c >= BAND_HALF
    last = c < Q_BLOCK + BAND_HALF
    variants = [band, band & first, band & last, band & first & last]
    bias = np.stack([np.where(v, 0.0, NEG_BIG) for v in variants]).astype(np.float32)
    ones = np.stack([np.broadcast_to(lane < HEAD_DIM, (ATTN_WIN, LANES)),
                     np.broadcast_to(lane >= HEAD_DIM, (ATTN_WIN, LANES))]).astype(np.float32)
    return (jnp.asarray(gsum, BF16), jnp.asarray(pswap, BF16), jnp.asarray(bias, F32),
            jnp.asarray(ones, BF16))


def _attn_kernel(q0, k0, v0, q1, k1, v1, q2, k2, v2, cos_ref, sin_ref, gq_ref, gk_ref,
                 gsum_ref, pswap_ref, bias_ref, ones_ref, out_ref, qs, ks, vs, o_scr, l_scr, *, seq):
    lane = lax.broadcasted_iota(jnp.int32, (1, LANES), 1)
    low_half = lane < HEAD_DIM
    pad = BAND_HALF
    n_blocks = seq // Q_BLOCK

    def split_dot(t, w_ref):
        hi = t.astype(BF16)
        lo = (t - hi.astype(F32)).astype(BF16)
        w = w_ref[...]
        return (jnp.dot(hi, w, preferred_element_type=F32)
                + jnp.dot(lo, w, preferred_element_type=F32))

    def norm_rope(t, gain, cos, sin):
        ms = split_dot(t * t, gsum_ref) * (1.0 / HEAD_DIM)
        tn = t * lax.rsqrt(ms + EPS) * gain
        return tn * cos + split_dot(tn, pswap_ref) * sin

    for gi, (q_ref, k_ref, v_ref) in enumerate(((q0, k0, v0), (q1, k1, v1), (q2, k2, v2))):
        d = DILATIONS[gi]
        length = seq // d
        nblk = length // Q_BLOCK
        region = length + pad

        zpad = jnp.zeros((pad, LANES), BF16)
        for r in range(d + 1):
            ks[r * region:r * region + pad, :] = zpad
            vs[0, r * region:r * region + pad, :] = zpad
            vs[1, r * region:r * region + pad, :] = zpad

        def split_index(b, nblk=nblk):
            if nblk == 1:
                return b, 0
            r = b // nblk
            return r, b - r * nblk

        def table_rows(r, n, d=d):
            if d == 1:
                return pl.ds(pl.multiple_of(n * Q_BLOCK, Q_BLOCK), Q_BLOCK)
            return pl.ds(r + d * n * Q_BLOCK, Q_BLOCK, stride=d)

        def prep(b, q_ref=q_ref, k_ref=k_ref, v_ref=v_ref, split_index=split_index,
                 table_rows=table_rows):
            r, n = split_index(b)
            src = pl.ds(pl.multiple_of(n * Q_BLOCK, Q_BLOCK), Q_BLOCK)
            rows = table_rows(r, n)
            cos = cos_ref[0, rows, :]
            sin = sin_ref[0, rows, :]
            qn = norm_rope(q_ref[0, r, src, :].astype(F32), gq_ref[...], cos, sin) * (HEAD_DIM ** -0.5)
            kn = norm_rope(k_ref[0, r, src, :].astype(F32), gk_ref[...], cos, sin)
            vv = v_ref[0, r, src, :]
            qdst = pl.ds(pl.multiple_of(b * Q_BLOCK, Q_BLOCK), Q_BLOCK)
            kdst = pl.ds(pl.multiple_of(b * Q_BLOCK + (r + 1) * pad, pad), Q_BLOCK)
            qs[0, qdst, :] = jnp.where(low_half, qn, 0.0).astype(BF16)
            qs[1, qdst, :] = jnp.where(low_half, 0.0, qn).astype(BF16)
            ks[kdst, :] = kn.astype(BF16)
            vs[0, kdst, :] = jnp.where(low_half, vv, jnp.zeros_like(vv))
            vs[1, kdst, :] = jnp.where(low_half, jnp.zeros_like(vv), vv)

        def block(b, gi=gi, nblk=nblk, split_index=split_index, table_rows=table_rows):
            r, n = split_index(b)
            if nblk == 1:
                bias = bias_ref[3]
            else:
                variant = jnp.where(n == 0, 1, 0) + jnp.where(n == nblk - 1, 2, 0)
                bias = bias_ref[variant]
            qsrc = pl.ds(pl.multiple_of(b * Q_BLOCK, Q_BLOCK), Q_BLOCK)
            ksrc = pl.ds(pl.multiple_of(b * Q_BLOCK + r * pad, pad), ATTN_WIN)
            kw = ks[ksrc, :]
            acc_o = None
            acc_d = None
            ms = []
            for hh in range(2):
                s = lax.dot_general(qs[hh, qsrc, :], kw, (((1,), (1,)), ((), ())),
                                    preferred_element_type=F32) + bias
                m = jnp.max(s, axis=-1, keepdims=True)
                p = jnp.exp(s - m).astype(BF16)
                po = jnp.dot(p, vs[hh, ksrc, :], preferred_element_type=F32)
                pd = jnp.dot(p, ones_ref[hh], preferred_element_type=F32)
                acc_o = po if acc_o is None else acc_o + po
                acc_d = pd if acc_d is None else acc_d + pd
                ms.append(m)
            m128 = jnp.where(low_half, ms[0], ms[1])
            dst = table_rows(r, n)
            o_scr[gi, dst, :] = acc_o / acc_d
            l_scr[gi, dst, :] = m128 + jnp.log(acc_d)

        def unrolled(fn):
            def body(it, carry):
                for u in range(ATTN_UNROLL):
                    fn(it * ATTN_UNROLL + u)
                return carry
            lax.fori_loop(0, n_blocks // ATTN_UNROLL, body, 0)

        unrolled(prep)
        unrolled(block)

    l0 = l_scr[0]
    l1 = l_scr[1]
    l2 = l_scr[2]
    m = jnp.maximum(jnp.maximum(l0, l1), l2)
    e0 = jnp.exp(l0 - m)
    e1 = jnp.exp(l1 - m)
    e2 = jnp.exp(l2 - m)
    mix = (e0 * o_scr[0] + e1 * o_scr[1] + e2 * o_scr[2]) / (e0 + e1 + e2)
    out_ref[0] = mix.astype(BF16)


def _attention(qkv, cos_t, sin_t, gq, gk):
    b, _, s, _ = qkv[0].shape
    in_specs = []
    args = []
    for gi, dil in enumerate(DILATIONS):
        for part in range(3):
            in_specs.append(pl.BlockSpec((1, dil, s // dil, LANES),
                                         lambda bi, hp, part=part: (bi, 0, 0, 2 * part + hp)))
            args.append(qkv[gi])
    consts = _attn_consts()
    tab_spec = pl.BlockSpec((1, s, LANES), lambda bi, hp: (bi, 0, 0))
    gain_spec = pl.BlockSpec((1, LANES), lambda bi, hp: (0, 0))
    const_specs = [pl.BlockSpec(c.shape, lambda bi, hp, nd=c.ndim: (0,) * nd) for c in consts]
    key_rows = max(d * (s // d + BAND_HALF) for d in DILATIONS) + BAND_HALF
    return pl.pallas_call(
        functools.partial(_attn_kernel, seq=s),
        grid=(b, 2),
        in_specs=in_specs + [tab_spec] * 2 + [gain_spec] * 2 + const_specs,
        out_specs=pl.BlockSpec((1, s, LANES), lambda bi, hp: (bi, 0, hp)),
        out_shape=jax.ShapeDtypeStruct((b, s, GROUP_W), BF16),
        scratch_shapes=[
            pltpu.VMEM((2, s, LANES), BF16),
            pltpu.VMEM((key_rows, LANES), BF16),
            pltpu.VMEM((2, key_rows, LANES), BF16),
            pltpu.VMEM((3, s, LANES), F32),
            pltpu.VMEM((3, s, LANES), F32),
        ],
        compiler_params=pltpu.CompilerParams(
            dimension_semantics=("parallel", "parallel"), vmem_limit_bytes=VMEM_LIMIT_BYTES),
        name="dilated_attn",
    )(*args, cos_t, sin_t, gq, gk, *consts)


def _merge_kernel(x_ref, oa_ref, u_ref, uprev_ref, unext_ref, cb_ref, qm_ref, km_ref, vm_ref,
                  gates_ref, wconv_ref, wpa_ref, wpc_ref, wpm_ref, wout_ref, gffn_ref, wr_ref, br_ref,
                  x1_ref, logit_ref, om_scr, *, tm, nt):
    i = pl.program_id(1)
    u = u_ref[0].astype(F32)
    prev_row = uprev_ref[0].astype(F32)[15:16, :] * (i > 0).astype(F32)
    next_row = unext_ref[0].astype(F32)[0:1, :] * (i < nt - 1).astype(F32)
    row = lax.broadcasted_iota(jnp.int32, (tm, 1), 0)
    um = jnp.where(row == 0, prev_row, pltpu.roll(u, 1, 0))
    up = jnp.where(row == tm - 1, next_row, pltpu.roll(u, tm - 1, 0))
    wc = wconv_ref[...]
    y = wc[0:1, :] * um + wc[1:2, :] * u + wc[2:3, :] * up
    z = (cb_ref[0].astype(F32) * y).astype(BF16)

    for hh in range(MEM_HEADS):
        sl = slice(hh * MEM_HEAD_DIM, (hh + 1) * MEM_HEAD_DIM)
        s = lax.dot_general(qm_ref[0, :, sl], km_ref[0, :, sl], (((1,), (1,)), ((), ())),
                            preferred_element_type=F32) * (MEM_HEAD_DIM ** -0.5)
        m = jnp.max(s, axis=-1, keepdims=True)
        p = jnp.exp(s - m)
        den = jnp.sum(p, axis=-1, keepdims=True)
        o = jnp.dot(p.astype(BF16), vm_ref[0, :, sl], preferred_element_type=F32) / den
        om_scr[:, sl] = o.astype(BF16)

    pa = jnp.dot(oa_ref[0], wpa_ref[...], preferred_element_type=F32)
    pc = jnp.dot(z, wpc_ref[...], preferred_element_type=F32)
    pm = jnp.dot(om_scr[...], wpm_ref[...], preferred_element_type=F32)
    merged = (gates_ref[0, :, 0:D_MODEL].astype(F32) * pa
              + gates_ref[0, :, D_MODEL:2 * D_MODEL].astype(F32) * pc
              + gates_ref[0, :, 2 * D_MODEL:3 * D_MODEL].astype(F32) * pm)
    x1 = x_ref[0] + jnp.dot(merged.astype(BF16), wout_ref[...], preferred_element_type=F32)
    for c in range(ROW_CHUNKS):
        x1_ref[pl.ds(c, tm, stride=ROW_CHUNKS), :] = x1[:, c * LANES:(c + 1) * LANES]

    h2 = _rms(x1, gffn_ref[...])
    logit_ref[0] = jnp.dot(h2, wr_ref[...], preferred_element_type=F32,
                           precision=lax.Precision.HIGHEST) + br_ref[...]


def _merge(x, o_attn, u, cb, qm, km, vm, gates, w_conv, wpa, wpc, wpm, wout, g_ffn, w_r, b_r):
    b, s, d = x.shape
    tm = ROW_TILE
    nt = s // tm
    halo = 16
    hb = tm // halo

    def tile(width):
        return pl.BlockSpec((1, tm, width), lambda bi, i: (bi, i, 0))

    def whole(arr):
        return pl.BlockSpec(arr.shape, lambda bi, i: (0,) * arr.ndim)

    def per_batch(arr):
        return pl.BlockSpec((1,) + arr.shape[1:], lambda bi, i: (bi, 0, 0))

    in_specs = [
        tile(d), tile(GROUP_W), tile(CONV_WIDTH),
        pl.BlockSpec((1, halo, CONV_WIDTH), lambda bi, i: (bi, jnp.maximum(i * hb - 1, 0), 0)),
        pl.BlockSpec((1, halo, CONV_WIDTH),
                     lambda bi, i: (bi, jnp.minimum((i + 1) * hb, s // halo - 1), 0)),
        tile(CONV_WIDTH), tile(MEM_WIDTH), per_batch(km), per_batch(vm), tile(N_BRANCH * d),
        whole(w_conv), whole(wpa), whole(wpc), whole(wpm), whole(wout), whole(g_ffn),
        whole(w_r), whole(b_r),
    ]
    return pl.pallas_call(
        functools.partial(_merge_kernel, tm=tm, nt=nt),
        grid=(b, nt),
        in_specs=in_specs,
        out_specs=[pl.BlockSpec((tm * ROW_CHUNKS, LANES), lambda bi, i: (bi * nt + i, 0)), tile(LANES)],
        out_shape=[jax.ShapeDtypeStruct((b * s * ROW_CHUNKS, LANES), F32),
                   jax.ShapeDtypeStruct((b, s, LANES), F32)],
        scratch_shapes=[pltpu.VMEM((tm, MEM_WIDTH), BF16)],
        compiler_params=pltpu.CompilerParams(
            dimension_semantics=("parallel", "parallel"), vmem_limit_bytes=VMEM_LIMIT_BYTES),
        name="merge",
    )(x, o_attn, u, u, u, cb, qm, km, vm, gates, w_conv, wpa, wpc, wpm, wout, g_ffn, w_r, b_r)


DRAIN_STEPS = 2


def _expert_kernel(blk_exp_ref, nvalid_ref, tok_ref, tok_next_ref, dst_ref, wt_ref, x1_hbm, gffn_ref,
                   wg_ref, wu_ref, wd_ref, y_hbm, xbuf, hbuf, ybuf, wg_s, wu_s, wd_s, gsem, ssem):
    i = pl.program_id(0)
    slot = lax.rem(i, 2)
    rows = MOE_BLOCK * ROW_CHUNKS

    def start_gather(idx_ref, s):
        def body(it, c):
            for u in range(GATHER_UNROLL):
                j = it * GATHER_UNROLL + u
                src_row = pl.multiple_of(idx_ref[0, 0, j], ROW_CHUNKS)
                pltpu.make_async_copy(x1_hbm.at[pl.ds(src_row, ROW_CHUNKS), :],
                                      xbuf.at[s, pl.ds(j * ROW_CHUNKS, ROW_CHUNKS), :],
                                      gsem.at[s]).start()
            return c
        lax.fori_loop(0, MOE_BLOCK // GATHER_UNROLL, body, 0)

    def wait_gather(s):
        pltpu.make_async_copy(x1_hbm.at[pl.ds(0, rows), :], xbuf.at[s], gsem.at[s]).wait()

    def start_scatter(s):
        def body(it, c):
            for u in range(GATHER_UNROLL):
                j = it * GATHER_UNROLL + u
                dst_row = pl.multiple_of(dst_ref[0, 0, j], ROW_CHUNKS)
                pltpu.make_async_copy(ybuf.at[s, pl.ds(j * ROW_CHUNKS, ROW_CHUNKS), :],
                                      y_hbm.at[pl.ds(dst_row, ROW_CHUNKS), :], ssem.at[s]).start()
            return c
        lax.fori_loop(0, MOE_BLOCK // GATHER_UNROLL, body, 0)

    def wait_scatter(s):
        pltpu.make_async_copy(ybuf.at[s], y_hbm.at[pl.ds(0, rows), :], ssem.at[s]).wait()

    nvalid = nvalid_ref[i]
    prev = jnp.maximum(i - 1, 0)
    prev2 = jnp.maximum(i - 2, 0)

    @pl.when(i == 0)
    def _():
        start_gather(tok_ref, 0)
        ybuf[0] = jnp.zeros((rows, LANES), F32)
        spare = pltpu.make_async_copy(ybuf.at[0], y_hbm.at[pl.ds(y_hbm.shape[0] - rows, rows), :],
                                      ssem.at[0])
        spare.start()
        spare.wait()

    @pl.when(jnp.logical_and(i >= 2, nvalid_ref[prev2] > 0))
    def _():
        wait_scatter(slot)

    @pl.when(jnp.logical_or(i == 0, blk_exp_ref[i] != blk_exp_ref[prev]))
    def _():
        wg_s[...] = wg_ref[0].astype(BF16)
        wu_s[...] = wu_ref[0].astype(BF16)
        wd_s[...] = wd_ref[0].astype(BF16)

    @pl.when(nvalid > 0)
    def _():
        wait_gather(slot)
        start_gather(tok_next_ref, 1 - slot)

        ss = None
        for c in range(ROW_CHUNKS):
            xc = xbuf[slot, pl.ds(c, MOE_BLOCK, stride=ROW_CHUNKS), :]
            ss = xc * xc if ss is None else ss + xc * xc
        scale = lax.rsqrt(jnp.sum(ss, axis=-1, keepdims=True) * (1.0 / D_MODEL) + EPS)
        for c in range(ROW_CHUNKS):
            sl = slice(c * LANES, (c + 1) * LANES)
            xc = xbuf[slot, pl.ds(c, MOE_BLOCK, stride=ROW_CHUNKS), :]
            hbuf[:, sl] = (xc * scale * gffn_ref[:, sl]).astype(BF16)
        h2 = hbuf[...]
        a = jnp.dot(h2, wg_s[...], preferred_element_type=F32)
        up = jnp.dot(h2, wu_s[...], preferred_element_type=F32)
        act = (a * jax.nn.sigmoid(a) * up).astype(BF16)
        y = jnp.dot(act, wd_s[...], preferred_element_type=F32) * wt_ref[0]
        for c in range(ROW_CHUNKS):
            ybuf[slot, pl.ds(c, MOE_BLOCK, stride=ROW_CHUNKS), :] = y[:, c * LANES:(c + 1) * LANES]
        start_scatter(slot)

    @pl.when(jnp.logical_and(nvalid == 0, jnp.logical_and(i > 0, nvalid_ref[prev] > 0)))
    def _():
        wait_gather(slot)


def _experts(x1r, g_ffn, blk_exp, nvalid, row_tok, row_dst, row_w, wg, wu, wd):
    rows_total, _ = x1r.shape
    d = D_MODEL
    n_blk = blk_exp.shape[0]
    idx_spec = lambda fn: pl.BlockSpec((1, 1, MOE_BLOCK), fn, memory_space=pltpu.SMEM)
    grid_spec = pltpu.PrefetchScalarGridSpec(
        num_scalar_prefetch=2,
        grid=(n_blk,),
        in_specs=[
            idx_spec(lambda i, be, nv: (i, 0, 0)),
            idx_spec(lambda i, be, nv: (jnp.minimum(i + 1, n_blk - 1), 0, 0)),
            idx_spec(lambda i, be, nv: (i, 0, 0)),
            pl.BlockSpec((1, MOE_BLOCK, 1), lambda i, be, nv: (i, 0, 0)),
            pl.BlockSpec(memory_space=pl.ANY),
            pl.BlockSpec((1, d), lambda i, be, nv: (0, 0)),
            pl.BlockSpec((1, d, EXPERT_FF), lambda i, be, nv: (be[i], 0, 0)),
            pl.BlockSpec((1, d, EXPERT_FF), lambda i, be, nv: (be[i], 0, 0)),
            pl.BlockSpec((1, EXPERT_FF, d), lambda i, be, nv: (be[i], 0, 0)),
        ],
        out_specs=pl.BlockSpec(memory_space=pl.ANY),
        scratch_shapes=[
            pltpu.VMEM((2, MOE_BLOCK * ROW_CHUNKS, LANES), F32),
            pltpu.VMEM((MOE_BLOCK, d), BF16),
            pltpu.VMEM((2, MOE_BLOCK * ROW_CHUNKS, LANES), F32),
            pltpu.VMEM((d, EXPERT_FF), BF16),
            pltpu.VMEM((d, EXPERT_FF), BF16),
            pltpu.VMEM((EXPERT_FF, d), BF16),
            pltpu.SemaphoreType.DMA((2,)),
            pltpu.SemaphoreType.DMA((2,)),
        ],
    )
    return pl.pallas_call(
        _expert_kernel,
        grid_spec=grid_spec,
        out_shape=jax.ShapeDtypeStruct((TOP_K * rows_total + MOE_BLOCK * ROW_CHUNKS, LANES), F32),
        compiler_params=pltpu.CompilerParams(
            dimension_semantics=("arbitrary",), vmem_limit_bytes=VMEM_LIMIT_BYTES),
        name="experts",
    )(blk_exp, nvalid, row_tok, row_tok, row_dst, row_w, x1r, g_ffn, wg, wu, wd)


def _combine_kernel(x1_ref, y0_ref, y1_ref, o_ref, *, tm):
    for c in range(ROW_CHUNKS):
        rows = pl.ds(c, tm, stride=ROW_CHUNKS)
        o_ref[:, c * LANES:(c + 1) * LANES] = x1_ref[rows, :] + y0_ref[rows, :] + y1_ref[rows, :]


def _combine(x1r, y):
    n = x1r.shape[0] // ROW_CHUNKS
    tm = ROW_TILE
    nt = n // tm
    blk = (tm * ROW_CHUNKS, LANES)
    return pl.pallas_call(
        functools.partial(_combine_kernel, tm=tm),
        grid=(nt,),
        in_specs=[pl.BlockSpec(blk, lambda i: (i, 0)),
                  pl.BlockSpec(blk, lambda i: (i, 0)),
                  pl.BlockSpec(blk, lambda i: (i + nt, 0))],
        out_specs=pl.BlockSpec((tm, D_MODEL), lambda i: (i, 0)),
        out_shape=jax.ShapeDtypeStruct((n, D_MODEL), F32),
        compiler_params=pltpu.CompilerParams(dimension_semantics=("parallel",)),
        name="combine",
    )(x1r, y, y)


def _rope_tables(positions):
    half = ROT_DIM // 2
    dim = np.arange(LANES) % HEAD_DIM
    rotary = dim < ROT_DIM
    inv = ROPE_THETA ** (-jnp.arange(0, ROT_DIM, 2, dtype=F32) / ROT_DIM)
    inv_lane = jnp.where(jnp.asarray(rotary), inv[jnp.asarray(dim % half)], 0.0)
    sign = jnp.asarray(np.where(dim < half, -1.0, 1.0), F32)
    ang = positions.astype(F32)[..., None] * inv_lane
    return jnp.cos(ang), jnp.sin(ang) * sign


def _route(logits, n):
    pg = jax.nn.softmax(logits[:, :N_EXPERT_GROUPS], axis=-1)
    g_idx = jnp.argmax(pg, axis=-1).astype(jnp.int32)
    pg_top = jnp.max(pg, axis=-1)
    el = logits[:, N_EXPERT_GROUPS:N_EXPERT_GROUPS + N_EXPERTS].reshape(
        n, N_EXPERT_GROUPS, EXPERTS_PER_GROUP)
    sel = jnp.take_along_axis(el, g_idx[:, None, None], axis=1)[:, 0]
    pe_top, e_idx = lax.top_k(jax.nn.softmax(sel, axis=-1), TOP_K)
    weights = pg_top[:, None] * pe_top / jnp.sum(pe_top, axis=-1, keepdims=True)
    ids = g_idx[:, None] * EXPERTS_PER_GROUP + e_idx.astype(jnp.int32)

    nk = n * TOP_K
    e = ids.reshape(nk)
    w = weights.reshape(nk)
    a = jnp.arange(nk, dtype=jnp.int32)
    _, a_s = lax.sort((e, a), num_keys=1, is_stable=True)
    counts = jnp.sum((e[:, None] == jnp.arange(N_EXPERTS, dtype=jnp.int32)[None, :]).astype(jnp.int32),
                     axis=0)
    padded = ((counts + MOE_BLOCK - 1) // MOE_BLOCK) * MOE_BLOCK
    pend = jnp.cumsum(padded)
    pstart = pend - padded
    start = jnp.cumsum(counts) - counts
    n_blk = nk // MOE_BLOCK + N_EXPERTS + DRAIN_STEPS
    blk_start = jnp.arange(n_blk, dtype=jnp.int32) * MOE_BLOCK
    blk_exp = jnp.minimum(jnp.sum((pend[None, :] <= blk_start[:, None]).astype(jnp.int32), axis=1),
                          N_EXPERTS - 1).astype(jnp.int32)
    nvalid = jnp.clip(counts[blk_exp] - (blk_start - pstart[blk_exp]), 0, MOE_BLOCK)
    nvalid = jnp.where(blk_start < pend[-1], nvalid, 0).astype(jnp.int32)
    j = jnp.arange(MOE_BLOCK, dtype=jnp.int32)[None, :]
    valid = j < nvalid[:, None]
    src = jnp.clip((start[blk_exp] + blk_start - pstart[blk_exp])[:, None] + j, 0, nk - 1)
    a_p = a_s[src]
    tok = (jnp.where(valid, a_p >> 1, 0) * ROW_CHUNKS).astype(jnp.int32)
    dst = (jnp.where(valid, (a_p & 1) * n + (a_p >> 1), TOP_K * n + j) * ROW_CHUNKS).astype(jnp.int32)
    w_p = jnp.where(valid, w[a_p], 0.0).astype(F32)
    return (blk_exp, nvalid, tok.reshape(n_blk, 1, MOE_BLOCK), dst.reshape(n_blk, 1, MOE_BLOCK),
            w_p.reshape(n_blk, MOE_BLOCK, 1))


def kernel(x, mem, positions, g_mix, g_mem, w_in, g_qn_attn, g_kn_attn, w_conv, w_mem_kv, g_qn_mem,
           g_kn_mem, w_proj_attn, w_proj_conv, w_proj_mem, w_out, g_ffn, w_router_group,
           b_router_group, w_router_expert, b_router_expert, w_gate, w_up, w_down):
    b, s, d = x.shape
    n = b * s
    cos_t, sin_t = _rope_tables(positions)
    for l in range(w_in.shape[0]):
        wl = w_in[l]
        aw = N_BRANCH * GROUP_W
        cols = []
        for gi in range(len(DILATIONS)):
            for part in range(3):
                c0 = part * aw + gi * GROUP_W
                cols.append(wl[:, c0:c0 + GROUP_W])
        cols.append(wl[:, 3 * aw:])
        w_all = jnp.concatenate(cols, axis=1).astype(BF16)

        q0, q1, q2, u, cb, qm, gates = _inproj(x, g_mix[l][None], g_qn_mem[l][None], w_all)
        km, vm = _memkv(mem, g_mem[l][None], g_kn_mem[l][None], w_mem_kv[l].astype(BF16))
        gq = jnp.tile(g_qn_attn[l], 2)[None]
        gk = jnp.tile(g_kn_attn[l], 2)[None]
        o_attn = _attention((q0, q1, q2), cos_t, sin_t, gq, gk)

        w_r = jnp.zeros((d, LANES), F32)
        w_r = w_r.at[:, :N_EXPERT_GROUPS].set(w_router_group[l])
        w_r = w_r.at[:, N_EXPERT_GROUPS:N_EXPERT_GROUPS + N_EXPERTS].set(w_router_expert[l])
        b_r = jnp.zeros((1, LANES), F32)
        b_r = b_r.at[0, :N_EXPERT_GROUPS].set(b_router_group[l])
        b_r = b_r.at[0, N_EXPERT_GROUPS:N_EXPERT_GROUPS + N_EXPERTS].set(b_router_expert[l])
        x1, logits = _merge(x, o_attn, u, cb, qm, km, vm, gates, w_conv[l],
                            w_proj_attn[l].astype(BF16), w_proj_conv[l].astype(BF16),
                            w_proj_mem[l].astype(BF16), w_out[l].astype(BF16), g_ffn[l][None],
                            w_r, b_r)

        blk_exp, nvalid, row_tok, row_dst, row_w = _route(logits.reshape(n, LANES), n)
        y = _experts(x1, g_ffn[l][None], blk_exp, nvalid, row_tok, row_dst, row_w,
                     w_gate[l], w_up[l], w_down[l])
        x = _combine(x1, y).reshape(b, s, d)
    return x
```

```python
import functools

import numpy as np
import jax
import jax.numpy as jnp
from jax import lax
from jax.experimental import pallas as pl
from jax.experimental.pallas import tpu as pltpu

F32 = jnp.float32
BF16 = jnp.bfloat16

D_MODEL = 1024
EPS = 1e-6
HEAD_DIM = 64
DILATIONS = (1, 4, 16)
BAND_HALF = 64
ATTN_SLOTS = 4
GROUP_W = ATTN_SLOTS * HEAD_DIM
QKV_W = 3 * GROUP_W
ROT_DIM = 16
ROPE_THETA = 500000.0
CONV_WIDTH = 768
MEM_HEADS = 4
MEM_HEAD_DIM = 128
MEM_WIDTH = 512
N_BRANCH = 3
N_EXPERT_GROUPS = 4
EXPERTS_PER_GROUP = 8
N_EXPERTS = 32
TOP_K = 2
EXPERT_FF = 512

LANES = 128
VMEM_LIMIT_BYTES = 56 * 1024 * 1024

ROW_TILE = 512
Q_BLOCK = 128
MOE_BLOCK = 256
ROW_CHUNKS = D_MODEL // LANES
GATHER_UNROLL = 8
NEG_BIG = -1e30
ROUTE_NEG = -3e38
ROUTE_ROWS = 8


def _rms(t, gain):
    return t * lax.rsqrt(jnp.mean(t * t, axis=-1, keepdims=True) + EPS) * gain


def _inproj_kernel(x_ref, g_ref, gqm_ref, w_ref,
                   qkv0_ref, qkv1_ref, qkv2_ref, u_ref, cb_ref, qm_ref, gates_ref,
                   h_scr, hp_scr, *, tm):
    x = x_ref[0]
    h = _rms(x, g_ref[...])
    n_chunks = h.shape[1] // LANES
    for c in range(n_chunks):
        h_scr[c] = h[:, c * LANES:(c + 1) * LANES]
    hb = h.astype(BF16)

    def mm(lhs, c0, c1):
        return jnp.dot(lhs, w_ref[:, c0:c1], preferred_element_type=F32)

    qkv0_ref[0, 0] = mm(hb, 0, QKV_W).astype(BF16)
    for gi, out_ref in ((1, qkv1_ref), (2, qkv2_ref)):
        d = DILATIONS[gi]
        rows = tm // d
        for r in range(d):
            for c in range(n_chunks):
                hp_scr[r * rows:(r + 1) * rows, c * LANES:(c + 1) * LANES] = (
                    h_scr[c, pl.ds(r, rows, stride=d), :].astype(BF16))
        res = mm(hp_scr[...], gi * QKV_W, (gi + 1) * QKV_W)
        out_ref[0] = res.reshape(d, rows, QKV_W).astype(BF16)

    c = 3 * QKV_W
    cx = mm(hb, c, c + CONV_WIDTH)
    cc = mm(hb, c + 2 * CONV_WIDTH, c + 3 * CONV_WIDTH)
    u_ref[0] = (cc * cx).astype(BF16)
    cb_ref[0] = mm(hb, c + CONV_WIDTH, c + 2 * CONV_WIDTH).astype(BF16)

    c += 3 * CONV_WIDTH
    qm = mm(hb, c, c + MEM_WIDTH)
    for hh in range(MEM_HEADS):
        sl = slice(hh * MEM_HEAD_DIM, (hh + 1) * MEM_HEAD_DIM)
        qm_ref[0, :, sl] = _rms(qm[:, sl], gqm_ref[...]).astype(BF16)

    c += MEM_WIDTH
    for j in range(N_BRANCH):
        gl = mm(hb, c + j * D_MODEL, c + (j + 1) * D_MODEL)
        gates_ref[0, :, j * D_MODEL:(j + 1) * D_MODEL] = jax.nn.sigmoid(gl).astype(BF16)


def _inproj(x, g_mix, g_qn_mem, w_all):
    b, s, d = x.shape
    tm = ROW_TILE
    nt = s // tm
    outs = []
    out_specs = []
    for dil in DILATIONS:
        outs.append(jax.ShapeDtypeStruct((b, dil, s // dil, QKV_W), BF16))
        out_specs.append(pl.BlockSpec((1, dil, tm // dil, QKV_W), lambda bi, i: (bi, 0, i, 0)))
    for width in (CONV_WIDTH, CONV_WIDTH, MEM_WIDTH, N_BRANCH * D_MODEL):
        outs.append(jax.ShapeDtypeStruct((b, s, width), BF16))
        out_specs.append(pl.BlockSpec((1, tm, width), lambda bi, i: (bi, i, 0)))
    return pl.pallas_call(
        functools.partial(_inproj_kernel, tm=tm),
        grid=(b, nt),
        in_specs=[
            pl.BlockSpec((1, tm, d), lambda bi, i: (bi, i, 0)),
            pl.BlockSpec((1, d), lambda bi, i: (0, 0)),
            pl.BlockSpec((1, MEM_HEAD_DIM), lambda bi, i: (0, 0)),
            pl.BlockSpec(w_all.shape, lambda bi, i: (0, 0), pipeline_mode=pl.Buffered(1)),
        ],
        out_specs=out_specs,
        out_shape=outs,
        scratch_shapes=[pltpu.VMEM((d // LANES, tm, LANES), F32), pltpu.VMEM((tm, d), BF16)],
        compiler_params=pltpu.CompilerParams(
            dimension_semantics=("parallel", "parallel"), vmem_limit_bytes=VMEM_LIMIT_BYTES),
        name="inproj",
    )(x, g_mix, g_qn_mem, w_all)


def _memkv_kernel(mem_ref, g_ref, gk_ref, w_ref, k_ref, v_ref):
    h = _rms(mem_ref[0], g_ref[...]).astype(BF16)
    kv = jnp.dot(h, w_ref[...], preferred_element_type=F32)
    for hh in range(MEM_HEADS):
        sl = slice(hh * MEM_HEAD_DIM, (hh + 1) * MEM_HEAD_DIM)
        k_ref[0, :, sl] = _rms(kv[:, sl], gk_ref[...]).astype(BF16)
    v_ref[0] = kv[:, MEM_WIDTH:].astype(BF16)


def _memkv(mem, g_mem, g_kn_mem, w_kv):
    b, m, d = mem.shape
    return pl.pallas_call(
        _memkv_kernel,
        grid=(b,),
        in_specs=[
            pl.BlockSpec((1, m, d), lambda bi: (bi, 0, 0)),
            pl.BlockSpec((1, d), lambda bi: (0, 0)),
            pl.BlockSpec((1, MEM_HEAD_DIM), lambda bi: (0, 0)),
            pl.BlockSpec(w_kv.shape, lambda bi: (0, 0)),
        ],
        out_specs=[pl.BlockSpec((1, m, MEM_WIDTH), lambda bi: (bi, 0, 0))] * 2,
        out_shape=[jax.ShapeDtypeStruct((b, m, MEM_WIDTH), BF16)] * 2,
        compiler_params=pltpu.CompilerParams(dimension_semantics=("parallel",)),
        name="memkv",
    )(mem, g_mem, g_kn_mem, w_kv)


ATTN_WIN = Q_BLOCK + 2 * BAND_HALF
ATTN_UNROLL = 4


def _attn_consts():
    lane = np.arange(LANES)
    gsum = (lane[:, None] // HEAD_DIM == lane[None, :] // HEAD_DIM).astype(np.float32)
    half = ROT_DIM // 2
    dim = lane % HEAD_DIM
    src = np.where(dim < half, lane + half, np.where(dim < ROT_DIM, lane - half, -1))
    pswap = (lane[:, None] == src[None, :]).astype(np.float32)
    i = np.arange(Q_BLOCK)[:, None]
    c = np.arange(ATTN_WIN)[None, :]
    band = (c >= i) & (c <= i + 2 * BAND_HALF)
    first = c >= BAND_HALF
    last = c < Q_BLOCK + BAND_HALF
    variants = [band, band & first, band & last, band & first & last]
    bias = np.stack([np.where(v, 0.0, NEG_BIG) for v in variants]).astype(np.float32)
    ones = np.stack([np.broadcast_to(lane < HEAD_DIM, (ATTN_WIN, LANES)),
                     np.broadcast_to(lane >= HEAD_DIM, (ATTN_WIN, LANES))]).astype(np.float32)
    return (jnp.asarray(gsum, BF16), jnp.asarray(pswap, BF16), jnp.asarray(bias, F32),
            jnp.asarray(ones, BF16))


def _attn_kernel(q0, k0, v0, q1, k1, v1, q2, k2, v2, cos_ref, sin_ref, gq_ref, gk_ref,
                 gsum_ref, pswap_ref, bias_ref, ones_ref, out_ref, qs, ks, vs, o_scr, l_scr, *, seq):
    lane = lax.broadcasted_iota(jnp.int32, (1, LANES), 1)
    low_half = lane < HEAD_DIM
    pad = BAND_HALF
    n_blocks = seq // Q_BLOCK

    def split_dot(t, w_ref):
        hi = t.astype(BF16)
        lo = (t - hi.astype(F32)).astype(BF16)
        w = w_ref[...]
        return (jnp.dot(hi, w, preferred_element_type=F32)
                + jnp.dot(lo, w, preferred_element_type=F32))

    def norm_rope(t, gain, cos, sin):
        ms = split_dot(t * t, gsum_ref) * (1.0 / HEAD_DIM)
        tn = t * lax.rsqrt(ms + EPS) * gain
        return tn * cos + split_dot(tn, pswap_ref) * sin

    for gi, (q_ref, k_ref, v_ref) in enumerate(((q0, k0, v0), (q1, k1, v1), (q2, k2, v2))):
        d = DILATIONS[gi]
        length = seq // d
        nblk = length // Q_BLOCK
        region = length + pad

        zpad = jnp.zeros((pad, LANES), BF16)
        for r in range(d + 1):
            ks[r * region:r * region + pad, :] = zpad
            vs[0, r * region:r * region + pad, :] = zpad
            vs[1, r * region:r * region + pad, :] = zpad

        def split_index(b, nblk=nblk):
            if nblk == 1:
                return b, 0
            r = b // nblk
            return r, b - r * nblk

        def table_rows(r, n, d=d):
            if d == 1:
                return pl.ds(pl.multiple_of(n * Q_BLOCK, Q_BLOCK), Q_BLOCK)
            return pl.ds(r + d * n * Q_BLOCK, Q_BLOCK, stride=d)

        def prep(b, q_ref=q_ref, k_ref=k_ref, v_ref=v_ref, split_index=split_index,
                 table_rows=table_rows):
            r, n = split_index(b)
            src = pl.ds(pl.multiple_of(n * Q_BLOCK, Q_BLOCK), Q_BLOCK)
            rows = table_rows(r, n)
            cos = cos_ref[0, rows, :]
            sin = sin_ref[0, rows, :]
            qn = norm_rope(q_ref[0, r, src, :].astype(F32), gq_ref[...], cos, sin) * (HEAD_DIM ** -0.5)
            kn = norm_rope(k_ref[0, r, src, :].astype(F32), gk_ref[...], cos, sin)
            vv = v_ref[0, r, src, :]
            qdst = pl.ds(pl.multiple_of(b * Q_BLOCK, Q_BLOCK), Q_BLOCK)
            kdst = pl.ds(pl.multiple_of(b * Q_BLOCK + (r + 1) * pad, pad), Q_BLOCK)
            qs[0, qdst, :] = jnp.where(low_half, qn, 0.0).astype(BF16)
            qs[1, qdst, :] = jnp.where(low_half, 0.0, qn).astype(BF16)
            ks[kdst, :] = kn.astype(BF16)
            vs[0, kdst, :] = jnp.where(low_half, vv, jnp.zeros_like(vv))
            vs[1, kdst, :] = jnp.where(low_half, jnp.zeros_like(vv), vv)

        def block(b, gi=gi, nblk=nblk, split_index=split_index, table_rows=table_rows):
            r, n = split_index(b)
            if nblk == 1:
                bias = bias_ref[3]
            else:
                variant = jnp.where(n == 0, 1, 0) + jnp.where(n == nblk - 1, 2, 0)
                bias = bias_ref[variant]
            qsrc = pl.ds(pl.multiple_of(b * Q_BLOCK, Q_BLOCK), Q_BLOCK)
            ksrc = pl.ds(pl.multiple_of(b * Q_BLOCK + r * pad, pad), ATTN_WIN)
            kw = ks[ksrc, :]
            acc_o = None
            acc_d = None
            ms = []
            for hh in range(2):
                s = lax.dot_general(qs[hh, qsrc, :], kw, (((1,), (1,)), ((), ())),
                                    preferred_element_type=F32) + bias
                m = jnp.max(s, axis=-1, keepdims=True)
                p = jnp.exp(s - m).astype(BF16)
                po = jnp.dot(p, vs[hh, ksrc, :], preferred_element_type=F32)
                pd = jnp.dot(p, ones_ref[hh], preferred_element_type=F32)
                acc_o = po if acc_o is None else acc_o + po
                acc_d = pd if acc_d is None else acc_d + pd
                ms.append(m)
            m128 = jnp.where(low_half, ms[0], ms[1])
            dst = table_rows(r, n)
            o_scr[gi, dst, :] = acc_o / acc_d
            l_scr[gi, dst, :] = m128 + jnp.log(acc_d)

        def unrolled(fn):
            def body(it, carry):
                for u in range(ATTN_UNROLL):
                    fn(it * ATTN_UNROLL + u)
                return carry
            lax.fori_loop(0, n_blocks // ATTN_UNROLL, body, 0)

        unrolled(prep)
        unrolled(block)

    l0 = l_scr[0]
    l1 = l_scr[1]
    l2 = l_scr[2]
    m = jnp.maximum(jnp.maximum(l0, l1), l2)
    e0 = jnp.exp(l0 - m)
    e1 = jnp.exp(l1 - m)
    e2 = jnp.exp(l2 - m)
    mix = (e0 * o_scr[0] + e1 * o_scr[1] + e2 * o_scr[2]) / (e0 + e1 + e2)
    out_ref[0] = mix.astype(BF16)


def _attention(qkv, cos_t, sin_t, gq, gk):
    b, _, s, _ = qkv[0].shape
    in_specs = []
    args = []
    for gi, dil in enumerate(DILATIONS):
        for part in range(3):
            in_specs.append(pl.BlockSpec((1, dil, s // dil, LANES),
                                         lambda bi, hp, part=part: (bi, 0, 0, 2 * part + hp)))
            args.append(qkv[gi])
    consts = _attn_consts()
    tab_spec = pl.BlockSpec((1, s, LANES), lambda bi, hp: (bi, 0, 0))
    gain_spec = pl.BlockSpec((1, LANES), lambda bi, hp: (0, 0))
    const_specs = [pl.BlockSpec(c.shape, lambda bi, hp, nd=c.ndim: (0,) * nd) for c in consts]
    key_rows = max(d * (s // d + BAND_HALF) for d in DILATIONS) + BAND_HALF
    return pl.pallas_call(
        functools.partial(_attn_kernel, seq=s),
        grid=(b, 2),
        in_specs=in_specs + [tab_spec] * 2 + [gain_spec] * 2 + const_specs,
        out_specs=pl.BlockSpec((1, s, LANES), lambda bi, hp: (bi, 0, hp)),
        out_shape=jax.ShapeDtypeStruct((b, s, GROUP_W), BF16),
        scratch_shapes=[
            pltpu.VMEM((2, s, LANES), BF16),
            pltpu.VMEM((key_rows, LANES), BF16),
            pltpu.VMEM((2, key_rows, LANES), BF16),
            pltpu.VMEM((3, s, LANES), F32),
            pltpu.VMEM((3, s, LANES), F32),
        ],
        compiler_params=pltpu.CompilerParams(
            dimension_semantics=("parallel", "parallel"), vmem_limit_bytes=VMEM_LIMIT_BYTES),
        name="dilated_attn",
    )(*args, cos_t, sin_t, gq, gk, *consts)


def _merge_kernel(x_ref, oa_ref, u_ref, uprev_ref, unext_ref, cb_ref, qm_ref, km_ref, vm_ref,
                  gates_ref, wconv_ref, wpa_ref, wpc_ref, wpm_ref, wout_ref, gffn_ref, wr_ref, br_ref,
                  x1_ref, route_ref, om_scr, *, tm, nt):
    i = pl.program_id(1)
    u = u_ref[0].astype(F32)
    prev_row = uprev_ref[0].astype(F32)[15:16, :] * (i > 0).astype(F32)
    next_row = unext_ref[0].astype(F32)[0:1, :] * (i < nt - 1).astype(F32)
    row = lax.broadcasted_iota(jnp.int32, (tm, 1), 0)
    um = jnp.where(row == 0, prev_row, pltpu.roll(u, 1, 0))
    up = jnp.where(row == tm - 1, next_row, pltpu.roll(u, tm - 1, 0))
    wc = wconv_ref[...]
    y = wc[0:1, :] * um + wc[1:2, :] * u + wc[2:3, :] * up
    z = (cb_ref[0].astype(F32) * y).astype(BF16)

    for hh in range(MEM_HEADS):
        sl = slice(hh * MEM_HEAD_DIM, (hh + 1) * MEM_HEAD_DIM)
        s = lax.dot_general(qm_ref[0, :, sl], km_ref[0, :, sl], (((1,), (1,)), ((), ())),
                            preferred_element_type=F32) * (MEM_HEAD_DIM ** -0.5)
        m = jnp.max(s, axis=-1, keepdims=True)
        p = jnp.exp(s - m)
        den = jnp.sum(p, axis=-1, keepdims=True)
        o = jnp.dot(p.astype(BF16), vm_ref[0, :, sl], preferred_element_type=F32) / den
        om_scr[:, sl] = o.astype(BF16)

    pa = jnp.dot(oa_ref[0], wpa_ref[...], preferred_element_type=F32)
    pc = jnp.dot(z, wpc_ref[...], preferred_element_type=F32)
    pm = jnp.dot(om_scr[...], wpm_ref[...], preferred_element_type=F32)
    merged = (gates_ref[0, :, 0:D_MODEL].astype(F32) * pa
              + gates_ref[0, :, D_MODEL:2 * D_MODEL].astype(F32) * pc
              + gates_ref[0, :, 2 * D_MODEL:3 * D_MODEL].astype(F32) * pm)
    x1 = x_ref[0] + jnp.dot(merged.astype(BF16), wout_ref[...], preferred_element_type=F32)
    for c in range(ROW_CHUNKS):
        x1_ref[pl.ds(c, tm, stride=ROW_CHUNKS), :] = x1[:, c * LANES:(c + 1) * LANES]

    h2 = _rms(x1, gffn_ref[...])
    h_hi = h2.astype(BF16)
    h_lo = (h2 - h_hi.astype(F32)).astype(BF16)
    lg = (jnp.dot(h_hi, wr_ref[0], preferred_element_type=F32)
          + jnp.dot(h_lo, wr_ref[0], preferred_element_type=F32)
          + jnp.dot(h_hi, wr_ref[1], preferred_element_type=F32)) + br_ref[...]

    lane = lax.broadcasted_iota(jnp.int32, (tm, LANES), 1).astype(F32)
    far = float(LANES)
    is_group = lane < N_EXPERT_GROUPS
    gl = jnp.where(is_group, lg, ROUTE_NEG)
    mg = jnp.max(gl, axis=-1, keepdims=True)
    gidx = jnp.min(jnp.where(gl == mg, lane, far), axis=-1, keepdims=True)
    pg_top = 1.0 / jnp.sum(jnp.where(is_group, jnp.exp(lg - mg), 0.0), axis=-1, keepdims=True)
    first = N_EXPERT_GROUPS + EXPERTS_PER_GROUP * gidx
    in_sel = jnp.logical_and(lane >= first, lane < first + EXPERTS_PER_GROUP)
    sel = jnp.where(in_sel, lg, ROUTE_NEG)
    m1 = jnp.max(sel, axis=-1, keepdims=True)
    i1 = jnp.min(jnp.where(jnp.logical_and(in_sel, sel == m1), lane, far), axis=-1, keepdims=True)
    keep = jnp.logical_and(in_sel, lane != i1)
    sel2 = jnp.where(keep, lg, ROUTE_NEG)
    m2 = jnp.max(sel2, axis=-1, keepdims=True)
    i2 = jnp.min(jnp.where(jnp.logical_and(keep, sel2 == m2), lane, far), axis=-1, keepdims=True)
    r = jnp.exp(m2 - m1)
    w1 = pg_top / (1.0 + r)
    w2 = w1 * r
    route = jnp.where(lane == 0, i1 - N_EXPERT_GROUPS,
                      jnp.where(lane == 1, i2 - N_EXPERT_GROUPS,
                                jnp.where(lane == 2, w1, jnp.where(lane == 3, w2, 0.0))))
    route_ref[...] = jnp.transpose(route)[0:8, :]


def _merge(x, o_attn, u, cb, qm, km, vm, gates, w_conv, wpa, wpc, wpm, wout, g_ffn, w_r, b_r):
    b, s, d = x.shape
    tm = ROW_TILE
    nt = s // tm
    halo = 16
    hb = tm // halo

    def tile(width):
        return pl.BlockSpec((1, tm, width), lambda bi, i: (bi, i, 0))

    def whole(arr):
        return pl.BlockSpec(arr.shape, lambda bi, i: (0,) * arr.ndim)

    def per_batch(arr):
        return pl.BlockSpec((1,) + arr.shape[1:], lambda bi, i: (bi, 0, 0))

    in_specs = [
        tile(d), tile(GROUP_W), tile(CONV_WIDTH),
        pl.BlockSpec((1, halo, CONV_WIDTH), lambda bi, i: (bi, jnp.maximum(i * hb - 1, 0), 0)),
        pl.BlockSpec((1, halo, CONV_WIDTH),
                     lambda bi, i: (bi, jnp.minimum((i + 1) * hb, s // halo - 1), 0)),
        tile(CONV_WIDTH), tile(MEM_WIDTH), per_batch(km), per_batch(vm), tile(N_BRANCH * d),
        whole(w_conv), whole(wpa), whole(wpc), whole(wpm), whole(wout), whole(g_ffn),
        whole(w_r), whole(b_r),
    ]
    return pl.pallas_call(
        functools.partial(_merge_kernel, tm=tm, nt=nt),
        grid=(b, nt),
        in_specs=in_specs,
        out_specs=[pl.BlockSpec((tm * ROW_CHUNKS, LANES), lambda bi, i: (bi * nt + i, 0)),
                   pl.BlockSpec((ROUTE_ROWS, tm), lambda bi, i: (0, bi * nt + i))],
        out_shape=[jax.ShapeDtypeStruct((b * s * ROW_CHUNKS, LANES), F32),
                   jax.ShapeDtypeStruct((ROUTE_ROWS, b * s), F32)],
        scratch_shapes=[pltpu.VMEM((tm, MEM_WIDTH), BF16)],
        compiler_params=pltpu.CompilerParams(
            dimension_semantics=("parallel", "parallel"), vmem_limit_bytes=VMEM_LIMIT_BYTES),
        name="merge",
    )(x, o_attn, u, u, u, cb, qm, km, vm, gates, w_conv, wpa, wpc, wpm, wout, g_ffn, w_r, b_r)


DRAIN_STEPS = 2


def _expert_kernel(blk_exp_ref, nvalid_ref, tok_ref, tok_next_ref, dst_ref, wt_ref, x1_hbm, gffn_ref,
                   wg_ref, wu_ref, wd_ref, y_hbm, xbuf, hbuf, ybuf, wg_s, wu_s, wd_s, gsem, ssem):
    i = pl.program_id(0)
    slot = lax.rem(i, 2)
    rows = MOE_BLOCK * ROW_CHUNKS

    def start_gather(idx_ref, s):
        def body(it, c):
            for u in range(GATHER_UNROLL):
                j = it * GATHER_UNROLL + u
                src_row = pl.multiple_of(idx_ref[0, 0, j], ROW_CHUNKS)
                pltpu.make_async_copy(x1_hbm.at[pl.ds(src_row, ROW_CHUNKS), :],
                                      xbuf.at[s, pl.ds(j * ROW_CHUNKS, ROW_CHUNKS), :],
                                      gsem.at[s]).start()
            return c
        lax.fori_loop(0, MOE_BLOCK // GATHER_UNROLL, body, 0)

    def wait_gather(s):
        pltpu.make_async_copy(x1_hbm.at[pl.ds(0, rows), :], xbuf.at[s], gsem.at[s]).wait()

    def start_scatter(s):
        def body(it, c):
            for u in range(GATHER_UNROLL):
                j = it * GATHER_UNROLL + u
                dst_row = pl.multiple_of(dst_ref[0, 0, j], ROW_CHUNKS)
                pltpu.make_async_copy(ybuf.at[s, pl.ds(j * ROW_CHUNKS, ROW_CHUNKS), :],
                                      y_hbm.at[pl.ds(dst_row, ROW_CHUNKS), :], ssem.at[s]).start()
            return c
        lax.fori_loop(0, MOE_BLOCK // GATHER_UNROLL, body, 0)

    def wait_scatter(s):
        pltpu.make_async_copy(ybuf.at[s], y_hbm.at[pl.ds(0, rows), :], ssem.at[s]).wait()

    nvalid = nvalid_ref[i]
    prev = jnp.maximum(i - 1, 0)
    prev2 = jnp.maximum(i - 2, 0)

    @pl.when(i == 0)
    def _():
        start_gather(tok_ref, 0)
        ybuf[0] = jnp.zeros((rows, LANES), F32)
        spare = pltpu.make_async_copy(ybuf.at[0], y_hbm.at[pl.ds(y_hbm.shape[0] - rows, rows), :],
                                      ssem.at[0])
        spare.start()
        spare.wait()

    @pl.when(jnp.logical_and(i >= 2, nvalid_ref[prev2] > 0))
    def _():
        wait_scatter(slot)

    @pl.when(jnp.logical_or(i == 0, blk_exp_ref[i] != blk_exp_ref[prev]))
    def _():
        wg_s[...] = wg_ref[0].astype(BF16)
        wu_s[...] = wu_ref[0].astype(BF16)
        wd_s[...] = wd_ref[0].astype(BF16)

    @pl.when(nvalid > 0)
    def _():
        wait_gather(slot)
        start_gather(tok_next_ref, 1 - slot)

        ss = None
        for c in range(ROW_CHUNKS):
            xc = xbuf[slot, pl.ds(c, MOE_BLOCK, stride=ROW_CHUNKS), :]
            ss = xc * xc if ss is None else ss + xc * xc
        scale = lax.rsqrt(jnp.sum(ss, axis=-1, keepdims=True) * (1.0 / D_MODEL) + EPS)
        for c in range(ROW_CHUNKS):
            sl = slice(c * LANES, (c + 1) * LANES)
            xc = xbuf[slot, pl.ds(c, MOE_BLOCK, stride=ROW_CHUNKS), :]
            hbuf[:, sl] = (xc * scale * gffn_ref[:, sl]).astype(BF16)
        h2 = hbuf[...]
        a = jnp.dot(h2, wg_s[...], preferred_element_type=F32)
        up = jnp.dot(h2, wu_s[...], preferred_element_type=F32)
        act = (a * jax.nn.sigmoid(a) * up).astype(BF16)
        y = jnp.dot(act, wd_s[...], preferred_element_type=F32) * wt_ref[0]
        for c in range(ROW_CHUNKS):
            ybuf[slot, pl.ds(c, MOE_BLOCK, stride=ROW_CHUNKS), :] = y[:, c * LANES:(c + 1) * LANES]
        start_scatter(slot)

    @pl.when(jnp.logical_and(nvalid == 0, jnp.logical_and(i > 0, nvalid_ref[prev] > 0)))
    def _():
        wait_gather(slot)


def _experts(x1r, g_ffn, blk_exp, nvalid, row_tok, row_dst, row_w, wg, wu, wd):
    rows_total, _ = x1r.shape
    d = D_MODEL
    n_blk = blk_exp.shape[0]
    idx_spec = lambda fn: pl.BlockSpec((1, 1, MOE_BLOCK), fn, memory_space=pltpu.SMEM)
    grid_spec = pltpu.PrefetchScalarGridSpec(
        num_scalar_prefetch=2,
        grid=(n_blk,),
        in_specs=[
            idx_spec(lambda i, be, nv: (i, 0, 0)),
            idx_spec(lambda i, be, nv: (jnp.minimum(i + 1, n_blk - 1), 0, 0)),
            idx_spec(lambda i, be, nv: (i, 0, 0)),
            pl.BlockSpec((1, MOE_BLOCK, 1), lambda i, be, nv: (i, 0, 0)),
            pl.BlockSpec(memory_space=pl.ANY),
            pl.BlockSpec((1, d), lambda i, be, nv: (0, 0)),
            pl.BlockSpec((1, d, EXPERT_FF), lambda i, be, nv: (be[i], 0, 0)),
            pl.BlockSpec((1, d, EXPERT_FF), lambda i, be, nv: (be[i], 0, 0)),
            pl.BlockSpec((1, EXPERT_FF, d), lambda i, be, nv: (be[i], 0, 0)),
        ],
        out_specs=pl.BlockSpec(memory_space=pl.ANY),
        scratch_shapes=[
            pltpu.VMEM((2, MOE_BLOCK * ROW_CHUNKS, LANES), F32),
            pltpu.VMEM((MOE_BLOCK, d), BF16),
            pltpu.VMEM((2, MOE_BLOCK * ROW_CHUNKS, LANES), F32),
            pltpu.VMEM((d, EXPERT_FF), BF16),
            pltpu.VMEM((d, EXPERT_FF), BF16),
            pltpu.VMEM((EXPERT_FF, d), BF16),
            pltpu.SemaphoreType.DMA((2,)),
            pltpu.SemaphoreType.DMA((2,)),
        ],
    )
    return pl.pallas_call(
        _expert_kernel,
        grid_spec=grid_spec,
        out_shape=jax.ShapeDtypeStruct((TOP_K * rows_total + MOE_BLOCK * ROW_CHUNKS, LANES), F32),
        compiler_params=pltpu.CompilerParams(
            dimension_semantics=("arbitrary",), vmem_limit_bytes=VMEM_LIMIT_BYTES),
        name="experts",
    )(blk_exp, nvalid, row_tok, row_tok, row_dst, row_w, x1r, g_ffn, wg, wu, wd)


def _combine_kernel(x1_ref, y0_ref, y1_ref, o_ref, *, tm):
    for c in range(ROW_CHUNKS):
        rows = pl.ds(c, tm, stride=ROW_CHUNKS)
        o_ref[:, c * LANES:(c + 1) * LANES] = x1_ref[rows, :] + y0_ref[rows, :] + y1_ref[rows, :]


def _combine(x1r, y):
    n = x1r.shape[0] // ROW_CHUNKS
    tm = ROW_TILE
    nt = n // tm
    blk = (tm * ROW_CHUNKS, LANES)
    return pl.pallas_call(
        functools.partial(_combine_kernel, tm=tm),
        grid=(nt,),
        in_specs=[pl.BlockSpec(blk, lambda i: (i, 0)),
                  pl.BlockSpec(blk, lambda i: (i, 0)),
                  pl.BlockSpec(blk, lambda i: (i + nt, 0))],
        out_specs=pl.BlockSpec((tm, D_MODEL), lambda i: (i, 0)),
        out_shape=jax.ShapeDtypeStruct((n, D_MODEL), F32),
        compiler_params=pltpu.CompilerParams(dimension_semantics=("parallel",)),
        name="combine",
    )(x1r, y, y)


def _rope_tables(positions):
    half = ROT_DIM // 2
    dim = np.arange(LANES) % HEAD_DIM
    rotary = dim < ROT_DIM
    inv = ROPE_THETA ** (-jnp.arange(0, ROT_DIM, 2, dtype=F32) / ROT_DIM)
    inv_lane = jnp.where(jnp.asarray(rotary), inv[jnp.asarray(dim % half)], 0.0)
    sign = jnp.asarray(np.where(dim < half, -1.0, 1.0), F32)
    ang = positions.astype(F32)[..., None] * inv_lane
    return jnp.cos(ang), jnp.sin(ang) * sign


def _route(route_t, n):
    nk = n * TOP_K
    e = route_t[0:TOP_K].reshape(nk).astype(jnp.int32)
    w = route_t[TOP_K:2 * TOP_K].reshape(nk)
    a = jnp.arange(nk, dtype=jnp.int32)
    _, a_s = lax.sort((e, a), num_keys=1, is_stable=True)
    counts = jnp.sum((e[:, None] == jnp.arange(N_EXPERTS, dtype=jnp.int32)[None, :]).astype(jnp.int32),
                     axis=0)
    padded = ((counts + MOE_BLOCK - 1) // MOE_BLOCK) * MOE_BLOCK
    pend = jnp.cumsum(padded)
    pstart = pend - padded
    start = jnp.cumsum(counts) - counts
    n_blk = nk // MOE_BLOCK + N_EXPERTS + DRAIN_STEPS
    blk_start = jnp.arange(n_blk, dtype=jnp.int32) * MOE_BLOCK
    blk_exp = jnp.minimum(jnp.sum((pend[None, :] <= blk_start[:, None]).astype(jnp.int32), axis=1),
                          N_EXPERTS - 1).astype(jnp.int32)
    nvalid = jnp.clip(counts[blk_exp] - (blk_start - pstart[blk_exp]), 0, MOE_BLOCK)
    nvalid = jnp.where(blk_start < pend[-1], nvalid, 0).astype(jnp.int32)
    j = jnp.arange(MOE_BLOCK, dtype=jnp.int32)[None, :]
    valid = j < nvalid[:, None]
    src = jnp.clip((start[blk_exp] + blk_start - pstart[blk_exp])[:, None] + j, 0, nk - 1)
    a_p = a_s[src]
    tok = (jnp.where(valid, a_p - n * (a_p >= n).astype(jnp.int32), 0) * ROW_CHUNKS).astype(jnp.int32)
    dst = (jnp.where(valid, a_p, TOP_K * n + j) * ROW_CHUNKS).astype(jnp.int32)
    w_p = jnp.where(valid, w[a_p], 0.0).astype(F32)
    return (blk_exp, nvalid, tok.reshape(n_blk, 1, MOE_BLOCK), dst.reshape(n_blk, 1, MOE_BLOCK),
            w_p.reshape(n_blk, MOE_BLOCK, 1))


def kernel(x, mem, positions, g_mix, g_mem, w_in, g_qn_attn, g_kn_attn, w_conv, w_mem_kv, g_qn_mem,
           g_kn_mem, w_proj_attn, w_proj_conv, w_proj_mem, w_out, g_ffn, w_router_group,
           b_router_group, w_router_expert, b_router_expert, w_gate, w_up, w_down):
    b, s, d = x.shape
    n = b * s
    cos_t, sin_t = _rope_tables(positions)
    for l in range(w_in.shape[0]):
        wl = w_in[l]
        aw = N_BRANCH * GROUP_W
        cols = []
        for gi in range(len(DILATIONS)):
            for part in range(3):
                c0 = part * aw + gi * GROUP_W
                cols.append(wl[:, c0:c0 + GROUP_W])
        cols.append(wl[:, 3 * aw:])
        w_all = jnp.concatenate(cols, axis=1).astype(BF16)

        q0, q1, q2, u, cb, qm, gates = _inproj(x, g_mix[l][None], g_qn_mem[l][None], w_all)
        km, vm = _memkv(mem, g_mem[l][None], g_kn_mem[l][None], w_mem_kv[l].astype(BF16))
        gq = jnp.tile(g_qn_attn[l], 2)[None]
        gk = jnp.tile(g_kn_attn[l], 2)[None]
        o_attn = _attention((q0, q1, q2), cos_t, sin_t, gq, gk)

        w_r = jnp.zeros((d, LANES), F32)
        w_r = w_r.at[:, :N_EXPERT_GROUPS].set(w_router_group[l])
        w_r = w_r.at[:, N_EXPERT_GROUPS:N_EXPERT_GROUPS + N_EXPERTS].set(w_router_expert[l])
        b_r = jnp.zeros((1, LANES), F32)
        b_r = b_r.at[0, :N_EXPERT_GROUPS].set(b_router_group[l])
        b_r = b_r.at[0, N_EXPERT_GROUPS:N_EXPERT_GROUPS + N_EXPERTS].set(b_router_expert[l])
        w_r_hi = w_r.astype(BF16)
        w_r_lo = (w_r - w_r_hi.astype(F32)).astype(BF16)
        x1, route_t = _merge(x, o_attn, u, cb, qm, km, vm, gates, w_conv[l],
                            w_proj_attn[l].astype(BF16), w_proj_conv[l].astype(BF16),
                            w_proj_mem[l].astype(BF16), w_out[l].astype(BF16), g_ffn[l][None],
                            jnp.stack([w_r_hi, w_r_lo]), b_r)

        blk_exp, nvalid, row_tok, row_dst, row_w = _route(route_t, n)
        y = _experts(x1, g_ffn[l][None], blk_exp, nvalid, row_tok, row_dst, row_w,
                     w_gate[l], w_up[l], w_down[l])
        x = _combine(x1, y).reshape(b, s, d)
    return x
```

```python
import functools

import numpy as np
import jax
import jax.numpy as jnp
from jax import lax
from jax.experimental import pallas as pl
from jax.experimental.pallas import tpu as pltpu

F32 = jnp.float32
BF16 = jnp.bfloat16

D_MODEL = 1024
EPS = 1e-6
HEAD_DIM = 64
DILATIONS = (1, 4, 16)
BAND_HALF = 64
ATTN_SLOTS = 4
GROUP_W = ATTN_SLOTS * HEAD_DIM
QKV_W = 3 * GROUP_W
ROT_DIM = 16
ROPE_THETA = 500000.0
CONV_WIDTH = 768
MEM_HEADS = 4
MEM_HEAD_DIM = 128
MEM_WIDTH = 512
N_BRANCH = 3
N_EXPERT_GROUPS = 4
EXPERTS_PER_GROUP = 8
N_EXPERTS = 32
TOP_K = 2
EXPERT_FF = 512

LANES = 128
VMEM_LIMIT_BYTES = 56 * 1024 * 1024

ROW_TILE = 512
Q_BLOCK = 128
MOE_BLOCK = 256
ROW_CHUNKS = D_MODEL // LANES
GATHER_UNROLL = 8
NEG_BIG = -1e30
ROUTE_NEG = -3e38
ROUTE_ROWS = 8


def _rms(t, gain):
    return t * lax.rsqrt(jnp.mean(t * t, axis=-1, keepdims=True) + EPS) * gain


def _inproj_kernel(x_ref, g_ref, gqm_ref, w_ref,
                   qkv0_ref, qkv1_ref, qkv2_ref, u_ref, cb_ref, qm_ref, gates_ref,
                   h_scr, hp_scr, *, tm):
    x = x_ref[0]
    h = _rms(x, g_ref[...])
    n_chunks = h.shape[1] // LANES
    for c in range(n_chunks):
        h_scr[c] = h[:, c * LANES:(c + 1) * LANES]
    hb = h.astype(BF16)

    def mm(lhs, c0, c1):
        return jnp.dot(lhs, w_ref[:, c0:c1], preferred_element_type=F32)

    qkv0_ref[0, 0] = mm(hb, 0, QKV_W).astype(BF16)
    for gi, out_ref in ((1, qkv1_ref), (2, qkv2_ref)):
        d = DILATIONS[gi]
        rows = tm // d
        for r in range(d):
            for c in range(n_chunks):
                hp_scr[r * rows:(r + 1) * rows, c * LANES:(c + 1) * LANES] = (
                    h_scr[c, pl.ds(r, rows, stride=d), :].astype(BF16))
        res = mm(hp_scr[...], gi * QKV_W, (gi + 1) * QKV_W)
        out_ref[0] = res.reshape(d, rows, QKV_W).astype(BF16)

    c = 3 * QKV_W
    cx = mm(hb, c, c + CONV_WIDTH)
    cc = mm(hb, c + 2 * CONV_WIDTH, c + 3 * CONV_WIDTH)
    u_ref[0] = (cc * cx).astype(BF16)
    cb_ref[0] = mm(hb, c + CONV_WIDTH, c + 2 * CONV_WIDTH).astype(BF16)

    c += 3 * CONV_WIDTH
    qm = mm(hb, c, c + MEM_WIDTH)
    for hh in range(MEM_HEADS):
        sl = slice(hh * MEM_HEAD_DIM, (hh + 1) * MEM_HEAD_DIM)
        qm_ref[0, :, sl] = _rms(qm[:, sl], gqm_ref[...]).astype(BF16)

    c += MEM_WIDTH
    for j in range(N_BRANCH):
        gl = mm(hb, c + j * D_MODEL, c + (j + 1) * D_MODEL)
        gates_ref[0, :, j * D_MODEL:(j + 1) * D_MODEL] = jax.nn.sigmoid(gl).astype(BF16)


def _inproj(x, g_mix, g_qn_mem, w_all):
    b, s, d = x.shape
    tm = ROW_TILE
    nt = s // tm
    outs = []
    out_specs = []
    for dil in DILATIONS:
        outs.append(jax.ShapeDtypeStruct((b, dil, s // dil, QKV_W), BF16))
        out_specs.append(pl.BlockSpec((1, dil, tm // dil, QKV_W), lambda bi, i: (bi, 0, i, 0)))
    for width in (CONV_WIDTH, CONV_WIDTH, MEM_WIDTH, N_BRANCH * D_MODEL):
        outs.append(jax.ShapeDtypeStruct((b, s, width), BF16))
        out_specs.append(pl.BlockSpec((1, tm, width), lambda bi, i: (bi, i, 0)))
    return pl.pallas_call(
        functools.partial(_inproj_kernel, tm=tm),
        grid=(b, nt),
        in_specs=[
            pl.BlockSpec((1, tm, d), lambda bi, i: (bi, i, 0)),
            pl.BlockSpec((1, d), lambda bi, i: (0, 0)),
            pl.BlockSpec((1, MEM_HEAD_DIM), lambda bi, i: (0, 0)),
            pl.BlockSpec(w_all.shape, lambda bi, i: (0, 0), pipeline_mode=pl.Buffered(1)),
        ],
        out_specs=out_specs,
        out_shape=outs,
        scratch_shapes=[pltpu.VMEM((d // LANES, tm, LANES), F32), pltpu.VMEM((tm, d), BF16)],
        compiler_params=pltpu.CompilerParams(
            dimension_semantics=("parallel", "parallel"), vmem_limit_bytes=VMEM_LIMIT_BYTES),
        name="inproj",
    )(x, g_mix, g_qn_mem, w_all)


def _memkv_kernel(mem_ref, g_ref, gk_ref, w_ref, k_ref, v_ref):
    h = _rms(mem_ref[0], g_ref[...]).astype(BF16)
    kv = jnp.dot(h, w_ref[...], preferred_element_type=F32)
    for hh in range(MEM_HEADS):
        sl = slice(hh * MEM_HEAD_DIM, (hh + 1) * MEM_HEAD_DIM)
        k_ref[0, :, sl] = _rms(kv[:, sl], gk_ref[...]).astype(BF16)
    v_ref[0] = kv[:, MEM_WIDTH:].astype(BF16)


def _memkv(mem, g_mem, g_kn_mem, w_kv):
    b, m, d = mem.shape
    return pl.pallas_call(
        _memkv_kernel,
        grid=(b,),
        in_specs=[
            pl.BlockSpec((1, m, d), lambda bi: (bi, 0, 0)),
            pl.BlockSpec((1, d), lambda bi: (0, 0)),
            pl.BlockSpec((1, MEM_HEAD_DIM), lambda bi: (0, 0)),
            pl.BlockSpec(w_kv.shape, lambda bi: (0, 0)),
        ],
        out_specs=[pl.BlockSpec((1, m, MEM_WIDTH), lambda bi: (bi, 0, 0))] * 2,
        out_shape=[jax.ShapeDtypeStruct((b, m, MEM_WIDTH), BF16)] * 2,
        compiler_params=pltpu.CompilerParams(dimension_semantics=("parallel",)),
        name="memkv",
    )(mem, g_mem, g_kn_mem, w_kv)


ATTN_WIN = Q_BLOCK + 2 * BAND_HALF
ATTN_UNROLL = 16
DEN_SAFE = 1e-30


def _attn_consts():
    lane = np.arange(LANES)
    gsum = (lane[:, None] // HEAD_DIM == lane[None, :] // HEAD_DIM).astype(np.float32)
    half = ROT_DIM // 2
    dim = lane % HEAD_DIM
    src = np.where(dim < half, lane + half, np.where(dim < ROT_DIM, lane - half, -1))
    pswap = (lane[:, None] == src[None, :]).astype(np.float32)
    i = np.arange(Q_BLOCK)[:, None]
    c = np.arange(ATTN_WIN)[None, :]
    band = (c >= i) & (c <= i + 2 * BAND_HALF)
    first = c >= BAND_HALF
    last = c < Q_BLOCK + BAND_HALF
    variants = [band, band & first, band & last, band & first & last]
    bias = np.stack([np.where(v, 0.0, NEG_BIG) for v in variants]).astype(np.float32)
    return jnp.asarray(gsum, BF16), jnp.asarray(pswap, BF16), jnp.asarray(bias, F32)


def _attn_kernel(q0, k0, v0, q1, k1, v1, q2, k2, v2, cos_ref, sin_ref, gq_ref, gk_ref,
                 gsum_ref, pswap_ref, bias_ref, out_ref, qs, ks, vs, o_scr, l_scr, bias_scr, *, seq):
    lane = lax.broadcasted_iota(jnp.int32, (1, LANES), 1)
    low_half = lane < HEAD_DIM
    pad = BAND_HALF
    n_blocks = seq // Q_BLOCK

    bound = HEAD_DIM ** 0.5 * jnp.max(jnp.abs(gq_ref[...])) * jnp.max(jnp.abs(gk_ref[...]))
    shifted = bias_ref[...] - bound
    bias_scr[:, 0:Q_BLOCK, :] = shifted
    bias_scr[:, Q_BLOCK:2 * Q_BLOCK, :] = shifted

    def norm_rope(t, gain, cos, sin):
        ms = jnp.dot((t * t).astype(BF16), gsum_ref[...], preferred_element_type=F32) * (1.0 / HEAD_DIM)
        tn = t * lax.rsqrt(ms + EPS) * gain
        partner = jnp.dot(tn.astype(BF16), pswap_ref[...], preferred_element_type=F32)
        return tn * cos + partner * sin

    for gi, (q_ref, k_ref, v_ref) in enumerate(((q0, k0, v0), (q1, k1, v1), (q2, k2, v2))):
        d = DILATIONS[gi]
        length = seq // d
        nblk = length // Q_BLOCK
        region = length + pad

        zpad = jnp.zeros((pad, LANES), BF16)
        for r in range(d + 1):
            ks[r * region:r * region + pad, :] = zpad
            vs[r * region:r * region + pad, :] = zpad

        def split_index(b, nblk=nblk):
            if nblk == 1:
                return b, 0
            r = b // nblk
            return r, b - r * nblk

        def table_rows(r, n, d=d):
            if d == 1:
                return pl.ds(pl.multiple_of(n * Q_BLOCK, Q_BLOCK), Q_BLOCK)
            return pl.ds(r + d * n * Q_BLOCK, Q_BLOCK, stride=d)

        def prep(b, q_ref=q_ref, k_ref=k_ref, v_ref=v_ref, split_index=split_index,
                 table_rows=table_rows):
            r, n = split_index(b)
            src = pl.ds(pl.multiple_of(n * Q_BLOCK, Q_BLOCK), Q_BLOCK)
            rows = table_rows(r, n)
            cos = cos_ref[0, rows, :]
            sin = sin_ref[0, rows, :]
            qn = norm_rope(q_ref[0, r, src, :].astype(F32), gq_ref[...], cos, sin) * (HEAD_DIM ** -0.5)
            kn = norm_rope(k_ref[0, r, src, :].astype(F32), gk_ref[...], cos, sin)
            qbase = pl.multiple_of(b * 2 * Q_BLOCK, 2 * Q_BLOCK)
            qs[pl.ds(qbase, Q_BLOCK), :] = jnp.where(low_half, qn, 0.0).astype(BF16)
            qs[pl.ds(qbase + Q_BLOCK, Q_BLOCK), :] = jnp.where(low_half, 0.0, qn).astype(BF16)
            kdst = pl.ds(pl.multiple_of(b * Q_BLOCK + (r + 1) * pad, pad), Q_BLOCK)
            ks[kdst, :] = kn.astype(BF16)
            vs[kdst, :] = v_ref[0, r, src, :]

        def block(b, exact, gi=gi, nblk=nblk, split_index=split_index, table_rows=table_rows):
            r, n = split_index(b)
            variant = 3 if nblk == 1 else jnp.where(n == 0, 1, 0) + jnp.where(n == nblk - 1, 2, 0)
            qsrc = pl.ds(pl.multiple_of(b * 2 * Q_BLOCK, 2 * Q_BLOCK), 2 * Q_BLOCK)
            ksrc = pl.ds(pl.multiple_of(b * Q_BLOCK + r * pad, pad), ATTN_WIN)
            s = lax.dot_general(qs[qsrc, :], ks[ksrc, :], (((1,), (1,)), ((), ())),
                                preferred_element_type=F32) + bias_scr[variant]
            if exact:
                m = jnp.max(s, axis=-1, keepdims=True)
                s = s - m
                shift = jnp.where(low_half, m[0:Q_BLOCK], m[Q_BLOCK:]) + bound
            else:
                shift = bound
            p = jnp.exp(s)
            den = jnp.sum(p, axis=-1, keepdims=True)
            o2 = jnp.dot(p.astype(BF16), vs[ksrc, :], preferred_element_type=F32)
            o = jnp.where(low_half, o2[0:Q_BLOCK], o2[Q_BLOCK:])
            den = jnp.where(low_half, den[0:Q_BLOCK], den[Q_BLOCK:])
            dst = table_rows(r, n)
            o_scr[gi, dst, :] = o / den
            l_scr[gi, dst, :] = shift + jnp.log(den)
            return den

        def unrolled(fn):
            def body(it, carry):
                for u in range(ATTN_UNROLL):
                    fn(it * ATTN_UNROLL + u)
                return carry
            lax.fori_loop(0, n_blocks // ATTN_UNROLL, body, 0)

        unrolled(prep)

        def fast_body(it, dmin, block=block):
            for u in range(ATTN_UNROLL):
                dmin = jnp.minimum(dmin, block(it * ATTN_UNROLL + u, False))
            return dmin
        dmin = lax.fori_loop(0, n_blocks // ATTN_UNROLL, fast_body,
                             jnp.full((Q_BLOCK, LANES), 1.0, F32))

        @pl.when(jnp.logical_not(jnp.min(dmin) > DEN_SAFE))
        def _(block=block, unrolled=unrolled):
            unrolled(lambda b: block(b, True))

    l0 = l_scr[0]
    l1 = l_scr[1]
    l2 = l_scr[2]
    m = jnp.maximum(jnp.maximum(l0, l1), l2)
    e0 = jnp.exp(l0 - m)
    e1 = jnp.exp(l1 - m)
    e2 = jnp.exp(l2 - m)
    mix = (e0 * o_scr[0] + e1 * o_scr[1] + e2 * o_scr[2]) / (e0 + e1 + e2)
    out_ref[0] = mix.astype(BF16)


def _attention(qkv, cos_t, sin_t, gq, gk):
    b, _, s, _ = qkv[0].shape
    in_specs = []
    args = []
    for gi, dil in enumerate(DILATIONS):
        for part in range(3):
            in_specs.append(pl.BlockSpec((1, dil, s // dil, LANES),
                                         lambda bi, hp, part=part: (bi, 0, 0, 2 * part + hp)))
            args.append(qkv[gi])
    consts = _attn_consts()
    tab_spec = pl.BlockSpec((1, s, LANES), lambda bi, hp: (bi, 0, 0))
    gain_spec = pl.BlockSpec((1, LANES), lambda bi, hp: (0, 0))
    const_specs = [pl.BlockSpec(c.shape, lambda bi, hp, nd=c.ndim: (0,) * nd) for c in consts]
    key_rows = max(d * (s // d + BAND_HALF) for d in DILATIONS) + BAND_HALF
    return pl.pallas_call(
        functools.partial(_attn_kernel, seq=s),
        grid=(b, 2),
        in_specs=in_specs + [tab_spec] * 2 + [gain_spec] * 2 + const_specs,
        out_specs=pl.BlockSpec((1, s, LANES), lambda bi, hp: (bi, 0, hp)),
        out_shape=jax.ShapeDtypeStruct((b, s, GROUP_W), BF16),
        scratch_shapes=[
            pltpu.VMEM((2 * s, LANES), BF16),
            pltpu.VMEM((key_rows, LANES), BF16),
            pltpu.VMEM((key_rows, LANES), BF16),
            pltpu.VMEM((3, s, LANES), F32),
            pltpu.VMEM((3, s, LANES), F32),
            pltpu.VMEM((4, 2 * Q_BLOCK, ATTN_WIN), F32),
        ],
        compiler_params=pltpu.CompilerParams(
            dimension_semantics=("parallel", "parallel"), vmem_limit_bytes=VMEM_LIMIT_BYTES),
        name="dilated_attn",
    )(*args, cos_t, sin_t, gq, gk, *consts)


def _merge_kernel(x_ref, oa_ref, u_ref, uprev_ref, unext_ref, cb_ref, qm_ref, km_ref, vm_ref,
                  gates_ref, wconv_ref, wpa_ref, wpc_ref, wpm_ref, wout_ref, gffn_ref, wr_ref, br_ref,
                  x1_ref, route_ref, om_scr, *, tm, nt):
    i = pl.program_id(1)
    u = u_ref[0].astype(F32)
    prev_row = uprev_ref[0].astype(F32)[15:16, :] * (i > 0).astype(F32)
    next_row = unext_ref[0].astype(F32)[0:1, :] * (i < nt - 1).astype(F32)
    row = lax.broadcasted_iota(jnp.int32, (tm, 1), 0)
    um = jnp.where(row == 0, prev_row, pltpu.roll(u, 1, 0))
    up = jnp.where(row == tm - 1, next_row, pltpu.roll(u, tm - 1, 0))
    wc = wconv_ref[...]
    y = wc[0:1, :] * um + wc[1:2, :] * u + wc[2:3, :] * up
    z = (cb_ref[0].astype(F32) * y).astype(BF16)

    for hh in range(MEM_HEADS):
        sl = slice(hh * MEM_HEAD_DIM, (hh + 1) * MEM_HEAD_DIM)
        s = lax.dot_general(qm_ref[0, :, sl], km_ref[0, :, sl], (((1,), (1,)), ((), ())),
                            preferred_element_type=F32) * (MEM_HEAD_DIM ** -0.5)
        m = jnp.max(s, axis=-1, keepdims=True)
        p = jnp.exp(s - m)
        den = jnp.sum(p, axis=-1, keepdims=True)
        o = jnp.dot(p.astype(BF16), vm_ref[0, :, sl], preferred_element_type=F32) / den
        om_scr[:, sl] = o.astype(BF16)

    pa = jnp.dot(oa_ref[0], wpa_ref[...], preferred_element_type=F32)
    pc = jnp.dot(z, wpc_ref[...], preferred_element_type=F32)
    pm = jnp.dot(om_scr[...], wpm_ref[...], preferred_element_type=F32)
    merged = (gates_ref[0, :, 0:D_MODEL].astype(F32) * pa
              + gates_ref[0, :, D_MODEL:2 * D_MODEL].astype(F32) * pc
              + gates_ref[0, :, 2 * D_MODEL:3 * D_MODEL].astype(F32) * pm)
    x1 = x_ref[0] + jnp.dot(merged.astype(BF16), wout_ref[...], preferred_element_type=F32)
    for c in range(ROW_CHUNKS):
        x1_ref[pl.ds(c, tm, stride=ROW_CHUNKS), :] = x1[:, c * LANES:(c + 1) * LANES]

    h2 = _rms(x1, gffn_ref[...])
    h_hi = h2.astype(BF16)
    h_lo = (h2 - h_hi.astype(F32)).astype(BF16)
    lg = (jnp.dot(h_hi, wr_ref[0], preferred_element_type=F32)
          + jnp.dot(h_lo, wr_ref[0], preferred_element_type=F32)
          + jnp.dot(h_hi, wr_ref[1], preferred_element_type=F32)) + br_ref[...]

    lane = lax.broadcasted_iota(jnp.int32, (tm, LANES), 1).astype(F32)
    far = float(LANES)
    is_group = lane < N_EXPERT_GROUPS
    gl = jnp.where(is_group, lg, ROUTE_NEG)
    mg = jnp.max(gl, axis=-1, keepdims=True)
    gidx = jnp.min(jnp.where(gl == mg, lane, far), axis=-1, keepdims=True)
    pg_top = 1.0 / jnp.sum(jnp.where(is_group, jnp.exp(lg - mg), 0.0), axis=-1, keepdims=True)
    first = N_EXPERT_GROUPS + EXPERTS_PER_GROUP * gidx
    in_sel = jnp.logical_and(lane >= first, lane < first + EXPERTS_PER_GROUP)
    sel = jnp.where(in_sel, lg, ROUTE_NEG)
    m1 = jnp.max(sel, axis=-1, keepdims=True)
    i1 = jnp.min(jnp.where(jnp.logical_and(in_sel, sel == m1), lane, far), axis=-1, keepdims=True)
    keep = jnp.logical_and(in_sel, lane != i1)
    sel2 = jnp.where(keep, lg, ROUTE_NEG)
    m2 = jnp.max(sel2, axis=-1, keepdims=True)
    i2 = jnp.min(jnp.where(jnp.logical_and(keep, sel2 == m2), lane, far), axis=-1, keepdims=True)
    r = jnp.exp(m2 - m1)
    w1 = pg_top / (1.0 + r)
    w2 = w1 * r
    route = jnp.where(lane == 0, i1 - N_EXPERT_GROUPS,
                      jnp.where(lane == 1, i2 - N_EXPERT_GROUPS,
                                jnp.where(lane == 2, w1, jnp.where(lane == 3, w2, 0.0))))
    route_ref[...] = jnp.transpose(route)[0:8, :]


def _merge(x, o_attn, u, cb, qm, km, vm, gates, w_conv, wpa, wpc, wpm, wout, g_ffn, w_r, b_r):
    b, s, d = x.shape
    tm = ROW_TILE
    nt = s // tm
    halo = 16
    hb = tm // halo

    def tile(width):
        return pl.BlockSpec((1, tm, width), lambda bi, i: (bi, i, 0))

    def whole(arr):
        return pl.BlockSpec(arr.shape, lambda bi, i: (0,) * arr.ndim)

    def per_batch(arr):
        return pl.BlockSpec((1,) + arr.shape[1:], lambda bi, i: (bi, 0, 0))

    in_specs = [
        tile(d), tile(GROUP_W), tile(CONV_WIDTH),
        pl.BlockSpec((1, halo, CONV_WIDTH), lambda bi, i: (bi, jnp.maximum(i * hb - 1, 0), 0)),
        pl.BlockSpec((1, halo, CONV_WIDTH),
                     lambda bi, i: (bi, jnp.minimum((i + 1) * hb, s // halo - 1), 0)),
        tile(CONV_WIDTH), tile(MEM_WIDTH), per_batch(km), per_batch(vm), tile(N_BRANCH * d),
        whole(w_conv), whole(wpa), whole(wpc), whole(wpm), whole(wout), whole(g_ffn),
        whole(w_r), whole(b_r),
    ]
    return pl.pallas_call(
        functools.partial(_merge_kernel, tm=tm, nt=nt),
        grid=(b, nt),
        in_specs=in_specs,
        out_specs=[pl.BlockSpec((tm * ROW_CHUNKS, LANES), lambda bi, i: (bi * nt + i, 0)),
                   pl.BlockSpec((ROUTE_ROWS, tm), lambda bi, i: (0, bi * nt + i))],
        out_shape=[jax.ShapeDtypeStruct((b * s * ROW_CHUNKS, LANES), F32),
                   jax.ShapeDtypeStruct((ROUTE_ROWS, b * s), F32)],
        scratch_shapes=[pltpu.VMEM((tm, MEM_WIDTH), BF16)],
        compiler_params=pltpu.CompilerParams(
            dimension_semantics=("parallel", "parallel"), vmem_limit_bytes=VMEM_LIMIT_BYTES),
        name="merge",
    )(x, o_attn, u, u, u, cb, qm, km, vm, gates, w_conv, wpa, wpc, wpm, wout, g_ffn, w_r, b_r)


DRAIN_STEPS = 2


def _expert_kernel(blk_exp_ref, nvalid_ref, tok_ref, tok_next_ref, dst_ref, wt_ref, x1_hbm, gffn_ref,
                   wg_ref, wu_ref, wd_ref, y_hbm, xbuf, hbuf, ybuf, wg_s, wu_s, wd_s, gsem, ssem):
    i = pl.program_id(0)
    slot = lax.rem(i, 2)
    rows = MOE_BLOCK * ROW_CHUNKS

    def start_gather(idx_ref, s):
        def body(it, c):
            for u in range(GATHER_UNROLL):
                j = it * GATHER_UNROLL + u
                src_row = pl.multiple_of(idx_ref[0, 0, j], ROW_CHUNKS)
                pltpu.make_async_copy(x1_hbm.at[pl.ds(src_row, ROW_CHUNKS), :],
                                      xbuf.at[s, pl.ds(j * ROW_CHUNKS, ROW_CHUNKS), :],
                                      gsem.at[s]).start()
            return c
        lax.fori_loop(0, MOE_BLOCK // GATHER_UNROLL, body, 0)

    def wait_gather(s):
        pltpu.make_async_copy(x1_hbm.at[pl.ds(0, rows), :], xbuf.at[s], gsem.at[s]).wait()

    def start_scatter(s):
        def body(it, c):
            for u in range(GATHER_UNROLL):
                j = it * GATHER_UNROLL + u
                dst_row = pl.multiple_of(dst_ref[0, 0, j], ROW_CHUNKS)
                pltpu.make_async_copy(ybuf.at[s, pl.ds(j * ROW_CHUNKS, ROW_CHUNKS), :],
                                      y_hbm.at[pl.ds(dst_row, ROW_CHUNKS), :], ssem.at[s]).start()
            return c
        lax.fori_loop(0, MOE_BLOCK // GATHER_UNROLL, body, 0)

    def wait_scatter(s):
        pltpu.make_async_copy(ybuf.at[s], y_hbm.at[pl.ds(0, rows), :], ssem.at[s]).wait()

    nvalid = nvalid_ref[i]
    prev = jnp.maximum(i - 1, 0)
    prev2 = jnp.maximum(i - 2, 0)

    @pl.when(i == 0)
    def _():
        start_gather(tok_ref, 0)
        ybuf[0] = jnp.zeros((rows, LANES), F32)
        spare = pltpu.make_async_copy(ybuf.at[0], y_hbm.at[pl.ds(y_hbm.shape[0] - rows, rows), :],
                                      ssem.at[0])
        spare.start()
        spare.wait()

    @pl.when(jnp.logical_and(i >= 2, nvalid_ref[prev2] > 0))
    def _():
        wait_scatter(slot)

    @pl.when(jnp.logical_or(i == 0, blk_exp_ref[i] != blk_exp_ref[prev]))
    def _():
        wg_s[...] = wg_ref[0].astype(BF16)
        wu_s[...] = wu_ref[0].astype(BF16)
        wd_s[...] = wd_ref[0].astype(BF16)

    @pl.when(nvalid > 0)
    def _():
        wait_gather(slot)
        start_gather(tok_next_ref, 1 - slot)

        ss = None
        for c in range(ROW_CHUNKS):
            xc = xbuf[slot, pl.ds(c, MOE_BLOCK, stride=ROW_CHUNKS), :]
            ss = xc * xc if ss is None else ss + xc * xc
        scale = lax.rsqrt(jnp.sum(ss, axis=-1, keepdims=True) * (1.0 / D_MODEL) + EPS)
        for c in range(ROW_CHUNKS):
            sl = slice(c * LANES, (c + 1) * LANES)
            xc = xbuf[slot, pl.ds(c, MOE_BLOCK, stride=ROW_CHUNKS), :]
            hbuf[:, sl] = (xc * scale * gffn_ref[:, sl]).astype(BF16)
        h2 = hbuf[...]
        a = jnp.dot(h2, wg_s[...], preferred_element_type=F32)
        up = jnp.dot(h2, wu_s[...], preferred_element_type=F32)
        act = (a * jax.nn.sigmoid(a) * up).astype(BF16)
        y = jnp.dot(act, wd_s[...], preferred_element_type=F32) * wt_ref[0]
        for c in range(ROW_CHUNKS):
            ybuf[slot, pl.ds(c, MOE_BLOCK, stride=ROW_CHUNKS), :] = y[:, c * LANES:(c + 1) * LANES]
        start_scatter(slot)

    @pl.when(jnp.logical_and(nvalid == 0, jnp.logical_and(i > 0, nvalid_ref[prev] > 0)))
    def _():
        wait_gather(slot)


def _experts(x1r, g_ffn, blk_exp, nvalid, row_tok, row_dst, row_w, wg, wu, wd):
    rows_total, _ = x1r.shape
    d = D_MODEL
    n_blk = blk_exp.shape[0]
    idx_spec = lambda fn: pl.BlockSpec((1, 1, MOE_BLOCK), fn, memory_space=pltpu.SMEM)
    grid_spec = pltpu.PrefetchScalarGridSpec(
        num_scalar_prefetch=2,
        grid=(n_blk,),
        in_specs=[
            idx_spec(lambda i, be, nv: (i, 0, 0)),
            idx_spec(lambda i, be, nv: (jnp.minimum(i + 1, n_blk - 1), 0, 0)),
            idx_spec(lambda i, be, nv: (i, 0, 0)),
            pl.BlockSpec((1, MOE_BLOCK, 1), lambda i, be, nv: (i, 0, 0)),
            pl.BlockSpec(memory_space=pl.ANY),
            pl.BlockSpec((1, d), lambda i, be, nv: (0, 0)),
            pl.BlockSpec((1, d, EXPERT_FF), lambda i, be, nv: (be[i], 0, 0)),
            pl.BlockSpec((1, d, EXPERT_FF), lambda i, be, nv: (be[i], 0, 0)),
            pl.BlockSpec((1, EXPERT_FF, d), lambda i, be, nv: (be[i], 0, 0)),
        ],
        out_specs=pl.BlockSpec(memory_space=pl.ANY),
        scratch_shapes=[
            pltpu.VMEM((2, MOE_BLOCK * ROW_CHUNKS, LANES), F32),
            pltpu.VMEM((MOE_BLOCK, d), BF16),
            pltpu.VMEM((2, MOE_BLOCK * ROW_CHUNKS, LANES), F32),
            pltpu.VMEM((d, EXPERT_FF), BF16),
            pltpu.VMEM((d, EXPERT_FF), BF16),
            pltpu.VMEM((EXPERT_FF, d), BF16),
            pltpu.SemaphoreType.DMA((2,)),
            pltpu.SemaphoreType.DMA((2,)),
        ],
    )
    return pl.pallas_call(
        _expert_kernel,
        grid_spec=grid_spec,
        out_shape=jax.ShapeDtypeStruct((TOP_K * rows_total + MOE_BLOCK * ROW_CHUNKS, LANES), F32),
        compiler_params=pltpu.CompilerParams(
            dimension_semantics=("arbitrary",), vmem_limit_bytes=VMEM_LIMIT_BYTES),
        name="experts",
    )(blk_exp, nvalid, row_tok, row_tok, row_dst, row_w, x1r, g_ffn, wg, wu, wd)


def _combine_kernel(x1_ref, y0_ref, y1_ref, o_ref, *, tm):
    for c in range(ROW_CHUNKS):
        rows = pl.ds(c, tm, stride=ROW_CHUNKS)
        o_ref[:, c * LANES:(c + 1) * LANES] = x1_ref[rows, :] + y0_ref[rows, :] + y1_ref[rows, :]


def _combine(x1r, y):
    n = x1r.shape[0] // ROW_CHUNKS
    tm = ROW_TILE
    nt = n // tm
    blk = (tm * ROW_CHUNKS, LANES)
    return pl.pallas_call(
        functools.partial(_combine_kernel, tm=tm),
        grid=(nt,),
        in_specs=[pl.BlockSpec(blk, lambda i: (i, 0)),
                  pl.BlockSpec(blk, lambda i: (i, 0)),
                  pl.BlockSpec(blk, lambda i: (i + nt, 0))],
        out_specs=pl.BlockSpec((tm, D_MODEL), lambda i: (i, 0)),
        out_shape=jax.ShapeDtypeStruct((n, D_MODEL), F32),
        compiler_params=pltpu.CompilerParams(dimension_semantics=("parallel",)),
        name="combine",
    )(x1r, y, y)


def _rope_tables(positions):
    half = ROT_DIM // 2
    dim = np.arange(LANES) % HEAD_DIM
    rotary = dim < ROT_DIM
    inv = ROPE_THETA ** (-jnp.arange(0, ROT_DIM, 2, dtype=F32) / ROT_DIM)
    inv_lane = jnp.where(jnp.asarray(rotary), inv[jnp.asarray(dim % half)], 0.0)
    sign = jnp.asarray(np.where(dim < half, -1.0, 1.0), F32)
    ang = positions.astype(F32)[..., None] * inv_lane
    return jnp.cos(ang), jnp.sin(ang) * sign


def _route(route_t, n):
    nk = n * TOP_K
    e = route_t[0:TOP_K].reshape(nk).astype(jnp.int32)
    w = route_t[TOP_K:2 * TOP_K].reshape(nk)
    a = jnp.arange(nk, dtype=jnp.int32)
    _, a_s = lax.sort((e, a), num_keys=1, is_stable=True)
    counts = jnp.sum((e[:, None] == jnp.arange(N_EXPERTS, dtype=jnp.int32)[None, :]).astype(jnp.int32),
                     axis=0)
    padded = ((counts + MOE_BLOCK - 1) // MOE_BLOCK) * MOE_BLOCK
    pend = jnp.cumsum(padded)
    pstart = pend - padded
    start = jnp.cumsum(counts) - counts
    n_blk = nk // MOE_BLOCK + N_EXPERTS + DRAIN_STEPS
    blk_start = jnp.arange(n_blk, dtype=jnp.int32) * MOE_BLOCK
    blk_exp = jnp.minimum(jnp.sum((pend[None, :] <= blk_start[:, None]).astype(jnp.int32), axis=1),
                          N_EXPERTS - 1).astype(jnp.int32)
    nvalid = jnp.clip(counts[blk_exp] - (blk_start - pstart[blk_exp]), 0, MOE_BLOCK)
    nvalid = jnp.where(blk_start < pend[-1], nvalid, 0).astype(jnp.int32)
    j = jnp.arange(MOE_BLOCK, dtype=jnp.int32)[None, :]
    valid = j < nvalid[:, None]
    src = jnp.clip((start[blk_exp] + blk_start - pstart[blk_exp])[:, None] + j, 0, nk - 1)
    a_p = a_s[src]
    tok = (jnp.where(valid, a_p - n * (a_p >= n).astype(jnp.int32), 0) * ROW_CHUNKS).astype(jnp.int32)
    dst = (jnp.where(valid, a_p, TOP_K * n + j) * ROW_CHUNKS).astype(jnp.int32)
    w_p = jnp.where(valid, w[a_p], 0.0).astype(F32)
    return (blk_exp, nvalid, tok.reshape(n_blk, 1, MOE_BLOCK), dst.reshape(n_blk, 1, MOE_BLOCK),
            w_p.reshape(n_blk, MOE_BLOCK, 1))


def kernel(x, mem, positions, g_mix, g_mem, w_in, g_qn_attn, g_kn_attn, w_conv, w_mem_kv, g_qn_mem,
           g_kn_mem, w_proj_attn, w_proj_conv, w_proj_mem, w_out, g_ffn, w_router_group,
           b_router_group, w_router_expert, b_router_expert, w_gate, w_up, w_down):
    b, s, d = x.shape
    n = b * s
    cos_t, sin_t = _rope_tables(positions)
    for l in range(w_in.shape[0]):
        wl = w_in[l]
        aw = N_BRANCH * GROUP_W
        cols = []
        for gi in range(len(DILATIONS)):
            for part in range(3):
                c0 = part * aw + gi * GROUP_W
                cols.append(wl[:, c0:c0 + GROUP_W])
        cols.append(wl[:, 3 * aw:])
        w_all = jnp.concatenate(cols, axis=1).astype(BF16)

        q0, q1, q2, u, cb, qm, gates = _inproj(x, g_mix[l][None], g_qn_mem[l][None], w_all)
        km, vm = _memkv(mem, g_mem[l][None], g_kn_mem[l][None], w_mem_kv[l].astype(BF16))
        gq = jnp.tile(g_qn_attn[l], 2)[None]
        gk = jnp.tile(g_kn_attn[l], 2)[None]
        o_attn = _attention((q0, q1, q2), cos_t, sin_t, gq, gk)

        w_r = jnp.zeros((d, LANES), F32)
        w_r = w_r.at[:, :N_EXPERT_GROUPS].set(w_router_group[l])
        w_r = w_r.at[:, N_EXPERT_GROUPS:N_EXPERT_GROUPS + N_EXPERTS].set(w_router_expert[l])
        b_r = jnp.zeros((1, LANES), F32)
        b_r = b_r.at[0, :N_EXPERT_GROUPS].set(b_router_group[l])
        b_r = b_r.at[0, N_EXPERT_GROUPS:N_EXPERT_GROUPS + N_EXPERTS].set(b_router_expert[l])
        w_r_hi = w_r.astype(BF16)
        w_r_lo = (w_r - w_r_hi.astype(F32)).astype(BF16)
        x1, route_t = _merge(x, o_attn, u, cb, qm, km, vm, gates, w_conv[l],
                            w_proj_attn[l].astype(BF16), w_proj_conv[l].astype(BF16),
                            w_proj_mem[l].astype(BF16), w_out[l].astype(BF16), g_ffn[l][None],
                            jnp.stack([w_r_hi, w_r_lo]), b_r)

        blk_exp, nvalid, row_tok, row_dst, row_w = _route(route_t, n)
        y = _experts(x1, g_ffn[l][None], blk_exp, nvalid, row_tok, row_dst, row_w,
                     w_gate[l], w_up[l], w_down[l])
        x = _combine(x1, y).reshape(b, s, d)
    return x
```

```python
import functools

import numpy as np
import jax
import jax.numpy as jnp
from jax import lax
from jax.experimental import pallas as pl
from jax.experimental.pallas import tpu as pltpu

F32 = jnp.float32
BF16 = jnp.bfloat16

D_MODEL = 1024
EPS = 1e-6
HEAD_DIM = 64
DILATIONS = (1, 4, 16)
BAND_HALF = 64
ATTN_SLOTS = 4
GROUP_W = ATTN_SLOTS * HEAD_DIM
QKV_W = 3 * GROUP_W
ROT_DIM = 16
ROPE_THETA = 500000.0
CONV_WIDTH = 768
MEM_HEADS = 4
MEM_HEAD_DIM = 128
MEM_WIDTH = 512
N_BRANCH = 3
N_EXPERT_GROUPS = 4
EXPERTS_PER_GROUP = 8
N_EXPERTS = 32
TOP_K = 2
EXPERT_FF = 512

LANES = 128
VMEM_LIMIT_BYTES = 56 * 1024 * 1024

ROW_TILE = 512
Q_BLOCK = 128
MOE_BLOCK = 128
PAIRS_PER_GROUP = EXPERTS_PER_GROUP * (EXPERTS_PER_GROUP - 1) // 2
N_SEGMENTS = N_EXPERT_GROUPS * PAIRS_PER_GROUP
ROW_CHUNKS = D_MODEL // LANES
GATHER_UNROLL = 8
NEG_BIG = -1e30
ROUTE_NEG = -3e38
ROUTE_ROWS = 8


def _rms(t, gain):
    return t * lax.rsqrt(jnp.mean(t * t, axis=-1, keepdims=True) + EPS) * gain


def _inproj_kernel(x_ref, g_ref, gqm_ref, w_ref,
                   qkv0_ref, qkv1_ref, qkv2_ref, u_ref, cb_ref, qm_ref, gates_ref,
                   h_scr, hp_scr, *, tm):
    x = x_ref[0]
    h = _rms(x, g_ref[...])
    n_chunks = h.shape[1] // LANES
    for c in range(n_chunks):
        h_scr[c] = h[:, c * LANES:(c + 1) * LANES]
    hb = h.astype(BF16)

    def mm(lhs, c0, c1):
        return jnp.dot(lhs, w_ref[:, c0:c1], preferred_element_type=F32)

    qkv0_ref[0, 0] = mm(hb, 0, QKV_W).astype(BF16)
    for gi, out_ref in ((1, qkv1_ref), (2, qkv2_ref)):
        d = DILATIONS[gi]
        rows = tm // d
        for r in range(d):
            for c in range(n_chunks):
                hp_scr[r * rows:(r + 1) * rows, c * LANES:(c + 1) * LANES] = (
                    h_scr[c, pl.ds(r, rows, stride=d), :].astype(BF16))
        res = mm(hp_scr[...], gi * QKV_W, (gi + 1) * QKV_W)
        out_ref[0] = res.reshape(d, rows, QKV_W).astype(BF16)

    c = 3 * QKV_W
    cx = mm(hb, c, c + CONV_WIDTH)
    cc = mm(hb, c + 2 * CONV_WIDTH, c + 3 * CONV_WIDTH)
    u_ref[0] = (cc * cx).astype(BF16)
    cb_ref[0] = mm(hb, c + CONV_WIDTH, c + 2 * CONV_WIDTH).astype(BF16)

    c += 3 * CONV_WIDTH
    qm = mm(hb, c, c + MEM_WIDTH)
    for hh in range(MEM_HEADS):
        sl = slice(hh * MEM_HEAD_DIM, (hh + 1) * MEM_HEAD_DIM)
        qm_ref[0, :, sl] = _rms(qm[:, sl], gqm_ref[...]).astype(BF16)

    c += MEM_WIDTH
    for j in range(N_BRANCH):
        gl = mm(hb, c + j * D_MODEL, c + (j + 1) * D_MODEL)
        gates_ref[0, :, j * D_MODEL:(j + 1) * D_MODEL] = jax.nn.sigmoid(gl).astype(BF16)


def _inproj(x, g_mix, g_qn_mem, w_all):
    b, s, d = x.shape
    tm = ROW_TILE
    nt = s // tm
    outs = []
    out_specs = []
    for dil in DILATIONS:
        outs.append(jax.ShapeDtypeStruct((b, dil, s // dil, QKV_W), BF16))
        out_specs.append(pl.BlockSpec((1, dil, tm // dil, QKV_W), lambda bi, i: (bi, 0, i, 0)))
    for width in (CONV_WIDTH, CONV_WIDTH, MEM_WIDTH, N_BRANCH * D_MODEL):
        outs.append(jax.ShapeDtypeStruct((b, s, width), BF16))
        out_specs.append(pl.BlockSpec((1, tm, width), lambda bi, i: (bi, i, 0)))
    return pl.pallas_call(
        functools.partial(_inproj_kernel, tm=tm),
        grid=(b, nt),
        in_specs=[
            pl.BlockSpec((1, tm, d), lambda bi, i: (bi, i, 0)),
            pl.BlockSpec((1, d), lambda bi, i: (0, 0)),
            pl.BlockSpec((1, MEM_HEAD_DIM), lambda bi, i: (0, 0)),
            pl.BlockSpec(w_all.shape, lambda bi, i: (0, 0), pipeline_mode=pl.Buffered(1)),
        ],
        out_specs=out_specs,
        out_shape=outs,
        scratch_shapes=[pltpu.VMEM((d // LANES, tm, LANES), F32), pltpu.VMEM((tm, d), BF16)],
        compiler_params=pltpu.CompilerParams(
            dimension_semantics=("parallel", "parallel"), vmem_limit_bytes=VMEM_LIMIT_BYTES),
        name="inproj",
    )(x, g_mix, g_qn_mem, w_all)


def _memkv_kernel(mem_ref, g_ref, gk_ref, w_ref, k_ref, v_ref):
    h = _rms(mem_ref[0], g_ref[...]).astype(BF16)
    kv = jnp.dot(h, w_ref[...], preferred_element_type=F32)
    for hh in range(MEM_HEADS):
        sl = slice(hh * MEM_HEAD_DIM, (hh + 1) * MEM_HEAD_DIM)
        k_ref[0, :, sl] = _rms(kv[:, sl], gk_ref[...]).astype(BF16)
    v_ref[0] = kv[:, MEM_WIDTH:].astype(BF16)


def _memkv(mem, g_mem, g_kn_mem, w_kv):
    b, m, d = mem.shape
    return pl.pallas_call(
        _memkv_kernel,
        grid=(b,),
        in_specs=[
            pl.BlockSpec((1, m, d), lambda bi: (bi, 0, 0)),
            pl.BlockSpec((1, d), lambda bi: (0, 0)),
            pl.BlockSpec((1, MEM_HEAD_DIM), lambda bi: (0, 0)),
            pl.BlockSpec(w_kv.shape, lambda bi: (0, 0)),
        ],
        out_specs=[pl.BlockSpec((1, m, MEM_WIDTH), lambda bi: (bi, 0, 0))] * 2,
        out_shape=[jax.ShapeDtypeStruct((b, m, MEM_WIDTH), BF16)] * 2,
        compiler_params=pltpu.CompilerParams(dimension_semantics=("parallel",)),
        name="memkv",
    )(mem, g_mem, g_kn_mem, w_kv)


ATTN_WIN = Q_BLOCK + 2 * BAND_HALF
ATTN_UNROLL = 16
DEN_SAFE = 1e-30


def _attn_consts():
    lane = np.arange(LANES)
    gsum = (lane[:, None] // HEAD_DIM == lane[None, :] // HEAD_DIM).astype(np.float32)
    half = ROT_DIM // 2
    dim = lane % HEAD_DIM
    src = np.where(dim < half, lane + half, np.where(dim < ROT_DIM, lane - half, -1))
    pswap = (lane[:, None] == src[None, :]).astype(np.float32)
    i = np.arange(Q_BLOCK)[:, None]
    c = np.arange(ATTN_WIN)[None, :]
    band = (c >= i) & (c <= i + 2 * BAND_HALF)
    first = c >= BAND_HALF
    last = c < Q_BLOCK + BAND_HALF
    variants = [band, band & first, band & last, band & first & last]
    bias = np.stack([np.where(v, 0.0, NEG_BIG) for v in variants]).astype(np.float32)
    return jnp.asarray(gsum, BF16), jnp.asarray(pswap, BF16), jnp.asarray(bias, F32)


def _attn_kernel(q0, k0, v0, q1, k1, v1, q2, k2, v2, cos_ref, sin_ref, gq_ref, gk_ref,
                 gsum_ref, pswap_ref, bias_ref, out_ref, qs, ks, vs, o_scr, l_scr, bias_scr, *, seq):
    lane = lax.broadcasted_iota(jnp.int32, (1, LANES), 1)
    low_half = lane < HEAD_DIM
    pad = BAND_HALF
    n_blocks = seq // Q_BLOCK

    bound = HEAD_DIM ** 0.5 * jnp.max(jnp.abs(gq_ref[...])) * jnp.max(jnp.abs(gk_ref[...]))
    shifted = bias_ref[...] - bound
    bias_scr[:, 0:Q_BLOCK, :] = shifted
    bias_scr[:, Q_BLOCK:2 * Q_BLOCK, :] = shifted

    def norm_rope(t, gain, cos, sin):
        ms = jnp.dot((t * t).astype(BF16), gsum_ref[...], preferred_element_type=F32) * (1.0 / HEAD_DIM)
        tn = t * lax.rsqrt(ms + EPS) * gain
        partner = jnp.dot(tn.astype(BF16), pswap_ref[...], preferred_element_type=F32)
        return tn * cos + partner * sin

    for gi, (q_ref, k_ref, v_ref) in enumerate(((q0, k0, v0), (q1, k1, v1), (q2, k2, v2))):
        d = DILATIONS[gi]
        length = seq // d
        nblk = length // Q_BLOCK
        region = length + pad

        zpad = jnp.zeros((pad, LANES), BF16)
        for r in range(d + 1):
            ks[r * region:r * region + pad, :] = zpad
            vs[r * region:r * region + pad, :] = zpad

        def split_index(b, nblk=nblk):
            if nblk == 1:
                return b, 0
            r = b // nblk
            return r, b - r * nblk

        def table_rows(r, n, d=d):
            if d == 1:
                return pl.ds(pl.multiple_of(n * Q_BLOCK, Q_BLOCK), Q_BLOCK)
            return pl.ds(r + d * n * Q_BLOCK, Q_BLOCK, stride=d)

        def prep(b, q_ref=q_ref, k_ref=k_ref, v_ref=v_ref, split_index=split_index,
                 table_rows=table_rows):
            r, n = split_index(b)
            src = pl.ds(pl.multiple_of(n * Q_BLOCK, Q_BLOCK), Q_BLOCK)
            rows = table_rows(r, n)
            cos = cos_ref[0, rows, :]
            sin = sin_ref[0, rows, :]
            qn = norm_rope(q_ref[0, r, src, :].astype(F32), gq_ref[...], cos, sin) * (HEAD_DIM ** -0.5)
            kn = norm_rope(k_ref[0, r, src, :].astype(F32), gk_ref[...], cos, sin)
            qbase = pl.multiple_of(b * 2 * Q_BLOCK, 2 * Q_BLOCK)
            qs[pl.ds(qbase, Q_BLOCK), :] = jnp.where(low_half, qn, 0.0).astype(BF16)
            qs[pl.ds(qbase + Q_BLOCK, Q_BLOCK), :] = jnp.where(low_half, 0.0, qn).astype(BF16)
            kdst = pl.ds(pl.multiple_of(b * Q_BLOCK + (r + 1) * pad, pad), Q_BLOCK)
            ks[kdst, :] = kn.astype(BF16)
            vs[kdst, :] = v_ref[0, r, src, :]

        def block(b, exact, gi=gi, nblk=nblk, split_index=split_index, table_rows=table_rows):
            r, n = split_index(b)
            variant = 3 if nblk == 1 else jnp.where(n == 0, 1, 0) + jnp.where(n == nblk - 1, 2, 0)
            qsrc = pl.ds(pl.multiple_of(b * 2 * Q_BLOCK, 2 * Q_BLOCK), 2 * Q_BLOCK)
            ksrc = pl.ds(pl.multiple_of(b * Q_BLOCK + r * pad, pad), ATTN_WIN)
            s = lax.dot_general(qs[qsrc, :], ks[ksrc, :], (((1,), (1,)), ((), ())),
                                preferred_element_type=F32) + bias_scr[variant]
            if exact:
                m = jnp.max(s, axis=-1, keepdims=True)
                s = s - m
                shift = jnp.where(low_half, m[0:Q_BLOCK], m[Q_BLOCK:]) + bound
            else:
                shift = bound
            p = jnp.exp(s)
            den = jnp.sum(p, axis=-1, keepdims=True)
            o2 = jnp.dot(p.astype(BF16), vs[ksrc, :], preferred_element_type=F32)
            o = jnp.where(low_half, o2[0:Q_BLOCK], o2[Q_BLOCK:])
            den = jnp.where(low_half, den[0:Q_BLOCK], den[Q_BLOCK:])
            dst = table_rows(r, n)
            o_scr[gi, dst, :] = o / den
            l_scr[gi, dst, :] = shift + jnp.log(den)
            return den

        def unrolled(fn):
            def body(it, carry):
                for u in range(ATTN_UNROLL):
                    fn(it * ATTN_UNROLL + u)
                return carry
            lax.fori_loop(0, n_blocks // ATTN_UNROLL, body, 0)

        unrolled(prep)

        def fast_body(it, dmin, block=block):
            for u in range(ATTN_UNROLL):
                dmin = jnp.minimum(dmin, block(it * ATTN_UNROLL + u, False))
            return dmin
        dmin = lax.fori_loop(0, n_blocks // ATTN_UNROLL, fast_body,
                             jnp.full((Q_BLOCK, LANES), 1.0, F32))

        @pl.when(jnp.logical_not(jnp.min(dmin) > DEN_SAFE))
        def _(block=block, unrolled=unrolled):
            unrolled(lambda b: block(b, True))

    l0 = l_scr[0]
    l1 = l_scr[1]
    l2 = l_scr[2]
    m = jnp.maximum(jnp.maximum(l0, l1), l2)
    e0 = jnp.exp(l0 - m)
    e1 = jnp.exp(l1 - m)
    e2 = jnp.exp(l2 - m)
    mix = (e0 * o_scr[0] + e1 * o_scr[1] + e2 * o_scr[2]) / (e0 + e1 + e2)
    out_ref[0] = mix.astype(BF16)


def _attention(qkv, cos_t, sin_t, gq, gk):
    b, _, s, _ = qkv[0].shape
    in_specs = []
    args = []
    for gi, dil in enumerate(DILATIONS):
        for part in range(3):
            in_specs.append(pl.BlockSpec((1, dil, s // dil, LANES),
                                         lambda bi, hp, part=part: (bi, 0, 0, 2 * part + hp)))
            args.append(qkv[gi])
    consts = _attn_consts()
    tab_spec = pl.BlockSpec((1, s, LANES), lambda bi, hp: (bi, 0, 0))
    gain_spec = pl.BlockSpec((1, LANES), lambda bi, hp: (0, 0))
    const_specs = [pl.BlockSpec(c.shape, lambda bi, hp, nd=c.ndim: (0,) * nd) for c in consts]
    key_rows = max(d * (s // d + BAND_HALF) for d in DILATIONS) + BAND_HALF
    return pl.pallas_call(
        functools.partial(_attn_kernel, seq=s),
        grid=(b, 2),
        in_specs=in_specs + [tab_spec] * 2 + [gain_spec] * 2 + const_specs,
        out_specs=pl.BlockSpec((1, s, LANES), lambda bi, hp: (bi, 0, hp)),
        out_shape=jax.ShapeDtypeStruct((b, s, GROUP_W), BF16),
        scratch_shapes=[
            pltpu.VMEM((2 * s, LANES), BF16),
            pltpu.VMEM((key_rows, LANES), BF16),
            pltpu.VMEM((key_rows, LANES), BF16),
            pltpu.VMEM((3, s, LANES), F32),
            pltpu.VMEM((3, s, LANES), F32),
            pltpu.VMEM((4, 2 * Q_BLOCK, ATTN_WIN), F32),
        ],
        compiler_params=pltpu.CompilerParams(
            dimension_semantics=("parallel", "parallel"), vmem_limit_bytes=VMEM_LIMIT_BYTES),
        name="dilated_attn",
    )(*args, cos_t, sin_t, gq, gk, *consts)


def _merge_kernel(x_ref, oa_ref, u_ref, uprev_ref, unext_ref, cb_ref, qm_ref, km_ref, vm_ref,
                  gates_ref, wconv_ref, wpa_ref, wpc_ref, wpm_ref, wout_ref, gffn_ref, wr_ref, br_ref,
                  x1_ref, route_ref, om_scr, *, tm, nt):
    i = pl.program_id(1)
    u = u_ref[0].astype(F32)
    prev_row = uprev_ref[0].astype(F32)[15:16, :] * (i > 0).astype(F32)
    next_row = unext_ref[0].astype(F32)[0:1, :] * (i < nt - 1).astype(F32)
    row = lax.broadcasted_iota(jnp.int32, (tm, 1), 0)
    um = jnp.where(row == 0, prev_row, pltpu.roll(u, 1, 0))
    up = jnp.where(row == tm - 1, next_row, pltpu.roll(u, tm - 1, 0))
    wc = wconv_ref[...]
    y = wc[0:1, :] * um + wc[1:2, :] * u + wc[2:3, :] * up
    z = (cb_ref[0].astype(F32) * y).astype(BF16)

    for hh in range(MEM_HEADS):
        sl = slice(hh * MEM_HEAD_DIM, (hh + 1) * MEM_HEAD_DIM)
        s = lax.dot_general(qm_ref[0, :, sl], km_ref[0, :, sl], (((1,), (1,)), ((), ())),
                            preferred_element_type=F32) * (MEM_HEAD_DIM ** -0.5)
        m = jnp.max(s, axis=-1, keepdims=True)
        p = jnp.exp(s - m)
        den = jnp.sum(p, axis=-1, keepdims=True)
        o = jnp.dot(p.astype(BF16), vm_ref[0, :, sl], preferred_element_type=F32) / den
        om_scr[:, sl] = o.astype(BF16)

    pa = jnp.dot(oa_ref[0], wpa_ref[...], preferred_element_type=F32)
    pc = jnp.dot(z, wpc_ref[...], preferred_element_type=F32)
    pm = jnp.dot(om_scr[...], wpm_ref[...], preferred_element_type=F32)
    merged = (gates_ref[0, :, 0:D_MODEL].astype(F32) * pa
              + gates_ref[0, :, D_MODEL:2 * D_MODEL].astype(F32) * pc
              + gates_ref[0, :, 2 * D_MODEL:3 * D_MODEL].astype(F32) * pm)
    x1 = x_ref[0] + jnp.dot(merged.astype(BF16), wout_ref[...], preferred_element_type=F32)
    for c in range(ROW_CHUNKS):
        x1_ref[pl.ds(c, tm, stride=ROW_CHUNKS), :] = x1[:, c * LANES:(c + 1) * LANES]

    h2 = _rms(x1, gffn_ref[...])
    h_hi = h2.astype(BF16)
    h_lo = (h2 - h_hi.astype(F32)).astype(BF16)
    lg = (jnp.dot(h_hi, wr_ref[0], preferred_element_type=F32)
          + jnp.dot(h_lo, wr_ref[0], preferred_element_type=F32)
          + jnp.dot(h_hi, wr_ref[1], preferred_element_type=F32)) + br_ref[...]

    lane = lax.broadcasted_iota(jnp.int32, (tm, LANES), 1).astype(F32)
    far = float(LANES)
    is_group = lane < N_EXPERT_GROUPS
    gl = jnp.where(is_group, lg, ROUTE_NEG)
    mg = jnp.max(gl, axis=-1, keepdims=True)
    gidx = jnp.min(jnp.where(gl == mg, lane, far), axis=-1, keepdims=True)
    pg_top = 1.0 / jnp.sum(jnp.where(is_group, jnp.exp(lg - mg), 0.0), axis=-1, keepdims=True)
    first = N_EXPERT_GROUPS + EXPERTS_PER_GROUP * gidx
    in_sel = jnp.logical_and(lane >= first, lane < first + EXPERTS_PER_GROUP)
    sel = jnp.where(in_sel, lg, ROUTE_NEG)
    m1 = jnp.max(sel, axis=-1, keepdims=True)
    i1 = jnp.min(jnp.where(jnp.logical_and(in_sel, sel == m1), lane, far), axis=-1, keepdims=True)
    keep = jnp.logical_and(in_sel, lane != i1)
    sel2 = jnp.where(keep, lg, ROUTE_NEG)
    m2 = jnp.max(sel2, axis=-1, keepdims=True)
    i2 = jnp.min(jnp.where(jnp.logical_and(keep, sel2 == m2), lane, far), axis=-1, keepdims=True)
    r = jnp.exp(m2 - m1)
    w1 = pg_top / (1.0 + r)
    w2 = w1 * r
    a = i1 - first
    b = i2 - first
    lo = jnp.minimum(a, b)
    hi = jnp.maximum(a, b)
    seg = gidx * PAIRS_PER_GROUP + lo * (2 * EXPERTS_PER_GROUP - 1 - lo) * 0.5 + (hi - lo - 1.0)
    w_lo = jnp.where(a < b, w1, w2)
    w_hi = jnp.where(a < b, w2, w1)
    route = jnp.where(lane == 0, seg, jnp.where(lane == 1, w_lo, jnp.where(lane == 2, w_hi, 0.0)))
    route_ref[...] = jnp.transpose(route)[0:8, :]


def _merge(x, o_attn, u, cb, qm, km, vm, gates, w_conv, wpa, wpc, wpm, wout, g_ffn, w_r, b_r):
    b, s, d = x.shape
    tm = ROW_TILE
    nt = s // tm
    halo = 16
    hb = tm // halo

    def tile(width):
        return pl.BlockSpec((1, tm, width), lambda bi, i: (bi, i, 0))

    def whole(arr):
        return pl.BlockSpec(arr.shape, lambda bi, i: (0,) * arr.ndim)

    def per_batch(arr):
        return pl.BlockSpec((1,) + arr.shape[1:], lambda bi, i: (bi, 0, 0))

    in_specs = [
        tile(d), tile(GROUP_W), tile(CONV_WIDTH),
        pl.BlockSpec((1, halo, CONV_WIDTH), lambda bi, i: (bi, jnp.maximum(i * hb - 1, 0), 0)),
        pl.BlockSpec((1, halo, CONV_WIDTH),
                     lambda bi, i: (bi, jnp.minimum((i + 1) * hb, s // halo - 1), 0)),
        tile(CONV_WIDTH), tile(MEM_WIDTH), per_batch(km), per_batch(vm), tile(N_BRANCH * d),
        whole(w_conv), whole(wpa), whole(wpc), whole(wpm), whole(wout), whole(g_ffn),
        whole(w_r), whole(b_r),
    ]
    return pl.pallas_call(
        functools.partial(_merge_kernel, tm=tm, nt=nt),
        grid=(b, nt),
        in_specs=in_specs,
        out_specs=[pl.BlockSpec((tm * ROW_CHUNKS, LANES), lambda bi, i: (bi * nt + i, 0)),
                   pl.BlockSpec((ROUTE_ROWS, tm), lambda bi, i: (0, bi * nt + i))],
        out_shape=[jax.ShapeDtypeStruct((b * s * ROW_CHUNKS, LANES), F32),
                   jax.ShapeDtypeStruct((ROUTE_ROWS, b * s), F32)],
        scratch_shapes=[pltpu.VMEM((tm, MEM_WIDTH), BF16)],
        compiler_params=pltpu.CompilerParams(
            dimension_semantics=("parallel", "parallel"), vmem_limit_bytes=VMEM_LIMIT_BYTES),
        name="merge",
    )(x, o_attn, u, u, u, cb, qm, km, vm, gates, w_conv, wpa, wpc, wpm, wout, g_ffn, w_r, b_r)


DRAIN_STEPS = 2


def _expert_kernel(ea_ref, eb_ref, nvalid_ref, tok_ref, tok_next_ref, dst_ref, wlo_ref, whi_ref,
                   x1_hbm, gffn_ref, wga_ref, wua_ref, wda_ref, wgb_ref, wub_ref, wdb_ref,
                   y_hbm, xbuf, hbuf, ybuf, gsem, ssem):
    i = pl.program_id(0)
    slot = lax.rem(i, 2)
    rows = MOE_BLOCK * ROW_CHUNKS

    def start_gather(idx_ref, s):
        def body(it, c):
            for u in range(GATHER_UNROLL):
                j = it * GATHER_UNROLL + u
                src_row = pl.multiple_of(idx_ref[0, 0, j], ROW_CHUNKS)
                pltpu.make_async_copy(x1_hbm.at[pl.ds(src_row, ROW_CHUNKS), :],
                                      xbuf.at[s, pl.ds(j * ROW_CHUNKS, ROW_CHUNKS), :],
                                      gsem.at[s]).start()
            return c
        lax.fori_loop(0, MOE_BLOCK // GATHER_UNROLL, body, 0)

    def wait_gather(s):
        pltpu.make_async_copy(x1_hbm.at[pl.ds(0, rows), :], xbuf.at[s], gsem.at[s]).wait()

    def start_scatter(s):
        def body(it, c):
            for u in range(GATHER_UNROLL):
                j = it * GATHER_UNROLL + u
                dst_row = pl.multiple_of(dst_ref[0, 0, j], ROW_CHUNKS)
                pltpu.make_async_copy(ybuf.at[s, pl.ds(j * ROW_CHUNKS, ROW_CHUNKS), :],
                                      y_hbm.at[pl.ds(dst_row, ROW_CHUNKS), :], ssem.at[s]).start()
            return c
        lax.fori_loop(0, MOE_BLOCK // GATHER_UNROLL, body, 0)

    def wait_scatter(s):
        pltpu.make_async_copy(ybuf.at[s], y_hbm.at[pl.ds(0, rows), :], ssem.at[s]).wait()

    nvalid = nvalid_ref[i]
    prev = jnp.maximum(i - 1, 0)
    prev2 = jnp.maximum(i - 2, 0)

    @pl.when(i == 0)
    def _():
        start_gather(tok_ref, 0)
        ybuf[0] = jnp.zeros((rows, LANES), F32)
        spare = pltpu.make_async_copy(ybuf.at[0], y_hbm.at[pl.ds(y_hbm.shape[0] - rows, rows), :],
                                      ssem.at[0])
        spare.start()
        spare.wait()

    @pl.when(jnp.logical_and(i >= 2, nvalid_ref[prev2] > 0))
    def _():
        wait_scatter(slot)

    @pl.when(nvalid > 0)
    def _():
        wait_gather(slot)
        start_gather(tok_next_ref, 1 - slot)

        ss = None
        for c in range(ROW_CHUNKS):
            xc = xbuf[slot, pl.ds(c, MOE_BLOCK, stride=ROW_CHUNKS), :]
            ss = xc * xc if ss is None else ss + xc * xc
        scale = lax.rsqrt(jnp.sum(ss, axis=-1, keepdims=True) * (1.0 / D_MODEL) + EPS)
        for c in range(ROW_CHUNKS):
            sl = slice(c * LANES, (c + 1) * LANES)
            xc = xbuf[slot, pl.ds(c, MOE_BLOCK, stride=ROW_CHUNKS), :]
            hbuf[:, sl] = (xc * scale * gffn_ref[:, sl]).astype(BF16)
        h2 = hbuf[...]
        w_cols = jnp.transpose(jnp.concatenate(
            [wlo_ref[0], whi_ref[0], jnp.zeros((6, MOE_BLOCK), F32)], axis=0))

        def expert(wg_ref, wu_ref, wd_ref, w_col):
            a = jnp.dot(h2, wg_ref[0], preferred_element_type=F32)
            up = jnp.dot(h2, wu_ref[0], preferred_element_type=F32)
            act = (a * jax.nn.sigmoid(a) * up * w_col).astype(BF16)
            return jnp.dot(act, wd_ref[0], preferred_element_type=F32)

        y = (expert(wga_ref, wua_ref, wda_ref, w_cols[:, 0:1])
             + expert(wgb_ref, wub_ref, wdb_ref, w_cols[:, 1:2]))
        for c in range(ROW_CHUNKS):
            rows_c = pl.ds(c, MOE_BLOCK, stride=ROW_CHUNKS)
            ybuf[slot, rows_c, :] = xbuf[slot, rows_c, :] + y[:, c * LANES:(c + 1) * LANES]
        start_scatter(slot)

    @pl.when(jnp.logical_and(nvalid == 0, jnp.logical_and(i > 0, nvalid_ref[prev] > 0)))
    def _():
        wait_gather(slot)


def _experts(x1r, g_ffn, blk_ea, blk_eb, nvalid, row_tok, row_dst, row_wlo, row_whi, wg, wu, wd):
    rows_total, _ = x1r.shape
    d = D_MODEL
    n_blk = nvalid.shape[0]
    idx_spec = lambda fn: pl.BlockSpec((1, 1, MOE_BLOCK), fn, memory_space=pltpu.SMEM)
    w_spec = pl.BlockSpec((1, 1, MOE_BLOCK), lambda i, ea, eb, nv: (i, 0, 0))
    grid_spec = pltpu.PrefetchScalarGridSpec(
        num_scalar_prefetch=3,
        grid=(n_blk,),
        in_specs=[
            idx_spec(lambda i, ea, eb, nv: (i, 0, 0)),
            idx_spec(lambda i, ea, eb, nv: (jnp.minimum(i + 1, n_blk - 1), 0, 0)),
            idx_spec(lambda i, ea, eb, nv: (i, 0, 0)),
            w_spec, w_spec,
            pl.BlockSpec(memory_space=pl.ANY),
            pl.BlockSpec((1, d), lambda i, ea, eb, nv: (0, 0)),
            pl.BlockSpec((1, d, EXPERT_FF), lambda i, ea, eb, nv: (ea[i], 0, 0)),
            pl.BlockSpec((1, d, EXPERT_FF), lambda i, ea, eb, nv: (ea[i], 0, 0)),
            pl.BlockSpec((1, EXPERT_FF, d), lambda i, ea, eb, nv: (ea[i], 0, 0)),
            pl.BlockSpec((1, d, EXPERT_FF), lambda i, ea, eb, nv: (eb[i], 0, 0)),
            pl.BlockSpec((1, d, EXPERT_FF), lambda i, ea, eb, nv: (eb[i], 0, 0)),
            pl.BlockSpec((1, EXPERT_FF, d), lambda i, ea, eb, nv: (eb[i], 0, 0)),
        ],
        out_specs=pl.BlockSpec(memory_space=pl.ANY),
        scratch_shapes=[
            pltpu.VMEM((2, MOE_BLOCK * ROW_CHUNKS, LANES), F32),
            pltpu.VMEM((MOE_BLOCK, d), BF16),
            pltpu.VMEM((2, MOE_BLOCK * ROW_CHUNKS, LANES), F32),
            pltpu.SemaphoreType.DMA((2,)),
            pltpu.SemaphoreType.DMA((2,)),
        ],
    )
    return pl.pallas_call(
        _expert_kernel,
        grid_spec=grid_spec,
        out_shape=jax.ShapeDtypeStruct((rows_total + MOE_BLOCK * ROW_CHUNKS, LANES), F32),
        compiler_params=pltpu.CompilerParams(
            dimension_semantics=("arbitrary",), vmem_limit_bytes=VMEM_LIMIT_BYTES),
        name="experts",
    )(blk_ea, blk_eb, nvalid, row_tok, row_tok, row_dst, row_wlo, row_whi, x1r, g_ffn,
      wg, wu, wd, wg, wu, wd)


def _unpack_kernel(y_ref, o_ref, *, tm):
    for c in range(ROW_CHUNKS):
        o_ref[:, c * LANES:(c + 1) * LANES] = y_ref[pl.ds(c, tm, stride=ROW_CHUNKS), :]


def _unpack(y, n):
    tm = ROW_TILE
    return pl.pallas_call(
        functools.partial(_unpack_kernel, tm=tm),
        grid=(n // tm,),
        in_specs=[pl.BlockSpec((tm * ROW_CHUNKS, LANES), lambda i: (i, 0))],
        out_specs=pl.BlockSpec((tm, D_MODEL), lambda i: (i, 0)),
        out_shape=jax.ShapeDtypeStruct((n, D_MODEL), F32),
        compiler_params=pltpu.CompilerParams(dimension_semantics=("parallel",)),
        name="unpack",
    )(y)


def _rope_tables(positions):
    half = ROT_DIM // 2
    dim = np.arange(LANES) % HEAD_DIM
    rotary = dim < ROT_DIM
    inv = ROPE_THETA ** (-jnp.arange(0, ROT_DIM, 2, dtype=F32) / ROT_DIM)
    inv_lane = jnp.where(jnp.asarray(rotary), inv[jnp.asarray(dim % half)], 0.0)
    sign = jnp.asarray(np.where(dim < half, -1.0, 1.0), F32)
    ang = positions.astype(F32)[..., None] * inv_lane
    return jnp.cos(ang), jnp.sin(ang) * sign


def _segment_experts():
    lo_ids, hi_ids = [], []
    for g in range(N_EXPERT_GROUPS):
        for lo in range(EXPERTS_PER_GROUP):
            for hi in range(lo + 1, EXPERTS_PER_GROUP):
                lo_ids.append(g * EXPERTS_PER_GROUP + lo)
                hi_ids.append(g * EXPERTS_PER_GROUP + hi)
    return np.asarray(lo_ids, np.int32), np.asarray(hi_ids, np.int32)


def _route(route_t, n):
    seg = route_t[0].astype(jnp.int32)
    t = jnp.arange(n, dtype=jnp.int32)
    _, t_s = lax.sort((seg, t), num_keys=1, is_stable=True)
    counts = jnp.sum((seg[:, None] == jnp.arange(N_SEGMENTS, dtype=jnp.int32)[None, :]).astype(jnp.int32),
                     axis=0)
    padded = ((counts + MOE_BLOCK - 1) // MOE_BLOCK) * MOE_BLOCK
    pend = jnp.cumsum(padded)
    pstart = pend - padded
    start = jnp.cumsum(counts) - counts
    n_blk = n // MOE_BLOCK + N_SEGMENTS + DRAIN_STEPS
    blk_start = jnp.arange(n_blk, dtype=jnp.int32) * MOE_BLOCK
    blk_seg = jnp.minimum(jnp.sum((pend[None, :] <= blk_start[:, None]).astype(jnp.int32), axis=1),
                          N_SEGMENTS - 1).astype(jnp.int32)
    nvalid = jnp.clip(counts[blk_seg] - (blk_start - pstart[blk_seg]), 0, MOE_BLOCK)
    nvalid = jnp.where(blk_start < pend[-1], nvalid, 0).astype(jnp.int32)
    j = jnp.arange(MOE_BLOCK, dtype=jnp.int32)[None, :]
    valid = j < nvalid[:, None]
    src = jnp.clip((start[blk_seg] + blk_start - pstart[blk_seg])[:, None] + j, 0, n - 1)
    t_p = t_s[src]
    tok = (jnp.where(valid, t_p, 0) * ROW_CHUNKS).astype(jnp.int32)
    dst = (jnp.where(valid, t_p, n + j) * ROW_CHUNKS).astype(jnp.int32)
    w_p = jnp.where(valid[None], route_t[1:3][:, t_p], 0.0).astype(F32)
    lo_ids, hi_ids = _segment_experts()
    shape = (n_blk, 1, MOE_BLOCK)
    return (jnp.asarray(lo_ids)[blk_seg], jnp.asarray(hi_ids)[blk_seg], nvalid,
            tok.reshape(shape), dst.reshape(shape), w_p[0].reshape(shape), w_p[1].reshape(shape))


def kernel(x, mem, positions, g_mix, g_mem, w_in, g_qn_attn, g_kn_attn, w_conv, w_mem_kv, g_qn_mem,
           g_kn_mem, w_proj_attn, w_proj_conv, w_proj_mem, w_out, g_ffn, w_router_group,
           b_router_group, w_router_expert, b_router_expert, w_gate, w_up, w_down):
    b, s, d = x.shape
    n = b * s
    cos_t, sin_t = _rope_tables(positions)
    for l in range(w_in.shape[0]):
        wl = w_in[l]
        aw = N_BRANCH * GROUP_W
        cols = []
        for gi in range(len(DILATIONS)):
            for part in range(3):
                c0 = part * aw + gi * GROUP_W
                cols.append(wl[:, c0:c0 + GROUP_W])
        cols.append(wl[:, 3 * aw:])
        w_all = jnp.concatenate(cols, axis=1).astype(BF16)

        q0, q1, q2, u, cb, qm, gates = _inproj(x, g_mix[l][None], g_qn_mem[l][None], w_all)
        km, vm = _memkv(mem, g_mem[l][None], g_kn_mem[l][None], w_mem_kv[l].astype(BF16))
        gq = jnp.tile(g_qn_attn[l], 2)[None]
        gk = jnp.tile(g_kn_attn[l], 2)[None]
        o_attn = _attention((q0, q1, q2), cos_t, sin_t, gq, gk)

        w_r = jnp.zeros((d, LANES), F32)
        w_r = w_r.at[:, :N_EXPERT_GROUPS].set(w_router_group[l])
        w_r = w_r.at[:, N_EXPERT_GROUPS:N_EXPERT_GROUPS + N_EXPERTS].set(w_router_expert[l])
        b_r = jnp.zeros((1, LANES), F32)
        b_r = b_r.at[0, :N_EXPERT_GROUPS].set(b_router_group[l])
        b_r = b_r.at[0, N_EXPERT_GROUPS:N_EXPERT_GROUPS + N_EXPERTS].set(b_router_expert[l])
        w_r_hi = w_r.astype(BF16)
        w_r_lo = (w_r - w_r_hi.astype(F32)).astype(BF16)
        x1, route_t = _merge(x, o_attn, u, cb, qm, km, vm, gates, w_conv[l],
                            w_proj_attn[l].astype(BF16), w_proj_conv[l].astype(BF16),
                            w_proj_mem[l].astype(BF16), w_out[l].astype(BF16), g_ffn[l][None],
                            jnp.stack([w_r_hi, w_r_lo]), b_r)

        blk_ea, blk_eb, nvalid, row_tok, row_dst, row_wlo, row_whi = _route(route_t, n)
        y = _experts(x1, g_ffn[l][None], blk_ea, blk_eb, nvalid, row_tok, row_dst, row_wlo, row_whi,
                     w_gate[l].astype(BF16), w_up[l].astype(BF16), w_down[l].astype(BF16))
        x = _unpack(y, n).reshape(b, s, d)
    return x
```

```python
import functools

import numpy as np
import jax
import jax.numpy as jnp
from jax import lax
from jax.experimental import pallas as pl
from jax.experimental.pallas import tpu as pltpu

F32 = jnp.float32
BF16 = jnp.bfloat16

D_MODEL = 1024
EPS = 1e-6
HEAD_DIM = 64
DILATIONS = (1, 4, 16)
BAND_HALF = 64
ATTN_SLOTS = 4
GROUP_W = ATTN_SLOTS * HEAD_DIM
QKV_W = 3 * GROUP_W
ROT_DIM = 16
ROPE_THETA = 500000.0
CONV_WIDTH = 768
MEM_HEADS = 4
MEM_HEAD_DIM = 128
MEM_WIDTH = 512
N_BRANCH = 3
N_EXPERT_GROUPS = 4
EXPERTS_PER_GROUP = 8
N_EXPERTS = 32
TOP_K = 2
EXPERT_FF = 512

LANES = 128
VMEM_LIMIT_BYTES = 56 * 1024 * 1024

ROW_TILE = 512
Q_BLOCK = 128
MOE_BLOCK = 256
ROW_CHUNKS = D_MODEL // LANES
GATHER_UNROLL = 8
NEG_BIG = -1e30
ROUTE_NEG = -3e38
ROUTE_ROWS = 8


def _rms(t, gain):
    return t * lax.rsqrt(jnp.mean(t * t, axis=-1, keepdims=True) + EPS) * gain


def _inproj_kernel(x_ref, g_ref, gqm_ref, w_ref,
                   qkv0_ref, qkv1_ref, qkv2_ref, u_ref, cb_ref, qm_ref, gates_ref,
                   h_scr, hp_scr, *, tm):
    x = x_ref[0]
    h = _rms(x, g_ref[...])
    n_chunks = h.shape[1] // LANES
    for c in range(n_chunks):
        h_scr[c] = h[:, c * LANES:(c + 1) * LANES]
    hb = h.astype(BF16)

    def mm(lhs, c0, c1):
        return jnp.dot(lhs, w_ref[:, c0:c1], preferred_element_type=F32)

    qkv0_ref[0, 0] = mm(hb, 0, QKV_W).astype(BF16)
    for gi, out_ref in ((1, qkv1_ref), (2, qkv2_ref)):
        d = DILATIONS[gi]
        rows = tm // d
        for r in range(d):
            for c in range(n_chunks):
                hp_scr[r * rows:(r + 1) * rows, c * LANES:(c + 1) * LANES] = (
                    h_scr[c, pl.ds(r, rows, stride=d), :].astype(BF16))
        res = mm(hp_scr[...], gi * QKV_W, (gi + 1) * QKV_W)
        out_ref[0] = res.reshape(d, rows, QKV_W).astype(BF16)

    c = 3 * QKV_W
    cx = mm(hb, c, c + CONV_WIDTH)
    cc = mm(hb, c + 2 * CONV_WIDTH, c + 3 * CONV_WIDTH)
    u_ref[0] = (cc * cx).astype(BF16)
    cb_ref[0] = mm(hb, c + CONV_WIDTH, c + 2 * CONV_WIDTH).astype(BF16)

    c += 3 * CONV_WIDTH
    qm = mm(hb, c, c + MEM_WIDTH)
    for hh in range(MEM_HEADS):
        sl = slice(hh * MEM_HEAD_DIM, (hh + 1) * MEM_HEAD_DIM)
        qm_ref[0, :, sl] = _rms(qm[:, sl], gqm_ref[...]).astype(BF16)

    c += MEM_WIDTH
    for j in range(N_BRANCH):
        gl = mm(hb, c + j * D_MODEL, c + (j + 1) * D_MODEL)
        gates_ref[0, :, j * D_MODEL:(j + 1) * D_MODEL] = jax.nn.sigmoid(gl).astype(BF16)


def _inproj(x, g_mix, g_qn_mem, w_all):
    b, s, d = x.shape
    tm = ROW_TILE
    nt = s // tm
    outs = []
    out_specs = []
    for dil in DILATIONS:
        outs.append(jax.ShapeDtypeStruct((b, dil, s // dil, QKV_W), BF16))
        out_specs.append(pl.BlockSpec((1, dil, tm // dil, QKV_W), lambda bi, i: (bi, 0, i, 0)))
    for width in (CONV_WIDTH, CONV_WIDTH, MEM_WIDTH, N_BRANCH * D_MODEL):
        outs.append(jax.ShapeDtypeStruct((b, s, width), BF16))
        out_specs.append(pl.BlockSpec((1, tm, width), lambda bi, i: (bi, i, 0)))
    return pl.pallas_call(
        functools.partial(_inproj_kernel, tm=tm),
        grid=(b, nt),
        in_specs=[
            pl.BlockSpec((1, tm, d), lambda bi, i: (bi, i, 0)),
            pl.BlockSpec((1, d), lambda bi, i: (0, 0)),
            pl.BlockSpec((1, MEM_HEAD_DIM), lambda bi, i: (0, 0)),
            pl.BlockSpec(w_all.shape, lambda bi, i: (0, 0), pipeline_mode=pl.Buffered(1)),
        ],
        out_specs=out_specs,
        out_shape=outs,
        scratch_shapes=[pltpu.VMEM((d // LANES, tm, LANES), F32), pltpu.VMEM((tm, d), BF16)],
        compiler_params=pltpu.CompilerParams(
            dimension_semantics=("parallel", "parallel"), vmem_limit_bytes=VMEM_LIMIT_BYTES),
        name="inproj",
    )(x, g_mix, g_qn_mem, w_all)


def _memkv_kernel(mem_ref, g_ref, gk_ref, w_ref, k_ref, v_ref):
    h = _rms(mem_ref[0], g_ref[...]).astype(BF16)
    kv = jnp.dot(h, w_ref[...], preferred_element_type=F32)
    for hh in range(MEM_HEADS):
        sl = slice(hh * MEM_HEAD_DIM, (hh + 1) * MEM_HEAD_DIM)
        k_ref[0, :, sl] = _rms(kv[:, sl], gk_ref[...]).astype(BF16)
    v_ref[0] = kv[:, MEM_WIDTH:].astype(BF16)


def _memkv(mem, g_mem, g_kn_mem, w_kv):
    b, m, d = mem.shape
    return pl.pallas_call(
        _memkv_kernel,
        grid=(b,),
        in_specs=[
            pl.BlockSpec((1, m, d), lambda bi: (bi, 0, 0)),
            pl.BlockSpec((1, d), lambda bi: (0, 0)),
            pl.BlockSpec((1, MEM_HEAD_DIM), lambda bi: (0, 0)),
            pl.BlockSpec(w_kv.shape, lambda bi: (0, 0)),
        ],
        out_specs=[pl.BlockSpec((1, m, MEM_WIDTH), lambda bi: (bi, 0, 0))] * 2,
        out_shape=[jax.ShapeDtypeStruct((b, m, MEM_WIDTH), BF16)] * 2,
        compiler_params=pltpu.CompilerParams(dimension_semantics=("parallel",)),
        name="memkv",
    )(mem, g_mem, g_kn_mem, w_kv)


ATTN_WIN = Q_BLOCK + 2 * BAND_HALF
ATTN_UNROLL = 16
DEN_SAFE = 1e-30


def _attn_consts():
    lane = np.arange(LANES)
    gsum = (lane[:, None] // HEAD_DIM == lane[None, :] // HEAD_DIM).astype(np.float32)
    half = ROT_DIM // 2
    dim = lane % HEAD_DIM
    src = np.where(dim < half, lane + half, np.where(dim < ROT_DIM, lane - half, -1))
    pswap = (lane[:, None] == src[None, :]).astype(np.float32)
    i = np.arange(Q_BLOCK)[:, None]
    c = np.arange(ATTN_WIN)[None, :]
    band = (c >= i) & (c <= i + 2 * BAND_HALF)
    first = c >= BAND_HALF
    last = c < Q_BLOCK + BAND_HALF
    variants = [band, band & first, band & last, band & first & last]
    bias = np.stack([np.where(v, 0.0, NEG_BIG) for v in variants]).astype(np.float32)
    return jnp.asarray(gsum, BF16), jnp.asarray(pswap, BF16), jnp.asarray(bias, F32)


def _attn_kernel(q0, k0, v0, q1, k1, v1, q2, k2, v2, cos_ref, sin_ref, gq_ref, gk_ref,
                 gsum_ref, pswap_ref, bias_ref, out_ref, qs, ks, vs, o_scr, l_scr, bias_scr, *, seq):
    lane = lax.broadcasted_iota(jnp.int32, (1, LANES), 1)
    low_half = lane < HEAD_DIM
    pad = BAND_HALF
    n_blocks = seq // Q_BLOCK

    bound = HEAD_DIM ** 0.5 * jnp.max(jnp.abs(gq_ref[...])) * jnp.max(jnp.abs(gk_ref[...]))
    shifted = bias_ref[...] - bound
    bias_scr[:, 0:Q_BLOCK, :] = shifted
    bias_scr[:, Q_BLOCK:2 * Q_BLOCK, :] = shifted

    def norm_rope(t, gain, cos, sin):
        ms = jnp.dot((t * t).astype(BF16), gsum_ref[...], preferred_element_type=F32) * (1.0 / HEAD_DIM)
        tn = t * lax.rsqrt(ms + EPS) * gain
        partner = jnp.dot(tn.astype(BF16), pswap_ref[...], preferred_element_type=F32)
        return tn * cos + partner * sin

    for gi, (q_ref, k_ref, v_ref) in enumerate(((q0, k0, v0), (q1, k1, v1), (q2, k2, v2))):
        d = DILATIONS[gi]
        length = seq // d
        nblk = length // Q_BLOCK
        region = length + pad

        zpad = jnp.zeros((pad, LANES), BF16)
        for r in range(d + 1):
            ks[r * region:r * region + pad, :] = zpad
            vs[r * region:r * region + pad, :] = zpad

        def split_index(b, nblk=nblk):
            if nblk == 1:
                return b, 0
            r = b // nblk
            return r, b - r * nblk

        def table_rows(r, n, d=d):
            if d == 1:
                return pl.ds(pl.multiple_of(n * Q_BLOCK, Q_BLOCK), Q_BLOCK)
            return pl.ds(r + d * n * Q_BLOCK, Q_BLOCK, stride=d)

        def prep(b, q_ref=q_ref, k_ref=k_ref, v_ref=v_ref, split_index=split_index,
                 table_rows=table_rows):
            r, n = split_index(b)
            src = pl.ds(pl.multiple_of(n * Q_BLOCK, Q_BLOCK), Q_BLOCK)
            rows = table_rows(r, n)
            cos = cos_ref[0, rows, :]
            sin = sin_ref[0, rows, :]
            qn = norm_rope(q_ref[0, r, src, :].astype(F32), gq_ref[...], cos, sin) * (HEAD_DIM ** -0.5)
            kn = norm_rope(k_ref[0, r, src, :].astype(F32), gk_ref[...], cos, sin)
            qbase = pl.multiple_of(b * 2 * Q_BLOCK, 2 * Q_BLOCK)
            qs[pl.ds(qbase, Q_BLOCK), :] = jnp.where(low_half, qn, 0.0).astype(BF16)
            qs[pl.ds(qbase + Q_BLOCK, Q_BLOCK), :] = jnp.where(low_half, 0.0, qn).astype(BF16)
            kdst = pl.ds(pl.multiple_of(b * Q_BLOCK + (r + 1) * pad, pad), Q_BLOCK)
            ks[kdst, :] = kn.astype(BF16)
            vs[kdst, :] = v_ref[0, r, src, :]

        def block(b, exact, gi=gi, nblk=nblk, split_index=split_index, table_rows=table_rows):
            r, n = split_index(b)
            variant = 3 if nblk == 1 else jnp.where(n == 0, 1, 0) + jnp.where(n == nblk - 1, 2, 0)
            qsrc = pl.ds(pl.multiple_of(b * 2 * Q_BLOCK, 2 * Q_BLOCK), 2 * Q_BLOCK)
            ksrc = pl.ds(pl.multiple_of(b * Q_BLOCK + r * pad, pad), ATTN_WIN)
            s = lax.dot_general(qs[qsrc, :], ks[ksrc, :], (((1,), (1,)), ((), ())),
                                preferred_element_type=F32) + bias_scr[variant]
            if exact:
                m = jnp.max(s, axis=-1, keepdims=True)
                s = s - m
                shift = jnp.where(low_half, m[0:Q_BLOCK], m[Q_BLOCK:]) + bound
            else:
                shift = bound
            p = jnp.exp(s)
            den = jnp.sum(p, axis=-1, keepdims=True)
            o2 = jnp.dot(p.astype(BF16), vs[ksrc, :], preferred_element_type=F32)
            o = jnp.where(low_half, o2[0:Q_BLOCK], o2[Q_BLOCK:])
            den = jnp.where(low_half, den[0:Q_BLOCK], den[Q_BLOCK:])
            dst = table_rows(r, n)
            o_scr[gi, dst, :] = o / den
            l_scr[gi, dst, :] = shift + jnp.log(den)
            return den

        def unrolled(fn):
            def body(it, carry):
                for u in range(ATTN_UNROLL):
                    fn(it * ATTN_UNROLL + u)
                return carry
            lax.fori_loop(0, n_blocks // ATTN_UNROLL, body, 0)

        unrolled(prep)

        def fast_body(it, dmin, block=block):
            for u in range(ATTN_UNROLL):
                dmin = jnp.minimum(dmin, block(it * ATTN_UNROLL + u, False))
            return dmin
        dmin = lax.fori_loop(0, n_blocks // ATTN_UNROLL, fast_body,
                             jnp.full((Q_BLOCK, LANES), 1.0, F32))

        @pl.when(jnp.logical_not(jnp.min(dmin) > DEN_SAFE))
        def _(block=block, unrolled=unrolled):
            unrolled(lambda b: block(b, True))

    l0 = l_scr[0]
    l1 = l_scr[1]
    l2 = l_scr[2]
    m = jnp.maximum(jnp.maximum(l0, l1), l2)
    e0 = jnp.exp(l0 - m)
    e1 = jnp.exp(l1 - m)
    e2 = jnp.exp(l2 - m)
    mix = (e0 * o_scr[0] + e1 * o_scr[1] + e2 * o_scr[2]) / (e0 + e1 + e2)
    out_ref[0] = mix.astype(BF16)


def _attention(qkv, cos_t, sin_t, gq, gk):
    b, _, s, _ = qkv[0].shape
    in_specs = []
    args = []
    for gi, dil in enumerate(DILATIONS):
        for part in range(3):
            in_specs.append(pl.BlockSpec((1, dil, s // dil, LANES),
                                         lambda bi, hp, part=part: (bi, 0, 0, 2 * part + hp)))
            args.append(qkv[gi])
    consts = _attn_consts()
    tab_spec = pl.BlockSpec((1, s, LANES), lambda bi, hp: (bi, 0, 0))
    gain_spec = pl.BlockSpec((1, LANES), lambda bi, hp: (0, 0))
    const_specs = [pl.BlockSpec(c.shape, lambda bi, hp, nd=c.ndim: (0,) * nd) for c in consts]
    key_rows = max(d * (s // d + BAND_HALF) for d in DILATIONS) + BAND_HALF
    return pl.pallas_call(
        functools.partial(_attn_kernel, seq=s),
        grid=(b, 2),
        in_specs=in_specs + [tab_spec] * 2 + [gain_spec] * 2 + const_specs,
        out_specs=pl.BlockSpec((1, s, LANES), lambda bi, hp: (bi, 0, hp)),
        out_shape=jax.ShapeDtypeStruct((b, s, GROUP_W), BF16),
        scratch_shapes=[
            pltpu.VMEM((2 * s, LANES), BF16),
            pltpu.VMEM((key_rows, LANES), BF16),
            pltpu.VMEM((key_rows, LANES), BF16),
            pltpu.VMEM((3, s, LANES), F32),
            pltpu.VMEM((3, s, LANES), F32),
            pltpu.VMEM((4, 2 * Q_BLOCK, ATTN_WIN), F32),
        ],
        compiler_params=pltpu.CompilerParams(
            dimension_semantics=("parallel", "parallel"), vmem_limit_bytes=VMEM_LIMIT_BYTES),
        name="dilated_attn",
    )(*args, cos_t, sin_t, gq, gk, *consts)


def _merge_kernel(x_ref, oa_ref, u_ref, uprev_ref, unext_ref, cb_ref, qm_ref, km_ref, vm_ref,
                  gates_ref, wconv_ref, wpa_ref, wpc_ref, wpm_ref, wout_ref, gffn_ref, wr_ref, br_ref,
                  x1_ref, route_ref, count_ref, om_scr, *, tm, nt):
    i = pl.program_id(1)
    u = u_ref[0].astype(F32)
    prev_row = uprev_ref[0].astype(F32)[15:16, :] * (i > 0).astype(F32)
    next_row = unext_ref[0].astype(F32)[0:1, :] * (i < nt - 1).astype(F32)
    row = lax.broadcasted_iota(jnp.int32, (tm, 1), 0)
    um = jnp.where(row == 0, prev_row, pltpu.roll(u, 1, 0))
    up = jnp.where(row == tm - 1, next_row, pltpu.roll(u, tm - 1, 0))
    wc = wconv_ref[...]
    y = wc[0:1, :] * um + wc[1:2, :] * u + wc[2:3, :] * up
    z = (cb_ref[0].astype(F32) * y).astype(BF16)

    for hh in range(MEM_HEADS):
        sl = slice(hh * MEM_HEAD_DIM, (hh + 1) * MEM_HEAD_DIM)
        s = lax.dot_general(qm_ref[0, :, sl], km_ref[0, :, sl], (((1,), (1,)), ((), ())),
                            preferred_element_type=F32) * (MEM_HEAD_DIM ** -0.5)
        m = jnp.max(s, axis=-1, keepdims=True)
        p = jnp.exp(s - m)
        den = jnp.sum(p, axis=-1, keepdims=True)
        o = jnp.dot(p.astype(BF16), vm_ref[0, :, sl], preferred_element_type=F32) / den
        om_scr[:, sl] = o.astype(BF16)

    pa = jnp.dot(oa_ref[0], wpa_ref[...], preferred_element_type=F32)
    pc = jnp.dot(z, wpc_ref[...], preferred_element_type=F32)
    pm = jnp.dot(om_scr[...], wpm_ref[...], preferred_element_type=F32)
    merged = (gates_ref[0, :, 0:D_MODEL].astype(F32) * pa
              + gates_ref[0, :, D_MODEL:2 * D_MODEL].astype(F32) * pc
              + gates_ref[0, :, 2 * D_MODEL:3 * D_MODEL].astype(F32) * pm)
    x1 = x_ref[0] + jnp.dot(merged.astype(BF16), wout_ref[...], preferred_element_type=F32)
    for c in range(ROW_CHUNKS):
        x1_ref[pl.ds(c, tm, stride=ROW_CHUNKS), :] = x1[:, c * LANES:(c + 1) * LANES]

    h2 = _rms(x1, gffn_ref[...])
    h_hi = h2.astype(BF16)
    h_lo = (h2 - h_hi.astype(F32)).astype(BF16)
    lg = (jnp.dot(h_hi, wr_ref[0], preferred_element_type=F32)
          + jnp.dot(h_lo, wr_ref[0], preferred_element_type=F32)
          + jnp.dot(h_hi, wr_ref[1], preferred_element_type=F32)) + br_ref[...]

    lane = lax.broadcasted_iota(jnp.int32, (tm, LANES), 1).astype(F32)
    far = float(LANES)
    is_group = lane < N_EXPERT_GROUPS
    gl = jnp.where(is_group, lg, ROUTE_NEG)
    mg = jnp.max(gl, axis=-1, keepdims=True)
    gidx = jnp.min(jnp.where(gl == mg, lane, far), axis=-1, keepdims=True)
    pg_top = 1.0 / jnp.sum(jnp.where(is_group, jnp.exp(lg - mg), 0.0), axis=-1, keepdims=True)
    first = N_EXPERT_GROUPS + EXPERTS_PER_GROUP * gidx
    in_sel = jnp.logical_and(lane >= first, lane < first + EXPERTS_PER_GROUP)
    sel = jnp.where(in_sel, lg, ROUTE_NEG)
    m1 = jnp.max(sel, axis=-1, keepdims=True)
    i1 = jnp.min(jnp.where(jnp.logical_and(in_sel, sel == m1), lane, far), axis=-1, keepdims=True)
    keep = jnp.logical_and(in_sel, lane != i1)
    sel2 = jnp.where(keep, lg, ROUTE_NEG)
    m2 = jnp.max(sel2, axis=-1, keepdims=True)
    i2 = jnp.min(jnp.where(jnp.logical_and(keep, sel2 == m2), lane, far), axis=-1, keepdims=True)
    r = jnp.exp(m2 - m1)
    w1 = pg_top / (1.0 + r)
    w2 = w1 * r
    route = jnp.where(lane == 0, i1 - N_EXPERT_GROUPS,
                      jnp.where(lane == 1, i2 - N_EXPERT_GROUPS,
                                jnp.where(lane == 2, w1, jnp.where(lane == 3, w2, 0.0))))
    route_ref[...] = jnp.transpose(route)[0:8, :]

    @pl.when(jnp.logical_and(pl.program_id(0) == 0, i == 0))
    def _():
        count_ref[...] = jnp.zeros_like(count_ref)
    picked = jnp.where(jnp.logical_or(lane == i1, lane == i2), 1.0, 0.0)
    count_ref[...] += jnp.sum(picked, axis=0, keepdims=True)


def _merge(x, o_attn, u, cb, qm, km, vm, gates, w_conv, wpa, wpc, wpm, wout, g_ffn, w_r, b_r):
    b, s, d = x.shape
    tm = ROW_TILE
    nt = s // tm
    halo = 16
    hb = tm // halo

    def tile(width):
        return pl.BlockSpec((1, tm, width), lambda bi, i: (bi, i, 0))

    def whole(arr):
        return pl.BlockSpec(arr.shape, lambda bi, i: (0,) * arr.ndim)

    def per_batch(arr):
        return pl.BlockSpec((1,) + arr.shape[1:], lambda bi, i: (bi, 0, 0))

    in_specs = [
        tile(d), tile(GROUP_W), tile(CONV_WIDTH),
        pl.BlockSpec((1, halo, CONV_WIDTH), lambda bi, i: (bi, jnp.maximum(i * hb - 1, 0), 0)),
        pl.BlockSpec((1, halo, CONV_WIDTH),
                     lambda bi, i: (bi, jnp.minimum((i + 1) * hb, s // halo - 1), 0)),
        tile(CONV_WIDTH), tile(MEM_WIDTH), per_batch(km), per_batch(vm), tile(N_BRANCH * d),
        whole(w_conv), whole(wpa), whole(wpc), whole(wpm), whole(wout), whole(g_ffn),
        whole(w_r), whole(b_r),
    ]
    return pl.pallas_call(
        functools.partial(_merge_kernel, tm=tm, nt=nt),
        grid=(b, nt),
        in_specs=in_specs,
        out_specs=[pl.BlockSpec((tm * ROW_CHUNKS, LANES), lambda bi, i: (bi * nt + i, 0)),
                   pl.BlockSpec((ROUTE_ROWS, tm), lambda bi, i: (0, bi * nt + i)),
                   pl.BlockSpec((ROUTE_ROWS, LANES), lambda bi, i: (0, 0))],
        out_shape=[jax.ShapeDtypeStruct((b * s * ROW_CHUNKS, LANES), F32),
                   jax.ShapeDtypeStruct((ROUTE_ROWS, b * s), F32),
                   jax.ShapeDtypeStruct((ROUTE_ROWS, LANES), F32)],
        scratch_shapes=[pltpu.VMEM((tm, MEM_WIDTH), BF16)],
        compiler_params=pltpu.CompilerParams(
            dimension_semantics=("arbitrary", "arbitrary"), vmem_limit_bytes=VMEM_LIMIT_BYTES),
        name="merge",
    )(x, o_attn, u, u, u, cb, qm, km, vm, gates, w_conv, wpa, wpc, wpm, wout, g_ffn, w_r, b_r)


DRAIN_STEPS = 2


def _expert_kernel(blk_exp_ref, nvalid_ref, tok_ref, tok_next_ref, dst_ref, x1_hbm, gffn_ref,
                   wg_ref, wu_ref, wd_ref, y_hbm, xbuf, hbuf, ybuf, wg_s, wu_s, wd_s, gsem, ssem):
    i = pl.program_id(0)
    slot = lax.rem(i, 2)
    rows = MOE_BLOCK * ROW_CHUNKS

    def start_gather(idx_ref, s):
        def body(it, c):
            for u in range(GATHER_UNROLL):
                j = it * GATHER_UNROLL + u
                src_row = pl.multiple_of(idx_ref[0, 0, j], ROW_CHUNKS)
                pltpu.make_async_copy(x1_hbm.at[pl.ds(src_row, ROW_CHUNKS), :],
                                      xbuf.at[s, pl.ds(j * ROW_CHUNKS, ROW_CHUNKS), :],
                                      gsem.at[s]).start()
            return c
        lax.fori_loop(0, MOE_BLOCK // GATHER_UNROLL, body, 0)

    def wait_gather(s):
        pltpu.make_async_copy(x1_hbm.at[pl.ds(0, rows), :], xbuf.at[s], gsem.at[s]).wait()

    def start_scatter(s):
        def body(it, c):
            for u in range(GATHER_UNROLL):
                j = it * GATHER_UNROLL + u
                dst_row = pl.multiple_of(dst_ref[0, 0, j], ROW_CHUNKS)
                pltpu.make_async_copy(ybuf.at[s, pl.ds(j * ROW_CHUNKS, ROW_CHUNKS), :],
                                      y_hbm.at[pl.ds(dst_row, ROW_CHUNKS), :], ssem.at[s]).start()
            return c
        lax.fori_loop(0, MOE_BLOCK // GATHER_UNROLL, body, 0)

    def wait_scatter(s):
        pltpu.make_async_copy(ybuf.at[s], y_hbm.at[pl.ds(0, rows), :], ssem.at[s]).wait()

    nvalid = nvalid_ref[i]
    prev = jnp.maximum(i - 1, 0)
    prev2 = jnp.maximum(i - 2, 0)

    @pl.when(i == 0)
    def _():
        start_gather(tok_ref, 0)
        ybuf[0] = jnp.zeros((rows, LANES), F32)
        spare = pltpu.make_async_copy(ybuf.at[0], y_hbm.at[pl.ds(y_hbm.shape[0] - rows, rows), :],
                                      ssem.at[0])
        spare.start()
        spare.wait()

    @pl.when(jnp.logical_and(i >= 2, nvalid_ref[prev2] > 0))
    def _():
        wait_scatter(slot)

    @pl.when(jnp.logical_or(i == 0, blk_exp_ref[i] != blk_exp_ref[prev]))
    def _():
        wg_s[...] = wg_ref[0].astype(BF16)
        wu_s[...] = wu_ref[0].astype(BF16)
        wd_s[...] = wd_ref[0].astype(BF16)

    @pl.when(nvalid > 0)
    def _():
        wait_gather(slot)
        start_gather(tok_next_ref, 1 - slot)

        ss = None
        for c in range(ROW_CHUNKS):
            xc = xbuf[slot, pl.ds(c, MOE_BLOCK, stride=ROW_CHUNKS), :]
            ss = xc * xc if ss is None else ss + xc * xc
        scale = lax.rsqrt(jnp.sum(ss, axis=-1, keepdims=True) * (1.0 / D_MODEL) + EPS)
        for c in range(ROW_CHUNKS):
            sl = slice(c * LANES, (c + 1) * LANES)
            xc = xbuf[slot, pl.ds(c, MOE_BLOCK, stride=ROW_CHUNKS), :]
            hbuf[:, sl] = (xc * scale * gffn_ref[:, sl]).astype(BF16)
        h2 = hbuf[...]
        a = jnp.dot(h2, wg_s[...], preferred_element_type=F32)
        up = jnp.dot(h2, wu_s[...], preferred_element_type=F32)
        act = (a * jax.nn.sigmoid(a) * up).astype(BF16)
        y = jnp.dot(act, wd_s[...], preferred_element_type=F32)
        for c in range(ROW_CHUNKS):
            ybuf[slot, pl.ds(c, MOE_BLOCK, stride=ROW_CHUNKS), :] = y[:, c * LANES:(c + 1) * LANES]
        start_scatter(slot)

    @pl.when(jnp.logical_and(nvalid == 0, jnp.logical_and(i > 0, nvalid_ref[prev] > 0)))
    def _():
        wait_gather(slot)


def _experts(x1r, g_ffn, blk_exp, nvalid, row_tok, row_dst, wg, wu, wd):
    rows_total, _ = x1r.shape
    d = D_MODEL
    n_blk = blk_exp.shape[0]
    idx_spec = lambda fn: pl.BlockSpec((1, 1, MOE_BLOCK), fn, memory_space=pltpu.SMEM)
    grid_spec = pltpu.PrefetchScalarGridSpec(
        num_scalar_prefetch=2,
        grid=(n_blk,),
        in_specs=[
            idx_spec(lambda i, be, nv: (i, 0, 0)),
            idx_spec(lambda i, be, nv: (jnp.minimum(i + 1, n_blk - 1), 0, 0)),
            idx_spec(lambda i, be, nv: (i, 0, 0)),
            pl.BlockSpec(memory_space=pl.ANY),
            pl.BlockSpec((1, d), lambda i, be, nv: (0, 0)),
            pl.BlockSpec((1, d, EXPERT_FF), lambda i, be, nv: (be[i], 0, 0)),
            pl.BlockSpec((1, d, EXPERT_FF), lambda i, be, nv: (be[i], 0, 0)),
            pl.BlockSpec((1, EXPERT_FF, d), lambda i, be, nv: (be[i], 0, 0)),
        ],
        out_specs=pl.BlockSpec(memory_space=pl.ANY),
        scratch_shapes=[
            pltpu.VMEM((2, MOE_BLOCK * ROW_CHUNKS, LANES), F32),
            pltpu.VMEM((MOE_BLOCK, d), BF16),
            pltpu.VMEM((2, MOE_BLOCK * ROW_CHUNKS, LANES), F32),
            pltpu.VMEM((d, EXPERT_FF), BF16),
            pltpu.VMEM((d, EXPERT_FF), BF16),
            pltpu.VMEM((EXPERT_FF, d), BF16),
            pltpu.SemaphoreType.DMA((2,)),
            pltpu.SemaphoreType.DMA((2,)),
        ],
    )
    return pl.pallas_call(
        _expert_kernel,
        grid_spec=grid_spec,
        out_shape=jax.ShapeDtypeStruct((TOP_K * rows_total + MOE_BLOCK * ROW_CHUNKS, LANES), F32),
        compiler_params=pltpu.CompilerParams(
            dimension_semantics=("arbitrary",), vmem_limit_bytes=VMEM_LIMIT_BYTES),
        name="experts",
    )(blk_exp, nvalid, row_tok, row_tok, row_dst, x1r, g_ffn, wg, wu, wd)


def _combine_kernel(x1_ref, y0_ref, y1_ref, route_ref, o_ref, *, tm):
    w_cols = jnp.transpose(route_ref[...])
    w0 = w_cols[:, 2:3]
    w1 = w_cols[:, 3:4]
    for c in range(ROW_CHUNKS):
        rows = pl.ds(c, tm, stride=ROW_CHUNKS)
        o_ref[:, c * LANES:(c + 1) * LANES] = (x1_ref[rows, :] + w0 * y0_ref[rows, :]
                                               + w1 * y1_ref[rows, :])


def _combine(x1r, y, route_t):
    n = x1r.shape[0] // ROW_CHUNKS
    tm = ROW_TILE
    nt = n // tm
    blk = (tm * ROW_CHUNKS, LANES)
    return pl.pallas_call(
        functools.partial(_combine_kernel, tm=tm),
        grid=(nt,),
        in_specs=[pl.BlockSpec(blk, lambda i: (i, 0)),
                  pl.BlockSpec(blk, lambda i: (i, 0)),
                  pl.BlockSpec(blk, lambda i: (i + nt, 0)),
                  pl.BlockSpec((ROUTE_ROWS, tm), lambda i: (0, i))],
        out_specs=pl.BlockSpec((tm, D_MODEL), lambda i: (i, 0)),
        out_shape=jax.ShapeDtypeStruct((n, D_MODEL), F32),
        compiler_params=pltpu.CompilerParams(dimension_semantics=("parallel",)),
        name="combine",
    )(x1r, y, y, route_t)


def _rope_tables(positions):
    half = ROT_DIM // 2
    dim = np.arange(LANES) % HEAD_DIM
    rotary = dim < ROT_DIM
    inv = ROPE_THETA ** (-jnp.arange(0, ROT_DIM, 2, dtype=F32) / ROT_DIM)
    inv_lane = jnp.where(jnp.asarray(rotary), inv[jnp.asarray(dim % half)], 0.0)
    sign = jnp.asarray(np.where(dim < half, -1.0, 1.0), F32)
    ang = positions.astype(F32)[..., None] * inv_lane
    return jnp.cos(ang), jnp.sin(ang) * sign


def _route(route_t, counts, n):
    nk = n * TOP_K
    e = route_t[0:TOP_K].reshape(nk).astype(jnp.int32)
    a = jnp.arange(nk, dtype=jnp.int32)
    _, a_s = lax.sort((e, a), num_keys=1, is_stable=True)
    padded = ((counts + MOE_BLOCK - 1) // MOE_BLOCK) * MOE_BLOCK
    pend = jnp.cumsum(padded)
    pstart = pend - padded
    start = jnp.cumsum(counts) - counts
    n_blk = nk // MOE_BLOCK + N_EXPERTS + DRAIN_STEPS
    blk_start = jnp.arange(n_blk, dtype=jnp.int32) * MOE_BLOCK
    blk_exp = jnp.minimum(jnp.sum((pend[None, :] <= blk_start[:, None]).astype(jnp.int32), axis=1),
                          N_EXPERTS - 1).astype(jnp.int32)
    nvalid = jnp.clip(counts[blk_exp] - (blk_start - pstart[blk_exp]), 0, MOE_BLOCK)
    nvalid = jnp.where(blk_start < pend[-1], nvalid, 0).astype(jnp.int32)
    j = jnp.arange(MOE_BLOCK, dtype=jnp.int32)[None, :]
    valid = j < nvalid[:, None]
    src = jnp.clip((start[blk_exp] + blk_start - pstart[blk_exp])[:, None] + j, 0, nk - 1)
    a_p = a_s[src]
    tok = (jnp.where(valid, a_p - n * (a_p >= n).astype(jnp.int32), 0) * ROW_CHUNKS).astype(jnp.int32)
    dst = (jnp.where(valid, a_p, TOP_K * n + j) * ROW_CHUNKS).astype(jnp.int32)
    return blk_exp, nvalid, tok.reshape(n_blk, 1, MOE_BLOCK), dst.reshape(n_blk, 1, MOE_BLOCK)


def kernel(x, mem, positions, g_mix, g_mem, w_in, g_qn_attn, g_kn_attn, w_conv, w_mem_kv, g_qn_mem,
           g_kn_mem, w_proj_attn, w_proj_conv, w_proj_mem, w_out, g_ffn, w_router_group,
           b_router_group, w_router_expert, b_router_expert, w_gate, w_up, w_down):
    b, s, d = x.shape
    n = b * s
    cos_t, sin_t = _rope_tables(positions)
    for l in range(w_in.shape[0]):
        wl = w_in[l]
        aw = N_BRANCH * GROUP_W
        cols = []
        for gi in range(len(DILATIONS)):
            for part in range(3):
                c0 = part * aw + gi * GROUP_W
                cols.append(wl[:, c0:c0 + GROUP_W])
        cols.append(wl[:, 3 * aw:])
        w_all = jnp.concatenate(cols, axis=1).astype(BF16)

        q0, q1, q2, u, cb, qm, gates = _inproj(x, g_mix[l][None], g_qn_mem[l][None], w_all)
        km, vm = _memkv(mem, g_mem[l][None], g_kn_mem[l][None], w_mem_kv[l].astype(BF16))
        gq = jnp.tile(g_qn_attn[l], 2)[None]
        gk = jnp.tile(g_kn_attn[l], 2)[None]
        o_attn = _attention((q0, q1, q2), cos_t, sin_t, gq, gk)

        w_r = jnp.zeros((d, LANES), F32)
        w_r = w_r.at[:, :N_EXPERT_GROUPS].set(w_router_group[l])
        w_r = w_r.at[:, N_EXPERT_GROUPS:N_EXPERT_GROUPS + N_EXPERTS].set(w_router_expert[l])
        b_r = jnp.zeros((1, LANES), F32)
        b_r = b_r.at[0, :N_EXPERT_GROUPS].set(b_router_group[l])
        b_r = b_r.at[0, N_EXPERT_GROUPS:N_EXPERT_GROUPS + N_EXPERTS].set(b_router_expert[l])
        w_r_hi = w_r.astype(BF16)
        w_r_lo = (w_r - w_r_hi.astype(F32)).astype(BF16)
        x1, route_t, count_t = _merge(x, o_attn, u, cb, qm, km, vm, gates, w_conv[l],
                                      w_proj_attn[l].astype(BF16), w_proj_conv[l].astype(BF16),
                                      w_proj_mem[l].astype(BF16), w_out[l].astype(BF16),
                                      g_ffn[l][None], jnp.stack([w_r_hi, w_r_lo]), b_r)

        counts = count_t[0, N_EXPERT_GROUPS:N_EXPERT_GROUPS + N_EXPERTS].astype(jnp.int32)
        blk_exp, nvalid, row_tok, row_dst = _route(route_t, counts, n)
        y = _experts(x1, g_ffn[l][None], blk_exp, nvalid, row_tok, row_dst,
                     w_gate[l], w_up[l], w_down[l])
        x = _combine(x1, y, route_t).reshape(b, s, d)
    return x
```

```python
import functools

import numpy as np
import jax
import jax.numpy as jnp
from jax import lax
from jax.experimental import pallas as pl
from jax.experimental.pallas import tpu as pltpu

F32 = jnp.float32
BF16 = jnp.bfloat16

D_MODEL = 1024
EPS = 1e-6
HEAD_DIM = 64
DILATIONS = (1, 4, 16)
BAND_HALF = 64
ATTN_SLOTS = 4
GROUP_W = ATTN_SLOTS * HEAD_DIM
QKV_W = 3 * GROUP_W
ROT_DIM = 16
ROPE_THETA = 500000.0
CONV_WIDTH = 768
MEM_HEADS = 4
MEM_HEAD_DIM = 128
MEM_WIDTH = 512
N_BRANCH = 3
N_EXPERT_GROUPS = 4
EXPERTS_PER_GROUP = 8
N_EXPERTS = 32
TOP_K = 2
EXPERT_FF = 512

LANES = 128
VMEM_LIMIT_BYTES = 56 * 1024 * 1024

ROW_TILE = 512
Q_BLOCK = 128
MOE_BLOCK = 256
ROW_CHUNKS = D_MODEL // LANES
GATHER_UNROLL = 8
NEG_BIG = -1e30
ROUTE_NEG = -3e38
ROUTE_ROWS = 8


def _rms(t, gain):
    return t * lax.rsqrt(jnp.mean(t * t, axis=-1, keepdims=True) + EPS) * gain


def _inproj_kernel(x_ref, g_ref, gqm_ref, w_ref,
                   qkv0_ref, qkv1_ref, qkv2_ref, u_ref, cb_ref, qm_ref, gates_ref,
                   h_scr, hp_scr, *, tm):
    x = x_ref[0]
    h = _rms(x, g_ref[...])
    n_chunks = h.shape[1] // LANES
    for c in range(n_chunks):
        h_scr[c] = h[:, c * LANES:(c + 1) * LANES]
    hb = h.astype(BF16)

    def mm(lhs, c0, c1):
        return jnp.dot(lhs, w_ref[:, c0:c1], preferred_element_type=F32)

    qkv0_ref[0, 0] = mm(hb, 0, QKV_W).astype(BF16)
    for gi, out_ref in ((1, qkv1_ref), (2, qkv2_ref)):
        d = DILATIONS[gi]
        rows = tm // d
        for r in range(d):
            for c in range(n_chunks):
                hp_scr[r * rows:(r + 1) * rows, c * LANES:(c + 1) * LANES] = (
                    h_scr[c, pl.ds(r, rows, stride=d), :].astype(BF16))
        res = mm(hp_scr[...], gi * QKV_W, (gi + 1) * QKV_W)
        out_ref[0] = res.reshape(d, rows, QKV_W).astype(BF16)

    c = 3 * QKV_W
    cx = mm(hb, c, c + CONV_WIDTH)
    cc = mm(hb, c + 2 * CONV_WIDTH, c + 3 * CONV_WIDTH)
    u_ref[0] = (cc * cx).astype(BF16)
    cb_ref[0] = mm(hb, c + CONV_WIDTH, c + 2 * CONV_WIDTH).astype(BF16)

    c += 3 * CONV_WIDTH
    qm = mm(hb, c, c + MEM_WIDTH)
    for hh in range(MEM_HEADS):
        sl = slice(hh * MEM_HEAD_DIM, (hh + 1) * MEM_HEAD_DIM)
        qm_ref[0, :, sl] = _rms(qm[:, sl], gqm_ref[...]).astype(BF16)

    c += MEM_WIDTH
    for j in range(N_BRANCH):
        gl = mm(hb, c + j * D_MODEL, c + (j + 1) * D_MODEL)
        gates_ref[0, :, j * D_MODEL:(j + 1) * D_MODEL] = jax.nn.sigmoid(gl).astype(BF16)


def _inproj(x, g_mix, g_qn_mem, w_all):
    b, s, d = x.shape
    tm = ROW_TILE
    nt = s // tm
    outs = []
    out_specs = []
    for dil in DILATIONS:
        outs.append(jax.ShapeDtypeStruct((b, dil, s // dil, QKV_W), BF16))
        out_specs.append(pl.BlockSpec((1, dil, tm // dil, QKV_W), lambda bi, i: (bi, 0, i, 0)))
    for width in (CONV_WIDTH, CONV_WIDTH, MEM_WIDTH, N_BRANCH * D_MODEL):
        outs.append(jax.ShapeDtypeStruct((b, s, width), BF16))
        out_specs.append(pl.BlockSpec((1, tm, width), lambda bi, i: (bi, i, 0)))
    return pl.pallas_call(
        functools.partial(_inproj_kernel, tm=tm),
        grid=(b, nt),
        in_specs=[
            pl.BlockSpec((1, tm, d), lambda bi, i: (bi, i, 0)),
            pl.BlockSpec((1, d), lambda bi, i: (0, 0)),
            pl.BlockSpec((1, MEM_HEAD_DIM), lambda bi, i: (0, 0)),
            pl.BlockSpec(w_all.shape, lambda bi, i: (0, 0), pipeline_mode=pl.Buffered(1)),
        ],
        out_specs=out_specs,
        out_shape=outs,
        scratch_shapes=[pltpu.VMEM((d // LANES, tm, LANES), F32), pltpu.VMEM((tm, d), BF16)],
        compiler_params=pltpu.CompilerParams(
            dimension_semantics=("parallel", "parallel"), vmem_limit_bytes=VMEM_LIMIT_BYTES),
        name="inproj",
    )(x, g_mix, g_qn_mem, w_all)


def _memkv_kernel(mem_ref, g_ref, gk_ref, w_ref, k_ref, v_ref):
    h = _rms(mem_ref[0], g_ref[...]).astype(BF16)
    kv = jnp.dot(h, w_ref[...], preferred_element_type=F32)
    for hh in range(MEM_HEADS):
        sl = slice(hh * MEM_HEAD_DIM, (hh + 1) * MEM_HEAD_DIM)
        k_ref[0, :, sl] = _rms(kv[:, sl], gk_ref[...]).astype(BF16)
    v_ref[0] = kv[:, MEM_WIDTH:].astype(BF16)


def _memkv(mem, g_mem, g_kn_mem, w_kv):
    b, m, d = mem.shape
    return pl.pallas_call(
        _memkv_kernel,
        grid=(b,),
        in_specs=[
            pl.BlockSpec((1, m, d), lambda bi: (bi, 0, 0)),
            pl.BlockSpec((1, d), lambda bi: (0, 0)),
            pl.BlockSpec((1, MEM_HEAD_DIM), lambda bi: (0, 0)),
            pl.BlockSpec(w_kv.shape, lambda bi: (0, 0)),
        ],
        out_specs=[pl.BlockSpec((1, m, MEM_WIDTH), lambda bi: (bi, 0, 0))] * 2,
        out_shape=[jax.ShapeDtypeStruct((b, m, MEM_WIDTH), BF16)] * 2,
        compiler_params=pltpu.CompilerParams(dimension_semantics=("parallel",)),
        name="memkv",
    )(mem, g_mem, g_kn_mem, w_kv)


ATTN_WIN = Q_BLOCK + 2 * BAND_HALF
ATTN_UNROLL = 16
DEN_SAFE = 1e-30


def _attn_consts():
    lane = np.arange(LANES)
    gsum = (lane[:, None] // HEAD_DIM == lane[None, :] // HEAD_DIM).astype(np.float32)
    half = ROT_DIM // 2
    dim = lane % HEAD_DIM
    src = np.where(dim < half, lane + half, np.where(dim < ROT_DIM, lane - half, -1))
    pswap = (lane[:, None] == src[None, :]).astype(np.float32)
    i = np.arange(Q_BLOCK)[:, None]
    c = np.arange(ATTN_WIN)[None, :]
    band = (c >= i) & (c <= i + 2 * BAND_HALF)
    first = c >= BAND_HALF
    last = c < Q_BLOCK + BAND_HALF
    variants = [band, band & first, band & last, band & first & last]
    bias = np.stack([np.where(v, 0.0, NEG_BIG) for v in variants]).astype(np.float32)
    return jnp.asarray(gsum, BF16), jnp.asarray(pswap, BF16), jnp.asarray(bias, F32)


def _attn_kernel(q0, k0, v0, q1, k1, v1, q2, k2, v2, cos_ref, sin_ref, gq_ref, gk_ref,
                 gsum_ref, pswap_ref, bias_ref, out_ref, qs, ks, vs, o_scr, l_scr, bias_scr, *, seq):
    lane = lax.broadcasted_iota(jnp.int32, (1, LANES), 1)
    low_half = lane < HEAD_DIM
    pad = BAND_HALF
    n_blocks = seq // Q_BLOCK

    bound = HEAD_DIM ** 0.5 * jnp.max(jnp.abs(gq_ref[...])) * jnp.max(jnp.abs(gk_ref[...]))
    shifted = bias_ref[...] - bound
    bias_scr[:, 0:Q_BLOCK, :] = shifted
    bias_scr[:, Q_BLOCK:2 * Q_BLOCK, :] = shifted

    def norm_rope(t, gain, cos, sin):
        ms = jnp.dot((t * t).astype(BF16), gsum_ref[...], preferred_element_type=F32) * (1.0 / HEAD_DIM)
        tn = t * lax.rsqrt(ms + EPS) * gain
        partner = jnp.dot(tn.astype(BF16), pswap_ref[...], preferred_element_type=F32)
        return tn * cos + partner * sin

    for gi, (q_ref, k_ref, v_ref) in enumerate(((q0, k0, v0), (q1, k1, v1), (q2, k2, v2))):
        d = DILATIONS[gi]
        length = seq // d
        nblk = length // Q_BLOCK
        region = length + pad

        zpad = jnp.zeros((pad, LANES), BF16)
        for r in range(d + 1):
            ks[r * region:r * region + pad, :] = zpad
            vs[r * region:r * region + pad, :] = zpad

        def split_index(b, nblk=nblk):
            if nblk == 1:
                return b, 0
            r = b // nblk
            return r, b - r * nblk

        def table_rows(r, n, d=d):
            if d == 1:
                return pl.ds(pl.multiple_of(n * Q_BLOCK, Q_BLOCK), Q_BLOCK)
            return pl.ds(r + d * n * Q_BLOCK, Q_BLOCK, stride=d)

        def prep(b, q_ref=q_ref, k_ref=k_ref, v_ref=v_ref, split_index=split_index,
                 table_rows=table_rows):
            r, n = split_index(b)
            src = pl.ds(pl.multiple_of(n * Q_BLOCK, Q_BLOCK), Q_BLOCK)
            rows = table_rows(r, n)
            cos = cos_ref[0, rows, :]
            sin = sin_ref[0, rows, :]
            qn = norm_rope(q_ref[0, r, src, :].astype(F32), gq_ref[...], cos, sin) * (HEAD_DIM ** -0.5)
            kn = norm_rope(k_ref[0, r, src, :].astype(F32), gk_ref[...], cos, sin)
            qbase = pl.multiple_of(b * 2 * Q_BLOCK, 2 * Q_BLOCK)
            qs[pl.ds(qbase, Q_BLOCK), :] = jnp.where(low_half, qn, 0.0).astype(BF16)
            qs[pl.ds(qbase + Q_BLOCK, Q_BLOCK), :] = jnp.where(low_half, 0.0, qn).astype(BF16)
            kdst = pl.ds(pl.multiple_of(b * Q_BLOCK + (r + 1) * pad, pad), Q_BLOCK)
            ks[kdst, :] = kn.astype(BF16)
            vs[kdst, :] = v_ref[0, r, src, :]

        def block(b, exact, gi=gi, nblk=nblk, split_index=split_index, table_rows=table_rows):
            r, n = split_index(b)
            variant = 3 if nblk == 1 else jnp.where(n == 0, 1, 0) + jnp.where(n == nblk - 1, 2, 0)
            qsrc = pl.ds(pl.multiple_of(b * 2 * Q_BLOCK, 2 * Q_BLOCK), 2 * Q_BLOCK)
            ksrc = pl.ds(pl.multiple_of(b * Q_BLOCK + r * pad, pad), ATTN_WIN)
            s = lax.dot_general(qs[qsrc, :], ks[ksrc, :], (((1,), (1,)), ((), ())),
                                preferred_element_type=F32) + bias_scr[variant]
            if exact:
                m = jnp.max(s, axis=-1, keepdims=True)
                s = s - m
                shift = jnp.where(low_half, m[0:Q_BLOCK], m[Q_BLOCK:]) + bound
            else:
                shift = bound
            p = jnp.exp(s)
            den = jnp.sum(p, axis=-1, keepdims=True)
            o2 = jnp.dot(p.astype(BF16), vs[ksrc, :], preferred_element_type=F32)
            o = jnp.where(low_half, o2[0:Q_BLOCK], o2[Q_BLOCK:])
            den = jnp.where(low_half, den[0:Q_BLOCK], den[Q_BLOCK:])
            dst = table_rows(r, n)
            o_scr[gi, dst, :] = o / den
            l_scr[gi, dst, :] = shift + jnp.log(den)
            return den

        def unrolled(fn):
            def body(it, carry):
                for u in range(ATTN_UNROLL):
                    fn(it * ATTN_UNROLL + u)
                return carry
            lax.fori_loop(0, n_blocks // ATTN_UNROLL, body, 0)

        unrolled(prep)

        def fast_body(it, dmin, block=block):
            for u in range(ATTN_UNROLL):
                dmin = jnp.minimum(dmin, block(it * ATTN_UNROLL + u, False))
            return dmin
        dmin = lax.fori_loop(0, n_blocks // ATTN_UNROLL, fast_body,
                             jnp.full((Q_BLOCK, LANES), 1.0, F32))

        @pl.when(jnp.logical_not(jnp.min(dmin) > DEN_SAFE))
        def _(block=block, unrolled=unrolled):
            unrolled(lambda b: block(b, True))

    l0 = l_scr[0]
    l1 = l_scr[1]
    l2 = l_scr[2]
    m = jnp.maximum(jnp.maximum(l0, l1), l2)
    e0 = jnp.exp(l0 - m)
    e1 = jnp.exp(l1 - m)
    e2 = jnp.exp(l2 - m)
    mix = (e0 * o_scr[0] + e1 * o_scr[1] + e2 * o_scr[2]) / (e0 + e1 + e2)
    out_ref[0] = mix.astype(BF16)


def _attention(qkv, cos_t, sin_t, gq, gk):
    b, _, s, _ = qkv[0].shape
    in_specs = []
    args = []
    for gi, dil in enumerate(DILATIONS):
        for part in range(3):
            in_specs.append(pl.BlockSpec((1, dil, s // dil, LANES),
                                         lambda bi, hp, part=part: (bi, 0, 0, 2 * part + hp)))
            args.append(qkv[gi])
    consts = _attn_consts()
    tab_spec = pl.BlockSpec((1, s, LANES), lambda bi, hp: (bi, 0, 0))
    gain_spec = pl.BlockSpec((1, LANES), lambda bi, hp: (0, 0))
    const_specs = [pl.BlockSpec(c.shape, lambda bi, hp, nd=c.ndim: (0,) * nd) for c in consts]
    key_rows = max(d * (s // d + BAND_HALF) for d in DILATIONS) + BAND_HALF
    return pl.pallas_call(
        functools.partial(_attn_kernel, seq=s),
        grid=(b, 2),
        in_specs=in_specs + [tab_spec] * 2 + [gain_spec] * 2 + const_specs,
        out_specs=pl.BlockSpec((1, s, LANES), lambda bi, hp: (bi, 0, hp)),
        out_shape=jax.ShapeDtypeStruct((b, s, GROUP_W), BF16),
        scratch_shapes=[
            pltpu.VMEM((2 * s, LANES), BF16),
            pltpu.VMEM((key_rows, LANES), BF16),
            pltpu.VMEM((key_rows, LANES), BF16),
            pltpu.VMEM((3, s, LANES), F32),
            pltpu.VMEM((3, s, LANES), F32),
            pltpu.VMEM((4, 2 * Q_BLOCK, ATTN_WIN), F32),
        ],
        compiler_params=pltpu.CompilerParams(
            dimension_semantics=("parallel", "parallel"), vmem_limit_bytes=VMEM_LIMIT_BYTES),
        name="dilated_attn",
    )(*args, cos_t, sin_t, gq, gk, *consts)


def _merge_kernel(x_ref, oa_ref, u_ref, uprev_ref, unext_ref, cb_ref, qm_ref, km_ref, vm_ref,
                  gates_ref, wconv_ref, wpa_ref, wpc_ref, wpm_ref, wout_ref, gffn_ref, wr_ref, br_ref,
                  x1_ref, route_ref, count_ref, om_scr, *, tm, nt):
    i = pl.program_id(1)
    u = u_ref[0].astype(F32)
    prev_row = uprev_ref[0].astype(F32)[15:16, :] * (i > 0).astype(F32)
    next_row = unext_ref[0].astype(F32)[0:1, :] * (i < nt - 1).astype(F32)
    row = lax.broadcasted_iota(jnp.int32, (tm, 1), 0)
    um = jnp.where(row == 0, prev_row, pltpu.roll(u, 1, 0))
    up = jnp.where(row == tm - 1, next_row, pltpu.roll(u, tm - 1, 0))
    wc = wconv_ref[...]
    y = wc[0:1, :] * um + wc[1:2, :] * u + wc[2:3, :] * up
    z = (cb_ref[0].astype(F32) * y).astype(BF16)

    for hh in range(MEM_HEADS):
        sl = slice(hh * MEM_HEAD_DIM, (hh + 1) * MEM_HEAD_DIM)
        s = lax.dot_general(qm_ref[0, :, sl], km_ref[0, :, sl], (((1,), (1,)), ((), ())),
                            preferred_element_type=F32) * (MEM_HEAD_DIM ** -0.5)
        m = jnp.max(s, axis=-1, keepdims=True)
        p = jnp.exp(s - m)
        den = jnp.sum(p, axis=-1, keepdims=True)
        o = jnp.dot(p.astype(BF16), vm_ref[0, :, sl], preferred_element_type=F32) / den
        om_scr[:, sl] = o.astype(BF16)

    pa = jnp.dot(oa_ref[0], wpa_ref[...], preferred_element_type=F32)
    pc = jnp.dot(z, wpc_ref[...], preferred_element_type=F32)
    pm = jnp.dot(om_scr[...], wpm_ref[...], preferred_element_type=F32)
    merged = (gates_ref[0, :, 0:D_MODEL].astype(F32) * pa
              + gates_ref[0, :, D_MODEL:2 * D_MODEL].astype(F32) * pc
              + gates_ref[0, :, 2 * D_MODEL:3 * D_MODEL].astype(F32) * pm)
    x1 = x_ref[0] + jnp.dot(merged.astype(BF16), wout_ref[...], preferred_element_type=F32)
    for c in range(ROW_CHUNKS):
        x1_ref[pl.ds(c, tm, stride=ROW_CHUNKS), :] = x1[:, c * LANES:(c + 1) * LANES]

    h2 = _rms(x1, gffn_ref[...])
    h_hi = h2.astype(BF16)
    h_lo = (h2 - h_hi.astype(F32)).astype(BF16)
    lg = (jnp.dot(h_hi, wr_ref[0], preferred_element_type=F32)
          + jnp.dot(h_lo, wr_ref[0], preferred_element_type=F32)
          + jnp.dot(h_hi, wr_ref[1], preferred_element_type=F32)) + br_ref[...]

    lane = lax.broadcasted_iota(jnp.int32, (tm, LANES), 1).astype(F32)
    far = float(LANES)
    is_group = lane < N_EXPERT_GROUPS
    gl = jnp.where(is_group, lg, ROUTE_NEG)
    mg = jnp.max(gl, axis=-1, keepdims=True)
    gidx = jnp.min(jnp.where(gl == mg, lane, far), axis=-1, keepdims=True)
    pg_top = 1.0 / jnp.sum(jnp.where(is_group, jnp.exp(lg - mg), 0.0), axis=-1, keepdims=True)
    first = N_EXPERT_GROUPS + EXPERTS_PER_GROUP * gidx
    in_sel = jnp.logical_and(lane >= first, lane < first + EXPERTS_PER_GROUP)
    sel = jnp.where(in_sel, lg, ROUTE_NEG)
    m1 = jnp.max(sel, axis=-1, keepdims=True)
    i1 = jnp.min(jnp.where(jnp.logical_and(in_sel, sel == m1), lane, far), axis=-1, keepdims=True)
    keep = jnp.logical_and(in_sel, lane != i1)
    sel2 = jnp.where(keep, lg, ROUTE_NEG)
    m2 = jnp.max(sel2, axis=-1, keepdims=True)
    i2 = jnp.min(jnp.where(jnp.logical_and(keep, sel2 == m2), lane, far), axis=-1, keepdims=True)
    r = jnp.exp(m2 - m1)
    w1 = pg_top / (1.0 + r)
    w2 = w1 * r
    route = jnp.where(lane == 0, i1 - N_EXPERT_GROUPS,
                      jnp.where(lane == 1, i2 - N_EXPERT_GROUPS,
                                jnp.where(lane == 2, w1, jnp.where(lane == 3, w2, 0.0))))
    route_ref[...] = jnp.transpose(route)[0:8, :]

    @pl.when(jnp.logical_and(pl.program_id(0) == 0, i == 0))
    def _():
        count_ref[...] = jnp.zeros_like(count_ref)
    picked = jnp.where(jnp.logical_or(lane == i1, lane == i2), 1.0, 0.0)
    count_ref[...] += jnp.sum(picked, axis=0, keepdims=True)


def _merge(x, o_attn, u, cb, qm, km, vm, gates, w_conv, wpa, wpc, wpm, wout, g_ffn, w_r, b_r):
    b, s, d = x.shape
    tm = ROW_TILE
    nt = s // tm
    halo = 16
    hb = tm // halo

    def tile(width):
        return pl.BlockSpec((1, tm, width), lambda bi, i: (bi, i, 0))

    def whole(arr):
        return pl.BlockSpec(arr.shape, lambda bi, i: (0,) * arr.ndim)

    def per_batch(arr):
        return pl.BlockSpec((1,) + arr.shape[1:], lambda bi, i: (bi, 0, 0))

    in_specs = [
        tile(d), tile(GROUP_W), tile(CONV_WIDTH),
        pl.BlockSpec((1, halo, CONV_WIDTH), lambda bi, i: (bi, jnp.maximum(i * hb - 1, 0), 0)),
        pl.BlockSpec((1, halo, CONV_WIDTH),
                     lambda bi, i: (bi, jnp.minimum((i + 1) * hb, s // halo - 1), 0)),
        tile(CONV_WIDTH), tile(MEM_WIDTH), per_batch(km), per_batch(vm), tile(N_BRANCH * d),
        whole(w_conv), whole(wpa), whole(wpc), whole(wpm), whole(wout), whole(g_ffn),
        whole(w_r), whole(b_r),
    ]
    return pl.pallas_call(
        functools.partial(_merge_kernel, tm=tm, nt=nt),
        grid=(b, nt),
        in_specs=in_specs,
        out_specs=[pl.BlockSpec((tm * ROW_CHUNKS, LANES), lambda bi, i: (bi * nt + i, 0)),
                   pl.BlockSpec((ROUTE_ROWS, tm), lambda bi, i: (0, bi * nt + i)),
                   pl.BlockSpec((ROUTE_ROWS, LANES), lambda bi, i: (0, 0))],
        out_shape=[jax.ShapeDtypeStruct((b * s * ROW_CHUNKS, LANES), F32),
                   jax.ShapeDtypeStruct((ROUTE_ROWS, b * s), F32),
                   jax.ShapeDtypeStruct((ROUTE_ROWS, LANES), F32)],
        scratch_shapes=[pltpu.VMEM((tm, MEM_WIDTH), BF16)],
        compiler_params=pltpu.CompilerParams(
            dimension_semantics=("arbitrary", "arbitrary"), vmem_limit_bytes=VMEM_LIMIT_BYTES),
        name="merge",
    )(x, o_attn, u, u, u, cb, qm, km, vm, gates, w_conv, wpa, wpc, wpm, wout, g_ffn, w_r, b_r)


DRAIN_STEPS = 2


def _expert_kernel(blk_exp_ref, nvalid_ref, tok_ref, tok_next_ref, dst_ref, x1_hbm, gffn_ref,
                   wg_ref, wu_ref, wd_ref, y_hbm, xbuf, hbuf, ybuf, wg_s, wu_s, wd_s, gsem, ssem):
    i = pl.program_id(0)
    slot = lax.rem(i, 2)
    rows = MOE_BLOCK * ROW_CHUNKS

    def start_gather(idx_ref, s):
        def body(it, c):
            for u in range(GATHER_UNROLL):
                j = it * GATHER_UNROLL + u
                src_row = pl.multiple_of(idx_ref[0, 0, j], ROW_CHUNKS)
                pltpu.make_async_copy(x1_hbm.at[pl.ds(src_row, ROW_CHUNKS), :],
                                      xbuf.at[s, pl.ds(j * ROW_CHUNKS, ROW_CHUNKS), :],
                                      gsem.at[s]).start()
            return c
        lax.fori_loop(0, MOE_BLOCK // GATHER_UNROLL, body, 0)

    def wait_gather(s):
        pltpu.make_async_copy(x1_hbm.at[pl.ds(0, rows), :], xbuf.at[s], gsem.at[s]).wait()

    def start_scatter(s):
        def body(it, c):
            for u in range(GATHER_UNROLL):
                j = it * GATHER_UNROLL + u
                dst_row = pl.multiple_of(dst_ref[0, 0, j], ROW_CHUNKS)
                pltpu.make_async_copy(ybuf.at[s, pl.ds(j * ROW_CHUNKS, ROW_CHUNKS), :],
                                      y_hbm.at[pl.ds(dst_row, ROW_CHUNKS), :], ssem.at[s]).start()
            return c
        lax.fori_loop(0, MOE_BLOCK // GATHER_UNROLL, body, 0)

    def wait_scatter(s):
        pltpu.make_async_copy(ybuf.at[s], y_hbm.at[pl.ds(0, rows), :], ssem.at[s]).wait()

    nvalid = nvalid_ref[i]
    prev = jnp.maximum(i - 1, 0)
    prev2 = jnp.maximum(i - 2, 0)

    @pl.when(i == 0)
    def _():
        start_gather(tok_ref, 0)
        ybuf[0] = jnp.zeros((rows, LANES), F32)
        spare = pltpu.make_async_copy(ybuf.at[0], y_hbm.at[pl.ds(y_hbm.shape[0] - rows, rows), :],
                                      ssem.at[0])
        spare.start()
        spare.wait()

    @pl.when(jnp.logical_and(i >= 2, nvalid_ref[prev2] > 0))
    def _():
        wait_scatter(slot)

    @pl.when(jnp.logical_or(i == 0, blk_exp_ref[i] != blk_exp_ref[prev]))
    def _():
        wg_s[...] = wg_ref[0].astype(BF16)
        wu_s[...] = wu_ref[0].astype(BF16)
        wd_s[...] = wd_ref[0].astype(BF16)

    @pl.when(nvalid > 0)
    def _():
        wait_gather(slot)
        start_gather(tok_next_ref, 1 - slot)

        ss = None
        for c in range(ROW_CHUNKS):
            xc = xbuf[slot, pl.ds(c, MOE_BLOCK, stride=ROW_CHUNKS), :]
            ss = xc * xc if ss is None else ss + xc * xc
        scale = lax.rsqrt(jnp.sum(ss, axis=-1, keepdims=True) * (1.0 / D_MODEL) + EPS)
        for c in range(ROW_CHUNKS):
            sl = slice(c * LANES, (c + 1) * LANES)
            xc = xbuf[slot, pl.ds(c, MOE_BLOCK, stride=ROW_CHUNKS), :]
            hbuf[:, sl] = (xc * scale * gffn_ref[:, sl]).astype(BF16)
        h2 = hbuf[...]
        a = jnp.dot(h2, wg_s[...], preferred_element_type=F32)
        up = jnp.dot(h2, wu_s[...], preferred_element_type=F32)
        act = (a * jax.nn.sigmoid(a) * up).astype(BF16)
        y = jnp.dot(act, wd_s[...], preferred_element_type=F32)
        for c in range(ROW_CHUNKS):
            ybuf[slot, pl.ds(c, MOE_BLOCK, stride=ROW_CHUNKS), :] = y[:, c * LANES:(c + 1) * LANES]
        start_scatter(slot)

    @pl.when(jnp.logical_and(nvalid == 0, jnp.logical_and(i > 0, nvalid_ref[prev] > 0)))
    def _():
        wait_gather(slot)


def _experts(x1r, g_ffn, blk_exp, nvalid, row_tok, row_dst, wg, wu, wd):
    rows_total, _ = x1r.shape
    d = D_MODEL
    n_blk = blk_exp.shape[0]
    idx_spec = lambda fn: pl.BlockSpec((1, 1, MOE_BLOCK), fn, memory_space=pltpu.SMEM)
    grid_spec = pltpu.PrefetchScalarGridSpec(
        num_scalar_prefetch=2,
        grid=(n_blk,),
        in_specs=[
            idx_spec(lambda i, be, nv: (i, 0, 0)),
            idx_spec(lambda i, be, nv: (jnp.minimum(i + 1, n_blk - 1), 0, 0)),
            idx_spec(lambda i, be, nv: (i, 0, 0)),
            pl.BlockSpec(memory_space=pl.ANY),
            pl.BlockSpec((1, d), lambda i, be, nv: (0, 0)),
            pl.BlockSpec((1, d, EXPERT_FF), lambda i, be, nv: (be[i], 0, 0)),
            pl.BlockSpec((1, d, EXPERT_FF), lambda i, be, nv: (be[i], 0, 0)),
            pl.BlockSpec((1, EXPERT_FF, d), lambda i, be, nv: (be[i], 0, 0)),
        ],
        out_specs=pl.BlockSpec(memory_space=pl.ANY),
        scratch_shapes=[
            pltpu.VMEM((2, MOE_BLOCK * ROW_CHUNKS, LANES), F32),
            pltpu.VMEM((MOE_BLOCK, d), BF16),
            pltpu.VMEM((2, MOE_BLOCK * ROW_CHUNKS, LANES), F32),
            pltpu.VMEM((d, EXPERT_FF), BF16),
            pltpu.VMEM((d, EXPERT_FF), BF16),
            pltpu.VMEM((EXPERT_FF, d), BF16),
            pltpu.SemaphoreType.DMA((2,)),
            pltpu.SemaphoreType.DMA((2,)),
        ],
    )
    return pl.pallas_call(
        _expert_kernel,
        grid_spec=grid_spec,
        out_shape=jax.ShapeDtypeStruct((TOP_K * rows_total + MOE_BLOCK * ROW_CHUNKS, LANES), F32),
        compiler_params=pltpu.CompilerParams(
            dimension_semantics=("arbitrary",), vmem_limit_bytes=VMEM_LIMIT_BYTES),
        name="experts",
    )(blk_exp, nvalid, row_tok, row_tok, row_dst, x1r, g_ffn, wg, wu, wd)


def _combine_kernel(x1_ref, y0_ref, y1_ref, route_ref, o_ref, *, tm):
    w_cols = jnp.transpose(route_ref[...])
    w0 = w_cols[:, 2:3]
    w1 = w_cols[:, 3:4]
    for c in range(ROW_CHUNKS):
        rows = pl.ds(c, tm, stride=ROW_CHUNKS)
        o_ref[:, c * LANES:(c + 1) * LANES] = (x1_ref[rows, :] + w0 * y0_ref[rows, :]
                                               + w1 * y1_ref[rows, :])


def _combine(x1r, y, route_t):
    n = x1r.shape[0] // ROW_CHUNKS
    tm = ROW_TILE
    nt = n // tm
    blk = (tm * ROW_CHUNKS, LANES)
    return pl.pallas_call(
        functools.partial(_combine_kernel, tm=tm),
        grid=(nt,),
        in_specs=[pl.BlockSpec(blk, lambda i: (i, 0)),
                  pl.BlockSpec(blk, lambda i: (i, 0)),
                  pl.BlockSpec(blk, lambda i: (i + nt, 0)),
                  pl.BlockSpec((ROUTE_ROWS, tm), lambda i: (0, i))],
        out_specs=pl.BlockSpec((tm, D_MODEL), lambda i: (i, 0)),
        out_shape=jax.ShapeDtypeStruct((n, D_MODEL), F32),
        compiler_params=pltpu.CompilerParams(dimension_semantics=("parallel",)),
        name="combine",
    )(x1r, y, y, route_t)


def _rope_tables(positions):
    half = ROT_DIM // 2
    lane = np.arange(LANES)
    dim = lane % HEAD_DIM
    rot = dim < ROT_DIM
    spread_cos = np.zeros((half, LANES), np.float32)
    spread_cos[dim[rot] % half, lane[rot]] = 1.0
    spread_sin = np.zeros((half, LANES), np.float32)
    spread_sin[dim[rot] % half, lane[rot]] = np.where(dim[rot] < half, -1.0, 1.0)
    inv = ROPE_THETA ** (-jnp.arange(0, ROT_DIM, 2, dtype=F32) / ROT_DIM)
    ang = inv[None, :, None] * positions.astype(F32)[:, None, :]
    spread = functools.partial(jnp.einsum, 'bjs,jl->bsl', precision=lax.Precision.HIGHEST)
    cos_t = spread(jnp.cos(ang), jnp.asarray(spread_cos)) + jnp.asarray((~rot).astype(np.float32))
    sin_t = spread(jnp.sin(ang), jnp.asarray(spread_sin))
    return cos_t, sin_t


def _route(route_t, counts, n):
    nk = n * TOP_K
    e = route_t[0:TOP_K].reshape(nk).astype(jnp.int32)
    a = jnp.arange(nk, dtype=jnp.int32)
    _, a_s = lax.sort((e, a), num_keys=1, is_stable=True)
    padded = ((counts + MOE_BLOCK - 1) // MOE_BLOCK) * MOE_BLOCK
    pend = jnp.cumsum(padded)
    pstart = pend - padded
    start = jnp.cumsum(counts) - counts
    n_blk = nk // MOE_BLOCK + N_EXPERTS + DRAIN_STEPS
    blk_start = jnp.arange(n_blk, dtype=jnp.int32) * MOE_BLOCK
    blk_exp = jnp.minimum(jnp.sum((pend[None, :] <= blk_start[:, None]).astype(jnp.int32), axis=1),
                          N_EXPERTS - 1).astype(jnp.int32)
    onehot = blk_exp[:, None] == jnp.arange(N_EXPERTS, dtype=jnp.int32)[None, :]
    pick = lambda table: jnp.sum(jnp.where(onehot, table[None, :], 0), axis=1)
    offset = blk_start - pick(pstart)
    nvalid = jnp.clip(pick(counts) - offset, 0, MOE_BLOCK)
    nvalid = jnp.where(blk_start < pend[-1], nvalid, 0).astype(jnp.int32)
    j = jnp.arange(MOE_BLOCK, dtype=jnp.int32)[None, :]
    valid = j < nvalid[:, None]
    src = jnp.clip((pick(start) + offset)[:, None] + j, 0, nk - 1)
    a_p = a_s[src]
    tok = (jnp.where(valid, a_p - n * (a_p >= n).astype(jnp.int32), 0) * ROW_CHUNKS).astype(jnp.int32)
    dst = (jnp.where(valid, a_p, TOP_K * n + j) * ROW_CHUNKS).astype(jnp.int32)
    return blk_exp, nvalid, tok.reshape(n_blk, 1, MOE_BLOCK), dst.reshape(n_blk, 1, MOE_BLOCK)


def kernel(x, mem, positions, g_mix, g_mem, w_in, g_qn_attn, g_kn_attn, w_conv, w_mem_kv, g_qn_mem,
           g_kn_mem, w_proj_attn, w_proj_conv, w_proj_mem, w_out, g_ffn, w_router_group,
           b_router_group, w_router_expert, b_router_expert, w_gate, w_up, w_down):
    b, s, d = x.shape
    n = b * s
    cos_t, sin_t = _rope_tables(positions)
    for l in range(w_in.shape[0]):
        wl = w_in[l]
        aw = N_BRANCH * GROUP_W
        cols = []
        for gi in range(len(DILATIONS)):
            for part in range(3):
                c0 = part * aw + gi * GROUP_W
                cols.append(wl[:, c0:c0 + GROUP_W])
        cols.append(wl[:, 3 * aw:])
        w_all = jnp.concatenate(cols, axis=1).astype(BF16)

        q0, q1, q2, u, cb, qm, gates = _inproj(x, g_mix[l][None], g_qn_mem[l][None], w_all)
        km, vm = _memkv(mem, g_mem[l][None], g_kn_mem[l][None], w_mem_kv[l].astype(BF16))
        gq = jnp.tile(g_qn_attn[l], 2)[None]
        gk = jnp.tile(g_kn_attn[l], 2)[None]
        o_attn = _attention((q0, q1, q2), cos_t, sin_t, gq, gk)

        w_r = jnp.zeros((d, LANES), F32)
        w_r = w_r.at[:, :N_EXPERT_GROUPS].set(w_router_group[l])
        w_r = w_r.at[:, N_EXPERT_GROUPS:N_EXPERT_GROUPS + N_EXPERTS].set(w_router_expert[l])
        b_r = jnp.zeros((1, LANES), F32)
        b_r = b_r.at[0, :N_EXPERT_GROUPS].set(b_router_group[l])
        b_r = b_r.at[0, N_EXPERT_GROUPS:N_EXPERT_GROUPS + N_EXPERTS].set(b_router_expert[l])
        w_r_hi = w_r.astype(BF16)
        w_r_lo = (w_r - w_r_hi.astype(F32)).astype(BF16)
        x1, route_t, count_t = _merge(x, o_attn, u, cb, qm, km, vm, gates, w_conv[l],
                                      w_proj_attn[l].astype(BF16), w_proj_conv[l].astype(BF16),
                                      w_proj_mem[l].astype(BF16), w_out[l].astype(BF16),
                                      g_ffn[l][None], jnp.stack([w_r_hi, w_r_lo]), b_r)

        counts = count_t[0, N_EXPERT_GROUPS:N_EXPERT_GROUPS + N_EXPERTS].astype(jnp.int32)
        blk_exp, nvalid, row_tok, row_dst = _route(route_t, counts, n)
        y = _experts(x1, g_ffn[l][None], blk_exp, nvalid, row_tok, row_dst,
                     w_gate[l], w_up[l], w_down[l])
        x = _combine(x1, y, route_t).reshape(b, s, d)
    return x
```

```python
import functools

import numpy as np
import jax
import jax.numpy as jnp
from jax import lax
from jax.experimental import pallas as pl
from jax.experimental.pallas import tpu as pltpu

F32 = jnp.float32
BF16 = jnp.bfloat16

D_MODEL = 1024
EPS = 1e-6
HEAD_DIM = 64
DILATIONS = (1, 4, 16)
BAND_HALF = 64
ATTN_SLOTS = 4
GROUP_W = ATTN_SLOTS * HEAD_DIM
QKV_W = 3 * GROUP_W
ROT_DIM = 16
ROPE_THETA = 500000.0
CONV_WIDTH = 768
MEM_HEADS = 4
MEM_HEAD_DIM = 128
MEM_WIDTH = 512
N_BRANCH = 3
N_EXPERT_GROUPS = 4
EXPERTS_PER_GROUP = 8
N_EXPERTS = 32
TOP_K = 2
EXPERT_FF = 512

LANES = 128
VMEM_LIMIT_BYTES = 56 * 1024 * 1024

ROW_TILE = 512
Q_BLOCK = 128
MOE_BLOCK = 256
ROW_CHUNKS = D_MODEL // LANES
GATHER_UNROLL = 8
NEG_BIG = -1e30
ROUTE_NEG = -3e38
ROUTE_ROWS = 8


def _rms(t, gain):
    return t * lax.rsqrt(jnp.mean(t * t, axis=-1, keepdims=True) + EPS) * gain


def _inproj_kernel(x_ref, g_ref, gqm_ref, w_ref,
                   qkv0_ref, qkv1_ref, qkv2_ref, u_ref, cb_ref, qm_ref, gates_ref,
                   h_scr, hp_scr, *, tm):
    x = x_ref[0]
    h = _rms(x, g_ref[...])
    n_chunks = h.shape[1] // LANES
    for c in range(n_chunks):
        h_scr[c] = h[:, c * LANES:(c + 1) * LANES]
    hb = h.astype(BF16)

    def mm(lhs, c0, c1):
        return jnp.dot(lhs, w_ref[:, c0:c1], preferred_element_type=F32)

    qkv0_ref[0, 0] = mm(hb, 0, QKV_W).astype(BF16)
    for gi, out_ref in ((1, qkv1_ref), (2, qkv2_ref)):
        d = DILATIONS[gi]
        rows = tm // d
        for r in range(d):
            for c in range(n_chunks):
                hp_scr[r * rows:(r + 1) * rows, c * LANES:(c + 1) * LANES] = (
                    h_scr[c, pl.ds(r, rows, stride=d), :].astype(BF16))
        res = mm(hp_scr[...], gi * QKV_W, (gi + 1) * QKV_W)
        out_ref[0] = res.reshape(d, rows, QKV_W).astype(BF16)

    c = 3 * QKV_W
    cx = mm(hb, c, c + CONV_WIDTH)
    cc = mm(hb, c + 2 * CONV_WIDTH, c + 3 * CONV_WIDTH)
    u_ref[0] = (cc * cx).astype(BF16)
    cb_ref[0] = mm(hb, c + CONV_WIDTH, c + 2 * CONV_WIDTH).astype(BF16)

    c += 3 * CONV_WIDTH
    qm = mm(hb, c, c + MEM_WIDTH)
    for hh in range(MEM_HEADS):
        sl = slice(hh * MEM_HEAD_DIM, (hh + 1) * MEM_HEAD_DIM)
        qm_ref[0, :, sl] = _rms(qm[:, sl], gqm_ref[...]).astype(BF16)

    c += MEM_WIDTH
    for j in range(N_BRANCH):
        gl = mm(hb, c + j * D_MODEL, c + (j + 1) * D_MODEL)
        gates_ref[0, :, j * D_MODEL:(j + 1) * D_MODEL] = jax.nn.sigmoid(gl).astype(BF16)


def _inproj(x, g_mix, g_qn_mem, w_all):
    b, s, d = x.shape
    tm = ROW_TILE
    nt = s // tm
    outs = []
    out_specs = []
    for dil in DILATIONS:
        outs.append(jax.ShapeDtypeStruct((b, dil, s // dil, QKV_W), BF16))
        out_specs.append(pl.BlockSpec((1, dil, tm // dil, QKV_W), lambda bi, i: (bi, 0, i, 0)))
    for width in (CONV_WIDTH, CONV_WIDTH, MEM_WIDTH, N_BRANCH * D_MODEL):
        outs.append(jax.ShapeDtypeStruct((b, s, width), BF16))
        out_specs.append(pl.BlockSpec((1, tm, width), lambda bi, i: (bi, i, 0)))
    return pl.pallas_call(
        functools.partial(_inproj_kernel, tm=tm),
        grid=(b, nt),
        in_specs=[
            pl.BlockSpec((1, tm, d), lambda bi, i: (bi, i, 0)),
            pl.BlockSpec((1, d), lambda bi, i: (0, 0)),
            pl.BlockSpec((1, MEM_HEAD_DIM), lambda bi, i: (0, 0)),
            pl.BlockSpec(w_all.shape, lambda bi, i: (0, 0), pipeline_mode=pl.Buffered(1)),
        ],
        out_specs=out_specs,
        out_shape=outs,
        scratch_shapes=[pltpu.VMEM((d // LANES, tm, LANES), F32), pltpu.VMEM((tm, d), BF16)],
        compiler_params=pltpu.CompilerParams(
            dimension_semantics=("parallel", "parallel"), vmem_limit_bytes=VMEM_LIMIT_BYTES),
        name="inproj",
    )(x, g_mix, g_qn_mem, w_all)


def _memkv_kernel(mem_ref, g_ref, gk_ref, w_ref, k_ref, v_ref):
    h = _rms(mem_ref[0], g_ref[...]).astype(BF16)
    kv = jnp.dot(h, w_ref[...], preferred_element_type=F32)
    for hh in range(MEM_HEADS):
        sl = slice(hh * MEM_HEAD_DIM, (hh + 1) * MEM_HEAD_DIM)
        k_ref[0, :, sl] = _rms(kv[:, sl], gk_ref[...]).astype(BF16)
    v_ref[0] = kv[:, MEM_WIDTH:].astype(BF16)


def _memkv(mem, g_mem, g_kn_mem, w_kv):
    b, m, d = mem.shape
    return pl.pallas_call(
        _memkv_kernel,
        grid=(b,),
        in_specs=[
            pl.BlockSpec((1, m, d), lambda bi: (bi, 0, 0)),
            pl.BlockSpec((1, d), lambda bi: (0, 0)),
            pl.BlockSpec((1, MEM_HEAD_DIM), lambda bi: (0, 0)),
            pl.BlockSpec(w_kv.shape, lambda bi: (0, 0)),
        ],
        out_specs=[pl.BlockSpec((1, m, MEM_WIDTH), lambda bi: (bi, 0, 0))] * 2,
        out_shape=[jax.ShapeDtypeStruct((b, m, MEM_WIDTH), BF16)] * 2,
        compiler_params=pltpu.CompilerParams(dimension_semantics=("parallel",)),
        name="memkv",
    )(mem, g_mem, g_kn_mem, w_kv)


ATTN_WIN = Q_BLOCK + 2 * BAND_HALF
ATTN_UNROLL = 16
DEN_SAFE = 1e-30


def _attn_consts():
    lane = np.arange(LANES)
    gsum = (lane[:, None] // HEAD_DIM == lane[None, :] // HEAD_DIM).astype(np.float32)
    half = ROT_DIM // 2
    dim = lane % HEAD_DIM
    src = np.where(dim < half, lane + half, np.where(dim < ROT_DIM, lane - half, -1))
    pswap = (lane[:, None] == src[None, :]).astype(np.float32)
    i = np.arange(Q_BLOCK)[:, None]
    c = np.arange(ATTN_WIN)[None, :]
    band = (c >= i) & (c <= i + 2 * BAND_HALF)
    first = c >= BAND_HALF
    last = c < Q_BLOCK + BAND_HALF
    variants = [band, band & first, band & last, band & first & last]
    bias = np.stack([np.where(v, 0.0, NEG_BIG) for v in variants]).astype(np.float32)
    return jnp.asarray(gsum, BF16), jnp.asarray(pswap, BF16), jnp.asarray(bias, F32)


def _attn_kernel(q0, k0, v0, q1, k1, v1, q2, k2, v2, cos_ref, sin_ref, gq_ref, gk_ref,
                 gsum_ref, pswap_ref, bias_ref, out_ref, qs, ks, vs, o_scr, l_scr, bias_scr, *, seq):
    lane = lax.broadcasted_iota(jnp.int32, (1, LANES), 1)
    low_half = lane < HEAD_DIM
    pad = BAND_HALF
    n_blocks = seq // Q_BLOCK

    bound = HEAD_DIM ** 0.5 * jnp.max(jnp.abs(gq_ref[...])) * jnp.max(jnp.abs(gk_ref[...]))
    shifted = bias_ref[...] - bound
    bias_scr[:, 0:Q_BLOCK, :] = shifted
    bias_scr[:, Q_BLOCK:2 * Q_BLOCK, :] = shifted

    def norm_rope(t, gain, cos, sin):
        ms = jnp.dot((t * t).astype(BF16), gsum_ref[...], preferred_element_type=F32) * (1.0 / HEAD_DIM)
        tn = t * lax.rsqrt(ms + EPS) * gain
        partner = jnp.dot(tn.astype(BF16), pswap_ref[...], preferred_element_type=F32)
        return tn * cos + partner * sin

    for gi, (q_ref, k_ref, v_ref) in enumerate(((q0, k0, v0), (q1, k1, v1), (q2, k2, v2))):
        d = DILATIONS[gi]
        length = seq // d
        nblk = length // Q_BLOCK
        region = length + pad

        zpad = jnp.zeros((pad, LANES), BF16)
        for r in range(d + 1):
            ks[r * region:r * region + pad, :] = zpad
            vs[r * region:r * region + pad, :] = zpad

        def split_index(b, nblk=nblk):
            if nblk == 1:
                return b, 0
            r = b // nblk
            return r, b - r * nblk

        def table_rows(r, n, d=d):
            if d == 1:
                return pl.ds(pl.multiple_of(n * Q_BLOCK, Q_BLOCK), Q_BLOCK)
            return pl.ds(r + d * n * Q_BLOCK, Q_BLOCK, stride=d)

        def prep(b, q_ref=q_ref, k_ref=k_ref, v_ref=v_ref, split_index=split_index,
                 table_rows=table_rows):
            r, n = split_index(b)
            src = pl.ds(pl.multiple_of(n * Q_BLOCK, Q_BLOCK), Q_BLOCK)
            rows = table_rows(r, n)
            cos = cos_ref[0, rows, :]
            sin = sin_ref[0, rows, :]
            qn = norm_rope(q_ref[0, r, src, :].astype(F32), gq_ref[...], cos, sin) * (HEAD_DIM ** -0.5)
            kn = norm_rope(k_ref[0, r, src, :].astype(F32), gk_ref[...], cos, sin)
            qbase = pl.multiple_of(b * 2 * Q_BLOCK, 2 * Q_BLOCK)
            qs[pl.ds(qbase, Q_BLOCK), :] = jnp.where(low_half, qn, 0.0).astype(BF16)
            qs[pl.ds(qbase + Q_BLOCK, Q_BLOCK), :] = jnp.where(low_half, 0.0, qn).astype(BF16)
            kdst = pl.ds(pl.multiple_of(b * Q_BLOCK + (r + 1) * pad, pad), Q_BLOCK)
            ks[kdst, :] = kn.astype(BF16)
            vs[kdst, :] = v_ref[0, r, src, :]

        def block(b, exact, gi=gi, nblk=nblk, split_index=split_index, table_rows=table_rows):
            r, n = split_index(b)
            variant = 3 if nblk == 1 else jnp.where(n == 0, 1, 0) + jnp.where(n == nblk - 1, 2, 0)
            qsrc = pl.ds(pl.multiple_of(b * 2 * Q_BLOCK, 2 * Q_BLOCK), 2 * Q_BLOCK)
            ksrc = pl.ds(pl.multiple_of(b * Q_BLOCK + r * pad, pad), ATTN_WIN)
            s = lax.dot_general(qs[qsrc, :], ks[ksrc, :], (((1,), (1,)), ((), ())),
                                preferred_element_type=F32) + bias_scr[variant]
            if exact:
                m = jnp.max(s, axis=-1, keepdims=True)
                s = s - m
                shift = jnp.where(low_half, m[0:Q_BLOCK], m[Q_BLOCK:]) + bound
            else:
                shift = bound
            p = jnp.exp(s)
            den = jnp.sum(p, axis=-1, keepdims=True)
            o2 = jnp.dot(p.astype(BF16), vs[ksrc, :], preferred_element_type=F32)
            o = jnp.where(low_half, o2[0:Q_BLOCK], o2[Q_BLOCK:])
            den = jnp.where(low_half, den[0:Q_BLOCK], den[Q_BLOCK:])
            dst = table_rows(r, n)
            o_scr[gi, dst, :] = o / den
            l_scr[gi, dst, :] = shift + jnp.log(den)
            return den

        def unrolled(fn):
            def body(it, carry):
                for u in range(ATTN_UNROLL):
                    fn(it * ATTN_UNROLL + u)
                return carry
            lax.fori_loop(0, n_blocks // ATTN_UNROLL, body, 0)

        unrolled(prep)

        def fast_body(it, dmin, block=block):
            for u in range(ATTN_UNROLL):
                dmin = jnp.minimum(dmin, block(it * ATTN_UNROLL + u, False))
            return dmin
        dmin = lax.fori_loop(0, n_blocks // ATTN_UNROLL, fast_body,
                             jnp.full((Q_BLOCK, LANES), 1.0, F32))

        @pl.when(jnp.logical_not(jnp.min(dmin) > DEN_SAFE))
        def _(block=block, unrolled=unrolled):
            unrolled(lambda b: block(b, True))

    l0 = l_scr[0]
    l1 = l_scr[1]
    l2 = l_scr[2]
    m = jnp.maximum(jnp.maximum(l0, l1), l2)
    e0 = jnp.exp(l0 - m)
    e1 = jnp.exp(l1 - m)
    e2 = jnp.exp(l2 - m)
    mix = (e0 * o_scr[0] + e1 * o_scr[1] + e2 * o_scr[2]) / (e0 + e1 + e2)
    out_ref[0] = mix.astype(BF16)


def _attention(qkv, cos_t, sin_t, gq, gk):
    b, _, s, _ = qkv[0].shape
    in_specs = []
    args = []
    for gi, dil in enumerate(DILATIONS):
        for part in range(3):
            in_specs.append(pl.BlockSpec((1, dil, s // dil, LANES),
                                         lambda bi, hp, part=part: (bi, 0, 0, 2 * part + hp)))
            args.append(qkv[gi])
    consts = _attn_consts()
    tab_spec = pl.BlockSpec((1, s, LANES), lambda bi, hp: (bi, 0, 0))
    gain_spec = pl.BlockSpec((1, LANES), lambda bi, hp: (0, 0))
    const_specs = [pl.BlockSpec(c.shape, lambda bi, hp, nd=c.ndim: (0,) * nd) for c in consts]
    key_rows = max(d * (s // d + BAND_HALF) for d in DILATIONS) + BAND_HALF
    return pl.pallas_call(
        functools.partial(_attn_kernel, seq=s),
        grid=(b, 2),
        in_specs=in_specs + [tab_spec] * 2 + [gain_spec] * 2 + const_specs,
        out_specs=pl.BlockSpec((1, s, LANES), lambda bi, hp: (bi, 0, hp)),
        out_shape=jax.ShapeDtypeStruct((b, s, GROUP_W), BF16),
        scratch_shapes=[
            pltpu.VMEM((2 * s, LANES), BF16),
            pltpu.VMEM((key_rows, LANES), BF16),
            pltpu.VMEM((key_rows, LANES), BF16),
            pltpu.VMEM((3, s, LANES), F32),
            pltpu.VMEM((3, s, LANES), F32),
            pltpu.VMEM((4, 2 * Q_BLOCK, ATTN_WIN), F32),
        ],
        compiler_params=pltpu.CompilerParams(
            dimension_semantics=("parallel", "parallel"), vmem_limit_bytes=VMEM_LIMIT_BYTES),
        name="dilated_attn",
    )(*args, cos_t, sin_t, gq, gk, *consts)


def _merge_kernel(x_ref, oa_ref, u_ref, uprev_ref, unext_ref, cb_ref, qm_ref, km_ref, vm_ref,
                  gates_ref, wconv_ref, wpa_ref, wpc_ref, wpm_ref, wout_ref, gffn_ref, wr_ref, br_ref,
                  x1_ref, route_ref, count_ref, om_scr, *, tm, nt):
    i = pl.program_id(1)
    u = u_ref[0].astype(F32)
    prev_row = uprev_ref[0].astype(F32)[15:16, :] * (i > 0).astype(F32)
    next_row = unext_ref[0].astype(F32)[0:1, :] * (i < nt - 1).astype(F32)
    row = lax.broadcasted_iota(jnp.int32, (tm, 1), 0)
    um = jnp.where(row == 0, prev_row, pltpu.roll(u, 1, 0))
    up = jnp.where(row == tm - 1, next_row, pltpu.roll(u, tm - 1, 0))
    wc = wconv_ref[...]
    y = wc[0:1, :] * um + wc[1:2, :] * u + wc[2:3, :] * up
    z = (cb_ref[0].astype(F32) * y).astype(BF16)

    for hh in range(MEM_HEADS):
        sl = slice(hh * MEM_HEAD_DIM, (hh + 1) * MEM_HEAD_DIM)
        s = lax.dot_general(qm_ref[0, :, sl], km_ref[0, :, sl], (((1,), (1,)), ((), ())),
                            preferred_element_type=F32) * (MEM_HEAD_DIM ** -0.5)
        m = jnp.max(s, axis=-1, keepdims=True)
        p = jnp.exp(s - m)
        den = jnp.sum(p, axis=-1, keepdims=True)
        o = jnp.dot(p.astype(BF16), vm_ref[0, :, sl], preferred_element_type=F32) / den
        om_scr[:, sl] = o.astype(BF16)

    pa = jnp.dot(oa_ref[0], wpa_ref[...], preferred_element_type=F32)
    pc = jnp.dot(z, wpc_ref[...], preferred_element_type=F32)
    pm = jnp.dot(om_scr[...], wpm_ref[...], preferred_element_type=F32)
    merged = (gates_ref[0, :, 0:D_MODEL].astype(F32) * pa
              + gates_ref[0, :, D_MODEL:2 * D_MODEL].astype(F32) * pc
              + gates_ref[0, :, 2 * D_MODEL:3 * D_MODEL].astype(F32) * pm)
    x1 = x_ref[0] + jnp.dot(merged.astype(BF16), wout_ref[...], preferred_element_type=F32)
    for c in range(ROW_CHUNKS):
        x1_ref[pl.ds(c, tm, stride=ROW_CHUNKS), :] = x1[:, c * LANES:(c + 1) * LANES]

    h2 = _rms(x1, gffn_ref[...])
    h_hi = h2.astype(BF16)
    h_lo = (h2 - h_hi.astype(F32)).astype(BF16)
    lg = (jnp.dot(h_hi, wr_ref[0], preferred_element_type=F32)
          + jnp.dot(h_lo, wr_ref[0], preferred_element_type=F32)
          + jnp.dot(h_hi, wr_ref[1], preferred_element_type=F32)) + br_ref[...]

    lane = lax.broadcasted_iota(jnp.int32, (tm, LANES), 1).astype(F32)
    far = float(LANES)
    is_group = lane < N_EXPERT_GROUPS
    gl = jnp.where(is_group, lg, ROUTE_NEG)
    mg = jnp.max(gl, axis=-1, keepdims=True)
    gidx = jnp.min(jnp.where(gl == mg, lane, far), axis=-1, keepdims=True)
    pg_top = 1.0 / jnp.sum(jnp.where(is_group, jnp.exp(lg - mg), 0.0), axis=-1, keepdims=True)
    first = N_EXPERT_GROUPS + EXPERTS_PER_GROUP * gidx
    in_sel = jnp.logical_and(lane >= first, lane < first + EXPERTS_PER_GROUP)
    sel = jnp.where(in_sel, lg, ROUTE_NEG)
    m1 = jnp.max(sel, axis=-1, keepdims=True)
    i1 = jnp.min(jnp.where(jnp.logical_and(in_sel, sel == m1), lane, far), axis=-1, keepdims=True)
    keep = jnp.logical_and(in_sel, lane != i1)
    sel2 = jnp.where(keep, lg, ROUTE_NEG)
    m2 = jnp.max(sel2, axis=-1, keepdims=True)
    i2 = jnp.min(jnp.where(jnp.logical_and(keep, sel2 == m2), lane, far), axis=-1, keepdims=True)
    r = jnp.exp(m2 - m1)
    w1 = pg_top / (1.0 + r)
    w2 = w1 * r
    route = jnp.where(lane == 0, i1 - N_EXPERT_GROUPS,
                      jnp.where(lane == 1, i2 - N_EXPERT_GROUPS,
                                jnp.where(lane == 2, w1, jnp.where(lane == 3, w2, 0.0))))
    route_ref[...] = jnp.transpose(route)[0:8, :]

    @pl.when(jnp.logical_and(pl.program_id(0) == 0, i == 0))
    def _():
        count_ref[...] = jnp.zeros_like(count_ref)
    picked = jnp.where(jnp.logical_or(lane == i1, lane == i2), 1.0, 0.0)
    count_ref[...] += jnp.sum(picked, axis=0, keepdims=True)


def _merge(x, o_attn, u, cb, qm, km, vm, gates, w_conv, wpa, wpc, wpm, wout, g_ffn, w_r, b_r):
    b, s, d = x.shape
    tm = ROW_TILE
    nt = s // tm
    halo = 16
    hb = tm // halo

    def tile(width):
        return pl.BlockSpec((1, tm, width), lambda bi, i: (bi, i, 0))

    def whole(arr):
        return pl.BlockSpec(arr.shape, lambda bi, i: (0,) * arr.ndim)

    def per_batch(arr):
        return pl.BlockSpec((1,) + arr.shape[1:], lambda bi, i: (bi, 0, 0))

    in_specs = [
        tile(d), tile(GROUP_W), tile(CONV_WIDTH),
        pl.BlockSpec((1, halo, CONV_WIDTH), lambda bi, i: (bi, jnp.maximum(i * hb - 1, 0), 0)),
        pl.BlockSpec((1, halo, CONV_WIDTH),
                     lambda bi, i: (bi, jnp.minimum((i + 1) * hb, s // halo - 1), 0)),
        tile(CONV_WIDTH), tile(MEM_WIDTH), per_batch(km), per_batch(vm), tile(N_BRANCH * d),
        whole(w_conv), whole(wpa), whole(wpc), whole(wpm), whole(wout), whole(g_ffn),
        whole(w_r), whole(b_r),
    ]
    return pl.pallas_call(
        functools.partial(_merge_kernel, tm=tm, nt=nt),
        grid=(b, nt),
        in_specs=in_specs,
        out_specs=[pl.BlockSpec((tm * ROW_CHUNKS, LANES), lambda bi, i: (bi * nt + i, 0)),
                   pl.BlockSpec((ROUTE_ROWS, tm), lambda bi, i: (0, bi * nt + i)),
                   pl.BlockSpec((ROUTE_ROWS, LANES), lambda bi, i: (0, 0))],
        out_shape=[jax.ShapeDtypeStruct((b * s * ROW_CHUNKS, LANES), F32),
                   jax.ShapeDtypeStruct((ROUTE_ROWS, b * s), F32),
                   jax.ShapeDtypeStruct((ROUTE_ROWS, LANES), F32)],
        scratch_shapes=[pltpu.VMEM((tm, MEM_WIDTH), BF16)],
        compiler_params=pltpu.CompilerParams(
            dimension_semantics=("arbitrary", "arbitrary"), vmem_limit_bytes=VMEM_LIMIT_BYTES),
        name="merge",
    )(x, o_attn, u, u, u, cb, qm, km, vm, gates, w_conv, wpa, wpc, wpm, wout, g_ffn, w_r, b_r)


DRAIN_STEPS = 2


def _expert_kernel(blk_exp_ref, nvalid_ref, tok_ref, tok_next_ref, dst_ref, x1_hbm, gffn_ref,
                   wg_ref, wu_ref, wd_ref, y_hbm, xbuf, hbuf, ybuf, wg_s, wu_s, wd_s, gsem, ssem):
    i = pl.program_id(0)
    slot = lax.rem(i, 2)
    rows = MOE_BLOCK * ROW_CHUNKS

    def start_gather(idx_ref, s):
        def body(it, c):
            for u in range(GATHER_UNROLL):
                j = it * GATHER_UNROLL + u
                src_row = pl.multiple_of(idx_ref[0, 0, j], ROW_CHUNKS)
                pltpu.make_async_copy(x1_hbm.at[pl.ds(src_row, ROW_CHUNKS), :],
                                      xbuf.at[s, pl.ds(j * ROW_CHUNKS, ROW_CHUNKS), :],
                                      gsem.at[s]).start(priority=u % 2)
            return c
        lax.fori_loop(0, MOE_BLOCK // GATHER_UNROLL, body, 0)

    def wait_gather(s):
        pltpu.make_async_copy(x1_hbm.at[pl.ds(0, rows), :], xbuf.at[s], gsem.at[s]).wait()

    def start_scatter(s):
        def body(it, c):
            for u in range(GATHER_UNROLL):
                j = it * GATHER_UNROLL + u
                dst_row = pl.multiple_of(dst_ref[0, 0, j], ROW_CHUNKS)
                pltpu.make_async_copy(ybuf.at[s, pl.ds(j * ROW_CHUNKS, ROW_CHUNKS), :],
                                      y_hbm.at[pl.ds(dst_row, ROW_CHUNKS), :],
                                      ssem.at[s]).start(priority=u % 2)
            return c
        lax.fori_loop(0, MOE_BLOCK // GATHER_UNROLL, body, 0)

    def wait_scatter(s):
        pltpu.make_async_copy(ybuf.at[s], y_hbm.at[pl.ds(0, rows), :], ssem.at[s]).wait()

    nvalid = nvalid_ref[i]
    prev = jnp.maximum(i - 1, 0)
    prev2 = jnp.maximum(i - 2, 0)

    @pl.when(i == 0)
    def _():
        start_gather(tok_ref, 0)
        ybuf[0] = jnp.zeros((rows, LANES), F32)
        spare = pltpu.make_async_copy(ybuf.at[0], y_hbm.at[pl.ds(y_hbm.shape[0] - rows, rows), :],
                                      ssem.at[0])
        spare.start()
        spare.wait()

    @pl.when(jnp.logical_and(i >= 2, nvalid_ref[prev2] > 0))
    def _():
        wait_scatter(slot)

    @pl.when(jnp.logical_or(i == 0, blk_exp_ref[i] != blk_exp_ref[prev]))
    def _():
        wg_s[...] = wg_ref[0].astype(BF16)
        wu_s[...] = wu_ref[0].astype(BF16)
        wd_s[...] = wd_ref[0].astype(BF16)

    @pl.when(nvalid > 0)
    def _():
        wait_gather(slot)
        start_gather(tok_next_ref, 1 - slot)

        ss = None
        for c in range(ROW_CHUNKS):
            xc = xbuf[slot, pl.ds(c, MOE_BLOCK, stride=ROW_CHUNKS), :]
            ss = xc * xc if ss is None else ss + xc * xc
        scale = lax.rsqrt(jnp.sum(ss, axis=-1, keepdims=True) * (1.0 / D_MODEL) + EPS)
        for c in range(ROW_CHUNKS):
            sl = slice(c * LANES, (c + 1) * LANES)
            xc = xbuf[slot, pl.ds(c, MOE_BLOCK, stride=ROW_CHUNKS), :]
            hbuf[:, sl] = (xc * scale * gffn_ref[:, sl]).astype(BF16)
        h2 = hbuf[...]
        a = jnp.dot(h2, wg_s[...], preferred_element_type=F32)
        up = jnp.dot(h2, wu_s[...], preferred_element_type=F32)
        act = (a * jax.nn.sigmoid(a) * up).astype(BF16)
        y = jnp.dot(act, wd_s[...], preferred_element_type=F32)
        for c in range(ROW_CHUNKS):
            ybuf[slot, pl.ds(c, MOE_BLOCK, stride=ROW_CHUNKS), :] = y[:, c * LANES:(c + 1) * LANES]
        start_scatter(slot)

    @pl.when(jnp.logical_and(nvalid == 0, jnp.logical_and(i > 0, nvalid_ref[prev] > 0)))
    def _():
        wait_gather(slot)


def _experts(x1r, g_ffn, blk_exp, nvalid, row_tok, row_dst, wg, wu, wd):
    rows_total, _ = x1r.shape
    d = D_MODEL
    n_blk = blk_exp.shape[0]
    idx_spec = lambda fn: pl.BlockSpec((1, 1, MOE_BLOCK), fn, memory_space=pltpu.SMEM)
    grid_spec = pltpu.PrefetchScalarGridSpec(
        num_scalar_prefetch=2,
        grid=(n_blk,),
        in_specs=[
            idx_spec(lambda i, be, nv: (i, 0, 0)),
            idx_spec(lambda i, be, nv: (jnp.minimum(i + 1, n_blk - 1), 0, 0)),
            idx_spec(lambda i, be, nv: (i, 0, 0)),
            pl.BlockSpec(memory_space=pl.ANY),
            pl.BlockSpec((1, d), lambda i, be, nv: (0, 0)),
            pl.BlockSpec((1, d, EXPERT_FF), lambda i, be, nv: (be[i], 0, 0)),
            pl.BlockSpec((1, d, EXPERT_FF), lambda i, be, nv: (be[i], 0, 0)),
            pl.BlockSpec((1, EXPERT_FF, d), lambda i, be, nv: (be[i], 0, 0)),
        ],
        out_specs=pl.BlockSpec(memory_space=pl.ANY),
        scratch_shapes=[
            pltpu.VMEM((2, MOE_BLOCK * ROW_CHUNKS, LANES), F32),
            pltpu.VMEM((MOE_BLOCK, d), BF16),
            pltpu.VMEM((2, MOE_BLOCK * ROW_CHUNKS, LANES), F32),
            pltpu.VMEM((d, EXPERT_FF), BF16),
            pltpu.VMEM((d, EXPERT_FF), BF16),
            pltpu.VMEM((EXPERT_FF, d), BF16),
            pltpu.SemaphoreType.DMA((2,)),
            pltpu.SemaphoreType.DMA((2,)),
        ],
    )
    return pl.pallas_call(
        _expert_kernel,
        grid_spec=grid_spec,
        out_shape=jax.ShapeDtypeStruct((TOP_K * rows_total + MOE_BLOCK * ROW_CHUNKS, LANES), F32),
        compiler_params=pltpu.CompilerParams(
            dimension_semantics=("arbitrary",), vmem_limit_bytes=VMEM_LIMIT_BYTES),
        name="experts",
    )(blk_exp, nvalid, row_tok, row_tok, row_dst, x1r, g_ffn, wg, wu, wd)


def _combine_kernel(x1_ref, y0_ref, y1_ref, route_ref, o_ref, *, tm):
    w_cols = jnp.transpose(route_ref[...])
    w0 = w_cols[:, 2:3]
    w1 = w_cols[:, 3:4]
    for c in range(ROW_CHUNKS):
        rows = pl.ds(c, tm, stride=ROW_CHUNKS)
        o_ref[:, c * LANES:(c + 1) * LANES] = (x1_ref[rows, :] + w0 * y0_ref[rows, :]
                                               + w1 * y1_ref[rows, :])


def _combine(x1r, y, route_t):
    n = x1r.shape[0] // ROW_CHUNKS
    tm = ROW_TILE
    nt = n // tm
    blk = (tm * ROW_CHUNKS, LANES)
    return pl.pallas_call(
        functools.partial(_combine_kernel, tm=tm),
        grid=(nt,),
        in_specs=[pl.BlockSpec(blk, lambda i: (i, 0)),
                  pl.BlockSpec(blk, lambda i: (i, 0)),
                  pl.BlockSpec(blk, lambda i: (i + nt, 0)),
                  pl.BlockSpec((ROUTE_ROWS, tm), lambda i: (0, i))],
        out_specs=pl.BlockSpec((tm, D_MODEL), lambda i: (i, 0)),
        out_shape=jax.ShapeDtypeStruct((n, D_MODEL), F32),
        compiler_params=pltpu.CompilerParams(dimension_semantics=("parallel",)),
        name="combine",
    )(x1r, y, y, route_t)


def _rope_tables(positions):
    half = ROT_DIM // 2
    lane = np.arange(LANES)
    dim = lane % HEAD_DIM
    rot = dim < ROT_DIM
    spread_cos = np.zeros((half, LANES), np.float32)
    spread_cos[dim[rot] % half, lane[rot]] = 1.0
    spread_sin = np.zeros((half, LANES), np.float32)
    spread_sin[dim[rot] % half, lane[rot]] = np.where(dim[rot] < half, -1.0, 1.0)
    inv = ROPE_THETA ** (-jnp.arange(0, ROT_DIM, 2, dtype=F32) / ROT_DIM)
    ang = inv[None, :, None] * positions.astype(F32)[:, None, :]
    spread = functools.partial(jnp.einsum, 'bjs,jl->bsl', precision=lax.Precision.HIGHEST)
    cos_t = spread(jnp.cos(ang), jnp.asarray(spread_cos)) + jnp.asarray((~rot).astype(np.float32))
    sin_t = spread(jnp.sin(ang), jnp.asarray(spread_sin))
    return cos_t, sin_t


def _route(route_t, counts, n):
    nk = n * TOP_K
    e = route_t[0:TOP_K].reshape(nk).astype(jnp.int32)
    a = jnp.arange(nk, dtype=jnp.int32)
    _, a_s = lax.sort((e, a), num_keys=1, is_stable=True)
    padded = ((counts + MOE_BLOCK - 1) // MOE_BLOCK) * MOE_BLOCK
    pend = jnp.cumsum(padded)
    pstart = pend - padded
    start = jnp.cumsum(counts) - counts
    n_blk = nk // MOE_BLOCK + N_EXPERTS + DRAIN_STEPS
    blk_start = jnp.arange(n_blk, dtype=jnp.int32) * MOE_BLOCK
    blk_exp = jnp.minimum(jnp.sum((pend[None, :] <= blk_start[:, None]).astype(jnp.int32), axis=1),
                          N_EXPERTS - 1).astype(jnp.int32)
    onehot = blk_exp[:, None] == jnp.arange(N_EXPERTS, dtype=jnp.int32)[None, :]
    pick = lambda table: jnp.sum(jnp.where(onehot, table[None, :], 0), axis=1)
    offset = blk_start - pick(pstart)
    nvalid = jnp.clip(pick(counts) - offset, 0, MOE_BLOCK)
    nvalid = jnp.where(blk_start < pend[-1], nvalid, 0).astype(jnp.int32)
    j = jnp.arange(MOE_BLOCK, dtype=jnp.int32)[None, :]
    valid = j < nvalid[:, None]
    src = jnp.clip((pick(start) + offset)[:, None] + j, 0, nk - 1)
    a_p = a_s[src]
    tok = (jnp.where(valid, a_p - n * (a_p >= n).astype(jnp.int32), 0) * ROW_CHUNKS).astype(jnp.int32)
    dst = (jnp.where(valid, a_p, TOP_K * n + j) * ROW_CHUNKS).astype(jnp.int32)
    return blk_exp, nvalid, tok.reshape(n_blk, 1, MOE_BLOCK), dst.reshape(n_blk, 1, MOE_BLOCK)


def kernel(x, mem, positions, g_mix, g_mem, w_in, g_qn_attn, g_kn_attn, w_conv, w_mem_kv, g_qn_mem,
           g_kn_mem, w_proj_attn, w_proj_conv, w_proj_mem, w_out, g_ffn, w_router_group,
           b_router_group, w_router_expert, b_router_expert, w_gate, w_up, w_down):
    b, s, d = x.shape
    n = b * s
    cos_t, sin_t = _rope_tables(positions)
    for l in range(w_in.shape[0]):
        wl = w_in[l]
        aw = N_BRANCH * GROUP_W
        cols = []
        for gi in range(len(DILATIONS)):
            for part in range(3):
                c0 = part * aw + gi * GROUP_W
                cols.append(wl[:, c0:c0 + GROUP_W])
        cols.append(wl[:, 3 * aw:])
        w_all = jnp.concatenate(cols, axis=1).astype(BF16)

        q0, q1, q2, u, cb, qm, gates = _inproj(x, g_mix[l][None], g_qn_mem[l][None], w_all)
        km, vm = _memkv(mem, g_mem[l][None], g_kn_mem[l][None], w_mem_kv[l].astype(BF16))
        gq = jnp.tile(g_qn_attn[l], 2)[None]
        gk = jnp.tile(g_kn_attn[l], 2)[None]
        o_attn = _attention((q0, q1, q2), cos_t, sin_t, gq, gk)

        w_r = jnp.zeros((d, LANES), F32)
        w_r = w_r.at[:, :N_EXPERT_GROUPS].set(w_router_group[l])
        w_r = w_r.at[:, N_EXPERT_GROUPS:N_EXPERT_GROUPS + N_EXPERTS].set(w_router_expert[l])
        b_r = jnp.zeros((1, LANES), F32)
        b_r = b_r.at[0, :N_EXPERT_GROUPS].set(b_router_group[l])
        b_r = b_r.at[0, N_EXPERT_GROUPS:N_EXPERT_GROUPS + N_EXPERTS].set(b_router_expert[l])
        w_r_hi = w_r.astype(BF16)
        w_r_lo = (w_r - w_r_hi.astype(F32)).astype(BF16)
        x1, route_t, count_t = _merge(x, o_attn, u, cb, qm, km, vm, gates, w_conv[l],
                                      w_proj_attn[l].astype(BF16), w_proj_conv[l].astype(BF16),
                                      w_proj_mem[l].astype(BF16), w_out[l].astype(BF16),
                                      g_ffn[l][None], jnp.stack([w_r_hi, w_r_lo]), b_r)

        counts = count_t[0, N_EXPERT_GROUPS:N_EXPERT_GROUPS + N_EXPERTS].astype(jnp.int32)
        blk_exp, nvalid, row_tok, row_dst = _route(route_t, counts, n)
        y = _experts(x1, g_ffn[l][None], blk_exp, nvalid, row_tok, row_dst,
                     w_gate[l], w_up[l], w_down[l])
        x = _combine(x1, y, route_t).reshape(b, s, d)
    return x
```

```python
import functools

import numpy as np
import jax
import jax.numpy as jnp
from jax import lax
from jax.experimental import pallas as pl
from jax.experimental.pallas import tpu as pltpu

F32 = jnp.float32
BF16 = jnp.bfloat16

D_MODEL = 1024
EPS = 1e-6
HEAD_DIM = 64
DILATIONS = (1, 4, 16)
BAND_HALF = 64
ATTN_SLOTS = 4
GROUP_W = ATTN_SLOTS * HEAD_DIM
QKV_W = 3 * GROUP_W
ROT_DIM = 16
ROPE_THETA = 500000.0
CONV_WIDTH = 768
MEM_HEADS = 4
MEM_HEAD_DIM = 128
MEM_WIDTH = 512
N_BRANCH = 3
N_EXPERT_GROUPS = 4
EXPERTS_PER_GROUP = 8
N_EXPERTS = 32
TOP_K = 2
EXPERT_FF = 512

LANES = 128
VMEM_LIMIT_BYTES = 56 * 1024 * 1024

ROW_TILE = 512
Q_BLOCK = 128
MOE_BLOCK = 256
ROW_CHUNKS = D_MODEL // LANES
GATHER_UNROLL = 8
GATHER_SLOTS = 3
NEG_BIG = -1e30
ROUTE_NEG = -3e38
ROUTE_ROWS = 8


def _rms(t, gain):
    return t * lax.rsqrt(jnp.mean(t * t, axis=-1, keepdims=True) + EPS) * gain


def _inproj_kernel(x_ref, g_ref, gqm_ref, w_ref,
                   qkv0_ref, qkv1_ref, qkv2_ref, u_ref, cb_ref, qm_ref, gates_ref,
                   h_scr, hp_scr, *, tm):
    x = x_ref[0]
    h = _rms(x, g_ref[...])
    n_chunks = h.shape[1] // LANES
    for c in range(n_chunks):
        h_scr[c] = h[:, c * LANES:(c + 1) * LANES]
    hb = h.astype(BF16)

    def mm(lhs, c0, c1):
        return jnp.dot(lhs, w_ref[:, c0:c1], preferred_element_type=F32)

    qkv0_ref[0, 0] = mm(hb, 0, QKV_W).astype(BF16)
    for gi, out_ref in ((1, qkv1_ref), (2, qkv2_ref)):
        d = DILATIONS[gi]
        rows = tm // d
        for r in range(d):
            for c in range(n_chunks):
                hp_scr[r * rows:(r + 1) * rows, c * LANES:(c + 1) * LANES] = (
                    h_scr[c, pl.ds(r, rows, stride=d), :].astype(BF16))
        res = mm(hp_scr[...], gi * QKV_W, (gi + 1) * QKV_W)
        out_ref[0] = res.reshape(d, rows, QKV_W).astype(BF16)

    c = 3 * QKV_W
    cx = mm(hb, c, c + CONV_WIDTH)
    cc = mm(hb, c + 2 * CONV_WIDTH, c + 3 * CONV_WIDTH)
    u_ref[0] = (cc * cx).astype(BF16)
    cb_ref[0] = mm(hb, c + CONV_WIDTH, c + 2 * CONV_WIDTH).astype(BF16)

    c += 3 * CONV_WIDTH
    qm = mm(hb, c, c + MEM_WIDTH)
    for hh in range(MEM_HEADS):
        sl = slice(hh * MEM_HEAD_DIM, (hh + 1) * MEM_HEAD_DIM)
        qm_ref[0, :, sl] = _rms(qm[:, sl], gqm_ref[...]).astype(BF16)

    c += MEM_WIDTH
    for j in range(N_BRANCH):
        gl = mm(hb, c + j * D_MODEL, c + (j + 1) * D_MODEL)
        gates_ref[0, :, j * D_MODEL:(j + 1) * D_MODEL] = jax.nn.sigmoid(gl).astype(BF16)


def _inproj(x, g_mix, g_qn_mem, w_all):
    b, s, d = x.shape
    tm = ROW_TILE
    nt = s // tm
    outs = []
    out_specs = []
    for dil in DILATIONS:
        outs.append(jax.ShapeDtypeStruct((b, dil, s // dil, QKV_W), BF16))
        out_specs.append(pl.BlockSpec((1, dil, tm // dil, QKV_W), lambda bi, i: (bi, 0, i, 0)))
    for width in (CONV_WIDTH, CONV_WIDTH, MEM_WIDTH, N_BRANCH * D_MODEL):
        outs.append(jax.ShapeDtypeStruct((b, s, width), BF16))
        out_specs.append(pl.BlockSpec((1, tm, width), lambda bi, i: (bi, i, 0)))
    return pl.pallas_call(
        functools.partial(_inproj_kernel, tm=tm),
        grid=(b, nt),
        in_specs=[
            pl.BlockSpec((1, tm, d), lambda bi, i: (bi, i, 0)),
            pl.BlockSpec((1, d), lambda bi, i: (0, 0)),
            pl.BlockSpec((1, MEM_HEAD_DIM), lambda bi, i: (0, 0)),
            pl.BlockSpec(w_all.shape, lambda bi, i: (0, 0), pipeline_mode=pl.Buffered(1)),
        ],
        out_specs=out_specs,
        out_shape=outs,
        scratch_shapes=[pltpu.VMEM((d // LANES, tm, LANES), F32), pltpu.VMEM((tm, d), BF16)],
        compiler_params=pltpu.CompilerParams(
            dimension_semantics=("parallel", "parallel"), vmem_limit_bytes=VMEM_LIMIT_BYTES),
        name="inproj",
    )(x, g_mix, g_qn_mem, w_all)


def _memkv_kernel(mem_ref, g_ref, gk_ref, w_ref, k_ref, v_ref):
    h = _rms(mem_ref[0], g_ref[...]).astype(BF16)
    kv = jnp.dot(h, w_ref[...], preferred_element_type=F32)
    for hh in range(MEM_HEADS):
        sl = slice(hh * MEM_HEAD_DIM, (hh + 1) * MEM_HEAD_DIM)
        k_ref[0, :, sl] = _rms(kv[:, sl], gk_ref[...]).astype(BF16)
    v_ref[0] = kv[:, MEM_WIDTH:].astype(BF16)


def _memkv(mem, g_mem, g_kn_mem, w_kv):
    b, m, d = mem.shape
    return pl.pallas_call(
        _memkv_kernel,
        grid=(b,),
        in_specs=[
            pl.BlockSpec((1, m, d), lambda bi: (bi, 0, 0)),
            pl.BlockSpec((1, d), lambda bi: (0, 0)),
            pl.BlockSpec((1, MEM_HEAD_DIM), lambda bi: (0, 0)),
            pl.BlockSpec(w_kv.shape, lambda bi: (0, 0)),
        ],
        out_specs=[pl.BlockSpec((1, m, MEM_WIDTH), lambda bi: (bi, 0, 0))] * 2,
        out_shape=[jax.ShapeDtypeStruct((b, m, MEM_WIDTH), BF16)] * 2,
        compiler_params=pltpu.CompilerParams(dimension_semantics=("parallel",)),
        name="memkv",
    )(mem, g_mem, g_kn_mem, w_kv)


ATTN_WIN = Q_BLOCK + 2 * BAND_HALF
ATTN_UNROLL = 16
DEN_SAFE = 1e-30


def _attn_consts():
    lane = np.arange(LANES)
    gsum = (lane[:, None] // HEAD_DIM == lane[None, :] // HEAD_DIM).astype(np.float32)
    half = ROT_DIM // 2
    dim = lane % HEAD_DIM
    src = np.where(dim < half, lane + half, np.where(dim < ROT_DIM, lane - half, -1))
    pswap = (lane[:, None] == src[None, :]).astype(np.float32)
    i = np.arange(Q_BLOCK)[:, None]
    c = np.arange(ATTN_WIN)[None, :]
    band = (c >= i) & (c <= i + 2 * BAND_HALF)
    first = c >= BAND_HALF
    last = c < Q_BLOCK + BAND_HALF
    variants = [band, band & first, band & last, band & first & last]
    bias = np.stack([np.where(v, 0.0, NEG_BIG) for v in variants]).astype(np.float32)
    return jnp.asarray(gsum, BF16), jnp.asarray(pswap, BF16), jnp.asarray(bias, F32)


def _attn_kernel(q0, k0, v0, q1, k1, v1, q2, k2, v2, cos_ref, sin_ref, gq_ref, gk_ref,
                 gsum_ref, pswap_ref, bias_ref, out_ref, qs, ks, vs, o_scr, l_scr, bias_scr, *, seq):
    lane = lax.broadcasted_iota(jnp.int32, (1, LANES), 1)
    low_half = lane < HEAD_DIM
    pad = BAND_HALF
    n_blocks = seq // Q_BLOCK

    bound = HEAD_DIM ** 0.5 * jnp.max(jnp.abs(gq_ref[...])) * jnp.max(jnp.abs(gk_ref[...]))
    shifted = bias_ref[...] - bound
    bias_scr[:, 0:Q_BLOCK, :] = shifted
    bias_scr[:, Q_BLOCK:2 * Q_BLOCK, :] = shifted

    def norm_rope(t, gain, cos, sin):
        ms = jnp.dot((t * t).astype(BF16), gsum_ref[...], preferred_element_type=F32) * (1.0 / HEAD_DIM)
        tn = t * lax.rsqrt(ms + EPS) * gain
        partner = jnp.dot(tn.astype(BF16), pswap_ref[...], preferred_element_type=F32)
        return tn * cos + partner * sin

    for gi, (q_ref, k_ref, v_ref) in enumerate(((q0, k0, v0), (q1, k1, v1), (q2, k2, v2))):
        d = DILATIONS[gi]
        length = seq // d
        nblk = length // Q_BLOCK
        region = length + pad

        zpad = jnp.zeros((pad, LANES), BF16)
        for r in range(d + 1):
            ks[r * region:r * region + pad, :] = zpad
            vs[r * region:r * region + pad, :] = zpad

        def split_index(b, nblk=nblk):
            if nblk == 1:
                return b, 0
            r = b // nblk
            return r, b - r * nblk

        def table_rows(r, n, d=d):
            if d == 1:
                return pl.ds(pl.multiple_of(n * Q_BLOCK, Q_BLOCK), Q_BLOCK)
            return pl.ds(r + d * n * Q_BLOCK, Q_BLOCK, stride=d)

        def prep(b, q_ref=q_ref, k_ref=k_ref, v_ref=v_ref, split_index=split_index,
                 table_rows=table_rows):
            r, n = split_index(b)
            src = pl.ds(pl.multiple_of(n * Q_BLOCK, Q_BLOCK), Q_BLOCK)
            rows = table_rows(r, n)
            cos = cos_ref[0, rows, :]
            sin = sin_ref[0, rows, :]
            qn = norm_rope(q_ref[0, r, src, :].astype(F32), gq_ref[...], cos, sin) * (HEAD_DIM ** -0.5)
            kn = norm_rope(k_ref[0, r, src, :].astype(F32), gk_ref[...], cos, sin)
            qbase = pl.multiple_of(b * 2 * Q_BLOCK, 2 * Q_BLOCK)
            qs[pl.ds(qbase, Q_BLOCK), :] = jnp.where(low_half, qn, 0.0).astype(BF16)
            qs[pl.ds(qbase + Q_BLOCK, Q_BLOCK), :] = jnp.where(low_half, 0.0, qn).astype(BF16)
            kdst = pl.ds(pl.multiple_of(b * Q_BLOCK + (r + 1) * pad, pad), Q_BLOCK)
            ks[kdst, :] = kn.astype(BF16)
            vs[kdst, :] = v_ref[0, r, src, :]

        def block(b, exact, gi=gi, nblk=nblk, split_index=split_index, table_rows=table_rows):
            r, n = split_index(b)
            variant = 3 if nblk == 1 else jnp.where(n == 0, 1, 0) + jnp.where(n == nblk - 1, 2, 0)
            qsrc = pl.ds(pl.multiple_of(b * 2 * Q_BLOCK, 2 * Q_BLOCK), 2 * Q_BLOCK)
            ksrc = pl.ds(pl.multiple_of(b * Q_BLOCK + r * pad, pad), ATTN_WIN)
            s = lax.dot_general(qs[qsrc, :], ks[ksrc, :], (((1,), (1,)), ((), ())),
                                preferred_element_type=F32) + bias_scr[variant]
            if exact:
                m = jnp.max(s, axis=-1, keepdims=True)
                s = s - m
                shift = jnp.where(low_half, m[0:Q_BLOCK], m[Q_BLOCK:]) + bound
            else:
                shift = bound
            p = jnp.exp(s)
            den = jnp.sum(p, axis=-1, keepdims=True)
            o2 = jnp.dot(p.astype(BF16), vs[ksrc, :], preferred_element_type=F32)
            o = jnp.where(low_half, o2[0:Q_BLOCK], o2[Q_BLOCK:])
            den = jnp.where(low_half, den[0:Q_BLOCK], den[Q_BLOCK:])
            dst = table_rows(r, n)
            o_scr[gi, dst, :] = o / den
            l_scr[gi, dst, :] = shift + jnp.log(den)
            return den

        def unrolled(fn):
            def body(it, carry):
                for u in range(ATTN_UNROLL):
                    fn(it * ATTN_UNROLL + u)
                return carry
            lax.fori_loop(0, n_blocks // ATTN_UNROLL, body, 0)

        unrolled(prep)

        def fast_body(it, dmin, block=block):
            for u in range(ATTN_UNROLL):
                dmin = jnp.minimum(dmin, block(it * ATTN_UNROLL + u, False))
            return dmin
        dmin = lax.fori_loop(0, n_blocks // ATTN_UNROLL, fast_body,
                             jnp.full((Q_BLOCK, LANES), 1.0, F32))

        @pl.when(jnp.logical_not(jnp.min(dmin) > DEN_SAFE))
        def _(block=block, unrolled=unrolled):
            unrolled(lambda b: block(b, True))

    l0 = l_scr[0]
    l1 = l_scr[1]
    l2 = l_scr[2]
    m = jnp.maximum(jnp.maximum(l0, l1), l2)
    e0 = jnp.exp(l0 - m)
    e1 = jnp.exp(l1 - m)
    e2 = jnp.exp(l2 - m)
    mix = (e0 * o_scr[0] + e1 * o_scr[1] + e2 * o_scr[2]) / (e0 + e1 + e2)
    out_ref[0] = mix.astype(BF16)


def _attention(qkv, cos_t, sin_t, gq, gk):
    b, _, s, _ = qkv[0].shape
    in_specs = []
    args = []
    for gi, dil in enumerate(DILATIONS):
        for part in range(3):
            in_specs.append(pl.BlockSpec((1, dil, s // dil, LANES),
                                         lambda bi, hp, part=part: (bi, 0, 0, 2 * part + hp)))
            args.append(qkv[gi])
    consts = _attn_consts()
    tab_spec = pl.BlockSpec((1, s, LANES), lambda bi, hp: (bi, 0, 0))
    gain_spec = pl.BlockSpec((1, LANES), lambda bi, hp: (0, 0))
    const_specs = [pl.BlockSpec(c.shape, lambda bi, hp, nd=c.ndim: (0,) * nd) for c in consts]
    key_rows = max(d * (s // d + BAND_HALF) for d in DILATIONS) + BAND_HALF
    return pl.pallas_call(
        functools.partial(_attn_kernel, seq=s),
        grid=(b, 2),
        in_specs=in_specs + [tab_spec] * 2 + [gain_spec] * 2 + const_specs,
        out_specs=pl.BlockSpec((1, s, LANES), lambda bi, hp: (bi, 0, hp)),
        out_shape=jax.ShapeDtypeStruct((b, s, GROUP_W), BF16),
        scratch_shapes=[
            pltpu.VMEM((2 * s, LANES), BF16),
            pltpu.VMEM((key_rows, LANES), BF16),
            pltpu.VMEM((key_rows, LANES), BF16),
            pltpu.VMEM((3, s, LANES), F32),
            pltpu.VMEM((3, s, LANES), F32),
            pltpu.VMEM((4, 2 * Q_BLOCK, ATTN_WIN), F32),
        ],
        compiler_params=pltpu.CompilerParams(
            dimension_semantics=("parallel", "parallel"), vmem_limit_bytes=VMEM_LIMIT_BYTES),
        name="dilated_attn",
    )(*args, cos_t, sin_t, gq, gk, *consts)


def _merge_kernel(x_ref, oa_ref, u_ref, uprev_ref, unext_ref, cb_ref, qm_ref, km_ref, vm_ref,
                  gates_ref, wconv_ref, wpa_ref, wpc_ref, wpm_ref, wout_ref, gffn_ref, wr_ref, br_ref,
                  x1_ref, route_ref, count_ref, om_scr, *, tm, nt):
    i = pl.program_id(1)
    u = u_ref[0].astype(F32)
    prev_row = uprev_ref[0].astype(F32)[15:16, :] * (i > 0).astype(F32)
    next_row = unext_ref[0].astype(F32)[0:1, :] * (i < nt - 1).astype(F32)
    row = lax.broadcasted_iota(jnp.int32, (tm, 1), 0)
    um = jnp.where(row == 0, prev_row, pltpu.roll(u, 1, 0))
    up = jnp.where(row == tm - 1, next_row, pltpu.roll(u, tm - 1, 0))
    wc = wconv_ref[...]
    y = wc[0:1, :] * um + wc[1:2, :] * u + wc[2:3, :] * up
    z = (cb_ref[0].astype(F32) * y).astype(BF16)

    for hh in range(MEM_HEADS):
        sl = slice(hh * MEM_HEAD_DIM, (hh + 1) * MEM_HEAD_DIM)
        s = lax.dot_general(qm_ref[0, :, sl], km_ref[0, :, sl], (((1,), (1,)), ((), ())),
                            preferred_element_type=F32) * (MEM_HEAD_DIM ** -0.5)
        m = jnp.max(s, axis=-1, keepdims=True)
        p = jnp.exp(s - m)
        den = jnp.sum(p, axis=-1, keepdims=True)
        o = jnp.dot(p.astype(BF16), vm_ref[0, :, sl], preferred_element_type=F32) / den
        om_scr[:, sl] = o.astype(BF16)

    pa = jnp.dot(oa_ref[0], wpa_ref[...], preferred_element_type=F32)
    pc = jnp.dot(z, wpc_ref[...], preferred_element_type=F32)
    pm = jnp.dot(om_scr[...], wpm_ref[...], preferred_element_type=F32)
    merged = (gates_ref[0, :, 0:D_MODEL].astype(F32) * pa
              + gates_ref[0, :, D_MODEL:2 * D_MODEL].astype(F32) * pc
              + gates_ref[0, :, 2 * D_MODEL:3 * D_MODEL].astype(F32) * pm)
    x1 = x_ref[0] + jnp.dot(merged.astype(BF16), wout_ref[...], preferred_element_type=F32)
    for c in range(ROW_CHUNKS):
        x1_ref[pl.ds(c, tm, stride=ROW_CHUNKS), :] = x1[:, c * LANES:(c + 1) * LANES]

    h2 = _rms(x1, gffn_ref[...])
    h_hi = h2.astype(BF16)
    h_lo = (h2 - h_hi.astype(F32)).astype(BF16)
    hh = jnp.dot(h_hi, wr_ref[...], preferred_element_type=F32)
    lg = (hh[:, 0:LANES] + hh[:, LANES:2 * LANES]
          + jnp.dot(h_lo, wr_ref[:, 0:LANES], preferred_element_type=F32)) + br_ref[...]

    lane = lax.broadcasted_iota(jnp.int32, (tm, LANES), 1).astype(F32)
    far = float(LANES)
    is_group = lane < N_EXPERT_GROUPS
    gl = jnp.where(is_group, lg, ROUTE_NEG)
    mg = jnp.max(gl, axis=-1, keepdims=True)
    gidx = jnp.min(jnp.where(gl == mg, lane, far), axis=-1, keepdims=True)
    pg_top = 1.0 / jnp.sum(jnp.where(is_group, jnp.exp(lg - mg), 0.0), axis=-1, keepdims=True)
    first = N_EXPERT_GROUPS + EXPERTS_PER_GROUP * gidx
    in_sel = jnp.logical_and(lane >= first, lane < first + EXPERTS_PER_GROUP)
    sel = jnp.where(in_sel, lg, ROUTE_NEG)
    m1 = jnp.max(sel, axis=-1, keepdims=True)
    i1 = jnp.min(jnp.where(jnp.logical_and(in_sel, sel == m1), lane, far), axis=-1, keepdims=True)
    keep = jnp.logical_and(in_sel, lane != i1)
    sel2 = jnp.where(keep, lg, ROUTE_NEG)
    m2 = jnp.max(sel2, axis=-1, keepdims=True)
    i2 = jnp.min(jnp.where(jnp.logical_and(keep, sel2 == m2), lane, far), axis=-1, keepdims=True)
    r = jnp.exp(m2 - m1)
    w1 = pg_top / (1.0 + r)
    w2 = w1 * r
    route = jnp.where(lane == 0, i1 - N_EXPERT_GROUPS,
                      jnp.where(lane == 1, i2 - N_EXPERT_GROUPS,
                                jnp.where(lane == 2, w1, jnp.where(lane == 3, w2, 0.0))))
    route_ref[...] = jnp.transpose(route)[0:8, :]

    @pl.when(jnp.logical_and(pl.program_id(0) == 0, i == 0))
    def _():
        count_ref[...] = jnp.zeros_like(count_ref)
    picked = jnp.where(jnp.logical_or(lane == i1, lane == i2), 1.0, 0.0)
    count_ref[...] += jnp.sum(picked, axis=0, keepdims=True)


def _merge(x, o_attn, u, cb, qm, km, vm, gates, w_conv, wpa, wpc, wpm, wout, g_ffn, w_r, b_r):
    b, s, d = x.shape
    tm = ROW_TILE
    nt = s // tm
    halo = 16
    hb = tm // halo

    def tile(width):
        return pl.BlockSpec((1, tm, width), lambda bi, i: (bi, i, 0))

    def whole(arr):
        return pl.BlockSpec(arr.shape, lambda bi, i: (0,) * arr.ndim)

    def per_batch(arr):
        return pl.BlockSpec((1,) + arr.shape[1:], lambda bi, i: (bi, 0, 0))

    in_specs = [
        tile(d), tile(GROUP_W), tile(CONV_WIDTH),
        pl.BlockSpec((1, halo, CONV_WIDTH), lambda bi, i: (bi, jnp.maximum(i * hb - 1, 0), 0)),
        pl.BlockSpec((1, halo, CONV_WIDTH),
                     lambda bi, i: (bi, jnp.minimum((i + 1) * hb, s // halo - 1), 0)),
        tile(CONV_WIDTH), tile(MEM_WIDTH), per_batch(km), per_batch(vm), tile(N_BRANCH * d),
        whole(w_conv), whole(wpa), whole(wpc), whole(wpm), whole(wout), whole(g_ffn),
        whole(w_r), whole(b_r),
    ]
    return pl.pallas_call(
        functools.partial(_merge_kernel, tm=tm, nt=nt),
        grid=(b, nt),
        in_specs=in_specs,
        out_specs=[pl.BlockSpec((tm * ROW_CHUNKS, LANES), lambda bi, i: (bi * nt + i, 0)),
                   pl.BlockSpec((ROUTE_ROWS, tm), lambda bi, i: (0, bi * nt + i)),
                   pl.BlockSpec((ROUTE_ROWS, LANES), lambda bi, i: (0, 0))],
        out_shape=[jax.ShapeDtypeStruct((b * s * ROW_CHUNKS, LANES), F32),
                   jax.ShapeDtypeStruct((ROUTE_ROWS, b * s), F32),
                   jax.ShapeDtypeStruct((ROUTE_ROWS, LANES), F32)],
        scratch_shapes=[pltpu.VMEM((tm, MEM_WIDTH), BF16)],
        compiler_params=pltpu.CompilerParams(
            dimension_semantics=("arbitrary", "arbitrary"), vmem_limit_bytes=VMEM_LIMIT_BYTES),
        name="merge",
    )(x, o_attn, u, u, u, cb, qm, km, vm, gates, w_conv, wpa, wpc, wpm, wout, g_ffn, w_r, b_r)


DRAIN_STEPS = 2


def _expert_kernel(blk_exp_ref, nvalid_ref, tok_ref, tok_next_ref, tok_ahead_ref, dst_ref, x1_hbm,
                   gffn_ref, wg_ref, wu_ref, wd_ref, y_hbm, xbuf, hbuf, ybuf, wg_s, wu_s, wd_s,
                   gsem, ssem):
    i = pl.program_id(0)
    slot = lax.rem(i, 2)
    gslot = lax.rem(i, GATHER_SLOTS)
    rows = MOE_BLOCK * ROW_CHUNKS

    def start_gather(idx_ref, s):
        def body(it, c):
            for u in range(GATHER_UNROLL):
                j = it * GATHER_UNROLL + u
                src_row = pl.multiple_of(idx_ref[0, 0, j], ROW_CHUNKS)
                pltpu.make_async_copy(x1_hbm.at[pl.ds(src_row, ROW_CHUNKS), :],
                                      xbuf.at[s, pl.ds(j * ROW_CHUNKS, ROW_CHUNKS), :],
                                      gsem.at[s]).start(priority=u % 2)
            return c
        lax.fori_loop(0, MOE_BLOCK // GATHER_UNROLL, body, 0)

    def wait_gather(s):
        pltpu.make_async_copy(x1_hbm.at[pl.ds(0, rows), :], xbuf.at[s], gsem.at[s]).wait()

    def start_scatter(s):
        def body(it, c):
            for u in range(GATHER_UNROLL):
                j = it * GATHER_UNROLL + u
                dst_row = pl.multiple_of(dst_ref[0, 0, j], ROW_CHUNKS)
                pltpu.make_async_copy(ybuf.at[s, pl.ds(j * ROW_CHUNKS, ROW_CHUNKS), :],
                                      y_hbm.at[pl.ds(dst_row, ROW_CHUNKS), :],
                                      ssem.at[s]).start(priority=u % 2)
            return c
        lax.fori_loop(0, MOE_BLOCK // GATHER_UNROLL, body, 0)

    def wait_scatter(s):
        pltpu.make_async_copy(ybuf.at[s], y_hbm.at[pl.ds(0, rows), :], ssem.at[s]).wait()

    nvalid = nvalid_ref[i]
    prev = jnp.maximum(i - 1, 0)
    prev2 = jnp.maximum(i - 2, 0)

    @pl.when(i == 0)
    def _():
        start_gather(tok_ref, 0)
        start_gather(tok_next_ref, 1)
        ybuf[0] = jnp.zeros((rows, LANES), F32)
        spare = pltpu.make_async_copy(ybuf.at[0], y_hbm.at[pl.ds(y_hbm.shape[0] - rows, rows), :],
                                      ssem.at[0])
        spare.start()
        spare.wait()

    @pl.when(jnp.logical_and(i >= 2, nvalid_ref[prev2] > 0))
    def _():
        wait_scatter(slot)

    @pl.when(jnp.logical_or(i == 0, blk_exp_ref[i] != blk_exp_ref[prev]))
    def _():
        wg_s[...] = wg_ref[0].astype(BF16)
        wu_s[...] = wu_ref[0].astype(BF16)
        wd_s[...] = wd_ref[0].astype(BF16)

    @pl.when(nvalid > 0)
    def _():
        wait_gather(gslot)
        start_gather(tok_ahead_ref, lax.rem(i + 2, GATHER_SLOTS))

        ss = None
        for c in range(ROW_CHUNKS):
            xc = xbuf[gslot, pl.ds(c, MOE_BLOCK, stride=ROW_CHUNKS), :]
            ss = xc * xc if ss is None else ss + xc * xc
        scale = lax.rsqrt(jnp.sum(ss, axis=-1, keepdims=True) * (1.0 / D_MODEL) + EPS)
        for c in range(ROW_CHUNKS):
            sl = slice(c * LANES, (c + 1) * LANES)
            xc = xbuf[gslot, pl.ds(c, MOE_BLOCK, stride=ROW_CHUNKS), :]
            hbuf[:, sl] = (xc * scale * gffn_ref[:, sl]).astype(BF16)
        h2 = hbuf[...]
        a = jnp.dot(h2, wg_s[...], preferred_element_type=F32)
        up = jnp.dot(h2, wu_s[...], preferred_element_type=F32)
        act = (a * jax.nn.sigmoid(a) * up).astype(BF16)
        y = jnp.dot(act, wd_s[...], preferred_element_type=F32)
        for c in range(ROW_CHUNKS):
            ybuf[slot, pl.ds(c, MOE_BLOCK, stride=ROW_CHUNKS), :] = y[:, c * LANES:(c + 1) * LANES]
        start_scatter(slot)

    @pl.when(jnp.logical_and(nvalid == 0, jnp.logical_or(i < 2, nvalid_ref[prev2] > 0)))
    def _():
        wait_gather(gslot)


def _experts(x1r, g_ffn, blk_exp, nvalid, row_tok, row_dst, wg, wu, wd):
    rows_total, _ = x1r.shape
    d = D_MODEL
    n_blk = blk_exp.shape[0]
    idx_spec = lambda fn: pl.BlockSpec((1, 1, MOE_BLOCK), fn, memory_space=pltpu.SMEM)
    grid_spec = pltpu.PrefetchScalarGridSpec(
        num_scalar_prefetch=2,
        grid=(n_blk,),
        in_specs=[
            idx_spec(lambda i, be, nv: (i, 0, 0)),
            idx_spec(lambda i, be, nv: (jnp.minimum(i + 1, n_blk - 1), 0, 0)),
            idx_spec(lambda i, be, nv: (jnp.minimum(i + 2, n_blk - 1), 0, 0)),
            idx_spec(lambda i, be, nv: (i, 0, 0)),
            pl.BlockSpec(memory_space=pl.ANY),
            pl.BlockSpec((1, d), lambda i, be, nv: (0, 0)),
            pl.BlockSpec((1, d, EXPERT_FF), lambda i, be, nv: (be[i], 0, 0)),
            pl.BlockSpec((1, d, EXPERT_FF), lambda i, be, nv: (be[i], 0, 0)),
            pl.BlockSpec((1, EXPERT_FF, d), lambda i, be, nv: (be[i], 0, 0)),
        ],
        out_specs=pl.BlockSpec(memory_space=pl.ANY),
        scratch_shapes=[
            pltpu.VMEM((GATHER_SLOTS, MOE_BLOCK * ROW_CHUNKS, LANES), F32),
            pltpu.VMEM((MOE_BLOCK, d), BF16),
            pltpu.VMEM((2, MOE_BLOCK * ROW_CHUNKS, LANES), F32),
            pltpu.VMEM((d, EXPERT_FF), BF16),
            pltpu.VMEM((d, EXPERT_FF), BF16),
            pltpu.VMEM((EXPERT_FF, d), BF16),
            pltpu.SemaphoreType.DMA((GATHER_SLOTS,)),
            pltpu.SemaphoreType.DMA((2,)),
        ],
    )
    return pl.pallas_call(
        _expert_kernel,
        grid_spec=grid_spec,
        out_shape=jax.ShapeDtypeStruct((TOP_K * rows_total + MOE_BLOCK * ROW_CHUNKS, LANES), F32),
        compiler_params=pltpu.CompilerParams(
            dimension_semantics=("arbitrary",), vmem_limit_bytes=VMEM_LIMIT_BYTES),
        name="experts",
    )(blk_exp, nvalid, row_tok, row_tok, row_tok, row_dst, x1r, g_ffn, wg, wu, wd)


def _combine_kernel(x1_ref, y0_ref, y1_ref, route_ref, o_ref, *, tm):
    w_cols = jnp.transpose(route_ref[...])
    w0 = w_cols[:, 2:3]
    w1 = w_cols[:, 3:4]
    for c in range(ROW_CHUNKS):
        rows = pl.ds(c, tm, stride=ROW_CHUNKS)
        o_ref[:, c * LANES:(c + 1) * LANES] = (x1_ref[rows, :] + w0 * y0_ref[rows, :]
                                               + w1 * y1_ref[rows, :])


def _combine(x1r, y, route_t):
    n = x1r.shape[0] // ROW_CHUNKS
    tm = ROW_TILE
    nt = n // tm
    blk = (tm * ROW_CHUNKS, LANES)
    return pl.pallas_call(
        functools.partial(_combine_kernel, tm=tm),
        grid=(nt,),
        in_specs=[pl.BlockSpec(blk, lambda i: (i, 0)),
                  pl.BlockSpec(blk, lambda i: (i, 0)),
                  pl.BlockSpec(blk, lambda i: (i + nt, 0)),
                  pl.BlockSpec((ROUTE_ROWS, tm), lambda i: (0, i))],
        out_specs=pl.BlockSpec((tm, D_MODEL), lambda i: (i, 0)),
        out_shape=jax.ShapeDtypeStruct((n, D_MODEL), F32),
        compiler_params=pltpu.CompilerParams(dimension_semantics=("parallel",)),
        name="combine",
    )(x1r, y, y, route_t)


def _rope_tables(positions):
    half = ROT_DIM // 2
    lane = np.arange(LANES)
    dim = lane % HEAD_DIM
    rot = dim < ROT_DIM
    spread_cos = np.zeros((half, LANES), np.float32)
    spread_cos[dim[rot] % half, lane[rot]] = 1.0
    spread_sin = np.zeros((half, LANES), np.float32)
    spread_sin[dim[rot] % half, lane[rot]] = np.where(dim[rot] < half, -1.0, 1.0)
    inv = ROPE_THETA ** (-jnp.arange(0, ROT_DIM, 2, dtype=F32) / ROT_DIM)
    ang = inv[None, :, None] * positions.astype(F32)[:, None, :]
    spread = functools.partial(jnp.einsum, 'bjs,jl->bsl', precision=lax.Precision.HIGHEST)
    cos_t = spread(jnp.cos(ang), jnp.asarray(spread_cos)) + jnp.asarray((~rot).astype(np.float32))
    sin_t = spread(jnp.sin(ang), jnp.asarray(spread_sin))
    return cos_t, sin_t


def _route(route_t, counts, n):
    nk = n * TOP_K
    e = route_t[0:TOP_K].reshape(nk).astype(jnp.int32)
    a = jnp.arange(nk, dtype=jnp.int32)
    _, a_s = lax.sort((e, a), num_keys=1, is_stable=True)
    padded = ((counts + MOE_BLOCK - 1) // MOE_BLOCK) * MOE_BLOCK
    pend = jnp.cumsum(padded)
    pstart = pend - padded
    start = jnp.cumsum(counts) - counts
    n_blk = nk // MOE_BLOCK + N_EXPERTS + DRAIN_STEPS
    blk_start = jnp.arange(n_blk, dtype=jnp.int32) * MOE_BLOCK
    blk_exp = jnp.minimum(jnp.sum((pend[None, :] <= blk_start[:, None]).astype(jnp.int32), axis=1),
                          N_EXPERTS - 1).astype(jnp.int32)
    onehot = blk_exp[:, None] == jnp.arange(N_EXPERTS, dtype=jnp.int32)[None, :]
    pick = lambda table: jnp.sum(jnp.where(onehot, table[None, :], 0), axis=1)
    offset = blk_start - pick(pstart)
    nvalid = jnp.clip(pick(counts) - offset, 0, MOE_BLOCK)
    nvalid = jnp.where(blk_start < pend[-1], nvalid, 0).astype(jnp.int32)
    j = jnp.arange(MOE_BLOCK, dtype=jnp.int32)[None, :]
    valid = j < nvalid[:, None]
    src = jnp.clip((pick(start) + offset)[:, None] + j, 0, nk - 1)
    a_p = a_s[src]
    tok = (jnp.where(valid, a_p - n * (a_p >= n).astype(jnp.int32), 0) * ROW_CHUNKS).astype(jnp.int32)
    dst = (jnp.where(valid, a_p, TOP_K * n + j) * ROW_CHUNKS).astype(jnp.int32)
    return blk_exp, nvalid, tok.reshape(n_blk, 1, MOE_BLOCK), dst.reshape(n_blk, 1, MOE_BLOCK)


def kernel(x, mem, positions, g_mix, g_mem, w_in, g_qn_attn, g_kn_attn, w_conv, w_mem_kv, g_qn_mem,
           g_kn_mem, w_proj_attn, w_proj_conv, w_proj_mem, w_out, g_ffn, w_router_group,
           b_router_group, w_router_expert, b_router_expert, w_gate, w_up, w_down):
    b, s, d = x.shape
    n = b * s
    cos_t, sin_t = _rope_tables(positions)
    for l in range(w_in.shape[0]):
        wl = w_in[l]
        aw = N_BRANCH * GROUP_W
        cols = []
        for gi in range(len(DILATIONS)):
            for part in range(3):
                c0 = part * aw + gi * GROUP_W
                cols.append(wl[:, c0:c0 + GROUP_W])
        cols.append(wl[:, 3 * aw:])
        w_all = jnp.concatenate(cols, axis=1).astype(BF16)

        q0, q1, q2, u, cb, qm, gates = _inproj(x, g_mix[l][None], g_qn_mem[l][None], w_all)
        km, vm = _memkv(mem, g_mem[l][None], g_kn_mem[l][None], w_mem_kv[l].astype(BF16))
        gq = jnp.tile(g_qn_attn[l], 2)[None]
        gk = jnp.tile(g_kn_attn[l], 2)[None]
        o_attn = _attention((q0, q1, q2), cos_t, sin_t, gq, gk)

        w_r = jnp.zeros((d, LANES), F32)
        w_r = w_r.at[:, :N_EXPERT_GROUPS].set(w_router_group[l])
        w_r = w_r.at[:, N_EXPERT_GROUPS:N_EXPERT_GROUPS + N_EXPERTS].set(w_router_expert[l])
        b_r = jnp.zeros((1, LANES), F32)
        b_r = b_r.at[0, :N_EXPERT_GROUPS].set(b_router_group[l])
        b_r = b_r.at[0, N_EXPERT_GROUPS:N_EXPERT_GROUPS + N_EXPERTS].set(b_router_expert[l])
        w_r_hi = w_r.astype(BF16)
        w_r_lo = (w_r - w_r_hi.astype(F32)).astype(BF16)
        x1, route_t, count_t = _merge(x, o_attn, u, cb, qm, km, vm, gates, w_conv[l],
                                      w_proj_attn[l].astype(BF16), w_proj_conv[l].astype(BF16),
                                      w_proj_mem[l].astype(BF16), w_out[l].astype(BF16),
                                      g_ffn[l][None], jnp.concatenate([w_r_hi, w_r_lo], axis=1), b_r)

        counts = count_t[0, N_EXPERT_GROUPS:N_EXPERT_GROUPS + N_EXPERTS].astype(jnp.int32)
        blk_exp, nvalid, row_tok, row_dst = _route(route_t, counts, n)
        y = _experts(x1, g_ffn[l][None], blk_exp, nvalid, row_tok, row_dst,
                     w_gate[l], w_up[l], w_down[l])
        x = _combine(x1, y, route_t).reshape(b, s, d)
    return x
```

```python
import functools

import numpy as np
import jax
import jax.numpy as jnp
from jax import lax
from jax.experimental import pallas as pl
from jax.experimental.pallas import tpu as pltpu

F32 = jnp.float32
BF16 = jnp.bfloat16

D_MODEL = 1024
EPS = 1e-6
HEAD_DIM = 64
DILATIONS = (1, 4, 16)
BAND_HALF = 64
ATTN_SLOTS = 4
GROUP_W = ATTN_SLOTS * HEAD_DIM
QKV_W = 3 * GROUP_W
ROT_DIM = 16
ROPE_THETA = 500000.0
CONV_WIDTH = 768
MEM_HEADS = 4
MEM_HEAD_DIM = 128
MEM_WIDTH = 512
N_BRANCH = 3
N_EXPERT_GROUPS = 4
EXPERTS_PER_GROUP = 8
N_EXPERTS = 32
TOP_K = 2
EXPERT_FF = 512

LANES = 128
VMEM_LIMIT_BYTES = 56 * 1024 * 1024

ROW_TILE = 512
Q_BLOCK = 128
MOE_BLOCK = 256
ROW_CHUNKS = D_MODEL // LANES
GATHER_UNROLL = 8
GATHER_SLOTS = 3
NEG_BIG = -1e30
ROUTE_NEG = -3e38
ROUTE_ROWS = 8


def _rms(t, gain):
    return t * lax.rsqrt(jnp.mean(t * t, axis=-1, keepdims=True) + EPS) * gain


def _inproj_kernel(x_ref, g_ref, gqm_ref, w_ref,
                   qkv0_ref, qkv1_ref, qkv2_ref, u_ref, cb_ref, qm_ref, gates_ref,
                   h_scr, hp_scr, *, tm):
    x = x_ref[0]
    h = _rms(x, g_ref[...])
    n_chunks = h.shape[1] // LANES
    for c in range(n_chunks):
        h_scr[c] = h[:, c * LANES:(c + 1) * LANES]
    hb = h.astype(BF16)

    def mm(lhs, c0, c1):
        return jnp.dot(lhs, w_ref[:, c0:c1], preferred_element_type=F32)

    qkv0_ref[0, 0] = mm(hb, 0, QKV_W).astype(BF16)
    for gi, out_ref in ((1, qkv1_ref), (2, qkv2_ref)):
        d = DILATIONS[gi]
        rows = tm // d
        for r in range(d):
            for c in range(n_chunks):
                hp_scr[r * rows:(r + 1) * rows, c * LANES:(c + 1) * LANES] = (
                    h_scr[c, pl.ds(r, rows, stride=d), :].astype(BF16))
        res = mm(hp_scr[...], gi * QKV_W, (gi + 1) * QKV_W)
        out_ref[0] = res.reshape(d, rows, QKV_W).astype(BF16)

    c = 3 * QKV_W
    cx = mm(hb, c, c + CONV_WIDTH)
    cc = mm(hb, c + 2 * CONV_WIDTH, c + 3 * CONV_WIDTH)
    u_ref[0] = (cc * cx).astype(BF16)
    cb_ref[0] = mm(hb, c + CONV_WIDTH, c + 2 * CONV_WIDTH).astype(BF16)

    c += 3 * CONV_WIDTH
    qm = mm(hb, c, c + MEM_WIDTH)
    for hh in range(MEM_HEADS):
        sl = slice(hh * MEM_HEAD_DIM, (hh + 1) * MEM_HEAD_DIM)
        qm_ref[0, :, sl] = _rms(qm[:, sl], gqm_ref[...]).astype(BF16)

    c += MEM_WIDTH
    for j in range(N_BRANCH):
        gl = mm(hb, c + j * D_MODEL, c + (j + 1) * D_MODEL)
        gates_ref[0, :, j * D_MODEL:(j + 1) * D_MODEL] = jax.nn.sigmoid(gl).astype(BF16)


def _inproj(x, g_mix, g_qn_mem, w_all):
    b, s, d = x.shape
    tm = ROW_TILE
    nt = s // tm
    outs = []
    out_specs = []
    for dil in DILATIONS:
        outs.append(jax.ShapeDtypeStruct((b, dil, s // dil, QKV_W), BF16))
        out_specs.append(pl.BlockSpec((1, dil, tm // dil, QKV_W), lambda bi, i: (bi, 0, i, 0)))
    for width in (CONV_WIDTH, CONV_WIDTH, MEM_WIDTH, N_BRANCH * D_MODEL):
        outs.append(jax.ShapeDtypeStruct((b, s, width), BF16))
        out_specs.append(pl.BlockSpec((1, tm, width), lambda bi, i: (bi, i, 0)))
    return pl.pallas_call(
        functools.partial(_inproj_kernel, tm=tm),
        grid=(b, nt),
        in_specs=[
            pl.BlockSpec((1, tm, d), lambda bi, i: (bi, i, 0)),
            pl.BlockSpec((1, d), lambda bi, i: (0, 0)),
            pl.BlockSpec((1, MEM_HEAD_DIM), lambda bi, i: (0, 0)),
            pl.BlockSpec(w_all.shape, lambda bi, i: (0, 0), pipeline_mode=pl.Buffered(1)),
        ],
        out_specs=out_specs,
        out_shape=outs,
        scratch_shapes=[pltpu.VMEM((d // LANES, tm, LANES), F32), pltpu.VMEM((tm, d), BF16)],
        compiler_params=pltpu.CompilerParams(
            dimension_semantics=("parallel", "parallel"), vmem_limit_bytes=VMEM_LIMIT_BYTES),
        name="inproj",
    )(x, g_mix, g_qn_mem, w_all)


def _memkv_kernel(mem_ref, g_ref, gk_ref, w_ref, k_ref, v_ref):
    h = _rms(mem_ref[0], g_ref[...]).astype(BF16)
    kv = jnp.dot(h, w_ref[...], preferred_element_type=F32)
    for hh in range(MEM_HEADS):
        sl = slice(hh * MEM_HEAD_DIM, (hh + 1) * MEM_HEAD_DIM)
        k_ref[0, :, sl] = _rms(kv[:, sl], gk_ref[...]).astype(BF16)
    v_ref[0] = kv[:, MEM_WIDTH:].astype(BF16)


def _memkv(mem, g_mem, g_kn_mem, w_kv):
    b, m, d = mem.shape
    return pl.pallas_call(
        _memkv_kernel,
        grid=(b,),
        in_specs=[
            pl.BlockSpec((1, m, d), lambda bi: (bi, 0, 0)),
            pl.BlockSpec((1, d), lambda bi: (0, 0)),
            pl.BlockSpec((1, MEM_HEAD_DIM), lambda bi: (0, 0)),
            pl.BlockSpec(w_kv.shape, lambda bi: (0, 0)),
        ],
        out_specs=[pl.BlockSpec((1, m, MEM_WIDTH), lambda bi: (bi, 0, 0))] * 2,
        out_shape=[jax.ShapeDtypeStruct((b, m, MEM_WIDTH), BF16)] * 2,
        compiler_params=pltpu.CompilerParams(dimension_semantics=("parallel",)),
        name="memkv",
    )(mem, g_mem, g_kn_mem, w_kv)


ATTN_WIN = Q_BLOCK + 2 * BAND_HALF
ATTN_UNROLL = 16
DEN_SAFE = 1e-30


def _attn_consts():
    lane = np.arange(LANES)
    gsum = (lane[:, None] // HEAD_DIM == lane[None, :] // HEAD_DIM).astype(np.float32)
    half = ROT_DIM // 2
    dim = lane % HEAD_DIM
    src = np.where(dim < half, lane + half, np.where(dim < ROT_DIM, lane - half, -1))
    pswap = (lane[:, None] == src[None, :]).astype(np.float32)
    i = np.arange(Q_BLOCK)[:, None]
    c = np.arange(ATTN_WIN)[None, :]
    band = (c >= i) & (c <= i + 2 * BAND_HALF)
    first = c >= BAND_HALF
    last = c < Q_BLOCK + BAND_HALF
    variants = [band, band & first, band & last, band & first & last]
    bias = np.stack([np.where(v, 0.0, NEG_BIG) for v in variants]).astype(np.float32)
    return jnp.asarray(gsum, BF16), jnp.asarray(pswap, BF16), jnp.asarray(bias, F32)


def _attn_kernel(q0, k0, v0, q1, k1, v1, q2, k2, v2, cos_ref, sin_ref, gq_ref, gk_ref,
                 gsum_ref, pswap_ref, bias_ref, out_ref, qs, ks, vs, o_scr, l_scr, bias_scr, *, seq):
    lane = lax.broadcasted_iota(jnp.int32, (1, LANES), 1)
    low_half = lane < HEAD_DIM
    pad = BAND_HALF
    n_blocks = seq // Q_BLOCK

    bound = HEAD_DIM ** 0.5 * jnp.max(jnp.abs(gq_ref[...])) * jnp.max(jnp.abs(gk_ref[...]))
    shifted = bias_ref[...] - bound
    bias_scr[:, 0:Q_BLOCK, :] = shifted
    bias_scr[:, Q_BLOCK:2 * Q_BLOCK, :] = shifted

    def norm_rope(t, gain, cos, sin):
        ms = jnp.dot((t * t).astype(BF16), gsum_ref[...], preferred_element_type=F32) * (1.0 / HEAD_DIM)
        tn = t * lax.rsqrt(ms + EPS) * gain
        partner = jnp.dot(tn.astype(BF16), pswap_ref[...], preferred_element_type=F32)
        return tn * cos + partner * sin

    for gi, (q_ref, k_ref, v_ref) in enumerate(((q0, k0, v0), (q1, k1, v1), (q2, k2, v2))):
        d = DILATIONS[gi]
        length = seq // d
        nblk = length // Q_BLOCK
        region = length + pad

        zpad = jnp.zeros((pad, LANES), BF16)
        for r in range(d + 1):
            ks[r * region:r * region + pad, :] = zpad
            vs[r * region:r * region + pad, :] = zpad

        def split_index(b, nblk=nblk):
            if nblk == 1:
                return b, 0
            r = b // nblk
            return r, b - r * nblk

        def table_rows(r, n, d=d):
            if d == 1:
                return pl.ds(pl.multiple_of(n * Q_BLOCK, Q_BLOCK), Q_BLOCK)
            return pl.ds(r + d * n * Q_BLOCK, Q_BLOCK, stride=d)

        def prep(b, q_ref=q_ref, k_ref=k_ref, v_ref=v_ref, split_index=split_index,
                 table_rows=table_rows):
            r, n = split_index(b)
            src = pl.ds(pl.multiple_of(n * Q_BLOCK, Q_BLOCK), Q_BLOCK)
            rows = table_rows(r, n)
            cos = cos_ref[0, rows, :]
            sin = sin_ref[0, rows, :]
            qn = norm_rope(q_ref[0, r, src, :].astype(F32), gq_ref[...], cos, sin) * (HEAD_DIM ** -0.5)
            kn = norm_rope(k_ref[0, r, src, :].astype(F32), gk_ref[...], cos, sin)
            qbase = pl.multiple_of(b * 2 * Q_BLOCK, 2 * Q_BLOCK)
            qs[pl.ds(qbase, Q_BLOCK), :] = jnp.where(low_half, qn, 0.0).astype(BF16)
            qs[pl.ds(qbase + Q_BLOCK, Q_BLOCK), :] = jnp.where(low_half, 0.0, qn).astype(BF16)
            kdst = pl.ds(pl.multiple_of(b * Q_BLOCK + (r + 1) * pad, pad), Q_BLOCK)
            ks[kdst, :] = kn.astype(BF16)
            vs[kdst, :] = v_ref[0, r, src, :]

        def block(b, exact, gi=gi, nblk=nblk, split_index=split_index, table_rows=table_rows):
            r, n = split_index(b)
            variant = 3 if nblk == 1 else jnp.where(n == 0, 1, 0) + jnp.where(n == nblk - 1, 2, 0)
            qsrc = pl.ds(pl.multiple_of(b * 2 * Q_BLOCK, 2 * Q_BLOCK), 2 * Q_BLOCK)
            ksrc = pl.ds(pl.multiple_of(b * Q_BLOCK + r * pad, pad), ATTN_WIN)
            s = lax.dot_general(qs[qsrc, :], ks[ksrc, :], (((1,), (1,)), ((), ())),
                                preferred_element_type=F32) + bias_scr[variant]
            if exact:
                m = jnp.max(s, axis=-1, keepdims=True)
                s = s - m
                shift = jnp.where(low_half, m[0:Q_BLOCK], m[Q_BLOCK:]) + bound
            else:
                shift = bound
            p = jnp.exp(s)
            den = jnp.sum(p, axis=-1, keepdims=True)
            o2 = jnp.dot(p.astype(BF16), vs[ksrc, :], preferred_element_type=F32)
            o = jnp.where(low_half, o2[0:Q_BLOCK], o2[Q_BLOCK:])
            den = jnp.where(low_half, den[0:Q_BLOCK], den[Q_BLOCK:])
            dst = table_rows(r, n)
            o_scr[gi, dst, :] = o / den
            l_scr[gi, dst, :] = shift + jnp.log(den)
            return den

        def unrolled(fn):
            def body(it, carry):
                for u in range(ATTN_UNROLL):
                    fn(it * ATTN_UNROLL + u)
                return carry
            lax.fori_loop(0, n_blocks // ATTN_UNROLL, body, 0)

        unrolled(prep)

        def fast_body(it, dmin, block=block):
            for u in range(ATTN_UNROLL):
                dmin = jnp.minimum(dmin, block(it * ATTN_UNROLL + u, False))
            return dmin
        dmin = lax.fori_loop(0, n_blocks // ATTN_UNROLL, fast_body,
                             jnp.full((Q_BLOCK, LANES), 1.0, F32))

        @pl.when(jnp.logical_not(jnp.min(dmin) > DEN_SAFE))
        def _(block=block, unrolled=unrolled):
            unrolled(lambda b: block(b, True))

    l0 = l_scr[0]
    l1 = l_scr[1]
    l2 = l_scr[2]
    m = jnp.maximum(jnp.maximum(l0, l1), l2)
    e0 = jnp.exp(l0 - m)
    e1 = jnp.exp(l1 - m)
    e2 = jnp.exp(l2 - m)
    mix = (e0 * o_scr[0] + e1 * o_scr[1] + e2 * o_scr[2]) / (e0 + e1 + e2)
    out_ref[0] = mix.astype(BF16)


def _attention(qkv, cos_t, sin_t, gq, gk):
    b, _, s, _ = qkv[0].shape
    in_specs = []
    args = []
    for gi, dil in enumerate(DILATIONS):
        for part in range(3):
            in_specs.append(pl.BlockSpec((1, dil, s // dil, LANES),
                                         lambda bi, hp, part=part: (bi, 0, 0, 2 * part + hp)))
            args.append(qkv[gi])
    consts = _attn_consts()
    tab_spec = pl.BlockSpec((1, s, LANES), lambda bi, hp: (bi, 0, 0))
    gain_spec = pl.BlockSpec((1, LANES), lambda bi, hp: (0, 0))
    const_specs = [pl.BlockSpec(c.shape, lambda bi, hp, nd=c.ndim: (0,) * nd) for c in consts]
    key_rows = max(d * (s // d + BAND_HALF) for d in DILATIONS) + BAND_HALF
    return pl.pallas_call(
        functools.partial(_attn_kernel, seq=s),
        grid=(b, 2),
        in_specs=in_specs + [tab_spec] * 2 + [gain_spec] * 2 + const_specs,
        out_specs=pl.BlockSpec((1, s, LANES), lambda bi, hp: (bi, 0, hp)),
        out_shape=jax.ShapeDtypeStruct((b, s, GROUP_W), BF16),
        scratch_shapes=[
            pltpu.VMEM((2 * s, LANES), BF16),
            pltpu.VMEM((key_rows, LANES), BF16),
            pltpu.VMEM((key_rows, LANES), BF16),
            pltpu.VMEM((3, s, LANES), F32),
            pltpu.VMEM((3, s, LANES), F32),
            pltpu.VMEM((4, 2 * Q_BLOCK, ATTN_WIN), F32),
        ],
        compiler_params=pltpu.CompilerParams(
            dimension_semantics=("parallel", "parallel"), vmem_limit_bytes=VMEM_LIMIT_BYTES),
        name="dilated_attn",
    )(*args, cos_t, sin_t, gq, gk, *consts)


def _merge_kernel(x_ref, oa_ref, u_ref, uprev_ref, unext_ref, cb_ref, qm_ref, km_ref, vm_ref,
                  gates_ref, wconv_ref, wpa_ref, wpc_ref, wpm_ref, wout_ref, gffn_ref, wr_ref, br_ref,
                  x1_ref, route_ref, count_ref, om_scr, *, tm, nt):
    i = pl.program_id(1)
    u = u_ref[0].astype(F32)
    prev_row = uprev_ref[0].astype(F32)[15:16, :] * (i > 0).astype(F32)
    next_row = unext_ref[0].astype(F32)[0:1, :] * (i < nt - 1).astype(F32)
    row = lax.broadcasted_iota(jnp.int32, (tm, 1), 0)
    um = jnp.where(row == 0, prev_row, pltpu.roll(u, 1, 0))
    up = jnp.where(row == tm - 1, next_row, pltpu.roll(u, tm - 1, 0))
    wc = wconv_ref[...]
    y = wc[0:1, :] * um + wc[1:2, :] * u + wc[2:3, :] * up
    z = (cb_ref[0].astype(F32) * y).astype(BF16)

    for hh in range(MEM_HEADS):
        sl = slice(hh * MEM_HEAD_DIM, (hh + 1) * MEM_HEAD_DIM)
        s = lax.dot_general(qm_ref[0, :, sl], km_ref[0, :, sl], (((1,), (1,)), ((), ())),
                            preferred_element_type=F32) * (MEM_HEAD_DIM ** -0.5)
        m = jnp.max(s, axis=-1, keepdims=True)
        p = jnp.exp(s - m)
        den = jnp.sum(p, axis=-1, keepdims=True)
        o = jnp.dot(p.astype(BF16), vm_ref[0, :, sl], preferred_element_type=F32) / den
        om_scr[:, sl] = o.astype(BF16)

    pa = jnp.dot(oa_ref[0], wpa_ref[...], preferred_element_type=F32)
    pc = jnp.dot(z, wpc_ref[...], preferred_element_type=F32)
    pm = jnp.dot(om_scr[...], wpm_ref[...], preferred_element_type=F32)
    merged = (gates_ref[0, :, 0:D_MODEL].astype(F32) * pa
              + gates_ref[0, :, D_MODEL:2 * D_MODEL].astype(F32) * pc
              + gates_ref[0, :, 2 * D_MODEL:3 * D_MODEL].astype(F32) * pm)
    x1 = x_ref[0] + jnp.dot(merged.astype(BF16), wout_ref[...], preferred_element_type=F32)
    for c in range(ROW_CHUNKS):
        x1_ref[pl.ds(c, tm, stride=ROW_CHUNKS), :] = x1[:, c * LANES:(c + 1) * LANES]

    h2 = _rms(x1, gffn_ref[...])
    h_hi = h2.astype(BF16)
    h_lo = (h2 - h_hi.astype(F32)).astype(BF16)
    hh = jnp.dot(h_hi, wr_ref[...], preferred_element_type=F32)
    lg = (hh[:, 0:LANES] + hh[:, LANES:2 * LANES]
          + jnp.dot(h_lo, wr_ref[:, 0:LANES], preferred_element_type=F32)) + br_ref[...]

    lane = lax.broadcasted_iota(jnp.int32, (tm, LANES), 1).astype(F32)
    far = float(LANES)
    is_group = lane < N_EXPERT_GROUPS
    gl = jnp.where(is_group, lg, ROUTE_NEG)
    mg = jnp.max(gl, axis=-1, keepdims=True)
    gidx = jnp.min(jnp.where(gl == mg, lane, far), axis=-1, keepdims=True)
    pg_top = 1.0 / jnp.sum(jnp.where(is_group, jnp.exp(lg - mg), 0.0), axis=-1, keepdims=True)
    first = N_EXPERT_GROUPS + EXPERTS_PER_GROUP * gidx
    in_sel = jnp.logical_and(lane >= first, lane < first + EXPERTS_PER_GROUP)
    sel = jnp.where(in_sel, lg, ROUTE_NEG)
    m1 = jnp.max(sel, axis=-1, keepdims=True)
    i1 = jnp.min(jnp.where(jnp.logical_and(in_sel, sel == m1), lane, far), axis=-1, keepdims=True)
    keep = jnp.logical_and(in_sel, lane != i1)
    sel2 = jnp.where(keep, lg, ROUTE_NEG)
    m2 = jnp.max(sel2, axis=-1, keepdims=True)
    i2 = jnp.min(jnp.where(jnp.logical_and(keep, sel2 == m2), lane, far), axis=-1, keepdims=True)
    r = jnp.exp(m2 - m1)
    w1 = pg_top / (1.0 + r)
    w2 = w1 * r
    route = jnp.where(lane == 0, i1 - N_EXPERT_GROUPS,
                      jnp.where(lane == 1, i2 - N_EXPERT_GROUPS,
                                jnp.where(lane == 2, w1, jnp.where(lane == 3, w2, 0.0))))
    route_ref[...] = jnp.transpose(route)[0:8, :]

    @pl.when(jnp.logical_and(pl.program_id(0) == 0, i == 0))
    def _():
        count_ref[...] = jnp.zeros_like(count_ref)
    picked = jnp.where(jnp.logical_or(lane == i1, lane == i2), 1.0, 0.0)
    count_ref[...] += jnp.sum(picked, axis=0, keepdims=True)


def _merge(x, o_attn, u, cb, qm, km, vm, gates, w_conv, wpa, wpc, wpm, wout, g_ffn, w_r, b_r):
    b, s, d = x.shape
    tm = ROW_TILE
    nt = s // tm
    halo = 16
    hb = tm // halo

    def tile(width):
        return pl.BlockSpec((1, tm, width), lambda bi, i: (bi, i, 0))

    def whole(arr):
        return pl.BlockSpec(arr.shape, lambda bi, i: (0,) * arr.ndim)

    def per_batch(arr):
        return pl.BlockSpec((1,) + arr.shape[1:], lambda bi, i: (bi, 0, 0))

    in_specs = [
        tile(d), tile(GROUP_W), tile(CONV_WIDTH),
        pl.BlockSpec((1, halo, CONV_WIDTH), lambda bi, i: (bi, jnp.maximum(i * hb - 1, 0), 0)),
        pl.BlockSpec((1, halo, CONV_WIDTH),
                     lambda bi, i: (bi, jnp.minimum((i + 1) * hb, s // halo - 1), 0)),
        tile(CONV_WIDTH), tile(MEM_WIDTH), per_batch(km), per_batch(vm), tile(N_BRANCH * d),
        whole(w_conv), whole(wpa), whole(wpc), whole(wpm), whole(wout), whole(g_ffn),
        whole(w_r), whole(b_r),
    ]
    return pl.pallas_call(
        functools.partial(_merge_kernel, tm=tm, nt=nt),
        grid=(b, nt),
        in_specs=in_specs,
        out_specs=[pl.BlockSpec((tm * ROW_CHUNKS, LANES), lambda bi, i: (bi * nt + i, 0)),
                   pl.BlockSpec((ROUTE_ROWS, tm), lambda bi, i: (0, bi * nt + i)),
                   pl.BlockSpec((ROUTE_ROWS, LANES), lambda bi, i: (0, 0))],
        out_shape=[jax.ShapeDtypeStruct((b * s * ROW_CHUNKS, LANES), F32),
                   jax.ShapeDtypeStruct((ROUTE_ROWS, b * s), F32),
                   jax.ShapeDtypeStruct((ROUTE_ROWS, LANES), F32)],
        scratch_shapes=[pltpu.VMEM((tm, MEM_WIDTH), BF16)],
        compiler_params=pltpu.CompilerParams(
            dimension_semantics=("arbitrary", "arbitrary"), vmem_limit_bytes=VMEM_LIMIT_BYTES),
        name="merge",
    )(x, o_attn, u, u, u, cb, qm, km, vm, gates, w_conv, wpa, wpc, wpm, wout, g_ffn, w_r, b_r)


DRAIN_STEPS = 2


def _expert_kernel(blk_exp_ref, nvalid_ref, niter_ref, tok_ref, tok_next_ref, tok_ahead_ref, dst_ref,
                   x1_hbm, gffn_ref, wg_ref, wu_ref, wd_ref, y_hbm, xbuf, hbuf, ybuf, wg_s, wu_s, wd_s,
                   gsem, ssem):
    i = pl.program_id(0)
    slot = lax.rem(i, 2)
    gslot = lax.rem(i, GATHER_SLOTS)
    rows = MOE_BLOCK * ROW_CHUNKS
    burst = GATHER_UNROLL * ROW_CHUNKS

    def start_gather(idx_ref, s, n_it):
        def body(it, c):
            for u in range(GATHER_UNROLL):
                j = it * GATHER_UNROLL + u
                src_row = pl.multiple_of(idx_ref[0, 0, j], ROW_CHUNKS)
                pltpu.make_async_copy(x1_hbm.at[pl.ds(src_row, ROW_CHUNKS), :],
                                      xbuf.at[s, pl.ds(j * ROW_CHUNKS, ROW_CHUNKS), :],
                                      gsem.at[s]).start(priority=u % 2)
            return c
        lax.fori_loop(0, n_it, body, 0)

    def wait_gather(s, n_it):
        moved = pl.ds(0, n_it * burst)
        pltpu.make_async_copy(x1_hbm.at[moved, :], xbuf.at[s, moved, :], gsem.at[s]).wait()

    def start_scatter(s, n_it):
        def body(it, c):
            for u in range(GATHER_UNROLL):
                j = it * GATHER_UNROLL + u
                dst_row = pl.multiple_of(dst_ref[0, 0, j], ROW_CHUNKS)
                pltpu.make_async_copy(ybuf.at[s, pl.ds(j * ROW_CHUNKS, ROW_CHUNKS), :],
                                      y_hbm.at[pl.ds(dst_row, ROW_CHUNKS), :],
                                      ssem.at[s]).start(priority=u % 2)
            return c
        lax.fori_loop(0, n_it, body, 0)

    def wait_scatter(s, n_it):
        moved = pl.ds(0, n_it * burst)
        pltpu.make_async_copy(ybuf.at[s, moved, :], y_hbm.at[moved, :], ssem.at[s]).wait()

    nvalid = nvalid_ref[i]
    prev = jnp.maximum(i - 1, 0)
    prev2 = jnp.maximum(i - 2, 0)

    @pl.when(i == 0)
    def _():
        xbuf[...] = jnp.zeros_like(xbuf)
        start_gather(tok_ref, 0, niter_ref[0])
        start_gather(tok_next_ref, 1, niter_ref[1])
        ybuf[0] = jnp.zeros((rows, LANES), F32)
        spare = pltpu.make_async_copy(ybuf.at[0], y_hbm.at[pl.ds(y_hbm.shape[0] - rows, rows), :],
                                      ssem.at[0])
        spare.start()
        spare.wait()

    @pl.when(jnp.logical_and(i >= 2, nvalid_ref[prev2] > 0))
    def _():
        wait_scatter(slot, niter_ref[prev2])

    @pl.when(jnp.logical_or(i == 0, blk_exp_ref[i] != blk_exp_ref[prev]))
    def _():
        wg_s[...] = wg_ref[0].astype(BF16)
        wu_s[...] = wu_ref[0].astype(BF16)
        wd_s[...] = wd_ref[0].astype(BF16)

    @pl.when(nvalid > 0)
    def _():
        wait_gather(gslot, niter_ref[i])
        start_gather(tok_ahead_ref, lax.rem(i + 2, GATHER_SLOTS), niter_ref[i + 2])

        ss = None
        for c in range(ROW_CHUNKS):
            xc = xbuf[gslot, pl.ds(c, MOE_BLOCK, stride=ROW_CHUNKS), :]
            ss = xc * xc if ss is None else ss + xc * xc
        scale = lax.rsqrt(jnp.sum(ss, axis=-1, keepdims=True) * (1.0 / D_MODEL) + EPS)
        for c in range(ROW_CHUNKS):
            sl = slice(c * LANES, (c + 1) * LANES)
            xc = xbuf[gslot, pl.ds(c, MOE_BLOCK, stride=ROW_CHUNKS), :]
            hbuf[:, sl] = (xc * scale * gffn_ref[:, sl]).astype(BF16)
        h2 = hbuf[...]
        a = jnp.dot(h2, wg_s[...], preferred_element_type=F32)
        up = jnp.dot(h2, wu_s[...], preferred_element_type=F32)
        act = (a * jax.nn.sigmoid(a) * up).astype(BF16)
        y = jnp.dot(act, wd_s[...], preferred_element_type=F32)
        for c in range(ROW_CHUNKS):
            ybuf[slot, pl.ds(c, MOE_BLOCK, stride=ROW_CHUNKS), :] = y[:, c * LANES:(c + 1) * LANES]
        start_scatter(slot, niter_ref[i])


def _experts(x1r, g_ffn, blk_exp, nvalid, niter, row_tok, row_dst, wg, wu, wd):
    rows_total, _ = x1r.shape
    d = D_MODEL
    n_blk = blk_exp.shape[0]
    idx_spec = lambda fn: pl.BlockSpec((1, 1, MOE_BLOCK), fn, memory_space=pltpu.SMEM)
    grid_spec = pltpu.PrefetchScalarGridSpec(
        num_scalar_prefetch=3,
        grid=(n_blk,),
        in_specs=[
            idx_spec(lambda i, be, nv, ni: (i, 0, 0)),
            idx_spec(lambda i, be, nv, ni: (jnp.minimum(i + 1, n_blk - 1), 0, 0)),
            idx_spec(lambda i, be, nv, ni: (jnp.minimum(i + 2, n_blk - 1), 0, 0)),
            idx_spec(lambda i, be, nv, ni: (i, 0, 0)),
            pl.BlockSpec(memory_space=pl.ANY),
            pl.BlockSpec((1, d), lambda i, be, nv, ni: (0, 0)),
            pl.BlockSpec((1, d, EXPERT_FF), lambda i, be, nv, ni: (be[i], 0, 0)),
            pl.BlockSpec((1, d, EXPERT_FF), lambda i, be, nv, ni: (be[i], 0, 0)),
            pl.BlockSpec((1, EXPERT_FF, d), lambda i, be, nv, ni: (be[i], 0, 0)),
        ],
        out_specs=pl.BlockSpec(memory_space=pl.ANY),
        scratch_shapes=[
            pltpu.VMEM((GATHER_SLOTS, MOE_BLOCK * ROW_CHUNKS, LANES), F32),
            pltpu.VMEM((MOE_BLOCK, d), BF16),
            pltpu.VMEM((2, MOE_BLOCK * ROW_CHUNKS, LANES), F32),
            pltpu.VMEM((d, EXPERT_FF), BF16),
            pltpu.VMEM((d, EXPERT_FF), BF16),
            pltpu.VMEM((EXPERT_FF, d), BF16),
            pltpu.SemaphoreType.DMA((GATHER_SLOTS,)),
            pltpu.SemaphoreType.DMA((2,)),
        ],
    )
    return pl.pallas_call(
        _expert_kernel,
        grid_spec=grid_spec,
        out_shape=jax.ShapeDtypeStruct((TOP_K * rows_total + MOE_BLOCK * ROW_CHUNKS, LANES), F32),
        compiler_params=pltpu.CompilerParams(
            dimension_semantics=("arbitrary",), vmem_limit_bytes=VMEM_LIMIT_BYTES),
        name="experts",
    )(blk_exp, nvalid, niter, row_tok, row_tok, row_tok, row_dst, x1r, g_ffn, wg, wu, wd)


def _combine_kernel(x1_ref, y0_ref, y1_ref, route_ref, o_ref, *, tm):
    w_cols = jnp.transpose(route_ref[...])
    w0 = w_cols[:, 2:3]
    w1 = w_cols[:, 3:4]
    for c in range(ROW_CHUNKS):
        rows = pl.ds(c, tm, stride=ROW_CHUNKS)
        o_ref[:, c * LANES:(c + 1) * LANES] = (x1_ref[rows, :] + w0 * y0_ref[rows, :]
                                               + w1 * y1_ref[rows, :])


def _combine(x1r, y, route_t):
    n = x1r.shape[0] // ROW_CHUNKS
    tm = ROW_TILE
    nt = n // tm
    blk = (tm * ROW_CHUNKS, LANES)
    return pl.pallas_call(
        functools.partial(_combine_kernel, tm=tm),
        grid=(nt,),
        in_specs=[pl.BlockSpec(blk, lambda i: (i, 0)),
                  pl.BlockSpec(blk, lambda i: (i, 0)),
                  pl.BlockSpec(blk, lambda i: (i + nt, 0)),
                  pl.BlockSpec((ROUTE_ROWS, tm), lambda i: (0, i))],
        out_specs=pl.BlockSpec((tm, D_MODEL), lambda i: (i, 0)),
        out_shape=jax.ShapeDtypeStruct((n, D_MODEL), F32),
        compiler_params=pltpu.CompilerParams(dimension_semantics=("parallel",)),
        name="combine",
    )(x1r, y, y, route_t)


def _rope_tables(positions):
    half = ROT_DIM // 2
    lane = np.arange(LANES)
    dim = lane % HEAD_DIM
    rot = dim < ROT_DIM
    spread_cos = np.zeros((half, LANES), np.float32)
    spread_cos[dim[rot] % half, lane[rot]] = 1.0
    spread_sin = np.zeros((half, LANES), np.float32)
    spread_sin[dim[rot] % half, lane[rot]] = np.where(dim[rot] < half, -1.0, 1.0)
    inv = ROPE_THETA ** (-jnp.arange(0, ROT_DIM, 2, dtype=F32) / ROT_DIM)
    ang = inv[None, :, None] * positions.astype(F32)[:, None, :]
    spread = functools.partial(jnp.einsum, 'bjs,jl->bsl', precision=lax.Precision.HIGHEST)
    cos_t = spread(jnp.cos(ang), jnp.asarray(spread_cos)) + jnp.asarray((~rot).astype(np.float32))
    sin_t = spread(jnp.sin(ang), jnp.asarray(spread_sin))
    return cos_t, sin_t


def _route(route_t, counts, n):
    nk = n * TOP_K
    e = route_t[0:TOP_K].reshape(nk).astype(jnp.int32)
    a = jnp.arange(nk, dtype=jnp.int32)
    _, a_s = lax.sort((e, a), num_keys=1, is_stable=True)
    padded = ((counts + MOE_BLOCK - 1) // MOE_BLOCK) * MOE_BLOCK
    pend = jnp.cumsum(padded)
    pstart = pend - padded
    start = jnp.cumsum(counts) - counts
    n_blk = nk // MOE_BLOCK + N_EXPERTS + DRAIN_STEPS
    blk_start = jnp.arange(n_blk, dtype=jnp.int32) * MOE_BLOCK
    blk_exp = jnp.minimum(jnp.sum((pend[None, :] <= blk_start[:, None]).astype(jnp.int32), axis=1),
                          N_EXPERTS - 1).astype(jnp.int32)
    onehot = blk_exp[:, None] == jnp.arange(N_EXPERTS, dtype=jnp.int32)[None, :]
    pick = lambda table: jnp.sum(jnp.where(onehot, table[None, :], 0), axis=1)
    offset = blk_start - pick(pstart)
    nvalid = jnp.clip(pick(counts) - offset, 0, MOE_BLOCK)
    nvalid = jnp.where(blk_start < pend[-1], nvalid, 0).astype(jnp.int32)
    j = jnp.arange(MOE_BLOCK, dtype=jnp.int32)[None, :]
    valid = j < nvalid[:, None]
    src = jnp.clip((pick(start) + offset)[:, None] + j, 0, nk - 1)
    a_p = a_s[src]
    tok = (jnp.where(valid, a_p - n * (a_p >= n).astype(jnp.int32), 0) * ROW_CHUNKS).astype(jnp.int32)
    dst = (jnp.where(valid, a_p, TOP_K * n + j) * ROW_CHUNKS).astype(jnp.int32)
    niter = (nvalid + GATHER_UNROLL - 1) // GATHER_UNROLL
    return blk_exp, nvalid, niter, tok.reshape(n_blk, 1, MOE_BLOCK), dst.reshape(n_blk, 1, MOE_BLOCK)


def kernel(x, mem, positions, g_mix, g_mem, w_in, g_qn_attn, g_kn_attn, w_conv, w_mem_kv, g_qn_mem,
           g_kn_mem, w_proj_attn, w_proj_conv, w_proj_mem, w_out, g_ffn, w_router_group,
           b_router_group, w_router_expert, b_router_expert, w_gate, w_up, w_down):
    b, s, d = x.shape
    n = b * s
    cos_t, sin_t = _rope_tables(positions)
    for l in range(w_in.shape[0]):
        wl = w_in[l]
        aw = N_BRANCH * GROUP_W
        cols = []
        for gi in range(len(DILATIONS)):
            for part in range(3):
                c0 = part * aw + gi * GROUP_W
                cols.append(wl[:, c0:c0 + GROUP_W])
        cols.append(wl[:, 3 * aw:])
        w_all = jnp.concatenate(cols, axis=1).astype(BF16)

        q0, q1, q2, u, cb, qm, gates = _inproj(x, g_mix[l][None], g_qn_mem[l][None], w_all)
        km, vm = _memkv(mem, g_mem[l][None], g_kn_mem[l][None], w_mem_kv[l].astype(BF16))
        gq = jnp.tile(g_qn_attn[l], 2)[None]
        gk = jnp.tile(g_kn_attn[l], 2)[None]
        o_attn = _attention((q0, q1, q2), cos_t, sin_t, gq, gk)

        w_r = jnp.zeros((d, LANES), F32)
        w_r = w_r.at[:, :N_EXPERT_GROUPS].set(w_router_group[l])
        w_r = w_r.at[:, N_EXPERT_GROUPS:N_EXPERT_GROUPS + N_EXPERTS].set(w_router_expert[l])
        b_r = jnp.zeros((1, LANES), F32)
        b_r = b_r.at[0, :N_EXPERT_GROUPS].set(b_router_group[l])
        b_r = b_r.at[0, N_EXPERT_GROUPS:N_EXPERT_GROUPS + N_EXPERTS].set(b_router_expert[l])
        w_r_hi = w_r.astype(BF16)
        w_r_lo = (w_r - w_r_hi.astype(F32)).astype(BF16)
        x1, route_t, count_t = _merge(x, o_attn, u, cb, qm, km, vm, gates, w_conv[l],
                                      w_proj_attn[l].astype(BF16), w_proj_conv[l].astype(BF16),
                                      w_proj_mem[l].astype(BF16), w_out[l].astype(BF16),
                                      g_ffn[l][None], jnp.concatenate([w_r_hi, w_r_lo], axis=1), b_r)

        counts = count_t[0, N_EXPERT_GROUPS:N_EXPERT_GROUPS + N_EXPERTS].astype(jnp.int32)
        blk_exp, nvalid, niter, row_tok, row_dst = _route(route_t, counts, n)
        y = _experts(x1, g_ffn[l][None], blk_exp, nvalid, niter, row_tok, row_dst,
                     w_gate[l], w_up[l], w_down[l])
        x = _combine(x1, y, route_t).reshape(b, s, d)
    return x
```

```python
import functools

import numpy as np
import jax
import jax.numpy as jnp
from jax import lax
from jax.experimental import pallas as pl
from jax.experimental.pallas import tpu as pltpu

F32 = jnp.float32
BF16 = jnp.bfloat16

D_MODEL = 1024
EPS = 1e-6
HEAD_DIM = 64
DILATIONS = (1, 4, 16)
BAND_HALF = 64
ATTN_SLOTS = 4
GROUP_W = ATTN_SLOTS * HEAD_DIM
QKV_W = 3 * GROUP_W
ROT_DIM = 16
ROPE_THETA = 500000.0
CONV_WIDTH = 768
MEM_HEADS = 4
MEM_HEAD_DIM = 128
MEM_WIDTH = 512
N_BRANCH = 3
N_EXPERT_GROUPS = 4
EXPERTS_PER_GROUP = 8
N_EXPERTS = 32
TOP_K = 2
EXPERT_FF = 512

LANES = 128
VMEM_LIMIT_BYTES = 56 * 1024 * 1024

ROW_TILE = 512
MERGE_TILE = 1024
Q_BLOCK = 128
MOE_BLOCK = 256
ROW_CHUNKS = D_MODEL // LANES
GATHER_UNROLL = 16
GATHER_SLOTS = 3
NEG_BIG = -1e30
ROUTE_NEG = -3e38
ROUTE_ROWS = 8


def _rms(t, gain):
    return t * lax.rsqrt(jnp.mean(t * t, axis=-1, keepdims=True) + EPS) * gain


def _inproj_kernel(x_ref, g_ref, gqm_ref, w_ref,
                   qkv0_ref, qkv1_ref, qkv2_ref, u_ref, cb_ref, qm_ref, gates_ref,
                   h_scr, hp_scr, *, tm):
    x = x_ref[0]
    h = _rms(x, g_ref[...])
    n_chunks = h.shape[1] // LANES
    for c in range(n_chunks):
        h_scr[c] = h[:, c * LANES:(c + 1) * LANES]
    hb = h.astype(BF16)

    def mm(lhs, c0, c1):
        return jnp.dot(lhs, w_ref[:, c0:c1], preferred_element_type=F32)

    qkv0_ref[0, 0] = mm(hb, 0, QKV_W).astype(BF16)
    for gi, out_ref in ((1, qkv1_ref), (2, qkv2_ref)):
        d = DILATIONS[gi]
        rows = tm // d
        for r in range(d):
            for c in range(n_chunks):
                hp_scr[r * rows:(r + 1) * rows, c * LANES:(c + 1) * LANES] = (
                    h_scr[c, pl.ds(r, rows, stride=d), :].astype(BF16))
        res = mm(hp_scr[...], gi * QKV_W, (gi + 1) * QKV_W)
        out_ref[0] = res.reshape(d, rows, QKV_W).astype(BF16)

    c = 3 * QKV_W
    cx = mm(hb, c, c + CONV_WIDTH)
    cc = mm(hb, c + 2 * CONV_WIDTH, c + 3 * CONV_WIDTH)
    u_ref[0] = (cc * cx).astype(BF16)
    cb_ref[0] = mm(hb, c + CONV_WIDTH, c + 2 * CONV_WIDTH).astype(BF16)

    c += 3 * CONV_WIDTH
    qm = mm(hb, c, c + MEM_WIDTH)
    for hh in range(MEM_HEADS):
        sl = slice(hh * MEM_HEAD_DIM, (hh + 1) * MEM_HEAD_DIM)
        qm_ref[0, :, sl] = _rms(qm[:, sl], gqm_ref[...]).astype(BF16)

    c += MEM_WIDTH
    for j in range(N_BRANCH):
        gl = mm(hb, c + j * D_MODEL, c + (j + 1) * D_MODEL)
        gates_ref[0, :, j * D_MODEL:(j + 1) * D_MODEL] = jax.nn.sigmoid(gl).astype(BF16)


def _inproj(x, g_mix, g_qn_mem, w_all):
    b, s, d = x.shape
    tm = ROW_TILE
    nt = s // tm
    outs = []
    out_specs = []
    for dil in DILATIONS:
        outs.append(jax.ShapeDtypeStruct((b, dil, s // dil, QKV_W), BF16))
        out_specs.append(pl.BlockSpec((1, dil, tm // dil, QKV_W), lambda bi, i: (bi, 0, i, 0)))
    for width in (CONV_WIDTH, CONV_WIDTH, MEM_WIDTH, N_BRANCH * D_MODEL):
        outs.append(jax.ShapeDtypeStruct((b, s, width), BF16))
        out_specs.append(pl.BlockSpec((1, tm, width), lambda bi, i: (bi, i, 0)))
    return pl.pallas_call(
        functools.partial(_inproj_kernel, tm=tm),
        grid=(b, nt),
        in_specs=[
            pl.BlockSpec((1, tm, d), lambda bi, i: (bi, i, 0)),
            pl.BlockSpec((1, d), lambda bi, i: (0, 0)),
            pl.BlockSpec((1, MEM_HEAD_DIM), lambda bi, i: (0, 0)),
            pl.BlockSpec(w_all.shape, lambda bi, i: (0, 0), pipeline_mode=pl.Buffered(1)),
        ],
        out_specs=out_specs,
        out_shape=outs,
        scratch_shapes=[pltpu.VMEM((d // LANES, tm, LANES), F32), pltpu.VMEM((tm, d), BF16)],
        compiler_params=pltpu.CompilerParams(
            dimension_semantics=("parallel", "parallel"), vmem_limit_bytes=VMEM_LIMIT_BYTES),
        name="inproj",
    )(x, g_mix, g_qn_mem, w_all)


def _memkv_kernel(mem_ref, g_ref, gk_ref, w_ref, k_ref, v_ref):
    h = _rms(mem_ref[0], g_ref[...]).astype(BF16)
    kv = jnp.dot(h, w_ref[...], preferred_element_type=F32)
    for hh in range(MEM_HEADS):
        sl = slice(hh * MEM_HEAD_DIM, (hh + 1) * MEM_HEAD_DIM)
        k_ref[0, :, sl] = _rms(kv[:, sl], gk_ref[...]).astype(BF16)
    v_ref[0] = kv[:, MEM_WIDTH:].astype(BF16)


def _memkv(mem, g_mem, g_kn_mem, w_kv):
    b, m, d = mem.shape
    return pl.pallas_call(
        _memkv_kernel,
        grid=(b,),
        in_specs=[
            pl.BlockSpec((1, m, d), lambda bi: (bi, 0, 0)),
            pl.BlockSpec((1, d), lambda bi: (0, 0)),
            pl.BlockSpec((1, MEM_HEAD_DIM), lambda bi: (0, 0)),
            pl.BlockSpec(w_kv.shape, lambda bi: (0, 0)),
        ],
        out_specs=[pl.BlockSpec((1, m, MEM_WIDTH), lambda bi: (bi, 0, 0))] * 2,
        out_shape=[jax.ShapeDtypeStruct((b, m, MEM_WIDTH), BF16)] * 2,
        compiler_params=pltpu.CompilerParams(dimension_semantics=("parallel",)),
        name="memkv",
    )(mem, g_mem, g_kn_mem, w_kv)


ATTN_WIN = Q_BLOCK + 2 * BAND_HALF
ATTN_UNROLL = 16
DEN_SAFE = 1e-30


def _attn_consts():
    lane = np.arange(LANES)
    gsum = (lane[:, None] // HEAD_DIM == lane[None, :] // HEAD_DIM).astype(np.float32)
    half = ROT_DIM // 2
    dim = lane % HEAD_DIM
    src = np.where(dim < half, lane + half, np.where(dim < ROT_DIM, lane - half, -1))
    pswap = (lane[:, None] == src[None, :]).astype(np.float32)
    i = np.arange(Q_BLOCK)[:, None]
    c = np.arange(ATTN_WIN)[None, :]
    band = (c >= i) & (c <= i + 2 * BAND_HALF)
    first = c >= BAND_HALF
    last = c < Q_BLOCK + BAND_HALF
    variants = [band, band & first, band & last, band & first & last]
    bias = np.stack([np.where(v, 0.0, NEG_BIG) for v in variants]).astype(np.float32)
    return jnp.asarray(gsum, BF16), jnp.asarray(pswap, BF16), jnp.asarray(bias, F32)


def _attn_kernel(q0, k0, v0, q1, k1, v1, q2, k2, v2, cos_ref, sin_ref, gq_ref, gk_ref,
                 gsum_ref, pswap_ref, bias_ref, out_ref, qs, ks, vs, o_scr, l_scr, bias_scr, *, seq):
    lane = lax.broadcasted_iota(jnp.int32, (1, LANES), 1)
    low_half = lane < HEAD_DIM
    pad = BAND_HALF
    n_blocks = seq // Q_BLOCK

    bound = HEAD_DIM ** 0.5 * jnp.max(jnp.abs(gq_ref[...])) * jnp.max(jnp.abs(gk_ref[...]))
    shifted = bias_ref[...] - bound
    bias_scr[:, 0:Q_BLOCK, :] = shifted
    bias_scr[:, Q_BLOCK:2 * Q_BLOCK, :] = shifted

    def norm_rope(t, gain, cos, sin):
        ms = jnp.dot((t * t).astype(BF16), gsum_ref[...], preferred_element_type=F32) * (1.0 / HEAD_DIM)
        tn = t * lax.rsqrt(ms + EPS) * gain
        partner = jnp.dot(tn.astype(BF16), pswap_ref[...], preferred_element_type=F32)
        return tn * cos + partner * sin

    for gi, (q_ref, k_ref, v_ref) in enumerate(((q0, k0, v0), (q1, k1, v1), (q2, k2, v2))):
        d = DILATIONS[gi]
        length = seq // d
        nblk = length // Q_BLOCK
        region = length + pad

        zpad = jnp.zeros((pad, LANES), BF16)
        for r in range(d + 1):
            ks[r * region:r * region + pad, :] = zpad
            vs[r * region:r * region + pad, :] = zpad

        def split_index(b, nblk=nblk):
            if nblk == 1:
                return b, 0
            r = b // nblk
            return r, b - r * nblk

        def table_rows(r, n, d=d):
            if d == 1:
                return pl.ds(pl.multiple_of(n * Q_BLOCK, Q_BLOCK), Q_BLOCK)
            return pl.ds(r + d * n * Q_BLOCK, Q_BLOCK, stride=d)

        def prep(b, q_ref=q_ref, k_ref=k_ref, v_ref=v_ref, split_index=split_index,
                 table_rows=table_rows):
            r, n = split_index(b)
            src = pl.ds(pl.multiple_of(n * Q_BLOCK, Q_BLOCK), Q_BLOCK)
            rows = table_rows(r, n)
            cos = cos_ref[0, rows, :]
            sin = sin_ref[0, rows, :]
            qn = norm_rope(q_ref[0, r, src, :].astype(F32), gq_ref[...], cos, sin) * (HEAD_DIM ** -0.5)
            kn = norm_rope(k_ref[0, r, src, :].astype(F32), gk_ref[...], cos, sin)
            qbase = pl.multiple_of(b * 2 * Q_BLOCK, 2 * Q_BLOCK)
            qs[pl.ds(qbase, Q_BLOCK), :] = jnp.where(low_half, qn, 0.0).astype(BF16)
            qs[pl.ds(qbase + Q_BLOCK, Q_BLOCK), :] = jnp.where(low_half, 0.0, qn).astype(BF16)
            kdst = pl.ds(pl.multiple_of(b * Q_BLOCK + (r + 1) * pad, pad), Q_BLOCK)
            ks[kdst, :] = kn.astype(BF16)
            vs[kdst, :] = v_ref[0, r, src, :]

        def block(b, exact, gi=gi, nblk=nblk, split_index=split_index, table_rows=table_rows):
            r, n = split_index(b)
            variant = 3 if nblk == 1 else jnp.where(n == 0, 1, 0) + jnp.where(n == nblk - 1, 2, 0)
            qsrc = pl.ds(pl.multiple_of(b * 2 * Q_BLOCK, 2 * Q_BLOCK), 2 * Q_BLOCK)
            ksrc = pl.ds(pl.multiple_of(b * Q_BLOCK + r * pad, pad), ATTN_WIN)
            s = lax.dot_general(qs[qsrc, :], ks[ksrc, :], (((1,), (1,)), ((), ())),
                                preferred_element_type=F32) + bias_scr[variant]
            if exact:
                m = jnp.max(s, axis=-1, keepdims=True)
                s = s - m
                shift = jnp.where(low_half, m[0:Q_BLOCK], m[Q_BLOCK:]) + bound
            else:
                shift = bound
            p = jnp.exp(s)
            den = jnp.sum(p, axis=-1, keepdims=True)
            o2 = jnp.dot(p.astype(BF16), vs[ksrc, :], preferred_element_type=F32)
            o = jnp.where(low_half, o2[0:Q_BLOCK], o2[Q_BLOCK:])
            den = jnp.where(low_half, den[0:Q_BLOCK], den[Q_BLOCK:])
            dst = table_rows(r, n)
            o_scr[gi, dst, :] = o / den
            l_scr[gi, dst, :] = shift + jnp.log(den)
            return den

        def unrolled(fn):
            def body(it, carry):
                for u in range(ATTN_UNROLL):
                    fn(it * ATTN_UNROLL + u)
                return carry
            lax.fori_loop(0, n_blocks // ATTN_UNROLL, body, 0)

        unrolled(prep)

        def fast_body(it, dmin, block=block):
            for u in range(ATTN_UNROLL):
                dmin = jnp.minimum(dmin, block(it * ATTN_UNROLL + u, False))
            return dmin
        dmin = lax.fori_loop(0, n_blocks // ATTN_UNROLL, fast_body,
                             jnp.full((Q_BLOCK, LANES), 1.0, F32))

        @pl.when(jnp.logical_not(jnp.min(dmin) > DEN_SAFE))
        def _(block=block, unrolled=unrolled):
            unrolled(lambda b: block(b, True))

    l0 = l_scr[0]
    l1 = l_scr[1]
    l2 = l_scr[2]
    m = jnp.maximum(jnp.maximum(l0, l1), l2)
    e0 = jnp.exp(l0 - m)
    e1 = jnp.exp(l1 - m)
    e2 = jnp.exp(l2 - m)
    mix = (e0 * o_scr[0] + e1 * o_scr[1] + e2 * o_scr[2]) / (e0 + e1 + e2)
    out_ref[0] = mix.astype(BF16)


def _attention(qkv, cos_t, sin_t, gq, gk):
    b, _, s, _ = qkv[0].shape
    in_specs = []
    args = []
    for gi, dil in enumerate(DILATIONS):
        for part in range(3):
            in_specs.append(pl.BlockSpec((1, dil, s // dil, LANES),
                                         lambda bi, hp, part=part: (bi, 0, 0, 2 * part + hp)))
            args.append(qkv[gi])
    consts = _attn_consts()
    tab_spec = pl.BlockSpec((1, s, LANES), lambda bi, hp: (bi, 0, 0))
    gain_spec = pl.BlockSpec((1, LANES), lambda bi, hp: (0, 0))
    const_specs = [pl.BlockSpec(c.shape, lambda bi, hp, nd=c.ndim: (0,) * nd) for c in consts]
    key_rows = max(d * (s // d + BAND_HALF) for d in DILATIONS) + BAND_HALF
    return pl.pallas_call(
        functools.partial(_attn_kernel, seq=s),
        grid=(b, 2),
        in_specs=in_specs + [tab_spec] * 2 + [gain_spec] * 2 + const_specs,
        out_specs=pl.BlockSpec((1, s, LANES), lambda bi, hp: (bi, 0, hp)),
        out_shape=jax.ShapeDtypeStruct((b, s, GROUP_W), BF16),
        scratch_shapes=[
            pltpu.VMEM((2 * s, LANES), BF16),
            pltpu.VMEM((key_rows, LANES), BF16),
            pltpu.VMEM((key_rows, LANES), BF16),
            pltpu.VMEM((3, s, LANES), F32),
            pltpu.VMEM((3, s, LANES), F32),
            pltpu.VMEM((4, 2 * Q_BLOCK, ATTN_WIN), F32),
        ],
        compiler_params=pltpu.CompilerParams(
            dimension_semantics=("parallel", "parallel"), vmem_limit_bytes=VMEM_LIMIT_BYTES),
        name="dilated_attn",
    )(*args, cos_t, sin_t, gq, gk, *consts)


def _merge_kernel(x_ref, oa_ref, u_ref, uprev_ref, unext_ref, cb_ref, qm_ref, km_ref, vm_ref,
                  gates_ref, wconv_ref, wpa_ref, wpc_ref, wpm_ref, wout_ref, gffn_ref, wr_ref, br_ref,
                  x1_ref, route_ref, count_ref, om_scr, *, tm, nt):
    i = pl.program_id(1)
    u = u_ref[0].astype(F32)
    prev_row = uprev_ref[0].astype(F32)[15:16, :] * (i > 0).astype(F32)
    next_row = unext_ref[0].astype(F32)[0:1, :] * (i < nt - 1).astype(F32)
    row = lax.broadcasted_iota(jnp.int32, (tm, 1), 0)
    um = jnp.where(row == 0, prev_row, pltpu.roll(u, 1, 0))
    up = jnp.where(row == tm - 1, next_row, pltpu.roll(u, tm - 1, 0))
    wc = wconv_ref[...]
    y = wc[0:1, :] * um + wc[1:2, :] * u + wc[2:3, :] * up
    z = (cb_ref[0].astype(F32) * y).astype(BF16)

    for hh in range(MEM_HEADS):
        sl = slice(hh * MEM_HEAD_DIM, (hh + 1) * MEM_HEAD_DIM)
        s = lax.dot_general(qm_ref[0, :, sl], km_ref[0, :, sl], (((1,), (1,)), ((), ())),
                            preferred_element_type=F32) * (MEM_HEAD_DIM ** -0.5)
        m = jnp.max(s, axis=-1, keepdims=True)
        p = jnp.exp(s - m)
        den = jnp.sum(p, axis=-1, keepdims=True)
        o = jnp.dot(p.astype(BF16), vm_ref[0, :, sl], preferred_element_type=F32) / den
        om_scr[:, sl] = o.astype(BF16)

    pa = jnp.dot(oa_ref[0], wpa_ref[...], preferred_element_type=F32)
    pc = jnp.dot(z, wpc_ref[...], preferred_element_type=F32)
    pm = jnp.dot(om_scr[...], wpm_ref[...], preferred_element_type=F32)
    merged = (gates_ref[0, :, 0:D_MODEL].astype(F32) * pa
              + gates_ref[0, :, D_MODEL:2 * D_MODEL].astype(F32) * pc
              + gates_ref[0, :, 2 * D_MODEL:3 * D_MODEL].astype(F32) * pm)
    x1 = x_ref[0] + jnp.dot(merged.astype(BF16), wout_ref[...], preferred_element_type=F32)
    for c in range(ROW_CHUNKS):
        x1_ref[pl.ds(c, tm, stride=ROW_CHUNKS), :] = x1[:, c * LANES:(c + 1) * LANES]

    h2 = _rms(x1, gffn_ref[...])
    h_hi = h2.astype(BF16)
    h_lo = (h2 - h_hi.astype(F32)).astype(BF16)
    hh = jnp.dot(h_hi, wr_ref[...], preferred_element_type=F32)
    lg = (hh[:, 0:LANES] + hh[:, LANES:2 * LANES]
          + jnp.dot(h_lo, wr_ref[:, 0:LANES], preferred_element_type=F32)) + br_ref[...]

    lane = lax.broadcasted_iota(jnp.int32, (tm, LANES), 1).astype(F32)
    far = float(LANES)
    is_group = lane < N_EXPERT_GROUPS
    gl = jnp.where(is_group, lg, ROUTE_NEG)
    mg = jnp.max(gl, axis=-1, keepdims=True)
    gidx = jnp.min(jnp.where(gl == mg, lane, far), axis=-1, keepdims=True)
    pg_top = 1.0 / jnp.sum(jnp.where(is_group, jnp.exp(lg - mg), 0.0), axis=-1, keepdims=True)
    first = N_EXPERT_GROUPS + EXPERTS_PER_GROUP * gidx
    in_sel = jnp.logical_and(lane >= first, lane < first + EXPERTS_PER_GROUP)
    sel = jnp.where(in_sel, lg, ROUTE_NEG)
    m1 = jnp.max(sel, axis=-1, keepdims=True)
    i1 = jnp.min(jnp.where(jnp.logical_and(in_sel, sel == m1), lane, far), axis=-1, keepdims=True)
    keep = jnp.logical_and(in_sel, lane != i1)
    sel2 = jnp.where(keep, lg, ROUTE_NEG)
    m2 = jnp.max(sel2, axis=-1, keepdims=True)
    i2 = jnp.min(jnp.where(jnp.logical_and(keep, sel2 == m2), lane, far), axis=-1, keepdims=True)
    r = jnp.exp(m2 - m1)
    w1 = pg_top / (1.0 + r)
    w2 = w1 * r
    route = jnp.where(lane == 0, i1 - N_EXPERT_GROUPS,
                      jnp.where(lane == 1, i2 - N_EXPERT_GROUPS,
                                jnp.where(lane == 2, w1, jnp.where(lane == 3, w2, 0.0))))
    route_ref[...] = jnp.transpose(route)[0:8, :]

    @pl.when(jnp.logical_and(pl.program_id(0) == 0, i == 0))
    def _():
        count_ref[...] = jnp.zeros_like(count_ref)
    picked = jnp.where(jnp.logical_or(lane == i1, lane == i2), 1.0, 0.0)
    count_ref[...] += jnp.sum(picked, axis=0, keepdims=True)


def _merge(x, o_attn, u, cb, qm, km, vm, gates, w_conv, wpa, wpc, wpm, wout, g_ffn, w_r, b_r):
    b, s, d = x.shape
    tm = MERGE_TILE
    nt = s // tm
    halo = 16
    hb = tm // halo

    def tile(width):
        return pl.BlockSpec((1, tm, width), lambda bi, i: (bi, i, 0))

    def whole(arr):
        return pl.BlockSpec(arr.shape, lambda bi, i: (0,) * arr.ndim)

    def per_batch(arr):
        return pl.BlockSpec((1,) + arr.shape[1:], lambda bi, i: (bi, 0, 0))

    in_specs = [
        tile(d), tile(GROUP_W), tile(CONV_WIDTH),
        pl.BlockSpec((1, halo, CONV_WIDTH), lambda bi, i: (bi, jnp.maximum(i * hb - 1, 0), 0)),
        pl.BlockSpec((1, halo, CONV_WIDTH),
                     lambda bi, i: (bi, jnp.minimum((i + 1) * hb, s // halo - 1), 0)),
        tile(CONV_WIDTH), tile(MEM_WIDTH), per_batch(km), per_batch(vm), tile(N_BRANCH * d),
        whole(w_conv), whole(wpa), whole(wpc), whole(wpm), whole(wout), whole(g_ffn),
        whole(w_r), whole(b_r),
    ]
    return pl.pallas_call(
        functools.partial(_merge_kernel, tm=tm, nt=nt),
        grid=(b, nt),
        in_specs=in_specs,
        out_specs=[pl.BlockSpec((tm * ROW_CHUNKS, LANES), lambda bi, i: (bi * nt + i, 0)),
                   pl.BlockSpec((ROUTE_ROWS, tm), lambda bi, i: (0, bi * nt + i)),
                   pl.BlockSpec((ROUTE_ROWS, LANES), lambda bi, i: (0, 0))],
        out_shape=[jax.ShapeDtypeStruct((b * s * ROW_CHUNKS, LANES), F32),
                   jax.ShapeDtypeStruct((ROUTE_ROWS, b * s), F32),
                   jax.ShapeDtypeStruct((ROUTE_ROWS, LANES), F32)],
        scratch_shapes=[pltpu.VMEM((tm, MEM_WIDTH), BF16)],
        compiler_params=pltpu.CompilerParams(
            dimension_semantics=("arbitrary", "arbitrary"), vmem_limit_bytes=VMEM_LIMIT_BYTES),
        name="merge",
    )(x, o_attn, u, u, u, cb, qm, km, vm, gates, w_conv, wpa, wpc, wpm, wout, g_ffn, w_r, b_r)


DRAIN_STEPS = 2


def _expert_kernel(blk_exp_ref, nvalid_ref, niter_ref, tok_ref, tok_next_ref, tok_ahead_ref, dst_ref,
                   x1_hbm, gffn_ref, wg_ref, wu_ref, wd_ref, y_hbm, xbuf, hbuf, ybuf, wg_s, wu_s, wd_s,
                   gsem, ssem):
    i = pl.program_id(0)
    slot = lax.rem(i, 2)
    gslot = lax.rem(i, GATHER_SLOTS)
    rows = MOE_BLOCK * ROW_CHUNKS
    burst = GATHER_UNROLL * ROW_CHUNKS

    def start_gather(idx_ref, s, n_it):
        def body(it, c):
            for u in range(GATHER_UNROLL):
                j = it * GATHER_UNROLL + u
                src_row = pl.multiple_of(idx_ref[0, 0, j], ROW_CHUNKS)
                pltpu.make_async_copy(x1_hbm.at[pl.ds(src_row, ROW_CHUNKS), :],
                                      xbuf.at[s, pl.ds(j * ROW_CHUNKS, ROW_CHUNKS), :],
                                      gsem.at[s]).start(priority=u % 2)
            return c
        lax.fori_loop(0, n_it, body, 0)

    def wait_gather(s, n_it):
        moved = pl.ds(0, n_it * burst)
        pltpu.make_async_copy(x1_hbm.at[moved, :], xbuf.at[s, moved, :], gsem.at[s]).wait()

    def start_scatter(s, n_it):
        def body(it, c):
            for u in range(GATHER_UNROLL):
                j = it * GATHER_UNROLL + u
                dst_row = pl.multiple_of(dst_ref[0, 0, j], ROW_CHUNKS)
                pltpu.make_async_copy(ybuf.at[s, pl.ds(j * ROW_CHUNKS, ROW_CHUNKS), :],
                                      y_hbm.at[pl.ds(dst_row, ROW_CHUNKS), :],
                                      ssem.at[s]).start(priority=u % 2)
            return c
        lax.fori_loop(0, n_it, body, 0)

    def wait_scatter(s, n_it):
        moved = pl.ds(0, n_it * burst)
        pltpu.make_async_copy(ybuf.at[s, moved, :], y_hbm.at[moved, :], ssem.at[s]).wait()

    nvalid = nvalid_ref[i]
    prev = jnp.maximum(i - 1, 0)
    prev2 = jnp.maximum(i - 2, 0)

    @pl.when(i == 0)
    def _():
        xbuf[...] = jnp.zeros_like(xbuf)
        start_gather(tok_ref, 0, niter_ref[0])
        start_gather(tok_next_ref, 1, niter_ref[1])
        ybuf[0] = jnp.zeros((rows, LANES), F32)
        spare = pltpu.make_async_copy(ybuf.at[0], y_hbm.at[pl.ds(y_hbm.shape[0] - rows, rows), :],
                                      ssem.at[0])
        spare.start()
        spare.wait()

    @pl.when(jnp.logical_and(i >= 2, nvalid_ref[prev2] > 0))
    def _():
        wait_scatter(slot, niter_ref[prev2])

    @pl.when(jnp.logical_or(i == 0, blk_exp_ref[i] != blk_exp_ref[prev]))
    def _():
        wg_s[...] = wg_ref[0].astype(BF16)
        wu_s[...] = wu_ref[0].astype(BF16)
        wd_s[...] = wd_ref[0].astype(BF16)

    @pl.when(nvalid > 0)
    def _():
        wait_gather(gslot, niter_ref[i])
        start_gather(tok_ahead_ref, lax.rem(i + 2, GATHER_SLOTS), niter_ref[i + 2])

        ss = None
        for c in range(ROW_CHUNKS):
            xc = xbuf[gslot, pl.ds(c, MOE_BLOCK, stride=ROW_CHUNKS), :]
            ss = xc * xc if ss is None else ss + xc * xc
        scale = lax.rsqrt(jnp.sum(ss, axis=-1, keepdims=True) * (1.0 / D_MODEL) + EPS)
        for c in range(ROW_CHUNKS):
            sl = slice(c * LANES, (c + 1) * LANES)
            xc = xbuf[gslot, pl.ds(c, MOE_BLOCK, stride=ROW_CHUNKS), :]
            hbuf[:, sl] = (xc * scale * gffn_ref[:, sl]).astype(BF16)
        h2 = hbuf[...]
        a = jnp.dot(h2, wg_s[...], preferred_element_type=F32)
        up = jnp.dot(h2, wu_s[...], preferred_element_type=F32)
        act = (a * jax.nn.sigmoid(a) * up).astype(BF16)
        y = jnp.dot(act, wd_s[...], preferred_element_type=F32)
        for c in range(ROW_CHUNKS):
            ybuf[slot, pl.ds(c, MOE_BLOCK, stride=ROW_CHUNKS), :] = y[:, c * LANES:(c + 1) * LANES]
        start_scatter(slot, niter_ref[i])


def _experts(x1r, g_ffn, blk_exp, nvalid, niter, row_tok, row_dst, wg, wu, wd):
    rows_total, _ = x1r.shape
    d = D_MODEL
    n_blk = blk_exp.shape[0]
    idx_spec = lambda fn: pl.BlockSpec((1, 1, MOE_BLOCK), fn, memory_space=pltpu.SMEM)
    grid_spec = pltpu.PrefetchScalarGridSpec(
        num_scalar_prefetch=3,
        grid=(n_blk,),
        in_specs=[
            idx_spec(lambda i, be, nv, ni: (i, 0, 0)),
            idx_spec(lambda i, be, nv, ni: (jnp.minimum(i + 1, n_blk - 1), 0, 0)),
            idx_spec(lambda i, be, nv, ni: (jnp.minimum(i + 2, n_blk - 1), 0, 0)),
            idx_spec(lambda i, be, nv, ni: (i, 0, 0)),
            pl.BlockSpec(memory_space=pl.ANY),
            pl.BlockSpec((1, d), lambda i, be, nv, ni: (0, 0)),
            pl.BlockSpec((1, d, EXPERT_FF), lambda i, be, nv, ni: (be[i], 0, 0)),
            pl.BlockSpec((1, d, EXPERT_FF), lambda i, be, nv, ni: (be[i], 0, 0)),
            pl.BlockSpec((1, EXPERT_FF, d), lambda i, be, nv, ni: (be[i], 0, 0)),
        ],
        out_specs=pl.BlockSpec(memory_space=pl.ANY),
        scratch_shapes=[
            pltpu.VMEM((GATHER_SLOTS, MOE_BLOCK * ROW_CHUNKS, LANES), F32),
            pltpu.VMEM((MOE_BLOCK, d), BF16),
            pltpu.VMEM((2, MOE_BLOCK * ROW_CHUNKS, LANES), F32),
            pltpu.VMEM((d, EXPERT_FF), BF16),
            pltpu.VMEM((d, EXPERT_FF), BF16),
            pltpu.VMEM((EXPERT_FF, d), BF16),
            pltpu.SemaphoreType.DMA((GATHER_SLOTS,)),
            pltpu.SemaphoreType.DMA((2,)),
        ],
    )
    return pl.pallas_call(
        _expert_kernel,
        grid_spec=grid_spec,
        out_shape=jax.ShapeDtypeStruct((TOP_K * rows_total + MOE_BLOCK * ROW_CHUNKS, LANES), F32),
        compiler_params=pltpu.CompilerParams(
            dimension_semantics=("arbitrary",), vmem_limit_bytes=VMEM_LIMIT_BYTES),
        name="experts",
    )(blk_exp, nvalid, niter, row_tok, row_tok, row_tok, row_dst, x1r, g_ffn, wg, wu, wd)


def _combine_kernel(x1_ref, y0_ref, y1_ref, route_ref, o_ref, *, tm):
    w_cols = jnp.transpose(route_ref[...])
    w0 = w_cols[:, 2:3]
    w1 = w_cols[:, 3:4]
    for c in range(ROW_CHUNKS):
        rows = pl.ds(c, tm, stride=ROW_CHUNKS)
        o_ref[:, c * LANES:(c + 1) * LANES] = (x1_ref[rows, :] + w0 * y0_ref[rows, :]
                                               + w1 * y1_ref[rows, :])


def _combine(x1r, y, route_t):
    n = x1r.shape[0] // ROW_CHUNKS
    tm = ROW_TILE
    nt = n // tm
    blk = (tm * ROW_CHUNKS, LANES)
    return pl.pallas_call(
        functools.partial(_combine_kernel, tm=tm),
        grid=(nt,),
        in_specs=[pl.BlockSpec(blk, lambda i: (i, 0)),
                  pl.BlockSpec(blk, lambda i: (i, 0)),
                  pl.BlockSpec(blk, lambda i: (i + nt, 0)),
                  pl.BlockSpec((ROUTE_ROWS, tm), lambda i: (0, i))],
        out_specs=pl.BlockSpec((tm, D_MODEL), lambda i: (i, 0)),
        out_shape=jax.ShapeDtypeStruct((n, D_MODEL), F32),
        compiler_params=pltpu.CompilerParams(dimension_semantics=("parallel",)),
        name="combine",
    )(x1r, y, y, route_t)


def _rope_tables(positions):
    half = ROT_DIM // 2
    lane = np.arange(LANES)
    dim = lane % HEAD_DIM
    rot = dim < ROT_DIM
    spread_cos = np.zeros((half, LANES), np.float32)
    spread_cos[dim[rot] % half, lane[rot]] = 1.0
    spread_sin = np.zeros((half, LANES), np.float32)
    spread_sin[dim[rot] % half, lane[rot]] = np.where(dim[rot] < half, -1.0, 1.0)
    inv = ROPE_THETA ** (-jnp.arange(0, ROT_DIM, 2, dtype=F32) / ROT_DIM)
    ang = inv[None, :, None] * positions.astype(F32)[:, None, :]
    spread = functools.partial(jnp.einsum, 'bjs,jl->bsl', precision=lax.Precision.HIGHEST)
    cos_t = spread(jnp.cos(ang), jnp.asarray(spread_cos)) + jnp.asarray((~rot).astype(np.float32))
    sin_t = spread(jnp.sin(ang), jnp.asarray(spread_sin))
    return cos_t, sin_t


def _route(route_t, counts, n):
    nk = n * TOP_K
    e = route_t[0:TOP_K].reshape(nk).astype(jnp.int32)
    a = jnp.arange(nk, dtype=jnp.int32)
    _, a_s = lax.sort((e, a), num_keys=1, is_stable=True)
    padded = ((counts + MOE_BLOCK - 1) // MOE_BLOCK) * MOE_BLOCK
    pend = jnp.cumsum(padded)
    pstart = pend - padded
    start = jnp.cumsum(counts) - counts
    n_blk = nk // MOE_BLOCK + N_EXPERTS + DRAIN_STEPS
    blk_start = jnp.arange(n_blk, dtype=jnp.int32) * MOE_BLOCK
    blk_exp = jnp.minimum(jnp.sum((pend[None, :] <= blk_start[:, None]).astype(jnp.int32), axis=1),
                          N_EXPERTS - 1).astype(jnp.int32)
    onehot = blk_exp[:, None] == jnp.arange(N_EXPERTS, dtype=jnp.int32)[None, :]
    pick = lambda table: jnp.sum(jnp.where(onehot, table[None, :], 0), axis=1)
    offset = blk_start - pick(pstart)
    nvalid = jnp.clip(pick(counts) - offset, 0, MOE_BLOCK)
    nvalid = jnp.where(blk_start < pend[-1], nvalid, 0).astype(jnp.int32)
    j = jnp.arange(MOE_BLOCK, dtype=jnp.int32)[None, :]
    valid = j < nvalid[:, None]
    src = jnp.clip((pick(start) + offset)[:, None] + j, 0, nk - 1)
    a_p = a_s[src]
    tok = (jnp.where(valid, a_p - n * (a_p >= n).astype(jnp.int32), 0) * ROW_CHUNKS).astype(jnp.int32)
    dst = (jnp.where(valid, a_p, TOP_K * n + j) * ROW_CHUNKS).astype(jnp.int32)
    niter = (nvalid + GATHER_UNROLL - 1) // GATHER_UNROLL
    return blk_exp, nvalid, niter, tok.reshape(n_blk, 1, MOE_BLOCK), dst.reshape(n_blk, 1, MOE_BLOCK)


def kernel(x, mem, positions, g_mix, g_mem, w_in, g_qn_attn, g_kn_attn, w_conv, w_mem_kv, g_qn_mem,
           g_kn_mem, w_proj_attn, w_proj_conv, w_proj_mem, w_out, g_ffn, w_router_group,
           b_router_group, w_router_expert, b_router_expert, w_gate, w_up, w_down):
    b, s, d = x.shape
    n = b * s
    cos_t, sin_t = _rope_tables(positions)
    for l in range(w_in.shape[0]):
        wl = w_in[l]
        aw = N_BRANCH * GROUP_W
        cols = []
        for gi in range(len(DILATIONS)):
            for part in range(3):
                c0 = part * aw + gi * GROUP_W
                cols.append(wl[:, c0:c0 + GROUP_W])
        cols.append(wl[:, 3 * aw:])
        w_all = jnp.concatenate(cols, axis=1).astype(BF16)

        q0, q1, q2, u, cb, qm, gates = _inproj(x, g_mix[l][None], g_qn_mem[l][None], w_all)
        km, vm = _memkv(mem, g_mem[l][None], g_kn_mem[l][None], w_mem_kv[l].astype(BF16))
        gq = jnp.tile(g_qn_attn[l], 2)[None]
        gk = jnp.tile(g_kn_attn[l], 2)[None]
        o_attn = _attention((q0, q1, q2), cos_t, sin_t, gq, gk)

        w_r = jnp.zeros((d, LANES), F32)
        w_r = w_r.at[:, :N_EXPERT_GROUPS].set(w_router_group[l])
        w_r = w_r.at[:, N_EXPERT_GROUPS:N_EXPERT_GROUPS + N_EXPERTS].set(w_router_expert[l])
        b_r = jnp.zeros((1, LANES), F32)
        b_r = b_r.at[0, :N_EXPERT_GROUPS].set(b_router_group[l])
        b_r = b_r.at[0, N_EXPERT_GROUPS:N_EXPERT_GROUPS + N_EXPERTS].set(b_router_expert[l])
        w_r_hi = w_r.astype(BF16)
        w_r_lo = (w_r - w_r_hi.astype(F32)).astype(BF16)
        x1, route_t, count_t = _merge(x, o_attn, u, cb, qm, km, vm, gates, w_conv[l],
                                      w_proj_attn[l].astype(BF16), w_proj_conv[l].astype(BF16),
                                      w_proj_mem[l].astype(BF16), w_out[l].astype(BF16),
                                      g_ffn[l][None], jnp.concatenate([w_r_hi, w_r_lo], axis=1), b_r)

        counts = count_t[0, N_EXPERT_GROUPS:N_EXPERT_GROUPS + N_EXPERTS].astype(jnp.int32)
        blk_exp, nvalid, niter, row_tok, row_dst = _route(route_t, counts, n)
        y = _experts(x1, g_ffn[l][None], blk_exp, nvalid, niter, row_tok, row_dst,
                     w_gate[l], w_up[l], w_down[l])
        x = _combine(x1, y, route_t).reshape(b, s, d)
    return x
```

```python
import functools

import numpy as np
import jax
import jax.numpy as jnp
from jax import lax
from jax.experimental import pallas as pl
from jax.experimental.pallas import tpu as pltpu

F32 = jnp.float32
BF16 = jnp.bfloat16

D_MODEL = 1024
EPS = 1e-6
HEAD_DIM = 64
DILATIONS = (1, 4, 16)
BAND_HALF = 64
ATTN_SLOTS = 4
GROUP_W = ATTN_SLOTS * HEAD_DIM
QKV_W = 3 * GROUP_W
ROT_DIM = 16
ROPE_THETA = 500000.0
CONV_WIDTH = 768
MEM_HEADS = 4
MEM_HEAD_DIM = 128
MEM_WIDTH = 512
N_BRANCH = 3
N_EXPERT_GROUPS = 4
EXPERTS_PER_GROUP = 8
N_EXPERTS = 32
TOP_K = 2
EXPERT_FF = 512

LANES = 128
VMEM_LIMIT_BYTES = 56 * 1024 * 1024

ROW_TILE = 512
MERGE_TILE = 1024
Q_BLOCK = 128
MOE_BLOCK = 256
ROW_CHUNKS = D_MODEL // LANES
GATHER_UNROLL = 16
GATHER_SLOTS = 3
NEG_BIG = -1e30
ROUTE_NEG = -3e38
ROUTE_ROWS = 8


def _rms(t, gain):
    return t * lax.rsqrt(jnp.mean(t * t, axis=-1, keepdims=True) + EPS) * gain


def _inproj_kernel(x_ref, g_ref, gqm_ref, w_ref,
                   qkv0_ref, qkv1_ref, qkv2_ref, u_ref, cb_ref, qm_ref, gates_ref,
                   h_scr, hp_scr, *, tm):
    x = x_ref[0]
    h = _rms(x, g_ref[...])
    n_chunks = h.shape[1] // LANES
    for c in range(n_chunks):
        h_scr[c] = h[:, c * LANES:(c + 1) * LANES]
    hb = h.astype(BF16)

    def mm(lhs, c0, c1):
        return jnp.dot(lhs, w_ref[:, c0:c1], preferred_element_type=F32)

    qkv0_ref[0, 0] = mm(hb, 0, QKV_W).astype(BF16)
    for gi, out_ref in ((1, qkv1_ref), (2, qkv2_ref)):
        d = DILATIONS[gi]
        rows = tm // d
        for r in range(d):
            for c in range(n_chunks):
                hp_scr[r * rows:(r + 1) * rows, c * LANES:(c + 1) * LANES] = (
                    h_scr[c, pl.ds(r, rows, stride=d), :].astype(BF16))
        res = mm(hp_scr[...], gi * QKV_W, (gi + 1) * QKV_W)
        out_ref[0] = res.reshape(d, rows, QKV_W).astype(BF16)

    c = 3 * QKV_W
    cx = mm(hb, c, c + CONV_WIDTH)
    cc = mm(hb, c + 2 * CONV_WIDTH, c + 3 * CONV_WIDTH)
    u_ref[0] = (cc * cx).astype(BF16)
    cb_ref[0] = mm(hb, c + CONV_WIDTH, c + 2 * CONV_WIDTH).astype(BF16)

    c += 3 * CONV_WIDTH
    qm = mm(hb, c, c + MEM_WIDTH)
    for hh in range(MEM_HEADS):
        sl = slice(hh * MEM_HEAD_DIM, (hh + 1) * MEM_HEAD_DIM)
        qm_ref[0, :, sl] = _rms(qm[:, sl], gqm_ref[...]).astype(BF16)

    c += MEM_WIDTH
    for j in range(N_BRANCH):
        gl = mm(hb, c + j * D_MODEL, c + (j + 1) * D_MODEL)
        gates_ref[0, :, j * D_MODEL:(j + 1) * D_MODEL] = jax.nn.sigmoid(gl).astype(BF16)


def _inproj(x, g_mix, g_qn_mem, w_all):
    b, s, d = x.shape
    tm = ROW_TILE
    nt = s // tm
    outs = []
    out_specs = []
    for dil in DILATIONS:
        outs.append(jax.ShapeDtypeStruct((b, dil, s // dil, QKV_W), BF16))
        out_specs.append(pl.BlockSpec((1, dil, tm // dil, QKV_W), lambda bi, i: (bi, 0, i, 0)))
    for width in (CONV_WIDTH, CONV_WIDTH, MEM_WIDTH, N_BRANCH * D_MODEL):
        outs.append(jax.ShapeDtypeStruct((b, s, width), BF16))
        out_specs.append(pl.BlockSpec((1, tm, width), lambda bi, i: (bi, i, 0)))
    return pl.pallas_call(
        functools.partial(_inproj_kernel, tm=tm),
        grid=(b, nt),
        in_specs=[
            pl.BlockSpec((1, tm, d), lambda bi, i: (bi, i, 0)),
            pl.BlockSpec((1, d), lambda bi, i: (0, 0)),
            pl.BlockSpec((1, MEM_HEAD_DIM), lambda bi, i: (0, 0)),
            pl.BlockSpec(w_all.shape, lambda bi, i: (0, 0), pipeline_mode=pl.Buffered(1)),
        ],
        out_specs=out_specs,
        out_shape=outs,
        scratch_shapes=[pltpu.VMEM((d // LANES, tm, LANES), F32), pltpu.VMEM((tm, d), BF16)],
        compiler_params=pltpu.CompilerParams(
            dimension_semantics=("parallel", "parallel"), vmem_limit_bytes=VMEM_LIMIT_BYTES),
        name="inproj",
    )(x, g_mix, g_qn_mem, w_all)


def _memkv_kernel(mem_ref, g_ref, gk_ref, w_ref, k_ref, v_ref):
    h = _rms(mem_ref[0], g_ref[...]).astype(BF16)
    kv = jnp.dot(h, w_ref[...], preferred_element_type=F32)
    for hh in range(MEM_HEADS):
        sl = slice(hh * MEM_HEAD_DIM, (hh + 1) * MEM_HEAD_DIM)
        k_ref[0, :, sl] = _rms(kv[:, sl], gk_ref[...]).astype(BF16)
    v_ref[0] = kv[:, MEM_WIDTH:].astype(BF16)


def _memkv(mem, g_mem, g_kn_mem, w_kv):
    b, m, d = mem.shape
    return pl.pallas_call(
        _memkv_kernel,
        grid=(b,),
        in_specs=[
            pl.BlockSpec((1, m, d), lambda bi: (bi, 0, 0)),
            pl.BlockSpec((1, d), lambda bi: (0, 0)),
            pl.BlockSpec((1, MEM_HEAD_DIM), lambda bi: (0, 0)),
            pl.BlockSpec(w_kv.shape, lambda bi: (0, 0)),
        ],
        out_specs=[pl.BlockSpec((1, m, MEM_WIDTH), lambda bi: (bi, 0, 0))] * 2,
        out_shape=[jax.ShapeDtypeStruct((b, m, MEM_WIDTH), BF16)] * 2,
        compiler_params=pltpu.CompilerParams(dimension_semantics=("parallel",)),
        name="memkv",
    )(mem, g_mem, g_kn_mem, w_kv)


ATTN_WIN = Q_BLOCK + 2 * BAND_HALF
ATTN_UNROLL = 16
DEN_SAFE = 1e-30


def _attn_consts():
    lane = np.arange(LANES)
    gsum = (lane[:, None] // HEAD_DIM == lane[None, :] // HEAD_DIM).astype(np.float32)
    half = ROT_DIM // 2
    dim = lane % HEAD_DIM
    src = np.where(dim < half, lane + half, np.where(dim < ROT_DIM, lane - half, -1))
    pswap = (lane[:, None] == src[None, :]).astype(np.float32)
    i = np.arange(Q_BLOCK)[:, None]
    c = np.arange(ATTN_WIN)[None, :]
    band = (c >= i) & (c <= i + 2 * BAND_HALF)
    first = c >= BAND_HALF
    last = c < Q_BLOCK + BAND_HALF
    variants = [band, band & first, band & last, band & first & last]
    bias = np.stack([np.where(v, 0.0, NEG_BIG) for v in variants]).astype(np.float32)
    return jnp.asarray(gsum, BF16), jnp.asarray(pswap, BF16), jnp.asarray(bias, F32)


def _attn_kernel(q0, k0, v0, q1, k1, v1, q2, k2, v2, cos_ref, sin_ref, gq_ref, gk_ref,
                 gsum_ref, pswap_ref, bias_ref, out_ref, qs, ks, vs, o_scr, l_scr, bias_scr, *, seq):
    lane = lax.broadcasted_iota(jnp.int32, (1, LANES), 1)
    low_half = lane < HEAD_DIM
    pad = BAND_HALF
    n_blocks = seq // Q_BLOCK

    bound = HEAD_DIM ** 0.5 * jnp.max(jnp.abs(gq_ref[...])) * jnp.max(jnp.abs(gk_ref[...]))
    shifted = bias_ref[...] - bound
    bias_scr[:, 0:Q_BLOCK, :] = shifted
    bias_scr[:, Q_BLOCK:2 * Q_BLOCK, :] = shifted

    def norm_rope(t, gain, cos, sin):
        ms = jnp.dot((t * t).astype(BF16), gsum_ref[...], preferred_element_type=F32) * (1.0 / HEAD_DIM)
        tn = t * lax.rsqrt(ms + EPS) * gain
        partner = jnp.dot(tn.astype(BF16), pswap_ref[...], preferred_element_type=F32)
        return tn * cos + partner * sin

    for gi, (q_ref, k_ref, v_ref) in enumerate(((q0, k0, v0), (q1, k1, v1), (q2, k2, v2))):
        d = DILATIONS[gi]
        length = seq // d
        nblk = length // Q_BLOCK
        region = length + pad

        zpad = jnp.zeros((pad, LANES), BF16)
        for r in range(d + 1):
            ks[r * region:r * region + pad, :] = zpad
            vs[r * region:r * region + pad, :] = zpad

        def split_index(b, nblk=nblk):
            if nblk == 1:
                return b, 0
            r = b // nblk
            return r, b - r * nblk

        def table_rows(r, n, d=d):
            if d == 1:
                return pl.ds(pl.multiple_of(n * Q_BLOCK, Q_BLOCK), Q_BLOCK)
            return pl.ds(r + d * n * Q_BLOCK, Q_BLOCK, stride=d)

        def prep(b, q_ref=q_ref, k_ref=k_ref, v_ref=v_ref, split_index=split_index,
                 table_rows=table_rows):
            r, n = split_index(b)
            src = pl.ds(pl.multiple_of(n * Q_BLOCK, Q_BLOCK), Q_BLOCK)
            rows = table_rows(r, n)
            cos = cos_ref[0, rows, :]
            sin = sin_ref[0, rows, :]
            qn = norm_rope(q_ref[0, r, src, :].astype(F32), gq_ref[...], cos, sin) * (HEAD_DIM ** -0.5)
            kn = norm_rope(k_ref[0, r, src, :].astype(F32), gk_ref[...], cos, sin)
            qbase = pl.multiple_of(b * 2 * Q_BLOCK, 2 * Q_BLOCK)
            qs[pl.ds(qbase, Q_BLOCK), :] = jnp.where(low_half, qn, 0.0).astype(BF16)
            qs[pl.ds(qbase + Q_BLOCK, Q_BLOCK), :] = jnp.where(low_half, 0.0, qn).astype(BF16)
            kdst = pl.ds(pl.multiple_of(b * Q_BLOCK + (r + 1) * pad, pad), Q_BLOCK)
            ks[kdst, :] = kn.astype(BF16)
            vs[kdst, :] = v_ref[0, r, src, :]

        def block(b, exact, gi=gi, nblk=nblk, split_index=split_index, table_rows=table_rows):
            r, n = split_index(b)
            variant = 3 if nblk == 1 else jnp.where(n == 0, 1, 0) + jnp.where(n == nblk - 1, 2, 0)
            qsrc = pl.ds(pl.multiple_of(b * 2 * Q_BLOCK, 2 * Q_BLOCK), 2 * Q_BLOCK)
            ksrc = pl.ds(pl.multiple_of(b * Q_BLOCK + r * pad, pad), ATTN_WIN)
            s = lax.dot_general(qs[qsrc, :], ks[ksrc, :], (((1,), (1,)), ((), ())),
                                preferred_element_type=F32) + bias_scr[variant]
            if exact:
                m = jnp.max(s, axis=-1, keepdims=True)
                s = s - m
                shift = jnp.where(low_half, m[0:Q_BLOCK], m[Q_BLOCK:]) + bound
            else:
                shift = bound
            p = jnp.exp(s)
            den = jnp.sum(p, axis=-1, keepdims=True)
            o2 = jnp.dot(p.astype(BF16), vs[ksrc, :], preferred_element_type=F32)
            o = jnp.where(low_half, o2[0:Q_BLOCK], o2[Q_BLOCK:])
            den = jnp.where(low_half, den[0:Q_BLOCK], den[Q_BLOCK:])
            dst = table_rows(r, n)
            o_scr[gi, dst, :] = o / den
            l_scr[gi, dst, :] = shift + jnp.log(den)
            return den

        def unrolled(fn):
            def body(it, carry):
                for u in range(ATTN_UNROLL):
                    fn(it * ATTN_UNROLL + u)
                return carry
            lax.fori_loop(0, n_blocks // ATTN_UNROLL, body, 0)

        unrolled(prep)

        def fast_body(it, dmin, block=block):
            for u in range(ATTN_UNROLL):
                dmin = jnp.minimum(dmin, block(it * ATTN_UNROLL + u, False))
            return dmin
        dmin = lax.fori_loop(0, n_blocks // ATTN_UNROLL, fast_body,
                             jnp.full((Q_BLOCK, LANES), 1.0, F32))

        @pl.when(jnp.logical_not(jnp.min(dmin) > DEN_SAFE))
        def _(block=block, unrolled=unrolled):
            unrolled(lambda b: block(b, True))

    l0 = l_scr[0]
    l1 = l_scr[1]
    l2 = l_scr[2]
    m = jnp.maximum(jnp.maximum(l0, l1), l2)
    e0 = jnp.exp(l0 - m)
    e1 = jnp.exp(l1 - m)
    e2 = jnp.exp(l2 - m)
    mix = (e0 * o_scr[0] + e1 * o_scr[1] + e2 * o_scr[2]) / (e0 + e1 + e2)
    out_ref[0] = mix.astype(BF16)


def _attention(qkv, cos_t, sin_t, gq, gk):
    b, _, s, _ = qkv[0].shape
    in_specs = []
    args = []
    for gi, dil in enumerate(DILATIONS):
        for part in range(3):
            in_specs.append(pl.BlockSpec((1, dil, s // dil, LANES),
                                         lambda bi, hp, part=part: (bi, 0, 0, 2 * part + hp)))
            args.append(qkv[gi])
    consts = _attn_consts()
    tab_spec = pl.BlockSpec((1, s, LANES), lambda bi, hp: (bi, 0, 0))
    gain_spec = pl.BlockSpec((1, LANES), lambda bi, hp: (0, 0))
    const_specs = [pl.BlockSpec(c.shape, lambda bi, hp, nd=c.ndim: (0,) * nd) for c in consts]
    key_rows = max(d * (s // d + BAND_HALF) for d in DILATIONS) + BAND_HALF
    return pl.pallas_call(
        functools.partial(_attn_kernel, seq=s),
        grid=(b, 2),
        in_specs=in_specs + [tab_spec] * 2 + [gain_spec] * 2 + const_specs,
        out_specs=pl.BlockSpec((1, s, LANES), lambda bi, hp: (bi, 0, hp)),
        out_shape=jax.ShapeDtypeStruct((b, s, GROUP_W), BF16),
        scratch_shapes=[
            pltpu.VMEM((2 * s, LANES), BF16),
            pltpu.VMEM((key_rows, LANES), BF16),
            pltpu.VMEM((key_rows, LANES), BF16),
            pltpu.VMEM((3, s, LANES), F32),
            pltpu.VMEM((3, s, LANES), F32),
            pltpu.VMEM((4, 2 * Q_BLOCK, ATTN_WIN), F32),
        ],
        compiler_params=pltpu.CompilerParams(
            dimension_semantics=("parallel", "parallel"), vmem_limit_bytes=VMEM_LIMIT_BYTES),
        name="dilated_attn",
    )(*args, cos_t, sin_t, gq, gk, *consts)


def _merge_kernel(x_ref, oa_ref, u_ref, uprev_ref, unext_ref, cb_ref, qm_ref, km_ref, vm_ref,
                  gates_ref, wconv_ref, wpa_ref, wpc_ref, wpm_ref, wout_ref, gffn_ref, wr_ref, br_ref,
                  x1_ref, route_ref, count_ref, om_scr, *, tm, nt):
    i = pl.program_id(1)
    u = u_ref[0].astype(F32)
    prev_row = uprev_ref[0].astype(F32)[15:16, :] * (i > 0).astype(F32)
    next_row = unext_ref[0].astype(F32)[0:1, :] * (i < nt - 1).astype(F32)
    row = lax.broadcasted_iota(jnp.int32, (tm, 1), 0)
    um = jnp.where(row == 0, prev_row, pltpu.roll(u, 1, 0))
    up = jnp.where(row == tm - 1, next_row, pltpu.roll(u, tm - 1, 0))
    wc = wconv_ref[...]
    y = wc[0:1, :] * um + wc[1:2, :] * u + wc[2:3, :] * up
    z = (cb_ref[0].astype(F32) * y).astype(BF16)

    for hh in range(MEM_HEADS):
        sl = slice(hh * MEM_HEAD_DIM, (hh + 1) * MEM_HEAD_DIM)
        s = lax.dot_general(qm_ref[0, :, sl], km_ref[0, :, sl], (((1,), (1,)), ((), ())),
                            preferred_element_type=F32) * (MEM_HEAD_DIM ** -0.5)
        m = jnp.max(s, axis=-1, keepdims=True)
        p = jnp.exp(s - m)
        den = jnp.sum(p, axis=-1, keepdims=True)
        o = jnp.dot(p.astype(BF16), vm_ref[0, :, sl], preferred_element_type=F32) / den
        om_scr[:, sl] = o.astype(BF16)

    pa = jnp.dot(oa_ref[0], wpa_ref[...], preferred_element_type=F32)
    pc = jnp.dot(z, wpc_ref[...], preferred_element_type=F32)
    pm = jnp.dot(om_scr[...], wpm_ref[...], preferred_element_type=F32)
    merged = (gates_ref[0, :, 0:D_MODEL].astype(F32) * pa
              + gates_ref[0, :, D_MODEL:2 * D_MODEL].astype(F32) * pc
              + gates_ref[0, :, 2 * D_MODEL:3 * D_MODEL].astype(F32) * pm)
    x1 = x_ref[0] + jnp.dot(merged.astype(BF16), wout_ref[...], preferred_element_type=F32)
    for c in range(ROW_CHUNKS):
        x1_ref[pl.ds(c, tm, stride=ROW_CHUNKS), :] = x1[:, c * LANES:(c + 1) * LANES]

    h2 = _rms(x1, gffn_ref[...])
    h_hi = h2.astype(BF16)
    h_lo = (h2 - h_hi.astype(F32)).astype(BF16)
    hh = jnp.dot(h_hi, wr_ref[...], preferred_element_type=F32)
    lg = (hh[:, 0:LANES] + hh[:, LANES:2 * LANES]
          + jnp.dot(h_lo, wr_ref[:, 0:LANES], preferred_element_type=F32)) + br_ref[...]

    lane = lax.broadcasted_iota(jnp.int32, (tm, LANES), 1).astype(F32)
    far = float(LANES)
    is_group = lane < N_EXPERT_GROUPS
    gl = jnp.where(is_group, lg, ROUTE_NEG)
    mg = jnp.max(gl, axis=-1, keepdims=True)
    gidx = jnp.min(jnp.where(gl == mg, lane, far), axis=-1, keepdims=True)
    pg_top = 1.0 / jnp.sum(jnp.where(is_group, jnp.exp(lg - mg), 0.0), axis=-1, keepdims=True)
    first = N_EXPERT_GROUPS + EXPERTS_PER_GROUP * gidx
    in_sel = jnp.logical_and(lane >= first, lane < first + EXPERTS_PER_GROUP)
    sel = jnp.where(in_sel, lg, ROUTE_NEG)
    m1 = jnp.max(sel, axis=-1, keepdims=True)
    i1 = jnp.min(jnp.where(jnp.logical_and(in_sel, sel == m1), lane, far), axis=-1, keepdims=True)
    keep = jnp.logical_and(in_sel, lane != i1)
    sel2 = jnp.where(keep, lg, ROUTE_NEG)
    m2 = jnp.max(sel2, axis=-1, keepdims=True)
    i2 = jnp.min(jnp.where(jnp.logical_and(keep, sel2 == m2), lane, far), axis=-1, keepdims=True)
    r = jnp.exp(m2 - m1)
    w1 = pg_top / (1.0 + r)
    w2 = w1 * r
    route = jnp.where(lane == 0, i1 - N_EXPERT_GROUPS,
                      jnp.where(lane == 1, i2 - N_EXPERT_GROUPS,
                                jnp.where(lane == 2, w1, jnp.where(lane == 3, w2, 0.0))))
    route_ref[...] = jnp.transpose(route)[0:8, :]

    @pl.when(jnp.logical_and(pl.program_id(0) == 0, i == 0))
    def _():
        count_ref[...] = jnp.zeros_like(count_ref)
    picked = jnp.where(jnp.logical_or(lane == i1, lane == i2), 1.0, 0.0)
    count_ref[...] += jnp.sum(picked, axis=0, keepdims=True)


def _merge(x, o_attn, u, cb, qm, km, vm, gates, w_conv, wpa, wpc, wpm, wout, g_ffn, w_r, b_r):
    b, s, d = x.shape
    tm = MERGE_TILE
    nt = s // tm
    halo = 16
    hb = tm // halo

    def tile(width):
        return pl.BlockSpec((1, tm, width), lambda bi, i: (bi, i, 0))

    def whole(arr):
        return pl.BlockSpec(arr.shape, lambda bi, i: (0,) * arr.ndim)

    def per_batch(arr):
        return pl.BlockSpec((1,) + arr.shape[1:], lambda bi, i: (bi, 0, 0))

    in_specs = [
        tile(d), tile(GROUP_W), tile(CONV_WIDTH),
        pl.BlockSpec((1, halo, CONV_WIDTH), lambda bi, i: (bi, jnp.maximum(i * hb - 1, 0), 0)),
        pl.BlockSpec((1, halo, CONV_WIDTH),
                     lambda bi, i: (bi, jnp.minimum((i + 1) * hb, s // halo - 1), 0)),
        tile(CONV_WIDTH), tile(MEM_WIDTH), per_batch(km), per_batch(vm), tile(N_BRANCH * d),
        whole(w_conv), whole(wpa), whole(wpc), whole(wpm), whole(wout), whole(g_ffn),
        whole(w_r), whole(b_r),
    ]
    return pl.pallas_call(
        functools.partial(_merge_kernel, tm=tm, nt=nt),
        grid=(b, nt),
        in_specs=in_specs,
        out_specs=[pl.BlockSpec((tm * ROW_CHUNKS, LANES), lambda bi, i: (bi * nt + i, 0)),
                   pl.BlockSpec((ROUTE_ROWS, tm), lambda bi, i: (0, bi * nt + i)),
                   pl.BlockSpec((ROUTE_ROWS, LANES), lambda bi, i: (0, 0))],
        out_shape=[jax.ShapeDtypeStruct((b * s * ROW_CHUNKS, LANES), F32),
                   jax.ShapeDtypeStruct((ROUTE_ROWS, b * s), F32),
                   jax.ShapeDtypeStruct((ROUTE_ROWS, LANES), F32)],
        scratch_shapes=[pltpu.VMEM((tm, MEM_WIDTH), BF16)],
        compiler_params=pltpu.CompilerParams(
            dimension_semantics=("arbitrary", "arbitrary"), vmem_limit_bytes=VMEM_LIMIT_BYTES),
        name="merge",
    )(x, o_attn, u, u, u, cb, qm, km, vm, gates, w_conv, wpa, wpc, wpm, wout, g_ffn, w_r, b_r)


DRAIN_STEPS = 2


def _expert_kernel(blk_exp_ref, nvalid_ref, tok_ref, tok_next_ref, tok_ahead_ref, dst_prev_ref, x1_hbm,
                   gffn_ref, wg_ref, wu_ref, wd_ref, y_hbm, xbuf, hbuf, ybuf, wg_s, wu_s, wd_s,
                   gsem, ssem):
    i = pl.program_id(0)
    slot = lax.rem(i, 2)
    gslot = lax.rem(i, GATHER_SLOTS)
    rows = MOE_BLOCK * ROW_CHUNKS

    def gather_copy(idx_ref, j, s, prio):
        src_row = pl.multiple_of(idx_ref[0, 0, j], ROW_CHUNKS)
        return pltpu.make_async_copy(x1_hbm.at[pl.ds(src_row, ROW_CHUNKS), :],
                                     xbuf.at[s, pl.ds(j * ROW_CHUNKS, ROW_CHUNKS), :], gsem.at[s]
                                     ).start(priority=prio)

    def scatter_copy(j, s, prio):
        dst_row = pl.multiple_of(dst_prev_ref[0, 0, j], ROW_CHUNKS)
        return pltpu.make_async_copy(ybuf.at[s, pl.ds(j * ROW_CHUNKS, ROW_CHUNKS), :],
                                     y_hbm.at[pl.ds(dst_row, ROW_CHUNKS), :], ssem.at[s]
                                     ).start(priority=prio)

    def looped(issue):
        def body(it, c):
            for u in range(GATHER_UNROLL):
                issue(it * GATHER_UNROLL + u, u % 2)
            return c
        lax.fori_loop(0, MOE_BLOCK // GATHER_UNROLL, body, 0)

    def wait_gather(s):
        pltpu.make_async_copy(x1_hbm.at[pl.ds(0, rows), :], xbuf.at[s], gsem.at[s]).wait()

    def wait_scatter(s):
        pltpu.make_async_copy(ybuf.at[s], y_hbm.at[pl.ds(0, rows), :], ssem.at[s]).wait()

    nvalid = nvalid_ref[i]
    prev = jnp.maximum(i - 1, 0)
    prev2 = jnp.maximum(i - 2, 0)
    prev_active = jnp.logical_and(i >= 1, nvalid_ref[prev] > 0)

    @pl.when(i == 0)
    def _():
        looped(lambda j, prio: gather_copy(tok_ref, j, 0, prio))
        looped(lambda j, prio: gather_copy(tok_next_ref, j, 1, prio))
        ybuf[...] = jnp.zeros_like(ybuf)
        spare = pltpu.make_async_copy(ybuf.at[0], y_hbm.at[pl.ds(y_hbm.shape[0] - rows, rows), :],
                                      ssem.at[0])
        spare.start()
        spare.wait()

    @pl.when(prev_active)
    def _():
        wait_scatter(slot)

    @pl.when(jnp.logical_or(i == 0, blk_exp_ref[i] != blk_exp_ref[prev]))
    def _():
        wg_s[...] = wg_ref[0].astype(BF16)
        wu_s[...] = wu_ref[0].astype(BF16)
        wd_s[...] = wd_ref[0].astype(BF16)

    @pl.when(nvalid > 0)
    def _():
        wait_gather(gslot)
        ss = None
        for c in range(ROW_CHUNKS):
            xc = xbuf[gslot, pl.ds(c, MOE_BLOCK, stride=ROW_CHUNKS), :]
            ss = xc * xc if ss is None else ss + xc * xc
        scale = lax.rsqrt(jnp.sum(ss, axis=-1, keepdims=True) * (1.0 / D_MODEL) + EPS)
        for c in range(ROW_CHUNKS):
            sl = slice(c * LANES, (c + 1) * LANES)
            xc = xbuf[gslot, pl.ds(c, MOE_BLOCK, stride=ROW_CHUNKS), :]
            hbuf[:, sl] = (xc * scale * gffn_ref[:, sl]).astype(BF16)
        h2 = hbuf[...]
        a = jnp.dot(h2, wg_s[...], preferred_element_type=F32)
        up = jnp.dot(h2, wu_s[...], preferred_element_type=F32)
        act = (a * jax.nn.sigmoid(a) * up).astype(BF16)
        ahead = lax.rem(i + 2, GATHER_SLOTS)
        for j in range(MOE_BLOCK):
            gather_copy(tok_ahead_ref, j, ahead, j % 2)
        for j in range(MOE_BLOCK):
            scatter_copy(j, 1 - slot, j % 2)
        y = jnp.dot(act, wd_s[...], preferred_element_type=F32)
        for c in range(ROW_CHUNKS):
            ybuf[slot, pl.ds(c, MOE_BLOCK, stride=ROW_CHUNKS), :] = y[:, c * LANES:(c + 1) * LANES]

    @pl.when(jnp.logical_and(nvalid == 0, prev_active))
    def _():
        looped(lambda j, prio: scatter_copy(j, 1 - slot, prio))
        wait_scatter(1 - slot)

    @pl.when(jnp.logical_and(nvalid == 0, jnp.logical_or(i < 2, nvalid_ref[prev2] > 0)))
    def _():
        wait_gather(gslot)


def _experts(x1r, g_ffn, blk_exp, nvalid, row_tok, row_dst_prev, wg, wu, wd):
    rows_total, _ = x1r.shape
    d = D_MODEL
    n_blk = blk_exp.shape[0]
    idx_spec = lambda fn: pl.BlockSpec((1, 1, MOE_BLOCK), fn, memory_space=pltpu.SMEM)
    grid_spec = pltpu.PrefetchScalarGridSpec(
        num_scalar_prefetch=2,
        grid=(n_blk,),
        in_specs=[
            idx_spec(lambda i, be, nv: (i, 0, 0)),
            idx_spec(lambda i, be, nv: (jnp.minimum(i + 1, n_blk - 1), 0, 0)),
            idx_spec(lambda i, be, nv: (jnp.minimum(i + 2, n_blk - 1), 0, 0)),
            idx_spec(lambda i, be, nv: (i, 0, 0)),
            pl.BlockSpec(memory_space=pl.ANY),
            pl.BlockSpec((1, d), lambda i, be, nv: (0, 0)),
            pl.BlockSpec((1, d, EXPERT_FF), lambda i, be, nv: (be[i], 0, 0)),
            pl.BlockSpec((1, d, EXPERT_FF), lambda i, be, nv: (be[i], 0, 0)),
            pl.BlockSpec((1, EXPERT_FF, d), lambda i, be, nv: (be[i], 0, 0)),
        ],
        out_specs=pl.BlockSpec(memory_space=pl.ANY),
        scratch_shapes=[
            pltpu.VMEM((GATHER_SLOTS, MOE_BLOCK * ROW_CHUNKS, LANES), F32),
            pltpu.VMEM((MOE_BLOCK, d), BF16),
            pltpu.VMEM((2, MOE_BLOCK * ROW_CHUNKS, LANES), F32),
            pltpu.VMEM((d, EXPERT_FF), BF16),
            pltpu.VMEM((d, EXPERT_FF), BF16),
            pltpu.VMEM((EXPERT_FF, d), BF16),
            pltpu.SemaphoreType.DMA((GATHER_SLOTS,)),
            pltpu.SemaphoreType.DMA((2,)),
        ],
    )
    return pl.pallas_call(
        _expert_kernel,
        grid_spec=grid_spec,
        out_shape=jax.ShapeDtypeStruct((TOP_K * rows_total + MOE_BLOCK * ROW_CHUNKS, LANES), F32),
        compiler_params=pltpu.CompilerParams(
            dimension_semantics=("arbitrary",), vmem_limit_bytes=VMEM_LIMIT_BYTES),
        name="experts",
    )(blk_exp, nvalid, row_tok, row_tok, row_tok, row_dst_prev, x1r, g_ffn, wg, wu, wd)


def _combine_kernel(x1_ref, y0_ref, y1_ref, route_ref, o_ref, *, tm):
    w_cols = jnp.transpose(route_ref[...])
    w0 = w_cols[:, 2:3]
    w1 = w_cols[:, 3:4]
    for c in range(ROW_CHUNKS):
        rows = pl.ds(c, tm, stride=ROW_CHUNKS)
        o_ref[:, c * LANES:(c + 1) * LANES] = (x1_ref[rows, :] + w0 * y0_ref[rows, :]
                                               + w1 * y1_ref[rows, :])


def _combine(x1r, y, route_t):
    n = x1r.shape[0] // ROW_CHUNKS
    tm = ROW_TILE
    nt = n // tm
    blk = (tm * ROW_CHUNKS, LANES)
    return pl.pallas_call(
        functools.partial(_combine_kernel, tm=tm),
        grid=(nt,),
        in_specs=[pl.BlockSpec(blk, lambda i: (i, 0)),
                  pl.BlockSpec(blk, lambda i: (i, 0)),
                  pl.BlockSpec(blk, lambda i: (i + nt, 0)),
                  pl.BlockSpec((ROUTE_ROWS, tm), lambda i: (0, i))],
        out_specs=pl.BlockSpec((tm, D_MODEL), lambda i: (i, 0)),
        out_shape=jax.ShapeDtypeStruct((n, D_MODEL), F32),
        compiler_params=pltpu.CompilerParams(dimension_semantics=("parallel",)),
        name="combine",
    )(x1r, y, y, route_t)


def _rope_tables(positions):
    half = ROT_DIM // 2
    lane = np.arange(LANES)
    dim = lane % HEAD_DIM
    rot = dim < ROT_DIM
    spread_cos = np.zeros((half, LANES), np.float32)
    spread_cos[dim[rot] % half, lane[rot]] = 1.0
    spread_sin = np.zeros((half, LANES), np.float32)
    spread_sin[dim[rot] % half, lane[rot]] = np.where(dim[rot] < half, -1.0, 1.0)
    inv = ROPE_THETA ** (-jnp.arange(0, ROT_DIM, 2, dtype=F32) / ROT_DIM)
    ang = inv[None, :, None] * positions.astype(F32)[:, None, :]
    spread = functools.partial(jnp.einsum, 'bjs,jl->bsl', precision=lax.Precision.HIGHEST)
    cos_t = spread(jnp.cos(ang), jnp.asarray(spread_cos)) + jnp.asarray((~rot).astype(np.float32))
    sin_t = spread(jnp.sin(ang), jnp.asarray(spread_sin))
    return cos_t, sin_t


def _route(route_t, counts, n):
    nk = n * TOP_K
    e = route_t[0:TOP_K].reshape(nk).astype(jnp.int32)
    a = jnp.arange(nk, dtype=jnp.int32)
    _, a_s = lax.sort((e, a), num_keys=1, is_stable=True)
    padded = ((counts + MOE_BLOCK - 1) // MOE_BLOCK) * MOE_BLOCK
    pend = jnp.cumsum(padded)
    pstart = pend - padded
    start = jnp.cumsum(counts) - counts
    n_blk = nk // MOE_BLOCK + N_EXPERTS + DRAIN_STEPS
    blk_start = jnp.arange(n_blk, dtype=jnp.int32) * MOE_BLOCK
    blk_exp = jnp.minimum(jnp.sum((pend[None, :] <= blk_start[:, None]).astype(jnp.int32), axis=1),
                          N_EXPERTS - 1).astype(jnp.int32)
    onehot = blk_exp[:, None] == jnp.arange(N_EXPERTS, dtype=jnp.int32)[None, :]
    pick = lambda table: jnp.sum(jnp.where(onehot, table[None, :], 0), axis=1)
    offset = blk_start - pick(pstart)
    nvalid = jnp.clip(pick(counts) - offset, 0, MOE_BLOCK)
    nvalid = jnp.where(blk_start < pend[-1], nvalid, 0).astype(jnp.int32)
    j = jnp.arange(MOE_BLOCK, dtype=jnp.int32)[None, :]
    valid = j < nvalid[:, None]
    src = jnp.clip((pick(start) + offset)[:, None] + j, 0, nk - 1)
    a_p = a_s[src]
    tok = (jnp.where(valid, a_p - n * (a_p >= n).astype(jnp.int32), j) * ROW_CHUNKS).astype(jnp.int32)
    dst = (jnp.where(valid, a_p, TOP_K * n + j) * ROW_CHUNKS).astype(jnp.int32)
    spare = jnp.broadcast_to((TOP_K * n + j) * ROW_CHUNKS, (1, MOE_BLOCK)).astype(jnp.int32)
    dst_prev = jnp.concatenate([spare, dst[:-1]], axis=0)
    return blk_exp, nvalid, tok.reshape(n_blk, 1, MOE_BLOCK), dst_prev.reshape(n_blk, 1, MOE_BLOCK)


def kernel(x, mem, positions, g_mix, g_mem, w_in, g_qn_attn, g_kn_attn, w_conv, w_mem_kv, g_qn_mem,
           g_kn_mem, w_proj_attn, w_proj_conv, w_proj_mem, w_out, g_ffn, w_router_group,
           b_router_group, w_router_expert, b_router_expert, w_gate, w_up, w_down):
    b, s, d = x.shape
    n = b * s
    cos_t, sin_t = _rope_tables(positions)
    for l in range(w_in.shape[0]):
        wl = w_in[l]
        aw = N_BRANCH * GROUP_W
        cols = []
        for gi in range(len(DILATIONS)):
            for part in range(3):
                c0 = part * aw + gi * GROUP_W
                cols.append(wl[:, c0:c0 + GROUP_W])
        cols.append(wl[:, 3 * aw:])
        w_all = jnp.concatenate(cols, axis=1).astype(BF16)

        q0, q1, q2, u, cb, qm, gates = _inproj(x, g_mix[l][None], g_qn_mem[l][None], w_all)
        km, vm = _memkv(mem, g_mem[l][None], g_kn_mem[l][None], w_mem_kv[l].astype(BF16))
        gq = jnp.tile(g_qn_attn[l], 2)[None]
        gk = jnp.tile(g_kn_attn[l], 2)[None]
        o_attn = _attention((q0, q1, q2), cos_t, sin_t, gq, gk)

        w_r = jnp.zeros((d, LANES), F32)
        w_r = w_r.at[:, :N_EXPERT_GROUPS].set(w_router_group[l])
        w_r = w_r.at[:, N_EXPERT_GROUPS:N_EXPERT_GROUPS + N_EXPERTS].set(w_router_expert[l])
        b_r = jnp.zeros((1, LANES), F32)
        b_r = b_r.at[0, :N_EXPERT_GROUPS].set(b_router_group[l])
        b_r = b_r.at[0, N_EXPERT_GROUPS:N_EXPERT_GROUPS + N_EXPERTS].set(b_router_expert[l])
        w_r_hi = w_r.astype(BF16)
        w_r_lo = (w_r - w_r_hi.astype(F32)).astype(BF16)
        x1, route_t, count_t = _merge(x, o_attn, u, cb, qm, km, vm, gates, w_conv[l],
                                      w_proj_attn[l].astype(BF16), w_proj_conv[l].astype(BF16),
                                      w_proj_mem[l].astype(BF16), w_out[l].astype(BF16),
                                      g_ffn[l][None], jnp.concatenate([w_r_hi, w_r_lo], axis=1), b_r)

        counts = count_t[0, N_EXPERT_GROUPS:N_EXPERT_GROUPS + N_EXPERTS].astype(jnp.int32)
        blk_exp, nvalid, row_tok, row_dst_prev = _route(route_t, counts, n)
        y = _experts(x1, g_ffn[l][None], blk_exp, nvalid, row_tok, row_dst_prev,
                     w_gate[l], w_up[l], w_down[l])
        x = _combine(x1, y, route_t).reshape(b, s, d)
    return x
```

```python
import functools

import numpy as np
import jax
import jax.numpy as jnp
from jax import lax
from jax.experimental import pallas as pl
from jax.experimental.pallas import tpu as pltpu

F32 = jnp.float32
BF16 = jnp.bfloat16

D_MODEL = 1024
EPS = 1e-6
HEAD_DIM = 64
DILATIONS = (1, 4, 16)
BAND_HALF = 64
ATTN_SLOTS = 4
GROUP_W = ATTN_SLOTS * HEAD_DIM
QKV_W = 3 * GROUP_W
ROT_DIM = 16
ROPE_THETA = 500000.0
CONV_WIDTH = 768
MEM_HEADS = 4
MEM_HEAD_DIM = 128
MEM_WIDTH = 512
N_BRANCH = 3
N_EXPERT_GROUPS = 4
EXPERTS_PER_GROUP = 8
N_EXPERTS = 32
TOP_K = 2
EXPERT_FF = 512

LANES = 128
VMEM_LIMIT_BYTES = 56 * 1024 * 1024

ROW_TILE = 512
MERGE_TILE = 1024
Q_BLOCK = 128
MOE_BLOCK = 256
ROW_CHUNKS = D_MODEL // LANES
GATHER_UNROLL = 16
GATHER_SLOTS = 3
NEG_BIG = -1e30
ROUTE_NEG = -3e38
ROUTE_ROWS = 8


def _rms(t, gain):
    return t * lax.rsqrt(jnp.mean(t * t, axis=-1, keepdims=True) + EPS) * gain


def _inproj_kernel(x_ref, g_ref, gqm_ref, w_ref,
                   qkv0_ref, qkv1_ref, qkv2_ref, u_ref, cb_ref, qm_ref, gates_ref,
                   h_scr, hp_scr, *, tm):
    x = x_ref[0]
    h = _rms(x, g_ref[...])
    n_chunks = h.shape[1] // LANES
    for c in range(n_chunks):
        h_scr[c] = h[:, c * LANES:(c + 1) * LANES]
    hb = h.astype(BF16)

    def mm(lhs, c0, c1):
        return jnp.dot(lhs, w_ref[:, c0:c1], preferred_element_type=F32)

    qkv0_ref[0, 0] = mm(hb, 0, QKV_W).astype(BF16)
    for gi, out_ref in ((1, qkv1_ref), (2, qkv2_ref)):
        d = DILATIONS[gi]
        rows = tm // d
        for r in range(d):
            for c in range(n_chunks):
                hp_scr[r * rows:(r + 1) * rows, c * LANES:(c + 1) * LANES] = (
                    h_scr[c, pl.ds(r, rows, stride=d), :].astype(BF16))
        res = mm(hp_scr[...], gi * QKV_W, (gi + 1) * QKV_W)
        out_ref[0] = res.reshape(d, rows, QKV_W).astype(BF16)

    c = 3 * QKV_W
    cx = mm(hb, c, c + CONV_WIDTH)
    cc = mm(hb, c + 2 * CONV_WIDTH, c + 3 * CONV_WIDTH)
    u_ref[0] = (cc * cx).astype(BF16)
    cb_ref[0] = mm(hb, c + CONV_WIDTH, c + 2 * CONV_WIDTH).astype(BF16)

    c += 3 * CONV_WIDTH
    qm = mm(hb, c, c + MEM_WIDTH)
    for hh in range(MEM_HEADS):
        sl = slice(hh * MEM_HEAD_DIM, (hh + 1) * MEM_HEAD_DIM)
        qm_ref[0, :, sl] = _rms(qm[:, sl], gqm_ref[...]).astype(BF16)

    c += MEM_WIDTH
    for j in range(N_BRANCH):
        gl = mm(hb, c + j * D_MODEL, c + (j + 1) * D_MODEL)
        gates_ref[0, :, j * D_MODEL:(j + 1) * D_MODEL] = jax.nn.sigmoid(gl).astype(BF16)


def _inproj(x, g_mix, g_qn_mem, w_all):
    b, s, d = x.shape
    tm = ROW_TILE
    nt = s // tm
    outs = []
    out_specs = []
    for dil in DILATIONS:
        outs.append(jax.ShapeDtypeStruct((b, dil, s // dil, QKV_W), BF16))
        out_specs.append(pl.BlockSpec((1, dil, tm // dil, QKV_W), lambda bi, i: (bi, 0, i, 0)))
    for width in (CONV_WIDTH, CONV_WIDTH, MEM_WIDTH, N_BRANCH * D_MODEL):
        outs.append(jax.ShapeDtypeStruct((b, s, width), BF16))
        out_specs.append(pl.BlockSpec((1, tm, width), lambda bi, i: (bi, i, 0)))
    return pl.pallas_call(
        functools.partial(_inproj_kernel, tm=tm),
        grid=(b, nt),
        in_specs=[
            pl.BlockSpec((1, tm, d), lambda bi, i: (bi, i, 0)),
            pl.BlockSpec((1, d), lambda bi, i: (0, 0)),
            pl.BlockSpec((1, MEM_HEAD_DIM), lambda bi, i: (0, 0)),
            pl.BlockSpec(w_all.shape, lambda bi, i: (0, 0), pipeline_mode=pl.Buffered(1)),
        ],
        out_specs=out_specs,
        out_shape=outs,
        scratch_shapes=[pltpu.VMEM((d // LANES, tm, LANES), F32), pltpu.VMEM((tm, d), BF16)],
        compiler_params=pltpu.CompilerParams(
            dimension_semantics=("parallel", "parallel"), vmem_limit_bytes=VMEM_LIMIT_BYTES),
        name="inproj",
    )(x, g_mix, g_qn_mem, w_all)


def _memkv_kernel(mem_ref, g_ref, gk_ref, w_ref, k_ref, v_ref):
    h = _rms(mem_ref[0], g_ref[...]).astype(BF16)
    kv = jnp.dot(h, w_ref[...], preferred_element_type=F32)
    for hh in range(MEM_HEADS):
        sl = slice(hh * MEM_HEAD_DIM, (hh + 1) * MEM_HEAD_DIM)
        k_ref[0, :, sl] = _rms(kv[:, sl], gk_ref[...]).astype(BF16)
    v_ref[0] = kv[:, MEM_WIDTH:].astype(BF16)


def _memkv(mem, g_mem, g_kn_mem, w_kv):
    b, m, d = mem.shape
    return pl.pallas_call(
        _memkv_kernel,
        grid=(b,),
        in_specs=[
            pl.BlockSpec((1, m, d), lambda bi: (bi, 0, 0)),
            pl.BlockSpec((1, d), lambda bi: (0, 0)),
            pl.BlockSpec((1, MEM_HEAD_DIM), lambda bi: (0, 0)),
            pl.BlockSpec(w_kv.shape, lambda bi: (0, 0)),
        ],
        out_specs=[pl.BlockSpec((1, m, MEM_WIDTH), lambda bi: (bi, 0, 0))] * 2,
        out_shape=[jax.ShapeDtypeStruct((b, m, MEM_WIDTH), BF16)] * 2,
        compiler_params=pltpu.CompilerParams(dimension_semantics=("parallel",)),
        name="memkv",
    )(mem, g_mem, g_kn_mem, w_kv)


ATTN_WIN = Q_BLOCK + 2 * BAND_HALF
ATTN_UNROLL = 16
DEN_SAFE = 1e-30


def _attn_consts():
    lane = np.arange(LANES)
    gsum = (lane[:, None] // HEAD_DIM == lane[None, :] // HEAD_DIM).astype(np.float32)
    half = ROT_DIM // 2
    dim = lane % HEAD_DIM
    src = np.where(dim < half, lane + half, np.where(dim < ROT_DIM, lane - half, -1))
    pswap = (lane[:, None] == src[None, :]).astype(np.float32)
    i = np.arange(Q_BLOCK)[:, None]
    c = np.arange(ATTN_WIN)[None, :]
    band = (c >= i) & (c <= i + 2 * BAND_HALF)
    first = c >= BAND_HALF
    last = c < Q_BLOCK + BAND_HALF
    variants = [band, band & first, band & last, band & first & last]
    bias = np.stack([np.where(v, 0.0, NEG_BIG) for v in variants]).astype(np.float32)
    return jnp.asarray(gsum, BF16), jnp.asarray(pswap, BF16), jnp.asarray(bias, F32)


def _attn_kernel(q0, k0, v0, q1, k1, v1, q2, k2, v2, cos_ref, sin_ref, gq_ref, gk_ref,
                 gsum_ref, pswap_ref, bias_ref, out_ref, qs, ks, vs, o_scr, l_scr, bias_scr, *, seq):
    lane = lax.broadcasted_iota(jnp.int32, (1, LANES), 1)
    low_half = lane < HEAD_DIM
    pad = BAND_HALF
    n_blocks = seq // Q_BLOCK

    bound = HEAD_DIM ** 0.5 * jnp.max(jnp.abs(gq_ref[...])) * jnp.max(jnp.abs(gk_ref[...]))
    shifted = bias_ref[...] - bound
    bias_scr[:, 0:Q_BLOCK, :] = shifted
    bias_scr[:, Q_BLOCK:2 * Q_BLOCK, :] = shifted

    def norm_rope(t, gain, cos, sin):
        ms = jnp.dot((t * t).astype(BF16), gsum_ref[...], preferred_element_type=F32) * (1.0 / HEAD_DIM)
        tn = t * lax.rsqrt(ms + EPS) * gain
        partner = jnp.dot(tn.astype(BF16), pswap_ref[...], preferred_element_type=F32)
        return tn * cos + partner * sin

    for gi, (q_ref, k_ref, v_ref) in enumerate(((q0, k0, v0), (q1, k1, v1), (q2, k2, v2))):
        d = DILATIONS[gi]
        length = seq // d
        nblk = length // Q_BLOCK
        region = length + pad

        zpad = jnp.zeros((pad, LANES), BF16)
        for r in range(d + 1):
            ks[r * region:r * region + pad, :] = zpad
            vs[r * region:r * region + pad, :] = zpad

        def split_index(b, nblk=nblk):
            if nblk == 1:
                return b, 0
            r = b // nblk
            return r, b - r * nblk

        def table_rows(r, n, d=d):
            if d == 1:
                return pl.ds(pl.multiple_of(n * Q_BLOCK, Q_BLOCK), Q_BLOCK)
            return pl.ds(r + d * n * Q_BLOCK, Q_BLOCK, stride=d)

        def prep(b, q_ref=q_ref, k_ref=k_ref, v_ref=v_ref, split_index=split_index,
                 table_rows=table_rows):
            r, n = split_index(b)
            src = pl.ds(pl.multiple_of(n * Q_BLOCK, Q_BLOCK), Q_BLOCK)
            rows = table_rows(r, n)
            cos = cos_ref[0, rows, :]
            sin = sin_ref[0, rows, :]
            qn = norm_rope(q_ref[0, r, src, :].astype(F32), gq_ref[...], cos, sin) * (HEAD_DIM ** -0.5)
            kn = norm_rope(k_ref[0, r, src, :].astype(F32), gk_ref[...], cos, sin)
            qbase = pl.multiple_of(b * 2 * Q_BLOCK, 2 * Q_BLOCK)
            qs[pl.ds(qbase, Q_BLOCK), :] = jnp.where(low_half, qn, 0.0).astype(BF16)
            qs[pl.ds(qbase + Q_BLOCK, Q_BLOCK), :] = jnp.where(low_half, 0.0, qn).astype(BF16)
            kdst = pl.ds(pl.multiple_of(b * Q_BLOCK + (r + 1) * pad, pad), Q_BLOCK)
            ks[kdst, :] = kn.astype(BF16)
            vs[kdst, :] = v_ref[0, r, src, :]

        def block(b, exact, gi=gi, nblk=nblk, split_index=split_index, table_rows=table_rows):
            r, n = split_index(b)
            variant = 3 if nblk == 1 else jnp.where(n == 0, 1, 0) + jnp.where(n == nblk - 1, 2, 0)
            qsrc = pl.ds(pl.multiple_of(b * 2 * Q_BLOCK, 2 * Q_BLOCK), 2 * Q_BLOCK)
            ksrc = pl.ds(pl.multiple_of(b * Q_BLOCK + r * pad, pad), ATTN_WIN)
            s = lax.dot_general(qs[qsrc, :], ks[ksrc, :], (((1,), (1,)), ((), ())),
                                preferred_element_type=F32) + bias_scr[variant]
            if exact:
                m = jnp.max(s, axis=-1, keepdims=True)
                s = s - m
                shift = jnp.where(low_half, m[0:Q_BLOCK], m[Q_BLOCK:]) + bound
            else:
                shift = bound
            p = jnp.exp(s)
            den = jnp.sum(p, axis=-1, keepdims=True)
            o2 = jnp.dot(p.astype(BF16), vs[ksrc, :], preferred_element_type=F32)
            o = jnp.where(low_half, o2[0:Q_BLOCK], o2[Q_BLOCK:])
            den = jnp.where(low_half, den[0:Q_BLOCK], den[Q_BLOCK:])
            dst = table_rows(r, n)
            o_scr[gi, dst, :] = o / den
            l_scr[gi, dst, :] = shift + jnp.log(den)
            return den

        def unrolled(fn):
            def body(it, carry):
                for u in range(ATTN_UNROLL):
                    fn(it * ATTN_UNROLL + u)
                return carry
            lax.fori_loop(0, n_blocks // ATTN_UNROLL, body, 0)

        unrolled(prep)

        def fast_body(it, dmin, block=block):
            for u in range(ATTN_UNROLL):
                dmin = jnp.minimum(dmin, block(it * ATTN_UNROLL + u, False))
            return dmin
        dmin = lax.fori_loop(0, n_blocks // ATTN_UNROLL, fast_body,
                             jnp.full((Q_BLOCK, LANES), 1.0, F32))

        @pl.when(jnp.logical_not(jnp.min(dmin) > DEN_SAFE))
        def _(block=block, unrolled=unrolled):
            unrolled(lambda b: block(b, True))

    l0 = l_scr[0]
    l1 = l_scr[1]
    l2 = l_scr[2]
    m = jnp.maximum(jnp.maximum(l0, l1), l2)
    e0 = jnp.exp(l0 - m)
    e1 = jnp.exp(l1 - m)
    e2 = jnp.exp(l2 - m)
    mix = (e0 * o_scr[0] + e1 * o_scr[1] + e2 * o_scr[2]) / (e0 + e1 + e2)
    out_ref[0] = mix.astype(BF16)


def _attention(qkv, cos_t, sin_t, gq, gk):
    b, _, s, _ = qkv[0].shape
    in_specs = []
    args = []
    for gi, dil in enumerate(DILATIONS):
        for part in range(3):
            in_specs.append(pl.BlockSpec((1, dil, s // dil, LANES),
                                         lambda bi, hp, part=part: (bi, 0, 0, 2 * part + hp)))
            args.append(qkv[gi])
    consts = _attn_consts()
    tab_spec = pl.BlockSpec((1, s, LANES), lambda bi, hp: (bi, 0, 0))
    gain_spec = pl.BlockSpec((1, LANES), lambda bi, hp: (0, 0))
    const_specs = [pl.BlockSpec(c.shape, lambda bi, hp, nd=c.ndim: (0,) * nd) for c in consts]
    key_rows = max(d * (s // d + BAND_HALF) for d in DILATIONS) + BAND_HALF
    return pl.pallas_call(
        functools.partial(_attn_kernel, seq=s),
        grid=(b, 2),
        in_specs=in_specs + [tab_spec] * 2 + [gain_spec] * 2 + const_specs,
        out_specs=pl.BlockSpec((1, s, LANES), lambda bi, hp: (bi, 0, hp)),
        out_shape=jax.ShapeDtypeStruct((b, s, GROUP_W), BF16),
        scratch_shapes=[
            pltpu.VMEM((2 * s, LANES), BF16),
            pltpu.VMEM((key_rows, LANES), BF16),
            pltpu.VMEM((key_rows, LANES), BF16),
            pltpu.VMEM((3, s, LANES), F32),
            pltpu.VMEM((3, s, LANES), F32),
            pltpu.VMEM((4, 2 * Q_BLOCK, ATTN_WIN), F32),
        ],
        compiler_params=pltpu.CompilerParams(
            dimension_semantics=("parallel", "parallel"), vmem_limit_bytes=VMEM_LIMIT_BYTES),
        name="dilated_attn",
    )(*args, cos_t, sin_t, gq, gk, *consts)


def _merge_kernel(x_ref, oa_ref, u_ref, uprev_ref, unext_ref, cb_ref, qm_ref, km_ref, vm_ref,
                  gates_ref, wconv_ref, wpa_ref, wpc_ref, wpm_ref, wout_ref, gffn_ref, wr_ref, br_ref,
                  x1_ref, route_ref, count_ref, om_scr, *, tm, nt):
    i = pl.program_id(1)
    u = u_ref[0].astype(F32)
    prev_row = uprev_ref[0].astype(F32)[15:16, :] * (i > 0).astype(F32)
    next_row = unext_ref[0].astype(F32)[0:1, :] * (i < nt - 1).astype(F32)
    row = lax.broadcasted_iota(jnp.int32, (tm, 1), 0)
    um = jnp.where(row == 0, prev_row, pltpu.roll(u, 1, 0))
    up = jnp.where(row == tm - 1, next_row, pltpu.roll(u, tm - 1, 0))
    wc = wconv_ref[...]
    y = wc[0:1, :] * um + wc[1:2, :] * u + wc[2:3, :] * up
    z = (cb_ref[0].astype(F32) * y).astype(BF16)

    for hh in range(MEM_HEADS):
        sl = slice(hh * MEM_HEAD_DIM, (hh + 1) * MEM_HEAD_DIM)
        s = lax.dot_general(qm_ref[0, :, sl], km_ref[0, :, sl], (((1,), (1,)), ((), ())),
                            preferred_element_type=F32) * (MEM_HEAD_DIM ** -0.5)
        m = jnp.max(s, axis=-1, keepdims=True)
        p = jnp.exp(s - m)
        den = jnp.sum(p, axis=-1, keepdims=True)
        o = jnp.dot(p.astype(BF16), vm_ref[0, :, sl], preferred_element_type=F32) / den
        om_scr[:, sl] = o.astype(BF16)

    pa = jnp.dot(oa_ref[0], wpa_ref[...], preferred_element_type=F32)
    pc = jnp.dot(z, wpc_ref[...], preferred_element_type=F32)
    pm = jnp.dot(om_scr[...], wpm_ref[...], preferred_element_type=F32)
    merged = (gates_ref[0, :, 0:D_MODEL].astype(F32) * pa
              + gates_ref[0, :, D_MODEL:2 * D_MODEL].astype(F32) * pc
              + gates_ref[0, :, 2 * D_MODEL:3 * D_MODEL].astype(F32) * pm)
    x1 = x_ref[0] + jnp.dot(merged.astype(BF16), wout_ref[...], preferred_element_type=F32)
    for c in range(ROW_CHUNKS):
        x1_ref[pl.ds(c, tm, stride=ROW_CHUNKS), :] = x1[:, c * LANES:(c + 1) * LANES]

    h2 = _rms(x1, gffn_ref[...])
    h_hi = h2.astype(BF16)
    h_lo = (h2 - h_hi.astype(F32)).astype(BF16)
    hh = jnp.dot(h_hi, wr_ref[...], preferred_element_type=F32)
    lg = (hh[:, 0:LANES] + hh[:, LANES:2 * LANES]
          + jnp.dot(h_lo, wr_ref[:, 0:LANES], preferred_element_type=F32)) + br_ref[...]

    lane = lax.broadcasted_iota(jnp.int32, (tm, LANES), 1).astype(F32)
    far = float(LANES)
    is_group = lane < N_EXPERT_GROUPS
    gl = jnp.where(is_group, lg, ROUTE_NEG)
    mg = jnp.max(gl, axis=-1, keepdims=True)
    gidx = jnp.min(jnp.where(gl == mg, lane, far), axis=-1, keepdims=True)
    pg_top = 1.0 / jnp.sum(jnp.where(is_group, jnp.exp(lg - mg), 0.0), axis=-1, keepdims=True)
    first = N_EXPERT_GROUPS + EXPERTS_PER_GROUP * gidx
    in_sel = jnp.logical_and(lane >= first, lane < first + EXPERTS_PER_GROUP)
    sel = jnp.where(in_sel, lg, ROUTE_NEG)
    m1 = jnp.max(sel, axis=-1, keepdims=True)
    i1 = jnp.min(jnp.where(jnp.logical_and(in_sel, sel == m1), lane, far), axis=-1, keepdims=True)
    keep = jnp.logical_and(in_sel, lane != i1)
    sel2 = jnp.where(keep, lg, ROUTE_NEG)
    m2 = jnp.max(sel2, axis=-1, keepdims=True)
    i2 = jnp.min(jnp.where(jnp.logical_and(keep, sel2 == m2), lane, far), axis=-1, keepdims=True)
    r = jnp.exp(m2 - m1)
    w1 = pg_top / (1.0 + r)
    w2 = w1 * r
    route = jnp.where(lane == 0, i1 - N_EXPERT_GROUPS,
                      jnp.where(lane == 1, i2 - N_EXPERT_GROUPS,
                                jnp.where(lane == 2, w1, jnp.where(lane == 3, w2, 0.0))))
    route_ref[...] = jnp.transpose(route)[0:8, :]

    @pl.when(jnp.logical_and(pl.program_id(0) == 0, i == 0))
    def _():
        count_ref[...] = jnp.zeros_like(count_ref)
    picked = jnp.where(jnp.logical_or(lane == i1, lane == i2), 1.0, 0.0)
    count_ref[...] += jnp.sum(picked, axis=0, keepdims=True)


def _merge(x, o_attn, u, cb, qm, km, vm, gates, w_conv, wpa, wpc, wpm, wout, g_ffn, w_r, b_r):
    b, s, d = x.shape
    tm = MERGE_TILE
    nt = s // tm
    halo = 16
    hb = tm // halo

    def tile(width):
        return pl.BlockSpec((1, tm, width), lambda bi, i: (bi, i, 0))

    def whole(arr):
        return pl.BlockSpec(arr.shape, lambda bi, i: (0,) * arr.ndim)

    def per_batch(arr):
        return pl.BlockSpec((1,) + arr.shape[1:], lambda bi, i: (bi, 0, 0))

    in_specs = [
        tile(d), tile(GROUP_W), tile(CONV_WIDTH),
        pl.BlockSpec((1, halo, CONV_WIDTH), lambda bi, i: (bi, jnp.maximum(i * hb - 1, 0), 0)),
        pl.BlockSpec((1, halo, CONV_WIDTH),
                     lambda bi, i: (bi, jnp.minimum((i + 1) * hb, s // halo - 1), 0)),
        tile(CONV_WIDTH), tile(MEM_WIDTH), per_batch(km), per_batch(vm), tile(N_BRANCH * d),
        whole(w_conv), whole(wpa), whole(wpc), whole(wpm), whole(wout), whole(g_ffn),
        whole(w_r), whole(b_r),
    ]
    return pl.pallas_call(
        functools.partial(_merge_kernel, tm=tm, nt=nt),
        grid=(b, nt),
        in_specs=in_specs,
        out_specs=[pl.BlockSpec((tm * ROW_CHUNKS, LANES), lambda bi, i: (bi * nt + i, 0)),
                   pl.BlockSpec((ROUTE_ROWS, tm), lambda bi, i: (0, bi * nt + i)),
                   pl.BlockSpec((ROUTE_ROWS, LANES), lambda bi, i: (0, 0))],
        out_shape=[jax.ShapeDtypeStruct((b * s * ROW_CHUNKS, LANES), F32),
                   jax.ShapeDtypeStruct((ROUTE_ROWS, b * s), F32),
                   jax.ShapeDtypeStruct((ROUTE_ROWS, LANES), F32)],
        scratch_shapes=[pltpu.VMEM((tm, MEM_WIDTH), BF16)],
        compiler_params=pltpu.CompilerParams(
            dimension_semantics=("arbitrary", "arbitrary"), vmem_limit_bytes=VMEM_LIMIT_BYTES),
        name="merge",
    )(x, o_attn, u, u, u, cb, qm, km, vm, gates, w_conv, wpa, wpc, wpm, wout, g_ffn, w_r, b_r)


DRAIN_STEPS = 2


def _expert_kernel(blk_exp_ref, nvalid_ref, tok_first_ref, tok_ahead_ref, dst_prev_ref, x1_hbm,
                   gffn_ref, wg_ref, wu_ref, wd_ref, y_hbm, xbuf, hbuf, ybuf, wg_s, wu_s, wd_s,
                   gsem, ssem):
    i = pl.program_id(0)
    slot = lax.rem(i, 2)
    gslot = lax.rem(i, GATHER_SLOTS)
    rows = MOE_BLOCK * ROW_CHUNKS

    def gather_copy(idx_ref, b, j, s, prio):
        src_row = pl.multiple_of(idx_ref[b, 0, j], ROW_CHUNKS)
        return pltpu.make_async_copy(x1_hbm.at[pl.ds(src_row, ROW_CHUNKS), :],
                                     xbuf.at[s, pl.ds(j * ROW_CHUNKS, ROW_CHUNKS), :], gsem.at[s]
                                     ).start(priority=prio)

    def scatter_copy(j, s, prio):
        dst_row = pl.multiple_of(dst_prev_ref[0, 0, j], ROW_CHUNKS)
        return pltpu.make_async_copy(ybuf.at[s, pl.ds(j * ROW_CHUNKS, ROW_CHUNKS), :],
                                     y_hbm.at[pl.ds(dst_row, ROW_CHUNKS), :], ssem.at[s]
                                     ).start(priority=prio)

    def looped(issue):
        def body(it, c):
            for u in range(GATHER_UNROLL):
                issue(it * GATHER_UNROLL + u, u % 2)
            return c
        lax.fori_loop(0, MOE_BLOCK // GATHER_UNROLL, body, 0)

    def wait_gather(s):
        pltpu.make_async_copy(x1_hbm.at[pl.ds(0, rows), :], xbuf.at[s], gsem.at[s]).wait()

    def wait_scatter(s):
        pltpu.make_async_copy(ybuf.at[s], y_hbm.at[pl.ds(0, rows), :], ssem.at[s]).wait()

    nvalid = nvalid_ref[i]
    prev = jnp.maximum(i - 1, 0)
    prev2 = jnp.maximum(i - 2, 0)
    prev_active = jnp.logical_and(i >= 1, nvalid_ref[prev] > 0)

    @pl.when(i == 0)
    def _():
        looped(lambda j, prio: gather_copy(tok_first_ref, 0, j, 0, prio))
        looped(lambda j, prio: gather_copy(tok_first_ref, 1, j, 1, prio))
        ybuf[...] = jnp.zeros_like(ybuf)
        spare = pltpu.make_async_copy(ybuf.at[0], y_hbm.at[pl.ds(y_hbm.shape[0] - rows, rows), :],
                                      ssem.at[0])
        spare.start()
        spare.wait()

    @pl.when(prev_active)
    def _():
        wait_scatter(slot)

    @pl.when(jnp.logical_or(i == 0, blk_exp_ref[i] != blk_exp_ref[prev]))
    def _():
        wg_s[...] = wg_ref[0].astype(BF16)
        wu_s[...] = wu_ref[0].astype(BF16)
        wd_s[...] = wd_ref[0].astype(BF16)

    @pl.when(nvalid > 0)
    def _():
        wait_gather(gslot)
        ss = None
        for c in range(ROW_CHUNKS):
            xc = xbuf[gslot, pl.ds(c, MOE_BLOCK, stride=ROW_CHUNKS), :]
            ss = xc * xc if ss is None else ss + xc * xc
        scale = lax.rsqrt(jnp.sum(ss, axis=-1, keepdims=True) * (1.0 / D_MODEL) + EPS)
        for c in range(ROW_CHUNKS):
            sl = slice(c * LANES, (c + 1) * LANES)
            xc = xbuf[gslot, pl.ds(c, MOE_BLOCK, stride=ROW_CHUNKS), :]
            hbuf[:, sl] = (xc * scale * gffn_ref[:, sl]).astype(BF16)
        h2 = hbuf[...]
        a = jnp.dot(h2, wg_s[...], preferred_element_type=F32)
        up = jnp.dot(h2, wu_s[...], preferred_element_type=F32)
        act = (a * jax.nn.sigmoid(a) * up).astype(BF16)
        ahead = lax.rem(i + 2, GATHER_SLOTS)
        for j in range(MOE_BLOCK):
            gather_copy(tok_ahead_ref, 0, j, ahead, j % 2)
        for j in range(MOE_BLOCK):
            scatter_copy(j, 1 - slot, j % 2)
        y = jnp.dot(act, wd_s[...], preferred_element_type=F32)
        for c in range(ROW_CHUNKS):
            ybuf[slot, pl.ds(c, MOE_BLOCK, stride=ROW_CHUNKS), :] = y[:, c * LANES:(c + 1) * LANES]

    @pl.when(jnp.logical_and(nvalid == 0, prev_active))
    def _():
        looped(lambda j, prio: scatter_copy(j, 1 - slot, prio))
        wait_scatter(1 - slot)

    @pl.when(jnp.logical_and(nvalid == 0, jnp.logical_or(i < 2, nvalid_ref[prev2] > 0)))
    def _():
        wait_gather(gslot)


def _experts(x1r, g_ffn, blk_exp, nvalid, row_tok, row_dst_prev, wg, wu, wd):
    rows_total, _ = x1r.shape
    d = D_MODEL
    n_blk = blk_exp.shape[0]
    idx_spec = lambda fn: pl.BlockSpec((1, 1, MOE_BLOCK), fn, memory_space=pltpu.SMEM)
    grid_spec = pltpu.PrefetchScalarGridSpec(
        num_scalar_prefetch=2,
        grid=(n_blk,),
        in_specs=[
            pl.BlockSpec((2, 1, MOE_BLOCK), lambda i, be, nv: (0, 0, 0), memory_space=pltpu.SMEM),
            idx_spec(lambda i, be, nv: (jnp.minimum(i + 2, n_blk - 1), 0, 0)),
            idx_spec(lambda i, be, nv: (i, 0, 0)),
            pl.BlockSpec(memory_space=pl.ANY),
            pl.BlockSpec((1, d), lambda i, be, nv: (0, 0)),
            pl.BlockSpec((1, d, EXPERT_FF), lambda i, be, nv: (be[i], 0, 0)),
            pl.BlockSpec((1, d, EXPERT_FF), lambda i, be, nv: (be[i], 0, 0)),
            pl.BlockSpec((1, EXPERT_FF, d), lambda i, be, nv: (be[i], 0, 0)),
        ],
        out_specs=pl.BlockSpec(memory_space=pl.ANY),
        scratch_shapes=[
            pltpu.VMEM((GATHER_SLOTS, MOE_BLOCK * ROW_CHUNKS, LANES), F32),
            pltpu.VMEM((MOE_BLOCK, d), BF16),
            pltpu.VMEM((2, MOE_BLOCK * ROW_CHUNKS, LANES), F32),
            pltpu.VMEM((d, EXPERT_FF), BF16),
            pltpu.VMEM((d, EXPERT_FF), BF16),
            pltpu.VMEM((EXPERT_FF, d), BF16),
            pltpu.SemaphoreType.DMA((GATHER_SLOTS,)),
            pltpu.SemaphoreType.DMA((2,)),
        ],
    )
    return pl.pallas_call(
        _expert_kernel,
        grid_spec=grid_spec,
        out_shape=jax.ShapeDtypeStruct((TOP_K * rows_total + MOE_BLOCK * ROW_CHUNKS, LANES), F32),
        compiler_params=pltpu.CompilerParams(
            dimension_semantics=("arbitrary",), vmem_limit_bytes=VMEM_LIMIT_BYTES),
        name="experts",
    )(blk_exp, nvalid, row_tok, row_tok, row_dst_prev, x1r, g_ffn, wg, wu, wd)


def _combine_kernel(x1_ref, y0_ref, y1_ref, route_ref, o_ref, *, tm):
    w_cols = jnp.transpose(route_ref[...])
    w0 = w_cols[:, 2:3]
    w1 = w_cols[:, 3:4]
    for c in range(ROW_CHUNKS):
        rows = pl.ds(c, tm, stride=ROW_CHUNKS)
        o_ref[:, c * LANES:(c + 1) * LANES] = (x1_ref[rows, :] + w0 * y0_ref[rows, :]
                                               + w1 * y1_ref[rows, :])


def _combine(x1r, y, route_t):
    n = x1r.shape[0] // ROW_CHUNKS
    tm = ROW_TILE
    nt = n // tm
    blk = (tm * ROW_CHUNKS, LANES)
    return pl.pallas_call(
        functools.partial(_combine_kernel, tm=tm),
        grid=(nt,),
        in_specs=[pl.BlockSpec(blk, lambda i: (i, 0)),
                  pl.BlockSpec(blk, lambda i: (i, 0)),
                  pl.BlockSpec(blk, lambda i: (i + nt, 0)),
                  pl.BlockSpec((ROUTE_ROWS, tm), lambda i: (0, i))],
        out_specs=pl.BlockSpec((tm, D_MODEL), lambda i: (i, 0)),
        out_shape=jax.ShapeDtypeStruct((n, D_MODEL), F32),
        compiler_params=pltpu.CompilerParams(dimension_semantics=("parallel",)),
        name="combine",
    )(x1r, y, y, route_t)


def _rope_tables(positions):
    half = ROT_DIM // 2
    lane = np.arange(LANES)
    dim = lane % HEAD_DIM
    rot = dim < ROT_DIM
    spread_cos = np.zeros((half, LANES), np.float32)
    spread_cos[dim[rot] % half, lane[rot]] = 1.0
    spread_sin = np.zeros((half, LANES), np.float32)
    spread_sin[dim[rot] % half, lane[rot]] = np.where(dim[rot] < half, -1.0, 1.0)
    inv = ROPE_THETA ** (-jnp.arange(0, ROT_DIM, 2, dtype=F32) / ROT_DIM)
    ang = inv[None, :, None] * positions.astype(F32)[:, None, :]
    spread = functools.partial(jnp.einsum, 'bjs,jl->bsl', precision=lax.Precision.HIGHEST)
    cos_t = spread(jnp.cos(ang), jnp.asarray(spread_cos)) + jnp.asarray((~rot).astype(np.float32))
    sin_t = spread(jnp.sin(ang), jnp.asarray(spread_sin))
    return cos_t, sin_t


def _route(route_t, counts, n):
    nk = n * TOP_K
    e = route_t[0:TOP_K].reshape(nk).astype(jnp.int32)
    a = jnp.arange(nk, dtype=jnp.int32)
    _, a_s = lax.sort((e, a), num_keys=1, is_stable=True)
    padded = ((counts + MOE_BLOCK - 1) // MOE_BLOCK) * MOE_BLOCK
    pend = jnp.cumsum(padded)
    pstart = pend - padded
    start = jnp.cumsum(counts) - counts
    n_blk = nk // MOE_BLOCK + N_EXPERTS + DRAIN_STEPS
    blk_start = jnp.arange(n_blk, dtype=jnp.int32) * MOE_BLOCK
    blk_exp = jnp.minimum(jnp.sum((pend[None, :] <= blk_start[:, None]).astype(jnp.int32), axis=1),
                          N_EXPERTS - 1).astype(jnp.int32)
    onehot = blk_exp[:, None] == jnp.arange(N_EXPERTS, dtype=jnp.int32)[None, :]
    pick = lambda table: jnp.sum(jnp.where(onehot, table[None, :], 0), axis=1)
    offset = blk_start - pick(pstart)
    nvalid = jnp.clip(pick(counts) - offset, 0, MOE_BLOCK)
    nvalid = jnp.where(blk_start < pend[-1], nvalid, 0).astype(jnp.int32)
    j = jnp.arange(MOE_BLOCK, dtype=jnp.int32)[None, :]
    valid = j < nvalid[:, None]
    src = jnp.clip((pick(start) + offset)[:, None] + j, 0, nk - 1)
    a_p = a_s[src]
    tok = (jnp.where(valid, a_p - n * (a_p >= n).astype(jnp.int32), j) * ROW_CHUNKS).astype(jnp.int32)
    dst = (jnp.where(valid, a_p, TOP_K * n + j) * ROW_CHUNKS).astype(jnp.int32)
    spare = jnp.broadcast_to((TOP_K * n + j) * ROW_CHUNKS, (1, MOE_BLOCK)).astype(jnp.int32)
    dst_prev = jnp.concatenate([spare, dst[:-1]], axis=0)
    return blk_exp, nvalid, tok.reshape(n_blk, 1, MOE_BLOCK), dst_prev.reshape(n_blk, 1, MOE_BLOCK)


def kernel(x, mem, positions, g_mix, g_mem, w_in, g_qn_attn, g_kn_attn, w_conv, w_mem_kv, g_qn_mem,
           g_kn_mem, w_proj_attn, w_proj_conv, w_proj_mem, w_out, g_ffn, w_router_group,
           b_router_group, w_router_expert, b_router_expert, w_gate, w_up, w_down):
    b, s, d = x.shape
    n = b * s
    cos_t, sin_t = _rope_tables(positions)
    for l in range(w_in.shape[0]):
        wl = w_in[l]
        aw = N_BRANCH * GROUP_W
        cols = []
        for gi in range(len(DILATIONS)):
            for part in range(3):
                c0 = part * aw + gi * GROUP_W
                cols.append(wl[:, c0:c0 + GROUP_W])
        cols.append(wl[:, 3 * aw:])
        w_all = jnp.concatenate(cols, axis=1).astype(BF16)

        q0, q1, q2, u, cb, qm, gates = _inproj(x, g_mix[l][None], g_qn_mem[l][None], w_all)
        km, vm = _memkv(mem, g_mem[l][None], g_kn_mem[l][None], w_mem_kv[l].astype(BF16))
        gq = jnp.tile(g_qn_attn[l], 2)[None]
        gk = jnp.tile(g_kn_attn[l], 2)[None]
        o_attn = _attention((q0, q1, q2), cos_t, sin_t, gq, gk)

        w_r = jnp.zeros((d, LANES), F32)
        w_r = w_r.at[:, :N_EXPERT_GROUPS].set(w_router_group[l])
        w_r = w_r.at[:, N_EXPERT_GROUPS:N_EXPERT_GROUPS + N_EXPERTS].set(w_router_expert[l])
        b_r = jnp.zeros((1, LANES), F32)
        b_r = b_r.at[0, :N_EXPERT_GROUPS].set(b_router_group[l])
        b_r = b_r.at[0, N_EXPERT_GROUPS:N_EXPERT_GROUPS + N_EXPERTS].set(b_router_expert[l])
        w_r_hi = w_r.astype(BF16)
        w_r_lo = (w_r - w_r_hi.astype(F32)).astype(BF16)
        x1, route_t, count_t = _merge(x, o_attn, u, cb, qm, km, vm, gates, w_conv[l],
                                      w_proj_attn[l].astype(BF16), w_proj_conv[l].astype(BF16),
                                      w_proj_mem[l].astype(BF16), w_out[l].astype(BF16),
                                      g_ffn[l][None], jnp.concatenate([w_r_hi, w_r_lo], axis=1), b_r)

        counts = count_t[0, N_EXPERT_GROUPS:N_EXPERT_GROUPS + N_EXPERTS].astype(jnp.int32)
        blk_exp, nvalid, row_tok, row_dst_prev = _route(route_t, counts, n)
        y = _experts(x1, g_ffn[l][None], blk_exp, nvalid, row_tok, row_dst_prev,
                     w_gate[l], w_up[l], w_down[l])
        x = _combine(x1, y, route_t).reshape(b, s, d)
    return x
```

```python
import functools

import numpy as np
import jax
import jax.numpy as jnp
from jax import lax
from jax.experimental import pallas as pl
from jax.experimental.pallas import tpu as pltpu

F32 = jnp.float32
BF16 = jnp.bfloat16

D_MODEL = 1024
EPS = 1e-6
HEAD_DIM = 64
DILATIONS = (1, 4, 16)
BAND_HALF = 64
ATTN_SLOTS = 4
GROUP_W = ATTN_SLOTS * HEAD_DIM
QKV_W = 3 * GROUP_W
ROT_DIM = 16
ROPE_THETA = 500000.0
CONV_WIDTH = 768
MEM_HEADS = 4
MEM_HEAD_DIM = 128
MEM_WIDTH = 512
N_BRANCH = 3
N_EXPERT_GROUPS = 4
EXPERTS_PER_GROUP = 8
N_EXPERTS = 32
TOP_K = 2
EXPERT_FF = 512

LANES = 128
VMEM_LIMIT_BYTES = 56 * 1024 * 1024

ROW_TILE = 512
MERGE_TILE = 1024
Q_BLOCK = 128
MOE_BLOCK = 256
ROW_CHUNKS = D_MODEL // LANES
GATHER_UNROLL = 16
GATHER_SLOTS = 3
NEG_BIG = -1e30
ROUTE_NEG = -3e38
ROUTE_ROWS = 8


def _rms(t, gain):
    return t * lax.rsqrt(jnp.mean(t * t, axis=-1, keepdims=True) + EPS) * gain


def _inproj_kernel(x_ref, g_ref, gqm_ref, w_ref,
                   qkv0_ref, qkv1_ref, qkv2_ref, u_ref, cb_ref, qm_ref, gates_ref,
                   h_scr, hp_scr, *, tm):
    x = x_ref[0]
    h = _rms(x, g_ref[...])
    n_chunks = h.shape[1] // LANES
    for c in range(n_chunks):
        h_scr[c] = h[:, c * LANES:(c + 1) * LANES]
    hb = h.astype(BF16)

    def mm(lhs, c0, c1):
        return jnp.dot(lhs, w_ref[:, c0:c1], preferred_element_type=F32)

    qkv0_ref[0, 0] = mm(hb, 0, QKV_W).astype(BF16)
    for gi, out_ref in ((1, qkv1_ref), (2, qkv2_ref)):
        d = DILATIONS[gi]
        rows = tm // d
        for r in range(d):
            for c in range(n_chunks):
                hp_scr[r * rows:(r + 1) * rows, c * LANES:(c + 1) * LANES] = (
                    h_scr[c, pl.ds(r, rows, stride=d), :].astype(BF16))
        res = mm(hp_scr[...], gi * QKV_W, (gi + 1) * QKV_W)
        out_ref[0] = res.reshape(d, rows, QKV_W).astype(BF16)

    c = 3 * QKV_W
    cx = mm(hb, c, c + CONV_WIDTH)
    cc = mm(hb, c + 2 * CONV_WIDTH, c + 3 * CONV_WIDTH)
    u_ref[0] = (cc * cx).astype(BF16)
    cb_ref[0] = mm(hb, c + CONV_WIDTH, c + 2 * CONV_WIDTH).astype(BF16)

    c += 3 * CONV_WIDTH
    qm = mm(hb, c, c + MEM_WIDTH)
    for hh in range(MEM_HEADS):
        sl = slice(hh * MEM_HEAD_DIM, (hh + 1) * MEM_HEAD_DIM)
        qm_ref[0, :, sl] = _rms(qm[:, sl], gqm_ref[...]).astype(BF16)

    c += MEM_WIDTH
    for j in range(N_BRANCH):
        gl = mm(hb, c + j * D_MODEL, c + (j + 1) * D_MODEL)
        gates_ref[0, :, j * D_MODEL:(j + 1) * D_MODEL] = jax.nn.sigmoid(gl).astype(BF16)


def _inproj(x, g_mix, g_qn_mem, w_all):
    b, s, d = x.shape
    tm = ROW_TILE
    nt = s // tm
    outs = []
    out_specs = []
    for dil in DILATIONS:
        outs.append(jax.ShapeDtypeStruct((b, dil, s // dil, QKV_W), BF16))
        out_specs.append(pl.BlockSpec((1, dil, tm // dil, QKV_W), lambda bi, i: (bi, 0, i, 0)))
    for width in (CONV_WIDTH, CONV_WIDTH, MEM_WIDTH, N_BRANCH * D_MODEL):
        outs.append(jax.ShapeDtypeStruct((b, s, width), BF16))
        out_specs.append(pl.BlockSpec((1, tm, width), lambda bi, i: (bi, i, 0)))
    return pl.pallas_call(
        functools.partial(_inproj_kernel, tm=tm),
        grid=(b, nt),
        in_specs=[
            pl.BlockSpec((1, tm, d), lambda bi, i: (bi, i, 0)),
            pl.BlockSpec((1, d), lambda bi, i: (0, 0)),
            pl.BlockSpec((1, MEM_HEAD_DIM), lambda bi, i: (0, 0)),
            pl.BlockSpec(w_all.shape, lambda bi, i: (0, 0), pipeline_mode=pl.Buffered(1)),
        ],
        out_specs=out_specs,
        out_shape=outs,
        scratch_shapes=[pltpu.VMEM((d // LANES, tm, LANES), F32), pltpu.VMEM((tm, d), BF16)],
        compiler_params=pltpu.CompilerParams(
            dimension_semantics=("parallel", "parallel"), vmem_limit_bytes=VMEM_LIMIT_BYTES),
        name="inproj",
    )(x, g_mix, g_qn_mem, w_all)


def _memkv_kernel(mem_ref, g_ref, gk_ref, w_ref, k_ref, v_ref):
    h = _rms(mem_ref[0], g_ref[...]).astype(BF16)
    kv = jnp.dot(h, w_ref[...], preferred_element_type=F32)
    for hh in range(MEM_HEADS):
        sl = slice(hh * MEM_HEAD_DIM, (hh + 1) * MEM_HEAD_DIM)
        k_ref[0, :, sl] = _rms(kv[:, sl], gk_ref[...]).astype(BF16)
    v_ref[0] = kv[:, MEM_WIDTH:].astype(BF16)


def _memkv(mem, g_mem, g_kn_mem, w_kv):
    b, m, d = mem.shape
    return pl.pallas_call(
        _memkv_kernel,
        grid=(b,),
        in_specs=[
            pl.BlockSpec((1, m, d), lambda bi: (bi, 0, 0)),
            pl.BlockSpec((1, d), lambda bi: (0, 0)),
            pl.BlockSpec((1, MEM_HEAD_DIM), lambda bi: (0, 0)),
            pl.BlockSpec(w_kv.shape, lambda bi: (0, 0)),
        ],
        out_specs=[pl.BlockSpec((1, m, MEM_WIDTH), lambda bi: (bi, 0, 0))] * 2,
        out_shape=[jax.ShapeDtypeStruct((b, m, MEM_WIDTH), BF16)] * 2,
        compiler_params=pltpu.CompilerParams(dimension_semantics=("parallel",)),
        name="memkv",
    )(mem, g_mem, g_kn_mem, w_kv)


ATTN_WIN = Q_BLOCK + 2 * BAND_HALF
ATTN_UNROLL = 16
DEN_SAFE = 1e-30


def _attn_consts():
    lane = np.arange(LANES)
    gsum = (lane[:, None] // HEAD_DIM == lane[None, :] // HEAD_DIM).astype(np.float32)
    half = ROT_DIM // 2
    dim = lane % HEAD_DIM
    src = np.where(dim < half, lane + half, np.where(dim < ROT_DIM, lane - half, -1))
    pswap = (lane[:, None] == src[None, :]).astype(np.float32)
    i = np.arange(Q_BLOCK)[:, None]
    c = np.arange(ATTN_WIN)[None, :]
    band = (c >= i) & (c <= i + 2 * BAND_HALF)
    first = c >= BAND_HALF
    last = c < Q_BLOCK + BAND_HALF
    variants = [band, band & first, band & last, band & first & last]
    bias = np.stack([np.where(v, 0.0, NEG_BIG) for v in variants]).astype(np.float32)
    return jnp.asarray(gsum, BF16), jnp.asarray(pswap, BF16), jnp.asarray(bias, F32)


def _attn_kernel(q0, k0, v0, q1, k1, v1, q2, k2, v2, rope_ref, gq_ref, gk_ref,
                 gsum_ref, pswap_ref, bias_ref, out_ref, qs, ks, vs, o_scr, l_scr, bias_scr, *, seq):
    lane = lax.broadcasted_iota(jnp.int32, (1, LANES), 1)
    low_half = lane < HEAD_DIM
    pad = BAND_HALF
    n_blocks = seq // Q_BLOCK

    bound = HEAD_DIM ** 0.5 * jnp.max(jnp.abs(gq_ref[...])) * jnp.max(jnp.abs(gk_ref[...]))
    shifted = bias_ref[...] - bound
    bias_scr[:, 0:Q_BLOCK, :] = shifted
    bias_scr[:, Q_BLOCK:2 * Q_BLOCK, :] = shifted

    def norm_rope(t, gain, cos, sin):
        ms = jnp.dot((t * t).astype(BF16), gsum_ref[...], preferred_element_type=F32) * (1.0 / HEAD_DIM)
        tn = t * lax.rsqrt(ms + EPS) * gain
        partner = jnp.dot(tn.astype(BF16), pswap_ref[...], preferred_element_type=F32)
        return tn * cos + partner * sin

    for gi, (q_ref, k_ref, v_ref) in enumerate(((q0, k0, v0), (q1, k1, v1), (q2, k2, v2))):
        d = DILATIONS[gi]
        length = seq // d
        nblk = length // Q_BLOCK
        region = length + pad

        zpad = jnp.zeros((pad, LANES), BF16)
        for r in range(d + 1):
            ks[r * region:r * region + pad, :] = zpad
            vs[r * region:r * region + pad, :] = zpad

        def split_index(b, nblk=nblk):
            if nblk == 1:
                return b, 0
            r = b // nblk
            return r, b - r * nblk

        def table_rows(r, n, d=d):
            if d == 1:
                return pl.ds(pl.multiple_of(n * Q_BLOCK, Q_BLOCK), Q_BLOCK)
            return pl.ds(r + d * n * Q_BLOCK, Q_BLOCK, stride=d)

        def prep(b, q_ref=q_ref, k_ref=k_ref, v_ref=v_ref, split_index=split_index,
                 table_rows=table_rows):
            r, n = split_index(b)
            src = pl.ds(pl.multiple_of(n * Q_BLOCK, Q_BLOCK), Q_BLOCK)
            rows = table_rows(r, n)
            cos = rope_ref[0, 0, rows, :]
            sin = rope_ref[0, 1, rows, :]
            qn = norm_rope(q_ref[0, r, src, :].astype(F32), gq_ref[...], cos, sin) * (HEAD_DIM ** -0.5)
            kn = norm_rope(k_ref[0, r, src, :].astype(F32), gk_ref[...], cos, sin)
            qbase = pl.multiple_of(b * 2 * Q_BLOCK, 2 * Q_BLOCK)
            qs[pl.ds(qbase, Q_BLOCK), :] = jnp.where(low_half, qn, 0.0).astype(BF16)
            qs[pl.ds(qbase + Q_BLOCK, Q_BLOCK), :] = jnp.where(low_half, 0.0, qn).astype(BF16)
            kdst = pl.ds(pl.multiple_of(b * Q_BLOCK + (r + 1) * pad, pad), Q_BLOCK)
            ks[kdst, :] = kn.astype(BF16)
            vs[kdst, :] = v_ref[0, r, src, :]

        def block(b, exact, gi=gi, nblk=nblk, split_index=split_index, table_rows=table_rows):
            r, n = split_index(b)
            variant = 3 if nblk == 1 else jnp.where(n == 0, 1, 0) + jnp.where(n == nblk - 1, 2, 0)
            qsrc = pl.ds(pl.multiple_of(b * 2 * Q_BLOCK, 2 * Q_BLOCK), 2 * Q_BLOCK)
            ksrc = pl.ds(pl.multiple_of(b * Q_BLOCK + r * pad, pad), ATTN_WIN)
            s = lax.dot_general(qs[qsrc, :], ks[ksrc, :], (((1,), (1,)), ((), ())),
                                preferred_element_type=F32) + bias_scr[variant]
            if exact:
                m = jnp.max(s, axis=-1, keepdims=True)
                s = s - m
                shift = jnp.where(low_half, m[0:Q_BLOCK], m[Q_BLOCK:]) + bound
            else:
                shift = bound
            p = jnp.exp(s)
            den = jnp.sum(p, axis=-1, keepdims=True)
            o2 = jnp.dot(p.astype(BF16), vs[ksrc, :], preferred_element_type=F32)
            o = jnp.where(low_half, o2[0:Q_BLOCK], o2[Q_BLOCK:])
            den = jnp.where(low_half, den[0:Q_BLOCK], den[Q_BLOCK:])
            dst = table_rows(r, n)
            o_scr[gi, dst, :] = o / den
            l_scr[gi, dst, :] = shift + jnp.log(den)
            return den

        def unrolled(fn):
            def body(it, carry):
                for u in range(ATTN_UNROLL):
                    fn(it * ATTN_UNROLL + u)
                return carry
            lax.fori_loop(0, n_blocks // ATTN_UNROLL, body, 0)

        unrolled(prep)

        def fast_body(it, dmin, block=block):
            for u in range(ATTN_UNROLL):
                dmin = jnp.minimum(dmin, block(it * ATTN_UNROLL + u, False))
            return dmin
        dmin = lax.fori_loop(0, n_blocks // ATTN_UNROLL, fast_body,
                             jnp.full((Q_BLOCK, LANES), 1.0, F32))

        @pl.when(jnp.logical_not(jnp.min(dmin) > DEN_SAFE))
        def _(block=block, unrolled=unrolled):
            unrolled(lambda b: block(b, True))

    l0 = l_scr[0]
    l1 = l_scr[1]
    l2 = l_scr[2]
    m = jnp.maximum(jnp.maximum(l0, l1), l2)
    e0 = jnp.exp(l0 - m)
    e1 = jnp.exp(l1 - m)
    e2 = jnp.exp(l2 - m)
    mix = (e0 * o_scr[0] + e1 * o_scr[1] + e2 * o_scr[2]) / (e0 + e1 + e2)
    out_ref[0] = mix.astype(BF16)


def _attention(qkv, rope_t, gq, gk):
    b, _, s, _ = qkv[0].shape
    in_specs = []
    args = []
    for gi, dil in enumerate(DILATIONS):
        for part in range(3):
            in_specs.append(pl.BlockSpec((1, dil, s // dil, LANES),
                                         lambda bi, hp, part=part: (bi, 0, 0, 2 * part + hp)))
            args.append(qkv[gi])
    consts = _attn_consts()
    tab_spec = pl.BlockSpec((1, 2, s, LANES), lambda bi, hp: (bi, 0, 0, 0))
    gain_spec = pl.BlockSpec((1, LANES), lambda bi, hp: (0, 0))
    const_specs = [pl.BlockSpec(c.shape, lambda bi, hp, nd=c.ndim: (0,) * nd) for c in consts]
    key_rows = max(d * (s // d + BAND_HALF) for d in DILATIONS) + BAND_HALF
    return pl.pallas_call(
        functools.partial(_attn_kernel, seq=s),
        grid=(b, 2),
        in_specs=in_specs + [tab_spec] + [gain_spec] * 2 + const_specs,
        out_specs=pl.BlockSpec((1, s, LANES), lambda bi, hp: (bi, 0, hp)),
        out_shape=jax.ShapeDtypeStruct((b, s, GROUP_W), BF16),
        scratch_shapes=[
            pltpu.VMEM((2 * s, LANES), BF16),
            pltpu.VMEM((key_rows, LANES), BF16),
            pltpu.VMEM((key_rows, LANES), BF16),
            pltpu.VMEM((3, s, LANES), F32),
            pltpu.VMEM((3, s, LANES), F32),
            pltpu.VMEM((4, 2 * Q_BLOCK, ATTN_WIN), F32),
        ],
        compiler_params=pltpu.CompilerParams(
            dimension_semantics=("parallel", "parallel"), vmem_limit_bytes=VMEM_LIMIT_BYTES),
        name="dilated_attn",
    )(*args, rope_t, gq, gk, *consts)


def _merge_kernel(x_ref, oa_ref, u_ref, uprev_ref, unext_ref, cb_ref, qm_ref, km_ref, vm_ref,
                  gates_ref, wconv_ref, wpa_ref, wpc_ref, wpm_ref, wout_ref, gffn_ref, wr_ref, br_ref,
                  x1_ref, route_ref, count_ref, om_scr, *, tm, nt):
    i = pl.program_id(1)
    u = u_ref[0].astype(F32)
    prev_row = uprev_ref[0].astype(F32)[15:16, :] * (i > 0).astype(F32)
    next_row = unext_ref[0].astype(F32)[0:1, :] * (i < nt - 1).astype(F32)
    row = lax.broadcasted_iota(jnp.int32, (tm, 1), 0)
    um = jnp.where(row == 0, prev_row, pltpu.roll(u, 1, 0))
    up = jnp.where(row == tm - 1, next_row, pltpu.roll(u, tm - 1, 0))
    wc = wconv_ref[...]
    y = wc[0:1, :] * um + wc[1:2, :] * u + wc[2:3, :] * up
    z = (cb_ref[0].astype(F32) * y).astype(BF16)

    for hh in range(MEM_HEADS):
        sl = slice(hh * MEM_HEAD_DIM, (hh + 1) * MEM_HEAD_DIM)
        s = lax.dot_general(qm_ref[0, :, sl], km_ref[0, :, sl], (((1,), (1,)), ((), ())),
                            preferred_element_type=F32) * (MEM_HEAD_DIM ** -0.5)
        m = jnp.max(s, axis=-1, keepdims=True)
        p = jnp.exp(s - m)
        den = jnp.sum(p, axis=-1, keepdims=True)
        o = jnp.dot(p.astype(BF16), vm_ref[0, :, sl], preferred_element_type=F32) / den
        om_scr[:, sl] = o.astype(BF16)

    pa = jnp.dot(oa_ref[0], wpa_ref[...], preferred_element_type=F32)
    pc = jnp.dot(z, wpc_ref[...], preferred_element_type=F32)
    pm = jnp.dot(om_scr[...], wpm_ref[...], preferred_element_type=F32)
    merged = (gates_ref[0, :, 0:D_MODEL].astype(F32) * pa
              + gates_ref[0, :, D_MODEL:2 * D_MODEL].astype(F32) * pc
              + gates_ref[0, :, 2 * D_MODEL:3 * D_MODEL].astype(F32) * pm)
    x1 = x_ref[0] + jnp.dot(merged.astype(BF16), wout_ref[...], preferred_element_type=F32)
    for c in range(ROW_CHUNKS):
        x1_ref[pl.ds(c, tm, stride=ROW_CHUNKS), :] = x1[:, c * LANES:(c + 1) * LANES]

    h2 = _rms(x1, gffn_ref[...])
    h_hi = h2.astype(BF16)
    h_lo = (h2 - h_hi.astype(F32)).astype(BF16)
    hh = jnp.dot(h_hi, wr_ref[...], preferred_element_type=F32)
    lg = (hh[:, 0:LANES] + hh[:, LANES:2 * LANES]
          + jnp.dot(h_lo, wr_ref[:, 0:LANES], preferred_element_type=F32)) + br_ref[...]

    lane = lax.broadcasted_iota(jnp.int32, (tm, LANES), 1).astype(F32)
    far = float(LANES)
    is_group = lane < N_EXPERT_GROUPS
    gl = jnp.where(is_group, lg, ROUTE_NEG)
    mg = jnp.max(gl, axis=-1, keepdims=True)
    gidx = jnp.min(jnp.where(gl == mg, lane, far), axis=-1, keepdims=True)
    pg_top = 1.0 / jnp.sum(jnp.where(is_group, jnp.exp(lg - mg), 0.0), axis=-1, keepdims=True)
    first = N_EXPERT_GROUPS + EXPERTS_PER_GROUP * gidx
    in_sel = jnp.logical_and(lane >= first, lane < first + EXPERTS_PER_GROUP)
    sel = jnp.where(in_sel, lg, ROUTE_NEG)
    m1 = jnp.max(sel, axis=-1, keepdims=True)
    i1 = jnp.min(jnp.where(jnp.logical_and(in_sel, sel == m1), lane, far), axis=-1, keepdims=True)
    keep = jnp.logical_and(in_sel, lane != i1)
    sel2 = jnp.where(keep, lg, ROUTE_NEG)
    m2 = jnp.max(sel2, axis=-1, keepdims=True)
    i2 = jnp.min(jnp.where(jnp.logical_and(keep, sel2 == m2), lane, far), axis=-1, keepdims=True)
    r = jnp.exp(m2 - m1)
    w1 = pg_top / (1.0 + r)
    w2 = w1 * r
    route = jnp.where(lane == 0, i1 - N_EXPERT_GROUPS,
                      jnp.where(lane == 1, i2 - N_EXPERT_GROUPS,
                                jnp.where(lane == 2, w1, jnp.where(lane == 3, w2, 0.0))))
    route_ref[...] = jnp.transpose(route)[0:8, :]

    @pl.when(jnp.logical_and(pl.program_id(0) == 0, i == 0))
    def _():
        count_ref[...] = jnp.zeros_like(count_ref)
    picked = jnp.where(jnp.logical_or(lane == i1, lane == i2), 1.0, 0.0)
    count_ref[...] += jnp.sum(picked, axis=0, keepdims=True)


def _merge(x, o_attn, u, cb, qm, km, vm, gates, w_conv, wpa, wpc, wpm, wout, g_ffn, w_r, b_r):
    b, s, d = x.shape
    tm = MERGE_TILE
    nt = s // tm
    halo = 16
    hb = tm // halo

    def tile(width):
        return pl.BlockSpec((1, tm, width), lambda bi, i: (bi, i, 0))

    def whole(arr):
        return pl.BlockSpec(arr.shape, lambda bi, i: (0,) * arr.ndim)

    def per_batch(arr):
        return pl.BlockSpec((1,) + arr.shape[1:], lambda bi, i: (bi, 0, 0))

    in_specs = [
        tile(d), tile(GROUP_W), tile(CONV_WIDTH),
        pl.BlockSpec((1, halo, CONV_WIDTH), lambda bi, i: (bi, jnp.maximum(i * hb - 1, 0), 0)),
        pl.BlockSpec((1, halo, CONV_WIDTH),
                     lambda bi, i: (bi, jnp.minimum((i + 1) * hb, s // halo - 1), 0)),
        tile(CONV_WIDTH), tile(MEM_WIDTH), per_batch(km), per_batch(vm), tile(N_BRANCH * d),
        whole(w_conv), whole(wpa), whole(wpc), whole(wpm), whole(wout), whole(g_ffn),
        whole(w_r), whole(b_r),
    ]
    return pl.pallas_call(
        functools.partial(_merge_kernel, tm=tm, nt=nt),
        grid=(b, nt),
        in_specs=in_specs,
        out_specs=[pl.BlockSpec((tm * ROW_CHUNKS, LANES), lambda bi, i: (bi * nt + i, 0)),
                   pl.BlockSpec((ROUTE_ROWS, tm), lambda bi, i: (0, bi * nt + i)),
                   pl.BlockSpec((ROUTE_ROWS, LANES), lambda bi, i: (0, 0))],
        out_shape=[jax.ShapeDtypeStruct((b * s * ROW_CHUNKS, LANES), F32),
                   jax.ShapeDtypeStruct((ROUTE_ROWS, b * s), F32),
                   jax.ShapeDtypeStruct((ROUTE_ROWS, LANES), F32)],
        scratch_shapes=[pltpu.VMEM((tm, MEM_WIDTH), BF16)],
        compiler_params=pltpu.CompilerParams(
            dimension_semantics=("arbitrary", "arbitrary"), vmem_limit_bytes=VMEM_LIMIT_BYTES),
        name="merge",
    )(x, o_attn, u, u, u, cb, qm, km, vm, gates, w_conv, wpa, wpc, wpm, wout, g_ffn, w_r, b_r)


DRAIN_STEPS = 2


def _expert_kernel(blk_exp_ref, nvalid_ref, tok_first_ref, tok_ahead_ref, dst_prev_ref, x1_hbm,
                   gffn_ref, wg_ref, wu_ref, wd_ref, y_hbm, xbuf, hbuf, ybuf, wg_s, wu_s, wd_s,
                   gsem, ssem):
    i = pl.program_id(0)
    slot = lax.rem(i, 2)
    gslot = lax.rem(i, GATHER_SLOTS)
    rows = MOE_BLOCK * ROW_CHUNKS

    def gather_copy(idx_ref, b, j, s, prio):
        src_row = pl.multiple_of(idx_ref[b, 0, j], ROW_CHUNKS)
        return pltpu.make_async_copy(x1_hbm.at[pl.ds(src_row, ROW_CHUNKS), :],
                                     xbuf.at[s, pl.ds(j * ROW_CHUNKS, ROW_CHUNKS), :], gsem.at[s]
                                     ).start(priority=prio)

    def scatter_copy(j, s, prio):
        dst_row = pl.multiple_of(dst_prev_ref[0, 0, j], ROW_CHUNKS)
        return pltpu.make_async_copy(ybuf.at[s, pl.ds(j * ROW_CHUNKS, ROW_CHUNKS), :],
                                     y_hbm.at[pl.ds(dst_row, ROW_CHUNKS), :], ssem.at[s]
                                     ).start(priority=prio)

    def looped(issue):
        def body(it, c):
            for u in range(GATHER_UNROLL):
                issue(it * GATHER_UNROLL + u, u % 2)
            return c
        lax.fori_loop(0, MOE_BLOCK // GATHER_UNROLL, body, 0)

    def wait_gather(s):
        pltpu.make_async_copy(x1_hbm.at[pl.ds(0, rows), :], xbuf.at[s], gsem.at[s]).wait()

    def wait_scatter(s):
        pltpu.make_async_copy(ybuf.at[s], y_hbm.at[pl.ds(0, rows), :], ssem.at[s]).wait()

    nvalid = nvalid_ref[i]
    prev = jnp.maximum(i - 1, 0)
    prev2 = jnp.maximum(i - 2, 0)
    prev_active = jnp.logical_and(i >= 1, nvalid_ref[prev] > 0)

    @pl.when(i == 0)
    def _():
        looped(lambda j, prio: gather_copy(tok_first_ref, 0, j, 0, prio))
        looped(lambda j, prio: gather_copy(tok_first_ref, 1, j, 1, prio))
        ybuf[...] = jnp.zeros_like(ybuf)
        spare = pltpu.make_async_copy(ybuf.at[0], y_hbm.at[pl.ds(y_hbm.shape[0] - rows, rows), :],
                                      ssem.at[0])
        spare.start()
        spare.wait()

    @pl.when(prev_active)
    def _():
        wait_scatter(slot)

    @pl.when(jnp.logical_or(i == 0, blk_exp_ref[i] != blk_exp_ref[prev]))
    def _():
        wg_s[...] = wg_ref[0].astype(BF16)
        wu_s[...] = wu_ref[0].astype(BF16)
        wd_s[...] = wd_ref[0].astype(BF16)

    @pl.when(nvalid > 0)
    def _():
        wait_gather(gslot)
        ss = None
        for c in range(ROW_CHUNKS):
            xc = xbuf[gslot, pl.ds(c, MOE_BLOCK, stride=ROW_CHUNKS), :]
            ss = xc * xc if ss is None else ss + xc * xc
        scale = lax.rsqrt(jnp.sum(ss, axis=-1, keepdims=True) * (1.0 / D_MODEL) + EPS)
        for c in range(ROW_CHUNKS):
            sl = slice(c * LANES, (c + 1) * LANES)
            xc = xbuf[gslot, pl.ds(c, MOE_BLOCK, stride=ROW_CHUNKS), :]
            hbuf[:, sl] = (xc * scale * gffn_ref[:, sl]).astype(BF16)
        h2 = hbuf[...]
        a = jnp.dot(h2, wg_s[...], preferred_element_type=F32)
        up = jnp.dot(h2, wu_s[...], preferred_element_type=F32)
        act = (a * jax.nn.sigmoid(a) * up).astype(BF16)
        ahead = lax.rem(i + 2, GATHER_SLOTS)
        for j in range(MOE_BLOCK):
            gather_copy(tok_ahead_ref, 0, j, ahead, j % 2)
        for j in range(MOE_BLOCK):
            scatter_copy(j, 1 - slot, j % 2)
        y = jnp.dot(act, wd_s[...], preferred_element_type=F32)
        for c in range(ROW_CHUNKS):
            ybuf[slot, pl.ds(c, MOE_BLOCK, stride=ROW_CHUNKS), :] = y[:, c * LANES:(c + 1) * LANES]

    @pl.when(jnp.logical_and(nvalid == 0, prev_active))
    def _():
        looped(lambda j, prio: scatter_copy(j, 1 - slot, prio))
        wait_scatter(1 - slot)

    @pl.when(jnp.logical_and(nvalid == 0, jnp.logical_or(i < 2, nvalid_ref[prev2] > 0)))
    def _():
        wait_gather(gslot)


def _experts(x1r, g_ffn, blk_exp, nvalid, row_tok, row_dst_prev, wg, wu, wd):
    rows_total, _ = x1r.shape
    d = D_MODEL
    n_blk = blk_exp.shape[0]
    idx_spec = lambda fn: pl.BlockSpec((1, 1, MOE_BLOCK), fn, memory_space=pltpu.SMEM)
    grid_spec = pltpu.PrefetchScalarGridSpec(
        num_scalar_prefetch=2,
        grid=(n_blk,),
        in_specs=[
            pl.BlockSpec((2, 1, MOE_BLOCK), lambda i, be, nv: (0, 0, 0), memory_space=pltpu.SMEM),
            idx_spec(lambda i, be, nv: (jnp.minimum(i + 2, n_blk - 1), 0, 0)),
            idx_spec(lambda i, be, nv: (i, 0, 0)),
            pl.BlockSpec(memory_space=pl.ANY),
            pl.BlockSpec((1, d), lambda i, be, nv: (0, 0)),
            pl.BlockSpec((1, d, EXPERT_FF), lambda i, be, nv: (be[i], 0, 0)),
            pl.BlockSpec((1, d, EXPERT_FF), lambda i, be, nv: (be[i], 0, 0)),
            pl.BlockSpec((1, EXPERT_FF, d), lambda i, be, nv: (be[i], 0, 0)),
        ],
        out_specs=pl.BlockSpec(memory_space=pl.ANY),
        scratch_shapes=[
            pltpu.VMEM((GATHER_SLOTS, MOE_BLOCK * ROW_CHUNKS, LANES), F32),
            pltpu.VMEM((MOE_BLOCK, d), BF16),
            pltpu.VMEM((2, MOE_BLOCK * ROW_CHUNKS, LANES), F32),
            pltpu.VMEM((d, EXPERT_FF), BF16),
            pltpu.VMEM((d, EXPERT_FF), BF16),
            pltpu.VMEM((EXPERT_FF, d), BF16),
            pltpu.SemaphoreType.DMA((GATHER_SLOTS,)),
            pltpu.SemaphoreType.DMA((2,)),
        ],
    )
    return pl.pallas_call(
        _expert_kernel,
        grid_spec=grid_spec,
        out_shape=jax.ShapeDtypeStruct((TOP_K * rows_total + MOE_BLOCK * ROW_CHUNKS, LANES), F32),
        compiler_params=pltpu.CompilerParams(
            dimension_semantics=("arbitrary",), vmem_limit_bytes=VMEM_LIMIT_BYTES),
        name="experts",
    )(blk_exp, nvalid, row_tok, row_tok, row_dst_prev, x1r, g_ffn, wg, wu, wd)


def _combine_kernel(x1_ref, y0_ref, y1_ref, route_ref, o_ref, *, tm):
    w_cols = jnp.transpose(route_ref[...])
    w0 = w_cols[:, 2:3]
    w1 = w_cols[:, 3:4]
    for c in range(ROW_CHUNKS):
        rows = pl.ds(c, tm, stride=ROW_CHUNKS)
        o_ref[:, c * LANES:(c + 1) * LANES] = (x1_ref[rows, :] + w0 * y0_ref[rows, :]
                                               + w1 * y1_ref[rows, :])


def _combine(x1r, y, route_t):
    n = x1r.shape[0] // ROW_CHUNKS
    tm = ROW_TILE
    nt = n // tm
    blk = (tm * ROW_CHUNKS, LANES)
    return pl.pallas_call(
        functools.partial(_combine_kernel, tm=tm),
        grid=(nt,),
        in_specs=[pl.BlockSpec(blk, lambda i: (i, 0)),
                  pl.BlockSpec(blk, lambda i: (i, 0)),
                  pl.BlockSpec(blk, lambda i: (i + nt, 0)),
                  pl.BlockSpec((ROUTE_ROWS, tm), lambda i: (0, i))],
        out_specs=pl.BlockSpec((tm, D_MODEL), lambda i: (i, 0)),
        out_shape=jax.ShapeDtypeStruct((n, D_MODEL), F32),
        compiler_params=pltpu.CompilerParams(dimension_semantics=("parallel",)),
        name="combine",
    )(x1r, y, y, route_t)


def _rope_tables(positions):
    half = ROT_DIM // 2
    lane = np.arange(LANES)
    dim = lane % HEAD_DIM
    rot = dim < ROT_DIM
    spread = np.zeros((ROT_DIM, 2, LANES), np.float32)
    spread[dim[rot] % half, 0, lane[rot]] = 1.0
    spread[half + dim[rot] % half, 1, lane[rot]] = np.where(dim[rot] < half, -1.0, 1.0)
    plain = np.stack([(~rot).astype(np.float32), np.zeros(LANES, np.float32)])[:, None, :]
    inv = ROPE_THETA ** (-jnp.arange(0, ROT_DIM, 2, dtype=F32) / ROT_DIM)
    ang = inv[None, :, None] * positions.astype(F32)[:, None, :]
    both = jnp.concatenate([jnp.cos(ang), jnp.sin(ang)], axis=1)
    return jnp.einsum('bjs,jtl->btsl', both, jnp.asarray(spread),
                      precision=lax.Precision.HIGHEST) + jnp.asarray(plain)


def _route(route_t, counts, n):
    nk = n * TOP_K
    e = route_t[0:TOP_K].reshape(nk).astype(jnp.int32)
    a = jnp.arange(nk, dtype=jnp.int32)
    _, a_s = lax.sort((e, a), num_keys=1, is_stable=True)
    padded = ((counts + MOE_BLOCK - 1) // MOE_BLOCK) * MOE_BLOCK
    pend = jnp.cumsum(padded)
    pstart = pend - padded
    start = jnp.cumsum(counts) - counts
    n_blk = nk // MOE_BLOCK + N_EXPERTS + DRAIN_STEPS
    blk_start = jnp.arange(n_blk, dtype=jnp.int32) * MOE_BLOCK
    blk_exp = jnp.minimum(jnp.sum((pend[None, :] <= blk_start[:, None]).astype(jnp.int32), axis=1),
                          N_EXPERTS - 1).astype(jnp.int32)
    onehot = blk_exp[:, None] == jnp.arange(N_EXPERTS, dtype=jnp.int32)[None, :]
    pick = lambda table: jnp.sum(jnp.where(onehot, table[None, :], 0), axis=1)
    offset = blk_start - pick(pstart)
    nvalid = jnp.clip(pick(counts) - offset, 0, MOE_BLOCK)
    nvalid = jnp.where(blk_start < pend[-1], nvalid, 0).astype(jnp.int32)
    j = jnp.arange(MOE_BLOCK, dtype=jnp.int32)[None, :]
    valid = j < nvalid[:, None]
    src = jnp.clip((pick(start) + offset)[:, None] + j, 0, nk - 1)
    a_p = a_s[src]
    tok = (jnp.where(valid, a_p - n * (a_p >= n).astype(jnp.int32), j) * ROW_CHUNKS).astype(jnp.int32)
    dst = (jnp.where(valid, a_p, TOP_K * n + j) * ROW_CHUNKS).astype(jnp.int32)
    spare = jnp.broadcast_to((TOP_K * n + j) * ROW_CHUNKS, (1, MOE_BLOCK)).astype(jnp.int32)
    dst_prev = jnp.concatenate([spare, dst[:-1]], axis=0)
    return blk_exp, nvalid, tok.reshape(n_blk, 1, MOE_BLOCK), dst_prev.reshape(n_blk, 1, MOE_BLOCK)


def kernel(x, mem, positions, g_mix, g_mem, w_in, g_qn_attn, g_kn_attn, w_conv, w_mem_kv, g_qn_mem,
           g_kn_mem, w_proj_attn, w_proj_conv, w_proj_mem, w_out, g_ffn, w_router_group,
           b_router_group, w_router_expert, b_router_expert, w_gate, w_up, w_down):
    b, s, d = x.shape
    n = b * s
    rope_t = _rope_tables(positions)
    for l in range(w_in.shape[0]):
        wl = w_in[l]
        aw = N_BRANCH * GROUP_W
        cols = []
        for gi in range(len(DILATIONS)):
            for part in range(3):
                c0 = part * aw + gi * GROUP_W
                cols.append(wl[:, c0:c0 + GROUP_W])
        cols.append(wl[:, 3 * aw:])
        w_all = jnp.concatenate(cols, axis=1).astype(BF16)

        q0, q1, q2, u, cb, qm, gates = _inproj(x, g_mix[l][None], g_qn_mem[l][None], w_all)
        km, vm = _memkv(mem, g_mem[l][None], g_kn_mem[l][None], w_mem_kv[l].astype(BF16))
        gq = jnp.tile(g_qn_attn[l], 2)[None]
        gk = jnp.tile(g_kn_attn[l], 2)[None]
        o_attn = _attention((q0, q1, q2), rope_t, gq, gk)

        w_r = jnp.zeros((d, LANES), F32)
        w_r = w_r.at[:, :N_EXPERT_GROUPS].set(w_router_group[l])
        w_r = w_r.at[:, N_EXPERT_GROUPS:N_EXPERT_GROUPS + N_EXPERTS].set(w_router_expert[l])
        b_r = jnp.zeros((1, LANES), F32)
        b_r = b_r.at[0, :N_EXPERT_GROUPS].set(b_router_group[l])
        b_r = b_r.at[0, N_EXPERT_GROUPS:N_EXPERT_GROUPS + N_EXPERTS].set(b_router_expert[l])
        w_r_hi = w_r.astype(BF16)
        w_r_lo = (w_r - w_r_hi.astype(F32)).astype(BF16)
        x1, route_t, count_t = _merge(x, o_attn, u, cb, qm, km, vm, gates, w_conv[l],
                                      w_proj_attn[l].astype(BF16), w_proj_conv[l].astype(BF16),
                                      w_proj_mem[l].astype(BF16), w_out[l].astype(BF16),
                                      g_ffn[l][None], jnp.concatenate([w_r_hi, w_r_lo], axis=1), b_r)

        counts = count_t[0, N_EXPERT_GROUPS:N_EXPERT_GROUPS + N_EXPERTS].astype(jnp.int32)
        blk_exp, nvalid, row_tok, row_dst_prev = _route(route_t, counts, n)
        y = _experts(x1, g_ffn[l][None], blk_exp, nvalid, row_tok, row_dst_prev,
                     w_gate[l], w_up[l], w_down[l])
        x = _combine(x1, y, route_t).reshape(b, s, d)
    return x
```

```python
import functools

import numpy as np
import jax
import jax.numpy as jnp
from jax import lax
from jax.experimental import pallas as pl
from jax.experimental.pallas import tpu as pltpu

F32 = jnp.float32
BF16 = jnp.bfloat16

D_MODEL = 1024
EPS = 1e-6
HEAD_DIM = 64
DILATIONS = (1, 4, 16)
BAND_HALF = 64
ATTN_SLOTS = 4
GROUP_W = ATTN_SLOTS * HEAD_DIM
QKV_W = 3 * GROUP_W
ROT_DIM = 16
ROPE_THETA = 500000.0
CONV_WIDTH = 768
MEM_HEADS = 4
MEM_HEAD_DIM = 128
MEM_WIDTH = 512
N_BRANCH = 3
N_EXPERT_GROUPS = 4
EXPERTS_PER_GROUP = 8
N_EXPERTS = 32
TOP_K = 2
EXPERT_FF = 512

LANES = 128
VMEM_LIMIT_BYTES = 56 * 1024 * 1024

ROW_TILE = 512
MERGE_TILE = 1024
Q_BLOCK = 128
MOE_BLOCK = 256
ROW_CHUNKS = D_MODEL // LANES
GATHER_UNROLL = 16
GATHER_SLOTS = 3
NEG_BIG = -1e30
ROUTE_NEG = -3e38
ROUTE_ROWS = 8


def _rms(t, gain):
    return t * lax.rsqrt(jnp.mean(t * t, axis=-1, keepdims=True) + EPS) * gain


def _inproj_kernel(x_ref, g_ref, gqm_ref, w_ref,
                   qkv0_ref, qkv1_ref, qkv2_ref, u_ref, cb_ref, qm_ref, gates_ref,
                   h_scr, hp_scr, *, tm):
    x = x_ref[0]
    h = _rms(x, g_ref[...])
    n_chunks = h.shape[1] // LANES
    for c in range(n_chunks):
        h_scr[c] = h[:, c * LANES:(c + 1) * LANES]
    hb = h.astype(BF16)

    def mm(lhs, c0, c1):
        return jnp.dot(lhs, w_ref[:, c0:c1], preferred_element_type=F32)

    qkv0_ref[0, 0] = mm(hb, 0, QKV_W).astype(BF16)
    for gi, out_ref in ((1, qkv1_ref), (2, qkv2_ref)):
        d = DILATIONS[gi]
        rows = tm // d
        for r in range(d):
            for c in range(n_chunks):
                hp_scr[r * rows:(r + 1) * rows, c * LANES:(c + 1) * LANES] = (
                    h_scr[c, pl.ds(r, rows, stride=d), :].astype(BF16))
        res = mm(hp_scr[...], gi * QKV_W, (gi + 1) * QKV_W)
        out_ref[0] = res.reshape(d, rows, QKV_W).astype(BF16)

    c = 3 * QKV_W
    cx = mm(hb, c, c + CONV_WIDTH)
    cc = mm(hb, c + 2 * CONV_WIDTH, c + 3 * CONV_WIDTH)
    u_ref[0] = (cc * cx).astype(BF16)
    cb_ref[0] = mm(hb, c + CONV_WIDTH, c + 2 * CONV_WIDTH).astype(BF16)

    c += 3 * CONV_WIDTH
    qm = mm(hb, c, c + MEM_WIDTH)
    for hh in range(MEM_HEADS):
        sl = slice(hh * MEM_HEAD_DIM, (hh + 1) * MEM_HEAD_DIM)
        qm_ref[0, :, sl] = _rms(qm[:, sl], gqm_ref[...]).astype(BF16)

    c += MEM_WIDTH
    for j in range(N_BRANCH):
        gl = mm(hb, c + j * D_MODEL, c + (j + 1) * D_MODEL)
        gates_ref[0, :, j * D_MODEL:(j + 1) * D_MODEL] = jax.nn.sigmoid(gl).astype(BF16)


def _inproj(x, g_mix, g_qn_mem, w_all):
    b, s, d = x.shape
    tm = ROW_TILE
    nt = s // tm
    outs = []
    out_specs = []
    for dil in DILATIONS:
        outs.append(jax.ShapeDtypeStruct((b, dil, s // dil, QKV_W), BF16))
        out_specs.append(pl.BlockSpec((1, dil, tm // dil, QKV_W), lambda bi, i: (bi, 0, i, 0)))
    for width in (CONV_WIDTH, CONV_WIDTH, MEM_WIDTH, N_BRANCH * D_MODEL):
        outs.append(jax.ShapeDtypeStruct((b, s, width), BF16))
        out_specs.append(pl.BlockSpec((1, tm, width), lambda bi, i: (bi, i, 0)))
    return pl.pallas_call(
        functools.partial(_inproj_kernel, tm=tm),
        grid=(b, nt),
        in_specs=[
            pl.BlockSpec((1, tm, d), lambda bi, i: (bi, i, 0)),
            pl.BlockSpec((1, d), lambda bi, i: (0, 0)),
            pl.BlockSpec((1, MEM_HEAD_DIM), lambda bi, i: (0, 0)),
            pl.BlockSpec(w_all.shape, lambda bi, i: (0, 0), pipeline_mode=pl.Buffered(1)),
        ],
        out_specs=out_specs,
        out_shape=outs,
        scratch_shapes=[pltpu.VMEM((d // LANES, tm, LANES), F32), pltpu.VMEM((tm, d), BF16)],
        compiler_params=pltpu.CompilerParams(
            dimension_semantics=("parallel", "parallel"), vmem_limit_bytes=VMEM_LIMIT_BYTES),
        name="inproj",
    )(x, g_mix, g_qn_mem, w_all)


def _memkv_kernel(mem_ref, g_ref, gk_ref, w_ref, k_ref, v_ref):
    h = _rms(mem_ref[0], g_ref[...]).astype(BF16)
    kv = jnp.dot(h, w_ref[...], preferred_element_type=F32)
    for hh in range(MEM_HEADS):
        sl = slice(hh * MEM_HEAD_DIM, (hh + 1) * MEM_HEAD_DIM)
        k_ref[0, :, sl] = _rms(kv[:, sl], gk_ref[...]).astype(BF16)
    v_ref[0] = kv[:, MEM_WIDTH:].astype(BF16)


def _memkv(mem, g_mem, g_kn_mem, w_kv):
    b, m, d = mem.shape
    return pl.pallas_call(
        _memkv_kernel,
        grid=(b,),
        in_specs=[
            pl.BlockSpec((1, m, d), lambda bi: (bi, 0, 0)),
            pl.BlockSpec((1, d), lambda bi: (0, 0)),
            pl.BlockSpec((1, MEM_HEAD_DIM), lambda bi: (0, 0)),
            pl.BlockSpec(w_kv.shape, lambda bi: (0, 0)),
        ],
        out_specs=[pl.BlockSpec((1, m, MEM_WIDTH), lambda bi: (bi, 0, 0))] * 2,
        out_shape=[jax.ShapeDtypeStruct((b, m, MEM_WIDTH), BF16)] * 2,
        compiler_params=pltpu.CompilerParams(dimension_semantics=("parallel",)),
        name="memkv",
    )(mem, g_mem, g_kn_mem, w_kv)


ATTN_WIN = Q_BLOCK + 2 * BAND_HALF
ATTN_UNROLL = 16
DEN_SAFE = 1e-30


def _attn_consts():
    lane = np.arange(LANES)
    gsum = (lane[:, None] // HEAD_DIM == lane[None, :] // HEAD_DIM).astype(np.float32)
    half = ROT_DIM // 2
    dim = lane % HEAD_DIM
    src = np.where(dim < half, lane + half, np.where(dim < ROT_DIM, lane - half, -1))
    pswap = (lane[:, None] == src[None, :]).astype(np.float32)
    i = np.arange(Q_BLOCK)[:, None]
    c = np.arange(ATTN_WIN)[None, :]
    band = (c >= i) & (c <= i + 2 * BAND_HALF)
    first = c >= BAND_HALF
    last = c < Q_BLOCK + BAND_HALF
    variants = [band, band & first, band & last, band & first & last]
    bias = np.stack([np.where(v, 0.0, NEG_BIG) for v in variants]).astype(np.float32)
    return jnp.asarray(gsum, BF16), jnp.asarray(pswap, BF16), jnp.asarray(bias, F32)


def _attn_kernel(q0, k0, v0, q1, k1, v1, q2, k2, v2, rope_ref, gq_ref, gk_ref,
                 gsum_ref, pswap_ref, bias_ref, out_ref, qs, ks, vs, o_scr, l_scr, bias_scr, *, seq):
    lane = lax.broadcasted_iota(jnp.int32, (1, LANES), 1)
    low_half = lane < HEAD_DIM
    pad = BAND_HALF
    n_blocks = seq // Q_BLOCK

    bound = HEAD_DIM ** 0.5 * jnp.max(jnp.abs(gq_ref[...])) * jnp.max(jnp.abs(gk_ref[...]))
    shifted = bias_ref[...] - bound
    bias_scr[:, 0:Q_BLOCK, :] = shifted
    bias_scr[:, Q_BLOCK:2 * Q_BLOCK, :] = shifted

    def norm_rope(t, gain, cos, sin):
        ms = jnp.dot((t * t).astype(BF16), gsum_ref[...], preferred_element_type=F32) * (1.0 / HEAD_DIM)
        tn = t * lax.rsqrt(ms + EPS) * gain
        partner = jnp.dot(tn.astype(BF16), pswap_ref[...], preferred_element_type=F32)
        return tn * cos + partner * sin

    for gi, (q_ref, k_ref, v_ref) in enumerate(((q0, k0, v0), (q1, k1, v1), (q2, k2, v2))):
        d = DILATIONS[gi]
        length = seq // d
        nblk = length // Q_BLOCK
        region = length + pad

        zpad = jnp.zeros((pad, LANES), BF16)
        for r in range(d + 1):
            ks[r * region:r * region + pad, :] = zpad
            vs[r * region:r * region + pad, :] = zpad

        def split_index(b, nblk=nblk):
            if nblk == 1:
                return b, 0
            r = b // nblk
            return r, b - r * nblk

        def table_rows(r, n, d=d):
            if d == 1:
                return pl.ds(pl.multiple_of(n * Q_BLOCK, Q_BLOCK), Q_BLOCK)
            return pl.ds(r + d * n * Q_BLOCK, Q_BLOCK, stride=d)

        def prep(b, q_ref=q_ref, k_ref=k_ref, v_ref=v_ref, split_index=split_index,
                 table_rows=table_rows):
            r, n = split_index(b)
            src = pl.ds(pl.multiple_of(n * Q_BLOCK, Q_BLOCK), Q_BLOCK)
            rows = table_rows(r, n)
            cos = rope_ref[0, 0, rows, :]
            sin = rope_ref[0, 1, rows, :]
            qn = norm_rope(q_ref[0, r, src, :].astype(F32), gq_ref[...], cos, sin) * (HEAD_DIM ** -0.5)
            kn = norm_rope(k_ref[0, r, src, :].astype(F32), gk_ref[...], cos, sin)
            qbase = pl.multiple_of(b * 2 * Q_BLOCK, 2 * Q_BLOCK)
            qs[pl.ds(qbase, Q_BLOCK), :] = jnp.where(low_half, qn, 0.0).astype(BF16)
            qs[pl.ds(qbase + Q_BLOCK, Q_BLOCK), :] = jnp.where(low_half, 0.0, qn).astype(BF16)
            kdst = pl.ds(pl.multiple_of(b * Q_BLOCK + (r + 1) * pad, pad), Q_BLOCK)
            ks[kdst, :] = kn.astype(BF16)
            vs[kdst, :] = v_ref[0, r, src, :]

        def block(b, exact, gi=gi, nblk=nblk, split_index=split_index, table_rows=table_rows):
            r, n = split_index(b)
            variant = 3 if nblk == 1 else jnp.where(n == 0, 1, 0) + jnp.where(n == nblk - 1, 2, 0)
            qsrc = pl.ds(pl.multiple_of(b * 2 * Q_BLOCK, 2 * Q_BLOCK), 2 * Q_BLOCK)
            ksrc = pl.ds(pl.multiple_of(b * Q_BLOCK + r * pad, pad), ATTN_WIN)
            s = lax.dot_general(qs[qsrc, :], ks[ksrc, :], (((1,), (1,)), ((), ())),
                                preferred_element_type=F32) + bias_scr[variant]
            if exact:
                m = jnp.max(s, axis=-1, keepdims=True)
                s = s - m
                shift = jnp.where(low_half, m[0:Q_BLOCK], m[Q_BLOCK:]) + bound
            else:
                shift = bound
            p = jnp.exp(s)
            den = jnp.sum(p, axis=-1, keepdims=True)
            o2 = jnp.dot(p.astype(BF16), vs[ksrc, :], preferred_element_type=F32)
            o = jnp.where(low_half, o2[0:Q_BLOCK], o2[Q_BLOCK:])
            den = jnp.where(low_half, den[0:Q_BLOCK], den[Q_BLOCK:])
            dst = table_rows(r, n)
            o_scr[gi, dst, :] = o / den
            l_scr[gi, dst, :] = shift + jnp.log(den)
            return den

        def unrolled(fn):
            def body(it, carry):
                for u in range(ATTN_UNROLL):
                    fn(it * ATTN_UNROLL + u)
                return carry
            lax.fori_loop(0, n_blocks // ATTN_UNROLL, body, 0)

        unrolled(prep)

        def fast_body(it, dmin, block=block):
            for u in range(ATTN_UNROLL):
                dmin = jnp.minimum(dmin, block(it * ATTN_UNROLL + u, False))
            return dmin
        dmin = lax.fori_loop(0, n_blocks // ATTN_UNROLL, fast_body,
                             jnp.full((Q_BLOCK, LANES), 1.0, F32))

        @pl.when(jnp.logical_not(jnp.min(dmin) > DEN_SAFE))
        def _(block=block, unrolled=unrolled):
            unrolled(lambda b: block(b, True))

    l0 = l_scr[0]
    l1 = l_scr[1]
    l2 = l_scr[2]
    m = jnp.maximum(jnp.maximum(l0, l1), l2)
    e0 = jnp.exp(l0 - m)
    e1 = jnp.exp(l1 - m)
    e2 = jnp.exp(l2 - m)
    mix = (e0 * o_scr[0] + e1 * o_scr[1] + e2 * o_scr[2]) / (e0 + e1 + e2)
    out_ref[0] = mix.astype(BF16)


def _attention(qkv, rope_t, gq, gk):
    b, _, s, _ = qkv[0].shape
    in_specs = []
    args = []
    for gi, dil in enumerate(DILATIONS):
        for part in range(3):
            in_specs.append(pl.BlockSpec((1, dil, s // dil, LANES),
                                         lambda bi, hp, part=part: (bi, 0, 0, 2 * part + hp)))
            args.append(qkv[gi])
    consts = _attn_consts()
    tab_spec = pl.BlockSpec((1, 2, s, LANES), lambda bi, hp: (bi, 0, 0, 0))
    gain_spec = pl.BlockSpec((1, LANES), lambda bi, hp: (0, 0))
    const_specs = [pl.BlockSpec(c.shape, lambda bi, hp, nd=c.ndim: (0,) * nd) for c in consts]
    key_rows = max(d * (s // d + BAND_HALF) for d in DILATIONS) + BAND_HALF
    return pl.pallas_call(
        functools.partial(_attn_kernel, seq=s),
        grid=(b, 2),
        in_specs=in_specs + [tab_spec] + [gain_spec] * 2 + const_specs,
        out_specs=pl.BlockSpec((1, s, LANES), lambda bi, hp: (bi, 0, hp)),
        out_shape=jax.ShapeDtypeStruct((b, s, GROUP_W), BF16),
        scratch_shapes=[
            pltpu.VMEM((2 * s, LANES), BF16),
            pltpu.VMEM((key_rows, LANES), BF16),
            pltpu.VMEM((key_rows, LANES), BF16),
            pltpu.VMEM((3, s, LANES), F32),
            pltpu.VMEM((3, s, LANES), F32),
            pltpu.VMEM((4, 2 * Q_BLOCK, ATTN_WIN), F32),
        ],
        compiler_params=pltpu.CompilerParams(
            dimension_semantics=("parallel", "parallel"), vmem_limit_bytes=VMEM_LIMIT_BYTES),
        name="dilated_attn",
    )(*args, rope_t, gq, gk, *consts)


def _merge_kernel(x_ref, oa_ref, u_ref, uprev_ref, unext_ref, cb_ref, qm_ref, km_ref, vm_ref,
                  gates_ref, wconv_ref, wpa_ref, wpc_ref, wpm_ref, wout_ref, gffn_ref, wr_ref, br_ref,
                  x1_ref, route_ref, count_ref, om_scr, *, tm, nt):
    i = pl.program_id(1)
    u = u_ref[0].astype(F32)
    prev_row = uprev_ref[0].astype(F32)[15:16, :] * (i > 0).astype(F32)
    next_row = unext_ref[0].astype(F32)[0:1, :] * (i < nt - 1).astype(F32)
    row = lax.broadcasted_iota(jnp.int32, (tm, 1), 0)
    um = jnp.where(row == 0, prev_row, pltpu.roll(u, 1, 0))
    up = jnp.where(row == tm - 1, next_row, pltpu.roll(u, tm - 1, 0))
    wc = wconv_ref[...]
    y = wc[0:1, :] * um + wc[1:2, :] * u + wc[2:3, :] * up
    z = (cb_ref[0].astype(F32) * y).astype(BF16)

    for hh in range(MEM_HEADS):
        sl = slice(hh * MEM_HEAD_DIM, (hh + 1) * MEM_HEAD_DIM)
        s = lax.dot_general(qm_ref[0, :, sl], km_ref[0, :, sl], (((1,), (1,)), ((), ())),
                            preferred_element_type=F32) * (MEM_HEAD_DIM ** -0.5)
        m = jnp.max(s, axis=-1, keepdims=True)
        p = jnp.exp(s - m)
        den = jnp.sum(p, axis=-1, keepdims=True)
        o = jnp.dot(p.astype(BF16), vm_ref[0, :, sl], preferred_element_type=F32) / den
        om_scr[:, sl] = o.astype(BF16)

    pa = jnp.dot(oa_ref[0], wpa_ref[...], preferred_element_type=F32)
    pc = jnp.dot(z, wpc_ref[...], preferred_element_type=F32)
    pm = jnp.dot(om_scr[...], wpm_ref[...], preferred_element_type=F32)
    merged = (gates_ref[0, :, 0:D_MODEL].astype(F32) * pa
              + gates_ref[0, :, D_MODEL:2 * D_MODEL].astype(F32) * pc
              + gates_ref[0, :, 2 * D_MODEL:3 * D_MODEL].astype(F32) * pm)
    x1 = x_ref[0] + jnp.dot(merged.astype(BF16), wout_ref[...], preferred_element_type=F32)
    for c in range(ROW_CHUNKS):
        x1_ref[pl.ds(c, tm, stride=ROW_CHUNKS), :] = x1[:, c * LANES:(c + 1) * LANES]

    h2 = _rms(x1, gffn_ref[...])
    h_hi = h2.astype(BF16)
    h_lo = (h2 - h_hi.astype(F32)).astype(BF16)
    hh = jnp.dot(h_hi, wr_ref[...], preferred_element_type=F32)
    lg = (hh[:, 0:LANES] + hh[:, LANES:2 * LANES]
          + jnp.dot(h_lo, wr_ref[:, 0:LANES], preferred_element_type=F32)) + br_ref[...]

    lane = lax.broadcasted_iota(jnp.int32, (tm, LANES), 1).astype(F32)
    far = float(LANES)
    is_group = lane < N_EXPERT_GROUPS
    gl = jnp.where(is_group, lg, ROUTE_NEG)
    mg = jnp.max(gl, axis=-1, keepdims=True)
    gidx = jnp.min(jnp.where(gl == mg, lane, far), axis=-1, keepdims=True)
    pg_top = 1.0 / jnp.sum(jnp.where(is_group, jnp.exp(lg - mg), 0.0), axis=-1, keepdims=True)
    first = N_EXPERT_GROUPS + EXPERTS_PER_GROUP * gidx
    in_sel = jnp.logical_and(lane >= first, lane < first + EXPERTS_PER_GROUP)
    sel = jnp.where(in_sel, lg, ROUTE_NEG)
    m1 = jnp.max(sel, axis=-1, keepdims=True)
    i1 = jnp.min(jnp.where(jnp.logical_and(in_sel, sel == m1), lane, far), axis=-1, keepdims=True)
    keep = jnp.logical_and(in_sel, lane != i1)
    sel2 = jnp.where(keep, lg, ROUTE_NEG)
    m2 = jnp.max(sel2, axis=-1, keepdims=True)
    i2 = jnp.min(jnp.where(jnp.logical_and(keep, sel2 == m2), lane, far), axis=-1, keepdims=True)
    r = jnp.exp(m2 - m1)
    w1 = pg_top / (1.0 + r)
    w2 = w1 * r
    route = jnp.where(lane == 0, i1 - N_EXPERT_GROUPS,
                      jnp.where(lane == 1, i2 - N_EXPERT_GROUPS,
                                jnp.where(lane == 2, w1, jnp.where(lane == 3, w2, 0.0))))
    route_ref[...] = jnp.transpose(route)[0:8, :]

    @pl.when(jnp.logical_and(pl.program_id(0) == 0, i == 0))
    def _():
        count_ref[...] = jnp.zeros_like(count_ref)
    picked = jnp.where(jnp.logical_or(lane == i1, lane == i2), 1.0, 0.0)
    count_ref[...] += jnp.sum(picked, axis=0, keepdims=True)


def _merge(x, o_attn, u, cb, qm, km, vm, gates, w_conv, wpa, wpc, wpm, wout, g_ffn, w_r, b_r):
    b, s, d = x.shape
    tm = MERGE_TILE
    nt = s // tm
    halo = 16
    hb = tm // halo

    def tile(width):
        return pl.BlockSpec((1, tm, width), lambda bi, i: (bi, i, 0))

    def whole(arr):
        return pl.BlockSpec(arr.shape, lambda bi, i: (0,) * arr.ndim)

    def per_batch(arr):
        return pl.BlockSpec((1,) + arr.shape[1:], lambda bi, i: (bi, 0, 0))

    in_specs = [
        tile(d), tile(GROUP_W), tile(CONV_WIDTH),
        pl.BlockSpec((1, halo, CONV_WIDTH), lambda bi, i: (bi, jnp.maximum(i * hb - 1, 0), 0)),
        pl.BlockSpec((1, halo, CONV_WIDTH),
                     lambda bi, i: (bi, jnp.minimum((i + 1) * hb, s // halo - 1), 0)),
        tile(CONV_WIDTH), tile(MEM_WIDTH), per_batch(km), per_batch(vm), tile(N_BRANCH * d),
        whole(w_conv), whole(wpa), whole(wpc), whole(wpm), whole(wout), whole(g_ffn),
        whole(w_r), whole(b_r),
    ]
    return pl.pallas_call(
        functools.partial(_merge_kernel, tm=tm, nt=nt),
        grid=(b, nt),
        in_specs=in_specs,
        out_specs=[pl.BlockSpec((tm * ROW_CHUNKS, LANES), lambda bi, i: (bi * nt + i, 0)),
                   pl.BlockSpec((ROUTE_ROWS, tm), lambda bi, i: (0, bi * nt + i)),
                   pl.BlockSpec((ROUTE_ROWS, LANES), lambda bi, i: (0, 0))],
        out_shape=[jax.ShapeDtypeStruct((b * s * ROW_CHUNKS, LANES), F32),
                   jax.ShapeDtypeStruct((ROUTE_ROWS, b * s), F32),
                   jax.ShapeDtypeStruct((ROUTE_ROWS, LANES), F32)],
        scratch_shapes=[pltpu.VMEM((tm, MEM_WIDTH), BF16)],
        compiler_params=pltpu.CompilerParams(
            dimension_semantics=("arbitrary", "arbitrary"), vmem_limit_bytes=VMEM_LIMIT_BYTES),
        name="merge",
    )(x, o_attn, u, u, u, cb, qm, km, vm, gates, w_conv, wpa, wpc, wpm, wout, g_ffn, w_r, b_r)


DRAIN_STEPS = 2


def _expert_kernel(blk_exp_ref, nvalid_ref, tok_first_ref, tok_ahead_ref, dst_prev_ref, x1_hbm,
                   gffn_ref, wg_ref, wu_ref, wd_ref, y_hbm, xbuf, hbuf, ybuf, wg_s, wu_s, wd_s,
                   gsem, ssem):
    i = pl.program_id(0)
    slot = lax.rem(i, 2)
    gslot = lax.rem(i, GATHER_SLOTS)
    rows = MOE_BLOCK * ROW_CHUNKS

    def gather_copy(idx_ref, b, j, s, prio):
        src_row = pl.multiple_of(idx_ref[b, 0, j], ROW_CHUNKS)
        return pltpu.make_async_copy(x1_hbm.at[pl.ds(src_row, ROW_CHUNKS), :],
                                     xbuf.at[s, pl.ds(j * ROW_CHUNKS, ROW_CHUNKS), :], gsem.at[s]
                                     ).start(priority=prio)

    def scatter_copy(j, s, prio):
        dst_row = pl.multiple_of(dst_prev_ref[0, 0, j], ROW_CHUNKS)
        return pltpu.make_async_copy(ybuf.at[s, pl.ds(j * ROW_CHUNKS, ROW_CHUNKS), :],
                                     y_hbm.at[pl.ds(dst_row, ROW_CHUNKS), :], ssem.at[s]
                                     ).start(priority=prio)

    def looped(issue):
        def body(it, c):
            for u in range(GATHER_UNROLL):
                issue(it * GATHER_UNROLL + u, u % 2)
            return c
        lax.fori_loop(0, MOE_BLOCK // GATHER_UNROLL, body, 0)

    def wait_gather(s):
        pltpu.make_async_copy(x1_hbm.at[pl.ds(0, rows), :], xbuf.at[s], gsem.at[s]).wait()

    def wait_scatter(s):
        pltpu.make_async_copy(ybuf.at[s], y_hbm.at[pl.ds(0, rows), :], ssem.at[s]).wait()

    nvalid = nvalid_ref[i]
    prev = jnp.maximum(i - 1, 0)
    prev2 = jnp.maximum(i - 2, 0)
    prev_active = jnp.logical_and(i >= 1, nvalid_ref[prev] > 0)

    @pl.when(i == 0)
    def _():
        looped(lambda j, prio: gather_copy(tok_first_ref, 0, j, 0, prio))
        looped(lambda j, prio: gather_copy(tok_first_ref, 1, j, 1, prio))
        ybuf[...] = jnp.zeros_like(ybuf)
        spare = pltpu.make_async_copy(ybuf.at[0], y_hbm.at[pl.ds(y_hbm.shape[0] - rows, rows), :],
                                      ssem.at[0])
        spare.start()
        spare.wait()

    @pl.when(prev_active)
    def _():
        wait_scatter(slot)

    @pl.when(jnp.logical_or(i == 0, blk_exp_ref[i] != blk_exp_ref[prev]))
    def _():
        wg_s[...] = wg_ref[0].astype(BF16)
        wu_s[...] = wu_ref[0].astype(BF16)
        wd_s[...] = wd_ref[0].astype(BF16)

    @pl.when(nvalid > 0)
    def _():
        wait_gather(gslot)
        ss = None
        for c in range(ROW_CHUNKS):
            xc = xbuf[gslot, pl.ds(c, MOE_BLOCK, stride=ROW_CHUNKS), :]
            ss = xc * xc if ss is None else ss + xc * xc
        scale = lax.rsqrt(jnp.sum(ss, axis=-1, keepdims=True) * (1.0 / D_MODEL) + EPS)
        for c in range(ROW_CHUNKS):
            sl = slice(c * LANES, (c + 1) * LANES)
            xc = xbuf[gslot, pl.ds(c, MOE_BLOCK, stride=ROW_CHUNKS), :]
            hbuf[:, sl] = (xc * scale * gffn_ref[:, sl]).astype(BF16)
        h2 = hbuf[...]
        a = jnp.dot(h2, wg_s[...], preferred_element_type=F32)
        up = jnp.dot(h2, wu_s[...], preferred_element_type=F32)
        act = (a * jax.nn.sigmoid(a) * up).astype(BF16)
        ahead = lax.rem(i + 2, GATHER_SLOTS)
        for j in range(MOE_BLOCK):
            gather_copy(tok_ahead_ref, 0, j, ahead, j % 2)
        for j in range(MOE_BLOCK):
            scatter_copy(j, 1 - slot, j % 2)
        y = jnp.dot(act, wd_s[...], preferred_element_type=F32)
        for c in range(ROW_CHUNKS):
            ybuf[slot, pl.ds(c, MOE_BLOCK, stride=ROW_CHUNKS), :] = y[:, c * LANES:(c + 1) * LANES]

    @pl.when(jnp.logical_and(nvalid == 0, prev_active))
    def _():
        looped(lambda j, prio: scatter_copy(j, 1 - slot, prio))
        wait_scatter(1 - slot)

    @pl.when(jnp.logical_and(nvalid == 0, jnp.logical_or(i < 2, nvalid_ref[prev2] > 0)))
    def _():
        wait_gather(gslot)


def _experts(x1r, g_ffn, blk_exp, nvalid, row_tok, row_dst_prev, wg, wu, wd):
    rows_total, _ = x1r.shape
    d = D_MODEL
    n_blk = blk_exp.shape[0]
    idx_spec = lambda fn: pl.BlockSpec((1, 1, MOE_BLOCK), fn, memory_space=pltpu.SMEM)
    grid_spec = pltpu.PrefetchScalarGridSpec(
        num_scalar_prefetch=2,
        grid=(n_blk,),
        in_specs=[
            pl.BlockSpec((2, 1, MOE_BLOCK), lambda i, be, nv: (0, 0, 0), memory_space=pltpu.SMEM),
            idx_spec(lambda i, be, nv: (jnp.minimum(i + 2, n_blk - 1), 0, 0)),
            idx_spec(lambda i, be, nv: (i, 0, 0)),
            pl.BlockSpec(memory_space=pl.ANY),
            pl.BlockSpec((1, d), lambda i, be, nv: (0, 0)),
            pl.BlockSpec((1, d, EXPERT_FF), lambda i, be, nv: (be[i], 0, 0)),
            pl.BlockSpec((1, d, EXPERT_FF), lambda i, be, nv: (be[i], 0, 0)),
            pl.BlockSpec((1, EXPERT_FF, d), lambda i, be, nv: (be[i], 0, 0)),
        ],
        out_specs=pl.BlockSpec(memory_space=pl.ANY),
        scratch_shapes=[
            pltpu.VMEM((GATHER_SLOTS, MOE_BLOCK * ROW_CHUNKS, LANES), F32),
            pltpu.VMEM((MOE_BLOCK, d), BF16),
            pltpu.VMEM((2, MOE_BLOCK * ROW_CHUNKS, LANES), F32),
            pltpu.VMEM((d, EXPERT_FF), BF16),
            pltpu.VMEM((d, EXPERT_FF), BF16),
            pltpu.VMEM((EXPERT_FF, d), BF16),
            pltpu.SemaphoreType.DMA((GATHER_SLOTS,)),
            pltpu.SemaphoreType.DMA((2,)),
        ],
    )
    return pl.pallas_call(
        _expert_kernel,
        grid_spec=grid_spec,
        out_shape=jax.ShapeDtypeStruct((TOP_K * rows_total + MOE_BLOCK * ROW_CHUNKS, LANES), F32),
        compiler_params=pltpu.CompilerParams(
            dimension_semantics=("arbitrary",), vmem_limit_bytes=VMEM_LIMIT_BYTES),
        name="experts",
    )(blk_exp, nvalid, row_tok, row_tok, row_dst_prev, x1r, g_ffn, wg, wu, wd)


def _combine_kernel(x1_ref, y0_ref, y1_ref, route_ref, o_ref, *, tm):
    w_cols = jnp.transpose(route_ref[...])
    w0 = w_cols[:, 2:3]
    w1 = w_cols[:, 3:4]
    for c in range(ROW_CHUNKS):
        rows = pl.ds(c, tm, stride=ROW_CHUNKS)
        o_ref[:, c * LANES:(c + 1) * LANES] = (x1_ref[rows, :] + w0 * y0_ref[rows, :]
                                               + w1 * y1_ref[rows, :])


def _combine(x1r, y, route_t):
    n = x1r.shape[0] // ROW_CHUNKS
    tm = ROW_TILE
    nt = n // tm
    blk = (tm * ROW_CHUNKS, LANES)
    return pl.pallas_call(
        functools.partial(_combine_kernel, tm=tm),
        grid=(nt,),
        in_specs=[pl.BlockSpec(blk, lambda i: (i, 0)),
                  pl.BlockSpec(blk, lambda i: (i, 0)),
                  pl.BlockSpec(blk, lambda i: (i + nt, 0)),
                  pl.BlockSpec((ROUTE_ROWS, tm), lambda i: (0, i))],
        out_specs=pl.BlockSpec((tm, D_MODEL), lambda i: (i, 0)),
        out_shape=jax.ShapeDtypeStruct((n, D_MODEL), F32),
        compiler_params=pltpu.CompilerParams(dimension_semantics=("parallel",)),
        name="combine",
    )(x1r, y, y, route_t)


def _rope_tables(positions):
    half = ROT_DIM // 2
    lane = np.arange(LANES)
    dim = lane % HEAD_DIM
    rot = dim < ROT_DIM
    spread = np.zeros((ROT_DIM, 2, LANES), np.float32)
    spread[dim[rot] % half, 0, lane[rot]] = 1.0
    spread[half + dim[rot] % half, 1, lane[rot]] = np.where(dim[rot] < half, -1.0, 1.0)
    plain = np.stack([(~rot).astype(np.float32), np.zeros(LANES, np.float32)])[:, None, :]
    inv = ROPE_THETA ** (-jnp.arange(0, ROT_DIM, 2, dtype=F32) / ROT_DIM)
    ang = inv[None, :, None] * positions.astype(F32)[:, None, :]
    both = jnp.concatenate([jnp.cos(ang), jnp.sin(ang)], axis=1)
    return jnp.einsum('bjs,jtl->btsl', both, jnp.asarray(spread),
                      precision=lax.Precision.HIGHEST) + jnp.asarray(plain)


def _route(route_t, counts, n):
    nk = n * TOP_K
    n_blk = nk // MOE_BLOCK + N_EXPERTS + DRAIN_STEPS
    n_rows = n_blk * MOE_BLOCK
    n_fill = n_rows - nk
    experts = jnp.arange(N_EXPERTS, dtype=jnp.int32)
    padded = ((counts + MOE_BLOCK - 1) // MOE_BLOCK) * MOE_BLOCK
    pend = jnp.cumsum(padded)
    pstart = pend - padded
    fill_end = jnp.cumsum(padded - counts)
    f = jnp.arange(n_fill, dtype=jnp.int32)
    fill_exp = jnp.sum((fill_end[None, :] <= f[:, None]).astype(jnp.int32), axis=1)
    e = route_t[0:TOP_K].reshape(nk).astype(jnp.int32)
    keys = jnp.concatenate([2 * e, 2 * fill_exp + 1])
    vals = jnp.concatenate([jnp.arange(nk, dtype=jnp.int32), jnp.full((n_fill,), -1, jnp.int32)])
    _, a_p = lax.sort((keys, vals), num_keys=1, is_stable=True)
    a_p = a_p.reshape(n_blk, MOE_BLOCK)
    valid = a_p >= 0

    blk_start = jnp.arange(n_blk, dtype=jnp.int32) * MOE_BLOCK
    blk_exp = jnp.minimum(jnp.sum((pend[None, :] <= blk_start[:, None]).astype(jnp.int32), axis=1),
                          N_EXPERTS - 1).astype(jnp.int32)
    onehot = blk_exp[:, None] == experts[None, :]
    pick = lambda table: jnp.sum(jnp.where(onehot, table[None, :], 0), axis=1)
    nvalid = jnp.clip(pick(counts) - (blk_start - pick(pstart)), 0, MOE_BLOCK)
    nvalid = jnp.where(blk_start < pend[-1], nvalid, 0).astype(jnp.int32)
    j = jnp.arange(MOE_BLOCK, dtype=jnp.int32)[None, :]
    tok = (jnp.where(valid, a_p - n * (a_p >= n).astype(jnp.int32), j) * ROW_CHUNKS).astype(jnp.int32)
    dst = (jnp.where(valid, a_p, TOP_K * n + j) * ROW_CHUNKS).astype(jnp.int32)
    spare = jnp.broadcast_to((TOP_K * n + j) * ROW_CHUNKS, (1, MOE_BLOCK)).astype(jnp.int32)
    dst_prev = jnp.concatenate([spare, dst[:-1]], axis=0)
    return blk_exp, nvalid, tok.reshape(n_blk, 1, MOE_BLOCK), dst_prev.reshape(n_blk, 1, MOE_BLOCK)


def kernel(x, mem, positions, g_mix, g_mem, w_in, g_qn_attn, g_kn_attn, w_conv, w_mem_kv, g_qn_mem,
           g_kn_mem, w_proj_attn, w_proj_conv, w_proj_mem, w_out, g_ffn, w_router_group,
           b_router_group, w_router_expert, b_router_expert, w_gate, w_up, w_down):
    b, s, d = x.shape
    n = b * s
    rope_t = _rope_tables(positions)
    for l in range(w_in.shape[0]):
        wl = w_in[l]
        aw = N_BRANCH * GROUP_W
        cols = []
        for gi in range(len(DILATIONS)):
            for part in range(3):
                c0 = part * aw + gi * GROUP_W
                cols.append(wl[:, c0:c0 + GROUP_W])
        cols.append(wl[:, 3 * aw:])
        w_all = jnp.concatenate(cols, axis=1).astype(BF16)

        q0, q1, q2, u, cb, qm, gates = _inproj(x, g_mix[l][None], g_qn_mem[l][None], w_all)
        km, vm = _memkv(mem, g_mem[l][None], g_kn_mem[l][None], w_mem_kv[l].astype(BF16))
        gq = jnp.tile(g_qn_attn[l], 2)[None]
        gk = jnp.tile(g_kn_attn[l], 2)[None]
        o_attn = _attention((q0, q1, q2), rope_t, gq, gk)

        w_r = jnp.zeros((d, LANES), F32)
        w_r = w_r.at[:, :N_EXPERT_GROUPS].set(w_router_group[l])
        w_r = w_r.at[:, N_EXPERT_GROUPS:N_EXPERT_GROUPS + N_EXPERTS].set(w_router_expert[l])
        b_r = jnp.zeros((1, LANES), F32)
        b_r = b_r.at[0, :N_EXPERT_GROUPS].set(b_router_group[l])
        b_r = b_r.at[0, N_EXPERT_GROUPS:N_EXPERT_GROUPS + N_EXPERTS].set(b_router_expert[l])
        w_r_hi = w_r.astype(BF16)
        w_r_lo = (w_r - w_r_hi.astype(F32)).astype(BF16)
        x1, route_t, count_t = _merge(x, o_attn, u, cb, qm, km, vm, gates, w_conv[l],
                                      w_proj_attn[l].astype(BF16), w_proj_conv[l].astype(BF16),
                                      w_proj_mem[l].astype(BF16), w_out[l].astype(BF16),
                                      g_ffn[l][None], jnp.concatenate([w_r_hi, w_r_lo], axis=1), b_r)

        counts = count_t[0, N_EXPERT_GROUPS:N_EXPERT_GROUPS + N_EXPERTS].astype(jnp.int32)
        blk_exp, nvalid, row_tok, row_dst_prev = _route(route_t, counts, n)
        y = _experts(x1, g_ffn[l][None], blk_exp, nvalid, row_tok, row_dst_prev,
                     w_gate[l], w_up[l], w_down[l])
        x = _combine(x1, y, route_t).reshape(b, s, d)
    return x
```

```python
import functools

import numpy as np
import jax
import jax.numpy as jnp
from jax import lax
from jax.experimental import pallas as pl
from jax.experimental.pallas import tpu as pltpu

F32 = jnp.float32
BF16 = jnp.bfloat16

D_MODEL = 1024
EPS = 1e-6
HEAD_DIM = 64
DILATIONS = (1, 4, 16)
BAND_HALF = 64
ATTN_SLOTS = 4
GROUP_W = ATTN_SLOTS * HEAD_DIM
QKV_W = 3 * GROUP_W
ROT_DIM = 16
ROPE_THETA = 500000.0
CONV_WIDTH = 768
MEM_HEADS = 4
MEM_HEAD_DIM = 128
MEM_WIDTH = 512
N_BRANCH = 3
N_EXPERT_GROUPS = 4
EXPERTS_PER_GROUP = 8
N_EXPERTS = 32
TOP_K = 2
EXPERT_FF = 512

LANES = 128
VMEM_LIMIT_BYTES = 56 * 1024 * 1024

ROW_TILE = 512
MERGE_TILE = 1024
Q_BLOCK = 128
MOE_BLOCK = 256
ROW_CHUNKS = D_MODEL // LANES
GATHER_UNROLL = 16
GATHER_SLOTS = 3
NEG_BIG = -1e30
ROUTE_NEG = -3e38
ROUTE_ROWS = 8


def _rms(t, gain):
    return t * lax.rsqrt(jnp.mean(t * t, axis=-1, keepdims=True) + EPS) * gain


def _inproj_kernel(x_ref, g_ref, gqm_ref, w_ref,
                   qkv0_ref, qkv1_ref, qkv2_ref, u_ref, cb_ref, qm_ref, gates_ref,
                   h_scr, hp_scr, *, tm):
    x = x_ref[0]
    h = _rms(x, g_ref[...])
    n_chunks = h.shape[1] // LANES
    for c in range(n_chunks):
        h_scr[c] = h[:, c * LANES:(c + 1) * LANES]
    hb = h.astype(BF16)

    def mm(lhs, c0, c1):
        return jnp.dot(lhs, w_ref[:, c0:c1], preferred_element_type=F32)

    qkv0_ref[0, 0] = mm(hb, 0, QKV_W).astype(BF16)
    for gi, out_ref in ((1, qkv1_ref), (2, qkv2_ref)):
        d = DILATIONS[gi]
        rows = tm // d
        for r in range(d):
            for c in range(n_chunks):
                hp_scr[r * rows:(r + 1) * rows, c * LANES:(c + 1) * LANES] = (
                    h_scr[c, pl.ds(r, rows, stride=d), :].astype(BF16))
        res = mm(hp_scr[...], gi * QKV_W, (gi + 1) * QKV_W)
        out_ref[0] = res.reshape(d, rows, QKV_W).astype(BF16)

    c = 3 * QKV_W
    cx = mm(hb, c, c + CONV_WIDTH)
    cc = mm(hb, c + 2 * CONV_WIDTH, c + 3 * CONV_WIDTH)
    u_ref[0] = (cc * cx).astype(BF16)
    cb_ref[0] = mm(hb, c + CONV_WIDTH, c + 2 * CONV_WIDTH).astype(BF16)

    c += 3 * CONV_WIDTH
    qm = mm(hb, c, c + MEM_WIDTH)
    for hh in range(MEM_HEADS):
        sl = slice(hh * MEM_HEAD_DIM, (hh + 1) * MEM_HEAD_DIM)
        qm_ref[0, :, sl] = _rms(qm[:, sl], gqm_ref[...]).astype(BF16)

    c += MEM_WIDTH
    for j in range(N_BRANCH):
        gl = mm(hb, c + j * D_MODEL, c + (j + 1) * D_MODEL)
        gates_ref[0, :, j * D_MODEL:(j + 1) * D_MODEL] = jax.nn.sigmoid(gl).astype(BF16)


def _inproj(x, g_mix, g_qn_mem, w_all):
    b, s, d = x.shape
    tm = ROW_TILE
    nt = s // tm
    outs = []
    out_specs = []
    for dil in DILATIONS:
        outs.append(jax.ShapeDtypeStruct((b, dil, s // dil, QKV_W), BF16))
        out_specs.append(pl.BlockSpec((1, dil, tm // dil, QKV_W), lambda bi, i: (bi, 0, i, 0)))
    for width in (CONV_WIDTH, CONV_WIDTH, MEM_WIDTH, N_BRANCH * D_MODEL):
        outs.append(jax.ShapeDtypeStruct((b, s, width), BF16))
        out_specs.append(pl.BlockSpec((1, tm, width), lambda bi, i: (bi, i, 0)))
    return pl.pallas_call(
        functools.partial(_inproj_kernel, tm=tm),
        grid=(b, nt),
        in_specs=[
            pl.BlockSpec((1, tm, d), lambda bi, i: (bi, i, 0)),
            pl.BlockSpec((1, d), lambda bi, i: (0, 0)),
            pl.BlockSpec((1, MEM_HEAD_DIM), lambda bi, i: (0, 0)),
            pl.BlockSpec(w_all.shape, lambda bi, i: (0, 0), pipeline_mode=pl.Buffered(1)),
        ],
        out_specs=out_specs,
        out_shape=outs,
        scratch_shapes=[pltpu.VMEM((d // LANES, tm, LANES), F32), pltpu.VMEM((tm, d), BF16)],
        compiler_params=pltpu.CompilerParams(
            dimension_semantics=("parallel", "parallel"), vmem_limit_bytes=VMEM_LIMIT_BYTES),
        name="inproj",
    )(x, g_mix, g_qn_mem, w_all)


def _memkv_kernel(mem_ref, g_ref, gk_ref, w_ref, k_ref, v_ref):
    h = _rms(mem_ref[0], g_ref[...]).astype(BF16)
    kv = jnp.dot(h, w_ref[...], preferred_element_type=F32)
    for hh in range(MEM_HEADS):
        sl = slice(hh * MEM_HEAD_DIM, (hh + 1) * MEM_HEAD_DIM)
        k_ref[0, :, sl] = _rms(kv[:, sl], gk_ref[...]).astype(BF16)
    v_ref[0] = kv[:, MEM_WIDTH:].astype(BF16)


def _memkv(mem, g_mem, g_kn_mem, w_kv):
    b, m, d = mem.shape
    return pl.pallas_call(
        _memkv_kernel,
        grid=(b,),
        in_specs=[
            pl.BlockSpec((1, m, d), lambda bi: (bi, 0, 0)),
            pl.BlockSpec((1, d), lambda bi: (0, 0)),
            pl.BlockSpec((1, MEM_HEAD_DIM), lambda bi: (0, 0)),
            pl.BlockSpec(w_kv.shape, lambda bi: (0, 0)),
        ],
        out_specs=[pl.BlockSpec((1, m, MEM_WIDTH), lambda bi: (bi, 0, 0))] * 2,
        out_shape=[jax.ShapeDtypeStruct((b, m, MEM_WIDTH), BF16)] * 2,
        compiler_params=pltpu.CompilerParams(dimension_semantics=("parallel",)),
        name="memkv",
    )(mem, g_mem, g_kn_mem, w_kv)


ATTN_WIN = Q_BLOCK + 2 * BAND_HALF
ATTN_UNROLL = 16
DEN_SAFE = 1e-30


def _attn_consts():
    lane = np.arange(LANES)
    gsum = (lane[:, None] // HEAD_DIM == lane[None, :] // HEAD_DIM).astype(np.float32)
    half = ROT_DIM // 2
    dim = lane % HEAD_DIM
    src = np.where(dim < half, lane + half, np.where(dim < ROT_DIM, lane - half, -1))
    pswap = (lane[:, None] == src[None, :]).astype(np.float32)
    i = np.arange(Q_BLOCK)[:, None]
    c = np.arange(ATTN_WIN)[None, :]
    band = (c >= i) & (c <= i + 2 * BAND_HALF)
    first = c >= BAND_HALF
    last = c < Q_BLOCK + BAND_HALF
    variants = [band, band & first, band & last, band & first & last]
    bias = np.stack([np.where(v, 0.0, NEG_BIG) for v in variants]).astype(np.float32)
    return jnp.asarray(gsum, BF16), jnp.asarray(pswap, BF16), jnp.asarray(bias, F32)


def _attn_kernel(q0, k0, v0, q1, k1, v1, q2, k2, v2, rope_ref, gq_ref, gk_ref,
                 gsum_ref, pswap_ref, bias_ref, out_ref, qs, ks, vs, o_scr, l_scr, bias_scr, *, seq):
    lane = lax.broadcasted_iota(jnp.int32, (1, LANES), 1)
    low_half = lane < HEAD_DIM
    pad = BAND_HALF
    n_blocks = seq // Q_BLOCK

    bound = HEAD_DIM ** 0.5 * jnp.max(jnp.abs(gq_ref[...])) * jnp.max(jnp.abs(gk_ref[...]))
    shifted = bias_ref[...] - bound
    bias_scr[:, 0:Q_BLOCK, :] = shifted
    bias_scr[:, Q_BLOCK:2 * Q_BLOCK, :] = shifted

    def norm_rope(t, gain, cos, sin):
        ms = jnp.dot((t * t).astype(BF16), gsum_ref[...], preferred_element_type=F32) * (1.0 / HEAD_DIM)
        tn = t * lax.rsqrt(ms + EPS) * gain
        partner = jnp.dot(tn.astype(BF16), pswap_ref[...], preferred_element_type=F32)
        return tn * cos + partner * sin

    for gi, (q_ref, k_ref, v_ref) in enumerate(((q0, k0, v0), (q1, k1, v1), (q2, k2, v2))):
        d = DILATIONS[gi]
        length = seq // d
        nblk = length // Q_BLOCK
        region = length + pad

        zpad = jnp.zeros((pad, LANES), BF16)
        for r in range(d + 1):
            ks[r * region:r * region + pad, :] = zpad
            vs[r * region:r * region + pad, :] = zpad

        def split_index(b, nblk=nblk):
            if nblk == 1:
                return b, 0
            r = b // nblk
            return r, b - r * nblk

        def table_rows(r, n, d=d):
            if d == 1:
                return pl.ds(pl.multiple_of(n * Q_BLOCK, Q_BLOCK), Q_BLOCK)
            return pl.ds(r + d * n * Q_BLOCK, Q_BLOCK, stride=d)

        def prep(b, q_ref=q_ref, k_ref=k_ref, v_ref=v_ref, split_index=split_index,
                 table_rows=table_rows):
            r, n = split_index(b)
            src = pl.ds(pl.multiple_of(n * Q_BLOCK, Q_BLOCK), Q_BLOCK)
            rows = table_rows(r, n)
            cos = rope_ref[0, 0, rows, :]
            sin = rope_ref[0, 1, rows, :]
            qn = norm_rope(q_ref[0, r, src, :].astype(F32), gq_ref[...], cos, sin) * (HEAD_DIM ** -0.5)
            kn = norm_rope(k_ref[0, r, src, :].astype(F32), gk_ref[...], cos, sin)
            qbase = pl.multiple_of(b * 2 * Q_BLOCK, 2 * Q_BLOCK)
            qs[pl.ds(qbase, Q_BLOCK), :] = jnp.where(low_half, qn, 0.0).astype(BF16)
            qs[pl.ds(qbase + Q_BLOCK, Q_BLOCK), :] = jnp.where(low_half, 0.0, qn).astype(BF16)
            kdst = pl.ds(pl.multiple_of(b * Q_BLOCK + (r + 1) * pad, pad), Q_BLOCK)
            ks[kdst, :] = kn.astype(BF16)
            vs[kdst, :] = v_ref[0, r, src, :]

        def block(b, exact, gi=gi, nblk=nblk, split_index=split_index, table_rows=table_rows):
            r, n = split_index(b)
            variant = 3 if nblk == 1 else jnp.where(n == 0, 1, 0) + jnp.where(n == nblk - 1, 2, 0)
            qsrc = pl.ds(pl.multiple_of(b * 2 * Q_BLOCK, 2 * Q_BLOCK), 2 * Q_BLOCK)
            ksrc = pl.ds(pl.multiple_of(b * Q_BLOCK + r * pad, pad), ATTN_WIN)
            s = lax.dot_general(qs[qsrc, :], ks[ksrc, :], (((1,), (1,)), ((), ())),
                                preferred_element_type=F32) + bias_scr[variant]
            if exact:
                m = jnp.max(s, axis=-1, keepdims=True)
                s = s - m
                shift = jnp.where(low_half, m[0:Q_BLOCK], m[Q_BLOCK:]) + bound
            else:
                shift = bound
            p = jnp.exp(s)
            den = jnp.sum(p, axis=-1, keepdims=True)
            o2 = jnp.dot(p.astype(BF16), vs[ksrc, :], preferred_element_type=F32)
            o = jnp.where(low_half, o2[0:Q_BLOCK], o2[Q_BLOCK:])
            den = jnp.where(low_half, den[0:Q_BLOCK], den[Q_BLOCK:])
            dst = table_rows(r, n)
            o_scr[gi, dst, :] = o / den
            l_scr[gi, dst, :] = shift + jnp.log(den)
            return den

        def unrolled(fn):
            def body(it, carry):
                for u in range(ATTN_UNROLL):
                    fn(it * ATTN_UNROLL + u)
                return carry
            lax.fori_loop(0, n_blocks // ATTN_UNROLL, body, 0)

        unrolled(prep)

        def fast_body(it, dmin, block=block):
            for u in range(ATTN_UNROLL):
                dmin = jnp.minimum(dmin, block(it * ATTN_UNROLL + u, False))
            return dmin
        dmin = lax.fori_loop(0, n_blocks // ATTN_UNROLL, fast_body,
                             jnp.full((Q_BLOCK, LANES), 1.0, F32))

        @pl.when(jnp.logical_not(jnp.min(dmin) > DEN_SAFE))
        def _(block=block, unrolled=unrolled):
            unrolled(lambda b: block(b, True))

    l0 = l_scr[0]
    l1 = l_scr[1]
    l2 = l_scr[2]
    m = jnp.maximum(jnp.maximum(l0, l1), l2)
    e0 = jnp.exp(l0 - m)
    e1 = jnp.exp(l1 - m)
    e2 = jnp.exp(l2 - m)
    mix = (e0 * o_scr[0] + e1 * o_scr[1] + e2 * o_scr[2]) / (e0 + e1 + e2)
    out_ref[0] = mix.astype(BF16)


def _attention(qkv, rope_t, gq, gk):
    b, _, s, _ = qkv[0].shape
    in_specs = []
    args = []
    for gi, dil in enumerate(DILATIONS):
        for part in range(3):
            in_specs.append(pl.BlockSpec((1, dil, s // dil, LANES),
                                         lambda bi, hp, part=part: (bi, 0, 0, 2 * part + hp)))
            args.append(qkv[gi])
    consts = _attn_consts()
    tab_spec = pl.BlockSpec((1, 2, s, LANES), lambda bi, hp: (bi, 0, 0, 0))
    gain_spec = pl.BlockSpec((1, LANES), lambda bi, hp: (0, 0))
    const_specs = [pl.BlockSpec(c.shape, lambda bi, hp, nd=c.ndim: (0,) * nd) for c in consts]
    key_rows = max(d * (s // d + BAND_HALF) for d in DILATIONS) + BAND_HALF
    return pl.pallas_call(
        functools.partial(_attn_kernel, seq=s),
        grid=(b, 2),
        in_specs=in_specs + [tab_spec] + [gain_spec] * 2 + const_specs,
        out_specs=pl.BlockSpec((1, s, LANES), lambda bi, hp: (bi, 0, hp)),
        out_shape=jax.ShapeDtypeStruct((b, s, GROUP_W), BF16),
        scratch_shapes=[
            pltpu.VMEM((2 * s, LANES), BF16),
            pltpu.VMEM((key_rows, LANES), BF16),
            pltpu.VMEM((key_rows, LANES), BF16),
            pltpu.VMEM((3, s, LANES), F32),
            pltpu.VMEM((3, s, LANES), F32),
            pltpu.VMEM((4, 2 * Q_BLOCK, ATTN_WIN), F32),
        ],
        compiler_params=pltpu.CompilerParams(
            dimension_semantics=("parallel", "parallel"), vmem_limit_bytes=VMEM_LIMIT_BYTES),
        name="dilated_attn",
    )(*args, rope_t, gq, gk, *consts)


def _merge_kernel(x_ref, oa_ref, u_ref, uprev_ref, unext_ref, cb_ref, qm_ref, km_ref, vm_ref,
                  gates_ref, wconv_ref, wpa_ref, wpc_ref, wpm_ref, wout_ref, gffn_ref, wr_ref, br_ref,
                  x1_ref, route_ref, count_ref, om_scr, *, tm, nt):
    i = pl.program_id(1)
    u = u_ref[0].astype(F32)
    prev_row = uprev_ref[0].astype(F32)[15:16, :] * (i > 0).astype(F32)
    next_row = unext_ref[0].astype(F32)[0:1, :] * (i < nt - 1).astype(F32)
    row = lax.broadcasted_iota(jnp.int32, (tm, 1), 0)
    um = jnp.where(row == 0, prev_row, pltpu.roll(u, 1, 0))
    up = jnp.where(row == tm - 1, next_row, pltpu.roll(u, tm - 1, 0))
    wc = wconv_ref[...]
    y = wc[0:1, :] * um + wc[1:2, :] * u + wc[2:3, :] * up
    z = (cb_ref[0].astype(F32) * y).astype(BF16)

    for hh in range(MEM_HEADS):
        sl = slice(hh * MEM_HEAD_DIM, (hh + 1) * MEM_HEAD_DIM)
        s = lax.dot_general(qm_ref[0, :, sl], km_ref[0, :, sl], (((1,), (1,)), ((), ())),
                            preferred_element_type=F32) * (MEM_HEAD_DIM ** -0.5)
        m = jnp.max(s, axis=-1, keepdims=True)
        p = jnp.exp(s - m)
        den = jnp.sum(p, axis=-1, keepdims=True)
        o = jnp.dot(p.astype(BF16), vm_ref[0, :, sl], preferred_element_type=F32) / den
        om_scr[:, sl] = o.astype(BF16)

    pa = jnp.dot(oa_ref[0], wpa_ref[...], preferred_element_type=F32)
    pc = jnp.dot(z, wpc_ref[...], preferred_element_type=F32)
    pm = jnp.dot(om_scr[...], wpm_ref[...], preferred_element_type=F32)
    merged = (gates_ref[0, :, 0:D_MODEL].astype(F32) * pa
              + gates_ref[0, :, D_MODEL:2 * D_MODEL].astype(F32) * pc
              + gates_ref[0, :, 2 * D_MODEL:3 * D_MODEL].astype(F32) * pm)
    x1 = x_ref[0] + jnp.dot(merged.astype(BF16), wout_ref[...], preferred_element_type=F32)
    for c in range(ROW_CHUNKS):
        x1_ref[pl.ds(c, tm, stride=ROW_CHUNKS), :] = x1[:, c * LANES:(c + 1) * LANES]

    h2 = _rms(x1, gffn_ref[...])
    h_hi = h2.astype(BF16)
    h_lo = (h2 - h_hi.astype(F32)).astype(BF16)
    hh = jnp.dot(h_hi, wr_ref[...], preferred_element_type=F32)
    lg = (hh[:, 0:LANES] + hh[:, LANES:2 * LANES]
          + jnp.dot(h_lo, wr_ref[:, 0:LANES], preferred_element_type=F32)) + br_ref[...]

    lane = lax.broadcasted_iota(jnp.int32, (tm, LANES), 1).astype(F32)
    far = float(LANES)
    is_group = lane < N_EXPERT_GROUPS
    gl = jnp.where(is_group, lg, ROUTE_NEG)
    mg = jnp.max(gl, axis=-1, keepdims=True)
    gidx = jnp.min(jnp.where(gl == mg, lane, far), axis=-1, keepdims=True)
    pg_top = 1.0 / jnp.sum(jnp.where(is_group, jnp.exp(lg - mg), 0.0), axis=-1, keepdims=True)
    first = N_EXPERT_GROUPS + EXPERTS_PER_GROUP * gidx
    in_sel = jnp.logical_and(lane >= first, lane < first + EXPERTS_PER_GROUP)
    sel = jnp.where(in_sel, lg, ROUTE_NEG)
    m1 = jnp.max(sel, axis=-1, keepdims=True)
    i1 = jnp.min(jnp.where(jnp.logical_and(in_sel, sel == m1), lane, far), axis=-1, keepdims=True)
    keep = jnp.logical_and(in_sel, lane != i1)
    sel2 = jnp.where(keep, lg, ROUTE_NEG)
    m2 = jnp.max(sel2, axis=-1, keepdims=True)
    i2 = jnp.min(jnp.where(jnp.logical_and(keep, sel2 == m2), lane, far), axis=-1, keepdims=True)
    r = jnp.exp(m2 - m1)
    w1 = pg_top / (1.0 + r)
    w2 = w1 * r
    route = jnp.where(lane == 0, i1 - N_EXPERT_GROUPS,
                      jnp.where(lane == 1, i2 - N_EXPERT_GROUPS,
                                jnp.where(lane == 2, w1, jnp.where(lane == 3, w2, 0.0))))
    route_ref[...] = jnp.transpose(route)[0:8, :]

    @pl.when(jnp.logical_and(pl.program_id(0) == 0, i == 0))
    def _():
        count_ref[...] = jnp.zeros_like(count_ref)
    picked = jnp.where(jnp.logical_or(lane == i1, lane == i2), 1.0, 0.0)
    count_ref[...] += jnp.sum(picked, axis=0, keepdims=True)


def _merge(x, o_attn, u, cb, qm, km, vm, gates, w_conv, wpa, wpc, wpm, wout, g_ffn, w_r, b_r):
    b, s, d = x.shape
    tm = MERGE_TILE
    nt = s // tm
    halo = 16
    hb = tm // halo

    def tile(width):
        return pl.BlockSpec((1, tm, width), lambda bi, i: (bi, i, 0))

    def whole(arr):
        return pl.BlockSpec(arr.shape, lambda bi, i: (0,) * arr.ndim)

    def per_batch(arr):
        return pl.BlockSpec((1,) + arr.shape[1:], lambda bi, i: (bi, 0, 0))

    in_specs = [
        tile(d), tile(GROUP_W), tile(CONV_WIDTH),
        pl.BlockSpec((1, halo, CONV_WIDTH), lambda bi, i: (bi, jnp.maximum(i * hb - 1, 0), 0)),
        pl.BlockSpec((1, halo, CONV_WIDTH),
                     lambda bi, i: (bi, jnp.minimum((i + 1) * hb, s // halo - 1), 0)),
        tile(CONV_WIDTH), tile(MEM_WIDTH), per_batch(km), per_batch(vm), tile(N_BRANCH * d),
        whole(w_conv), whole(wpa), whole(wpc), whole(wpm), whole(wout), whole(g_ffn),
        whole(w_r), whole(b_r),
    ]
    return pl.pallas_call(
        functools.partial(_merge_kernel, tm=tm, nt=nt),
        grid=(b, nt),
        in_specs=in_specs,
        out_specs=[pl.BlockSpec((tm * ROW_CHUNKS, LANES), lambda bi, i: (bi * nt + i, 0)),
                   pl.BlockSpec((ROUTE_ROWS, tm), lambda bi, i: (0, bi * nt + i)),
                   pl.BlockSpec((ROUTE_ROWS, LANES), lambda bi, i: (0, 0))],
        out_shape=[jax.ShapeDtypeStruct((b * s * ROW_CHUNKS, LANES), F32),
                   jax.ShapeDtypeStruct((ROUTE_ROWS, b * s), F32),
                   jax.ShapeDtypeStruct((ROUTE_ROWS, LANES), F32)],
        scratch_shapes=[pltpu.VMEM((tm, MEM_WIDTH), BF16)],
        compiler_params=pltpu.CompilerParams(
            dimension_semantics=("arbitrary", "arbitrary"), vmem_limit_bytes=VMEM_LIMIT_BYTES),
        name="merge",
    )(x, o_attn, u, u, u, cb, qm, km, vm, gates, w_conv, wpa, wpc, wpm, wout, g_ffn, w_r, b_r)


DRAIN_STEPS = 2


def _expert_kernel(blk_exp_ref, nvalid_ref, tok_first_ref, tok_ahead_ref, dst_prev_ref, x1_hbm,
                   gffn_ref, wg_ref, wu_ref, wd_ref, y_hbm, xbuf, hbuf, ybuf, wg_s, wu_s, wd_s,
                   gsem, ssem):
    i = pl.program_id(0)
    slot = lax.rem(i, 2)
    gslot = lax.rem(i, GATHER_SLOTS)
    rows = MOE_BLOCK * ROW_CHUNKS

    def gather_copy(idx_ref, b, j, s, prio):
        src_row = pl.multiple_of(idx_ref[b, 0, j], ROW_CHUNKS)
        return pltpu.make_async_copy(x1_hbm.at[pl.ds(src_row, ROW_CHUNKS), :],
                                     xbuf.at[s, pl.ds(j * ROW_CHUNKS, ROW_CHUNKS), :], gsem.at[s]
                                     ).start(priority=prio)

    def scatter_copy(j, s, prio):
        dst_row = pl.multiple_of(dst_prev_ref[0, 0, j], ROW_CHUNKS)
        return pltpu.make_async_copy(ybuf.at[s, pl.ds(j * ROW_CHUNKS, ROW_CHUNKS), :],
                                     y_hbm.at[pl.ds(dst_row, ROW_CHUNKS), :], ssem.at[s]
                                     ).start(priority=prio)

    def looped(issue):
        def body(it, c):
            for u in range(GATHER_UNROLL):
                issue(it * GATHER_UNROLL + u, u % 2)
            return c
        lax.fori_loop(0, MOE_BLOCK // GATHER_UNROLL, body, 0)

    def wait_gather(s):
        pltpu.make_async_copy(x1_hbm.at[pl.ds(0, rows), :], xbuf.at[s], gsem.at[s]).wait()

    def wait_scatter(s):
        pltpu.make_async_copy(ybuf.at[s], y_hbm.at[pl.ds(0, rows), :], ssem.at[s]).wait()

    nvalid = nvalid_ref[i]
    prev = jnp.maximum(i - 1, 0)
    prev2 = jnp.maximum(i - 2, 0)
    prev_active = jnp.logical_and(i >= 1, nvalid_ref[prev] > 0)

    @pl.when(i == 0)
    def _():
        looped(lambda j, prio: gather_copy(tok_first_ref, 0, j, 0, prio))
        looped(lambda j, prio: gather_copy(tok_first_ref, 1, j, 1, prio))
        ybuf[...] = jnp.zeros_like(ybuf)
        spare = pltpu.make_async_copy(ybuf.at[0], y_hbm.at[pl.ds(y_hbm.shape[0] - rows, rows), :],
                                      ssem.at[0])
        spare.start()
        spare.wait()

    @pl.when(prev_active)
    def _():
        wait_scatter(slot)

    @pl.when(jnp.logical_or(i == 0, blk_exp_ref[i] != blk_exp_ref[prev]))
    def _():
        wg_s[...] = wg_ref[0].astype(BF16)
        wu_s[...] = wu_ref[0].astype(BF16)
        wd_s[...] = wd_ref[0].astype(BF16)

    @pl.when(nvalid > 0)
    def _():
        wait_gather(gslot)
        ss = None
        for c in range(ROW_CHUNKS):
            xc = xbuf[gslot, pl.ds(c, MOE_BLOCK, stride=ROW_CHUNKS), :]
            ss = xc * xc if ss is None else ss + xc * xc
        scale = lax.rsqrt(jnp.sum(ss, axis=-1, keepdims=True) * (1.0 / D_MODEL) + EPS)
        for c in range(ROW_CHUNKS):
            sl = slice(c * LANES, (c + 1) * LANES)
            xc = xbuf[gslot, pl.ds(c, MOE_BLOCK, stride=ROW_CHUNKS), :]
            hbuf[:, sl] = (xc * scale * gffn_ref[:, sl]).astype(BF16)
        h2 = hbuf[...]
        a = jnp.dot(h2, wg_s[...], preferred_element_type=F32)
        up = jnp.dot(h2, wu_s[...], preferred_element_type=F32)
        act = (a * jax.nn.sigmoid(a) * up).astype(BF16)
        ahead = lax.rem(i + 2, GATHER_SLOTS)
        for j in range(MOE_BLOCK):
            gather_copy(tok_ahead_ref, 0, j, ahead, j % 2)
        for j in range(MOE_BLOCK):
            scatter_copy(j, 1 - slot, j % 2)
        y = jnp.dot(act, wd_s[...], preferred_element_type=F32)
        for c in range(ROW_CHUNKS):
            ybuf[slot, pl.ds(c, MOE_BLOCK, stride=ROW_CHUNKS), :] = y[:, c * LANES:(c + 1) * LANES]

    @pl.when(jnp.logical_and(nvalid == 0, prev_active))
    def _():
        looped(lambda j, prio: scatter_copy(j, 1 - slot, prio))
        wait_scatter(1 - slot)

    @pl.when(jnp.logical_and(nvalid == 0, jnp.logical_or(i < 2, nvalid_ref[prev2] > 0)))
    def _():
        wait_gather(gslot)


def _experts(x1r, g_ffn, blk_exp, nvalid, row_tok, row_dst_prev, wg, wu, wd):
    rows_total, _ = x1r.shape
    d = D_MODEL
    n_blk = blk_exp.shape[0]
    idx_spec = lambda fn: pl.BlockSpec((1, 1, MOE_BLOCK), fn, memory_space=pltpu.SMEM)
    grid_spec = pltpu.PrefetchScalarGridSpec(
        num_scalar_prefetch=2,
        grid=(n_blk,),
        in_specs=[
            pl.BlockSpec((2, 1, MOE_BLOCK), lambda i, be, nv: (0, 0, 0), memory_space=pltpu.SMEM),
            idx_spec(lambda i, be, nv: (jnp.minimum(i + 2, n_blk - 1), 0, 0)),
            idx_spec(lambda i, be, nv: (i, 0, 0)),
            pl.BlockSpec(memory_space=pl.ANY),
            pl.BlockSpec((1, d), lambda i, be, nv: (0, 0)),
            pl.BlockSpec((1, d, EXPERT_FF), lambda i, be, nv: (be[i], 0, 0)),
            pl.BlockSpec((1, d, EXPERT_FF), lambda i, be, nv: (be[i], 0, 0)),
            pl.BlockSpec((1, EXPERT_FF, d), lambda i, be, nv: (be[i], 0, 0)),
        ],
        out_specs=pl.BlockSpec(memory_space=pl.ANY),
        scratch_shapes=[
            pltpu.VMEM((GATHER_SLOTS, MOE_BLOCK * ROW_CHUNKS, LANES), F32),
            pltpu.VMEM((MOE_BLOCK, d), BF16),
            pltpu.VMEM((2, MOE_BLOCK * ROW_CHUNKS, LANES), F32),
            pltpu.VMEM((d, EXPERT_FF), BF16),
            pltpu.VMEM((d, EXPERT_FF), BF16),
            pltpu.VMEM((EXPERT_FF, d), BF16),
            pltpu.SemaphoreType.DMA((GATHER_SLOTS,)),
            pltpu.SemaphoreType.DMA((2,)),
        ],
    )
    return pl.pallas_call(
        _expert_kernel,
        grid_spec=grid_spec,
        out_shape=jax.ShapeDtypeStruct((TOP_K * rows_total + MOE_BLOCK * ROW_CHUNKS, LANES), F32),
        compiler_params=pltpu.CompilerParams(
            dimension_semantics=("arbitrary",), vmem_limit_bytes=VMEM_LIMIT_BYTES),
        name="experts",
    )(blk_exp, nvalid, row_tok, row_tok, row_dst_prev, x1r, g_ffn, wg, wu, wd)


def _combine_kernel(x1_ref, y0_ref, y1_ref, route_ref, o_ref, *, tm):
    w_cols = jnp.transpose(route_ref[...])
    w0 = w_cols[:, 2:3]
    w1 = w_cols[:, 3:4]
    for c in range(ROW_CHUNKS):
        rows = pl.ds(c, tm, stride=ROW_CHUNKS)
        o_ref[:, c * LANES:(c + 1) * LANES] = (x1_ref[rows, :] + w0 * y0_ref[rows, :]
                                               + w1 * y1_ref[rows, :])


def _combine(x1r, y, route_t):
    n = x1r.shape[0] // ROW_CHUNKS
    tm = MERGE_TILE
    nt = n // tm
    blk = (tm * ROW_CHUNKS, LANES)
    return pl.pallas_call(
        functools.partial(_combine_kernel, tm=tm),
        grid=(nt,),
        in_specs=[pl.BlockSpec(blk, lambda i: (i, 0)),
                  pl.BlockSpec(blk, lambda i: (i, 0)),
                  pl.BlockSpec(blk, lambda i: (i + nt, 0)),
                  pl.BlockSpec((ROUTE_ROWS, tm), lambda i: (0, i))],
        out_specs=pl.BlockSpec((tm, D_MODEL), lambda i: (i, 0)),
        out_shape=jax.ShapeDtypeStruct((n, D_MODEL), F32),
        compiler_params=pltpu.CompilerParams(
            dimension_semantics=("parallel",), vmem_limit_bytes=VMEM_LIMIT_BYTES),
        name="combine",
    )(x1r, y, y, route_t)


def _rope_tables(positions):
    half = ROT_DIM // 2
    lane = np.arange(LANES)
    dim = lane % HEAD_DIM
    rot = dim < ROT_DIM
    spread = np.zeros((ROT_DIM, 2, LANES), np.float32)
    spread[dim[rot] % half, 0, lane[rot]] = 1.0
    spread[half + dim[rot] % half, 1, lane[rot]] = np.where(dim[rot] < half, -1.0, 1.0)
    plain = np.stack([(~rot).astype(np.float32), np.zeros(LANES, np.float32)])[:, None, :]
    inv = ROPE_THETA ** (-jnp.arange(0, ROT_DIM, 2, dtype=F32) / ROT_DIM)
    ang = inv[None, :, None] * positions.astype(F32)[:, None, :]
    both = jnp.concatenate([jnp.cos(ang), jnp.sin(ang)], axis=1)
    return jnp.einsum('bjs,jtl->btsl', both, jnp.asarray(spread),
                      precision=lax.Precision.HIGHEST) + jnp.asarray(plain)


def _route(route_t, counts, n):
    nk = n * TOP_K
    e = route_t[0:TOP_K].reshape(nk).astype(jnp.int32)
    a_s = lax.sort(e * nk + jnp.arange(nk, dtype=jnp.int32)) % nk
    padded = ((counts + MOE_BLOCK - 1) // MOE_BLOCK) * MOE_BLOCK
    pend = jnp.cumsum(padded)
    pstart = pend - padded
    start = jnp.cumsum(counts) - counts
    n_blk = nk // MOE_BLOCK + N_EXPERTS + DRAIN_STEPS
    blk_start = jnp.arange(n_blk, dtype=jnp.int32) * MOE_BLOCK
    blk_exp = jnp.minimum(jnp.sum((pend[None, :] <= blk_start[:, None]).astype(jnp.int32), axis=1),
                          N_EXPERTS - 1).astype(jnp.int32)
    onehot = blk_exp[:, None] == jnp.arange(N_EXPERTS, dtype=jnp.int32)[None, :]
    pick = lambda table: jnp.sum(jnp.where(onehot, table[None, :], 0), axis=1)
    offset = blk_start - pick(pstart)
    nvalid = jnp.clip(pick(counts) - offset, 0, MOE_BLOCK)
    nvalid = jnp.where(blk_start < pend[-1], nvalid, 0).astype(jnp.int32)
    j = jnp.arange(MOE_BLOCK, dtype=jnp.int32)[None, :]
    valid = j < nvalid[:, None]
    src = jnp.clip((pick(start) + offset)[:, None] + j, 0, nk - 1)
    a_p = a_s[src]
    tok = (jnp.where(valid, a_p - n * (a_p >= n).astype(jnp.int32), j) * ROW_CHUNKS).astype(jnp.int32)
    dst = (jnp.where(valid, a_p, TOP_K * n + j) * ROW_CHUNKS).astype(jnp.int32)
    spare = jnp.broadcast_to((TOP_K * n + j) * ROW_CHUNKS, (1, MOE_BLOCK)).astype(jnp.int32)
    dst_prev = jnp.concatenate([spare, dst[:-1]], axis=0)
    return blk_exp, nvalid, tok.reshape(n_blk, 1, MOE_BLOCK), dst_prev.reshape(n_blk, 1, MOE_BLOCK)


def kernel(x, mem, positions, g_mix, g_mem, w_in, g_qn_attn, g_kn_attn, w_conv, w_mem_kv, g_qn_mem,
           g_kn_mem, w_proj_attn, w_proj_conv, w_proj_mem, w_out, g_ffn, w_router_group,
           b_router_group, w_router_expert, b_router_expert, w_gate, w_up, w_down):
    b, s, d = x.shape
    n = b * s
    rope_t = _rope_tables(positions)
    for l in range(w_in.shape[0]):
        wl = w_in[l]
        aw = N_BRANCH * GROUP_W
        cols = []
        for gi in range(len(DILATIONS)):
            for part in range(3):
                c0 = part * aw + gi * GROUP_W
                cols.append(wl[:, c0:c0 + GROUP_W])
        cols.append(wl[:, 3 * aw:])
        w_all = jnp.concatenate(cols, axis=1).astype(BF16)

        q0, q1, q2, u, cb, qm, gates = _inproj(x, g_mix[l][None], g_qn_mem[l][None], w_all)
        km, vm = _memkv(mem, g_mem[l][None], g_kn_mem[l][None], w_mem_kv[l].astype(BF16))
        gq = jnp.tile(g_qn_attn[l], 2)[None]
        gk = jnp.tile(g_kn_attn[l], 2)[None]
        o_attn = _attention((q0, q1, q2), rope_t, gq, gk)

        w_r = jnp.zeros((d, LANES), F32)
        w_r = w_r.at[:, :N_EXPERT_GROUPS].set(w_router_group[l])
        w_r = w_r.at[:, N_EXPERT_GROUPS:N_EXPERT_GROUPS + N_EXPERTS].set(w_router_expert[l])
        b_r = jnp.zeros((1, LANES), F32)
        b_r = b_r.at[0, :N_EXPERT_GROUPS].set(b_router_group[l])
        b_r = b_r.at[0, N_EXPERT_GROUPS:N_EXPERT_GROUPS + N_EXPERTS].set(b_router_expert[l])
        w_r_hi = w_r.astype(BF16)
        w_r_lo = (w_r - w_r_hi.astype(F32)).astype(BF16)
        x1, route_t, count_t = _merge(x, o_attn, u, cb, qm, km, vm, gates, w_conv[l],
                                      w_proj_attn[l].astype(BF16), w_proj_conv[l].astype(BF16),
                                      w_proj_mem[l].astype(BF16), w_out[l].astype(BF16),
                                      g_ffn[l][None], jnp.concatenate([w_r_hi, w_r_lo], axis=1), b_r)

        counts = count_t[0, N_EXPERT_GROUPS:N_EXPERT_GROUPS + N_EXPERTS].astype(jnp.int32)
        blk_exp, nvalid, row_tok, row_dst_prev = _route(route_t, counts, n)
        y = _experts(x1, g_ffn[l][None], blk_exp, nvalid, row_tok, row_dst_prev,
                     w_gate[l], w_up[l], w_down[l])
        x = _combine(x1, y, route_t).reshape(b, s, d)
    return x
```
